```python
import jax, jax.numpy as jnp
from jax import lax
import numpy as np

D_MODEL = 1024
BATCH = 1
SEQ = 16384
DEPTH = 1

CHUNK = 64
CONV_DIM = 512
CONV_WIDTH = 3
HGRN_HEADS = 4
HGRN_DK = 128
HGRN_DV = 128
HGRN_QK = HGRN_HEADS * HGRN_DK
HGRN_V = HGRN_HEADS * HGRN_DV
N_EXPERTS = 256
TOP_K = 8
N_GROUPS = 8
TOPK_GROUPS = 4
EXPERT_FF = 256
SHARED_FF = 256
ROUTED_SCALE = 2.5
MOE_BLOCK = 128
NORM_EPS = 1e-6
IN_SIZES = [CONV_DIM, CONV_DIM, CONV_DIM, HGRN_QK, HGRN_QK, HGRN_V, HGRN_V, D_MODEL, D_MODEL]
IN_COLS = sum(IN_SIZES)

kernel_name = "hybrid_conv_hgrn2_moe_block"


def rmsnorm(x, w):
    xf = x.astype(jnp.float32)
    y = xf * lax.rsqrt(jnp.mean(xf * xf, axis=-1, keepdims=True) + NORM_EPS)
    return (y * w.astype(jnp.float32)).astype(x.dtype)


def hgrn2_recurrence(q, k, v, logf):
    B, S, H, DK = q.shape
    DV = v.shape[-1]
    NC = S // CHUNK

    def chunks(t):
        return t.astype(jnp.float32).reshape(B, NC, CHUNK, H, t.shape[-1]).transpose(1, 0, 3, 2, 4)

    causal = jnp.tril(jnp.ones((CHUNK, CHUNK), dtype=bool))[:, :, None]

    def step(state, inp):
        qc, kc, vc, gc = inp
        b = jnp.cumsum(gc, axis=2)
        diff = b[:, :, :, None, :] - b[:, :, None, :, :]
        decay = jnp.where(causal, jnp.exp(jnp.where(causal, diff, 0.0)), 0.0)
        attn = jnp.einsum('bhid,bhjd,bhijd->bhij', qc, kc, decay)
        o = (jnp.einsum('bhij,bhjv->bhiv', attn, vc)
             + jnp.einsum('bhid,bhdv->bhiv', qc * jnp.exp(b), state))
        b_last = b[:, :, -1:, :]
        state = (jnp.exp(b_last[:, :, 0, :])[..., None] * state
                 + jnp.einsum('bhjd,bhjv->bhdv', kc * jnp.exp(b_last - b), vc))
        return state, o

    s0 = jnp.zeros((B, H, DK, DV), jnp.float32)
    _, o = lax.scan(step, s0, (chunks(q), chunks(k), chunks(v), chunks(logf)))
    return o.transpose(1, 0, 3, 2, 4).reshape(B, S, H, DV)


def token_mixer(h, w_in, conv_w, w_conv_out, lb, hgrn_norm_w, w_hgrn_out, w_o):
    B, S, _ = h.shape
    proj = h @ w_in
    cb, cc, cx, q, f, i, g, ga, gb = jnp.split(proj, list(np.cumsum(IN_SIZES)[:-1]), axis=-1)
    u = cc * cx
    up = jnp.pad(u, ((0, 0), (CONV_WIDTH - 1, 0), (0, 0)))
    conv = sum(up[:, t:t + S, :] * conv_w[t] for t in range(CONV_WIDTH))
    y_a = (cb * conv) @ w_conv_out
    qh = jax.nn.silu(q).reshape(B, S, HGRN_HEADS, HGRN_DK)
    fg = lb + (1.0 - lb) * jax.nn.sigmoid(f.astype(jnp.float32))
    logf = jnp.log(fg).reshape(B, S, HGRN_HEADS, HGRN_DK)
    kh = (1.0 - fg).reshape(B, S, HGRN_HEADS, HGRN_DK)
    vh = i.reshape(B, S, HGRN_HEADS, HGRN_DV)
    o = hgrn2_recurrence(qh, kh, vh, logf)
    o = rmsnorm(o, hgrn_norm_w).reshape(B, S, HGRN_V).astype(h.dtype) * jax.nn.silu(g)
    y_b = o @ w_hgrn_out
    m = jax.nn.sigmoid(ga) * y_a + jax.nn.sigmoid(gb) * y_b
    return m @ w_o


def moe_ffn(h, w_router, router_bias, w_gate_e, w_up_e, w_down_e, w_gate_s, w_up_s, w_down_s):
    B, S, D = h.shape
    T = B * S
    hf = h.reshape(T, D)
    scores = jax.nn.sigmoid((hf @ w_router).astype(jnp.float32))
    sel = scores + router_bias.astype(jnp.float32)
    grp = sel.reshape(T, N_GROUPS, N_EXPERTS // N_GROUPS)
    gscore = jnp.sum(lax.top_k(grp, 2)[0], axis=-1)
    _, gidx = lax.top_k(gscore, TOPK_GROUPS)
    gmask = jnp.any(jax.nn.one_hot(gidx, N_GROUPS, dtype=jnp.bool_), axis=1)
    emask = jnp.repeat(gmask, N_EXPERTS // N_GROUPS, axis=-1)
    _, eidx = lax.top_k(jnp.where(emask, sel, -jnp.inf), TOP_K)
    wsel = jnp.take_along_axis(scores, eidx, axis=-1)
    wsel = wsel / (jnp.sum(wsel, axis=-1, keepdims=True) + 1e-20) * ROUTED_SCALE

    TK = T * TOP_K
    e_flat = eidx.reshape(TK)
    w_flat = wsel.reshape(TK)
    tok_flat = jnp.arange(TK, dtype=jnp.int32) // TOP_K
    order = jnp.argsort(e_flat)
    se = e_flat[order]
    counts = jnp.zeros((N_EXPERTS,), jnp.int32).at[e_flat].add(1)
    starts = jnp.cumsum(counts) - counts
    padded = (counts + MOE_BLOCK - 1) // MOE_BLOCK * MOE_BLOCK
    pad_ends = jnp.cumsum(padded)
    pad_starts = pad_ends - padded
    dest = pad_starts[se] + (jnp.arange(TK, dtype=jnp.int32) - starts[se])
    n_blocks = -(-TK // MOE_BLOCK) + N_EXPERTS
    P = n_blocks * MOE_BLOCK
    buf_tok = jnp.zeros((P,), jnp.int32).at[dest].set(tok_flat[order])
    buf_w = jnp.zeros((P,), jnp.float32).at[dest].set(w_flat[order])
    block_e = jnp.minimum(
        jnp.searchsorted(pad_ends, jnp.arange(n_blocks, dtype=jnp.int32) * MOE_BLOCK, side='right'),
        N_EXPERTS - 1).astype(jnp.int32)

    def block_step(out, inp):
        tok, wb, e = inp
        xb = hf[tok]
        yb = (jax.nn.silu(xb @ w_gate_e[e]) * (xb @ w_up_e[e])) @ w_down_e[e]
        return out.at[tok].add(yb.astype(jnp.float32) * wb[:, None]), None

    out0 = jnp.zeros((T, D), jnp.float32)
    routed, _ = lax.scan(block_step, out0,
                         (buf_tok.reshape(n_blocks, MOE_BLOCK), buf_w.reshape(n_blocks, MOE_BLOCK), block_e))
    shared = (jax.nn.silu(hf @ w_gate_s) * (hf @ w_up_s)) @ w_down_s
    return (routed + shared.astype(jnp.float32)).astype(h.dtype).reshape(B, S, D)


def setup_inputs(seed: int = 0) -> dict:
    key = jax.random.key(seed)
    ks = jax.random.split(key, 24)
    n = jax.random.normal
    D = D_MODEL
    f32 = jnp.float32
    return {
        "x": n(ks[0], (BATCH, SEQ, D), f32),
        "c": n(ks[1], (BATCH, D), f32),
        "w_ada": n(ks[2], (DEPTH, D, 6 * D), f32) * (0.5 * D ** -0.5),
        "b_ada": n(ks[3], (DEPTH, 6 * D), f32) * 0.01,
        "norm_pre_mix": 1.0 + 0.01 * n(ks[4], (DEPTH, D), f32),
        "norm_post_mix": 1.0 + 0.01 * n(ks[5], (DEPTH, D), f32),
        "w_in": n(ks[6], (DEPTH, D, IN_COLS), f32) * D ** -0.5,
        "conv_w": n(ks[7], (DEPTH, CONV_WIDTH, CONV_DIM), f32) * CONV_WIDTH ** -0.5,
        "w_conv_out": n(ks[8], (DEPTH, CONV_DIM, D), f32) * CONV_DIM ** -0.5,
        "hgrn_lower_bounds": n(ks[9], (DEPTH + 1, HGRN_QK), f32) * 0.1,
        "hgrn_norm_w": 1.0 + 0.01 * n(ks[10], (DEPTH, HGRN_DV), f32),
        "w_hgrn_out": n(ks[11], (DEPTH, HGRN_V, D), f32) * HGRN_V ** -0.5,
        "w_o": n(ks[12], (DEPTH, D, D), f32) * D ** -0.5,
        "norm_pre_ffn": 1.0 + 0.01 * n(ks[13], (DEPTH, D), f32),
        "norm_post_ffn": 1.0 + 0.01 * n(ks[14], (DEPTH, D), f32),
        "w_router": n(ks[15], (DEPTH, D, N_EXPERTS), f32) * D ** -0.5,
        "router_bias": n(ks[16], (DEPTH, N_EXPERTS), f32) * 0.01,
        "w_gate_e": n(ks[17], (DEPTH, N_EXPERTS, D, EXPERT_FF), f32) * D ** -0.5,
        "w_up_e": n(ks[18], (DEPTH, N_EXPERTS, D, EXPERT_FF), f32) * D ** -0.5,
        "w_down_e": n(ks[19], (DEPTH, N_EXPERTS, EXPERT_FF, D), f32) * EXPERT_FF ** -0.5,
        "w_gate_s": n(ks[20], (DEPTH, D, SHARED_FF), f32) * D ** -0.5,
        "w_up_s": n(ks[21], (DEPTH, D, SHARED_FF), f32) * D ** -0.5,
        "w_down_s": n(ks[22], (DEPTH, SHARED_FF, D), f32) * SHARED_FF ** -0.5,
    }


def reference(x, c, w_ada, b_ada, norm_pre_mix, norm_post_mix, w_in, conv_w, w_conv_out,
              hgrn_lower_bounds, hgrn_norm_w, w_hgrn_out, w_o, norm_pre_ffn, norm_post_ffn,
              w_router, router_bias, w_gate_e, w_up_e, w_down_e, w_gate_s, w_up_s, w_down_s):
    lb_all = jnp.cumsum(jax.nn.softmax(hgrn_lower_bounds.astype(jnp.float32), axis=0), axis=0)
    cs = jax.nn.silu(c)
    for l in range(DEPTH):
        mod = cs @ w_ada[l] + b_ada[l]
        sh1, sc1, g1, sh2, sc2, g2 = jnp.split(mod, 6, axis=-1)
        h = rmsnorm(x, norm_pre_mix[l]) * (1.0 + sc1[:, None, :]) + sh1[:, None, :]
        y = token_mixer(h, w_in[l], conv_w[l], w_conv_out[l], lb_all[l], hgrn_norm_w[l],
                        w_hgrn_out[l], w_o[l])
        x = x + g1[:, None, :] * rmsnorm(y, norm_post_mix[l])
        h = rmsnorm(x, norm_pre_ffn[l]) * (1.0 + sc2[:, None, :]) + sh2[:, None, :]
        y = moe_ffn(h, w_router[l], router_bias[l], w_gate_e[l], w_up_e[l], w_down_e[l],
                    w_gate_s[l], w_up_s[l], w_down_s[l])
        x = x + g2[:, None, :] * rmsnorm(y, norm_post_ffn[l])
    return x
```

```python
import functools

import jax
import jax.numpy as jnp
from jax import lax
from jax.experimental import pallas as pl
from jax.experimental.pallas import tpu as pltpu

F32 = jnp.float32
BF16 = jnp.bfloat16

NORM_EPS = 1e-6
CONV_DIM = 512
CONV_WIDTH = 3
HGRN_HEADS = 4
HGRN_DK = 128
HGRN_QK = HGRN_HEADS * HGRN_DK
N_GROUPS = 8
TOPK_GROUPS = 4
TOP_K = 8
ROUTED_SCALE = 2.5

SEQ_TILE = 256
VMEM_LIMIT_BYTES = 56 * 1024 * 1024


def _dot(a, b):
    return jnp.dot(a, b, preferred_element_type=F32)


def _dot_nt(a, b):
    return lax.dot_general(a, b, (((1,), (1,)), ((), ())), preferred_element_type=F32)


def _dot_tn(a, b):
    return lax.dot_general(a, b, (((0,), (0,)), ((), ())), preferred_element_type=F32)


def _split3(x):
    hi = x.astype(BF16)
    r1 = x - hi.astype(F32)
    mid = r1.astype(BF16)
    lo = (r1 - mid.astype(F32)).astype(BF16)
    return hi, mid, lo


def _sigmoid(x):
    return 1.0 / (1.0 + jnp.exp(-x))


def _silu(x):
    return x * _sigmoid(x)


def _rms(x, w):
    ms = jnp.mean(x * x, axis=-1, keepdims=True)
    return x * lax.rsqrt(ms + NORM_EPS) * w


def _ada_kernel(c_ref, w_ref, b_ref, o_ref):
    cs = _silu(c_ref[...])
    h1, h2, h3 = _split3(cs)
    w1, w2, w3 = _split3(w_ref[...])
    acc = _dot(h1, w1) + (_dot(h1, w2) + _dot(h2, w1)) + (_dot(h1, w3) + _dot(h2, w2) + _dot(h3, w1))
    o_ref[...] = acc + b_ref[...]


def _ada_mod(c, w_ada, b_ada):
    d = c.shape[-1]
    n = w_ada.shape[-1]
    bn = 1024
    c8 = jnp.broadcast_to(c.reshape(1, d), (8, d))
    out = pl.pallas_call(
        _ada_kernel,
        out_shape=jax.ShapeDtypeStruct((8, n), F32),
        grid=(n // bn,),
        in_specs=[pl.BlockSpec((8, d), lambda j: (0, 0)),
                  pl.BlockSpec((d, bn), lambda j: (0, j)),
                  pl.BlockSpec((1, bn), lambda j: (0, j))],
        out_specs=pl.BlockSpec((8, bn), lambda j: (0, j)),
        compiler_params=pltpu.CompilerParams(dimension_semantics=("arbitrary",),
                                             vmem_limit_bytes=VMEM_LIMIT_BYTES),
        name="ada_mod",
    )(c8, w_ada, b_ada.reshape(1, n))
    return out[0:1]


def _level_reference(b, rolled, s, row):
    ts = b.shape[0]
    c = s // 2 - 1
    if s >= 16:
        pieces = []
        for blk in range(ts // s):
            r = blk * s + c
            pieces.append(jnp.broadcast_to(b[r:r + 1, :], (s, b.shape[1])))
        return pieces[0] if len(pieces) == 1 else jnp.concatenate(pieces, axis=0)
    pos = row & (s - 1)
    out = b
    for p in range(s):
        if p == c:
            continue
        out = jnp.where(pos == p, rolled[p - c], out)
    return out


def _mixer_kernel(x_ref, mod_ref, npre_ref, npost_ref, win_ref, convw_ref, wco_ref,
                  hlb_ref, hnw_ref, who_ref, wo_ref, o_ref, state_ref, ubuf_ref):
    ts, d = x_ref.shape
    step = pl.program_id(0)

    @pl.when(step == 0)
    def _():
        state_ref[...] = jnp.zeros_like(state_ref)
        ubuf_ref[0:8, :] = jnp.zeros((8, CONV_DIM), F32)

    x = x_ref[...]
    sh1 = mod_ref[:, 0:d]
    sc1 = mod_ref[:, d:2 * d]
    g1 = mod_ref[:, 2 * d:3 * d]
    h = (_rms(x, npre_ref[...]) * (1.0 + sc1) + sh1).astype(BF16)

    def proj(lo, width):
        return _dot(h, win_ref[:, lo:lo + width])

    c0 = 0
    cb = proj(c0, CONV_DIM)
    u = proj(c0 + CONV_DIM, CONV_DIM) * proj(c0 + 2 * CONV_DIM, CONV_DIM)
    ubuf_ref[8:8 + ts, :] = u
    conv = (ubuf_ref[6:6 + ts, :] * convw_ref[0:1, :]
            + ubuf_ref[7:7 + ts, :] * convw_ref[1:2, :]
            + u * convw_ref[2:3, :])
    ubuf_ref[0:8, :] = ubuf_ref[ts:ts + 8, :]
    y_a = _dot((cb * conv).astype(BF16), wco_ref[...])

    c1 = 3 * CONV_DIM
    q = _silu(proj(c1, HGRN_QK))
    hl = hlb_ref[...]
    hm = jnp.max(hl, axis=0, keepdims=True)
    he = jnp.exp(hl - hm)
    lb = he[0:1, :] / jnp.sum(he, axis=0, keepdims=True)
    fg = lb + (1.0 - lb) * _sigmoid(proj(c1 + HGRN_QK, HGRN_QK))
    k = 1.0 - fg
    g = jnp.log(fg)
    v = proj(c1 + 2 * HGRN_QK, HGRN_QK)
    gg = proj(c1 + 3 * HGRN_QK, HGRN_QK)

    row = lax.broadcasted_iota(jnp.int32, (ts, ts), 0)
    col = lax.broadcasted_iota(jnp.int32, (ts, ts), 1)
    tril = jnp.where(row >= col, 1.0, 0.0).astype(BF16)
    ghi, gmid, glo = _split3(g)
    b = _dot(tril, ghi) + _dot(tril, gmid) + _dot(tril, glo)
    b_last = b[ts - 1:ts, :]

    rowq = lax.broadcasted_iota(jnp.int32, (ts, HGRN_QK), 0)
    rolled = {sft: pltpu.roll(b, sft % ts, axis=0) for sft in (-3, -2, -1, 1, 2, 3, 4)}
    xor = row ^ col

    levels = []
    s = ts
    while s >= 2:
        levels.append(s)
        s //= 2
    qt, kt = [], []
    for s in levels:
        bref = _level_reference(b, rolled, s, rowq)
        e = jnp.exp(-jnp.abs(b - bref))
        upper = (rowq & (s // 2)) != 0
        qt.append(jnp.where(upper, q * e, 0.0).astype(BF16))
        kt.append(jnp.where(upper, 0.0, k * e).astype(BF16))
    q_in = (q * jnp.exp(b)).astype(BF16)
    k_out = (k * jnp.exp(b_last - b)).astype(BF16)
    v16 = v.astype(BF16)
    qk = q * k
    decay_last = jnp.exp(b_last)

    hnw = hnw_ref[...]
    outs = []
    for hd in range(HGRN_HEADS):
        sl = slice(hd * HGRN_DK, (hd + 1) * HGRN_DK)
        a = jnp.zeros((ts, ts), F32)
        for li in range(len(levels) - 1, -1, -1):
            s = levels[li]
            p = _dot_nt(qt[li][:, sl], kt[li][:, sl])
            a = jnp.where(xor >= s // 2, p, a)
        st = state_ref[hd]
        o_h = (_dot(a.astype(BF16), v16[:, sl])
               + jnp.sum(qk[:, sl], axis=-1, keepdims=True) * v[:, sl]
               + _dot_nt(q_in[:, sl], st.astype(BF16)))
        state_ref[hd] = st * decay_last[:, sl] + _dot_tn(v16[:, sl], k_out[:, sl])
        outs.append(_rms(o_h, hnw))
    o = jnp.concatenate(outs, axis=-1) * _silu(gg)
    y_b = _dot(o.astype(BF16), who_ref[...])

    c2 = c1 + 4 * HGRN_QK
    m = _sigmoid(proj(c2, d)) * y_a + _sigmoid(proj(c2 + d, d)) * y_b
    y = _dot(m.astype(BF16), wo_ref[...])
    o_ref[...] = x + g1 * _rms(y, npost_ref[...])


def _const_spec(shape):
    nd = len(shape)
    return pl.BlockSpec(shape, lambda i: (0,) * nd)


def _mixer(x2, mod, npre, npost, w_in, conv_w, w_conv_out, hlb, hnw, w_hgrn_out, w_o):
    t, d = x2.shape
    ts = SEQ_TILE
    args = (x2, mod, npre.reshape(1, d), npost.reshape(1, d), w_in.astype(BF16), conv_w,
            w_conv_out.astype(BF16), hlb, hnw.reshape(1, -1), w_hgrn_out.astype(BF16),
            w_o.astype(BF16))
    in_specs = [pl.BlockSpec((ts, d), lambda i: (i, 0))] + [_const_spec(a.shape) for a in args[1:]]
    return pl.pallas_call(
        _mixer_kernel,
        out_shape=jax.ShapeDtypeStruct((t, d), F32),
        grid=(t // ts,),
        in_specs=in_specs,
        out_specs=pl.BlockSpec((ts, d), lambda i: (i, 0)),
        scratch_shapes=[pltpu.VMEM((HGRN_HEADS, HGRN_DK, HGRN_DK), F32),
                        pltpu.VMEM((ts + 8, CONV_DIM), F32)],
        compiler_params=pltpu.CompilerParams(dimension_semantics=("arbitrary",),
                                             vmem_limit_bytes=VMEM_LIMIT_BYTES),
        name="token_mixer",
    )(*args)


def _pack_bf16_pairs(x):
    m = x.shape[1] // 2
    hi = lax.bitcast_convert_type(x[:, :m].astype(BF16).astype(F32), jnp.uint32)
    lo = lax.bitcast_convert_type(x[:, m:].astype(BF16).astype(F32), jnp.uint32)
    return hi | (lo >> 16)


def _unpack_bf16_pairs(p):
    hi = lax.bitcast_convert_type(p & jnp.uint32(0xFFFF0000), F32)
    lo = lax.bitcast_convert_type(p << 16, F32)
    return hi, lo


def _router_kernel(x1_ref, mod_ref, npre_ref, wrh_ref, wrl_ref, rb_ref,
                   hp_ref, eidx_ref, rank_ref, wcol_ref, cnt_ref, carry_ref, rows_ref):
    assert N_GROUPS == TOP_K == rows_ref.shape[0]
    tr, d = x1_ref.shape
    n_e = wrh_ref.shape[0]
    gsz = n_e // N_GROUPS
    neg = -jnp.inf

    @pl.when(pl.program_id(0) == 0)
    def _():
        carry_ref[...] = jnp.zeros_like(carry_ref)

    sh2 = mod_ref[:, 3 * d:4 * d]
    sc2 = mod_ref[:, 4 * d:5 * d]
    h2 = _rms(x1_ref[...], npre_ref[...]) * (1.0 + sc2) + sh2
    hp_ref[...] = _pack_bf16_pairs(h2)
    h_hi = h2.astype(BF16)
    h_lo = (h2 - h_hi.astype(F32)).astype(BF16)
    wrh = wrh_ref[...]
    logits = _dot_nt(wrh, h_hi) + (_dot_nt(wrh, h_lo) + _dot_nt(wrl_ref[...], h_hi))
    scores = _sigmoid(logits)
    sel = scores + rb_ref[:, 0:1]

    io_e = lax.broadcasted_iota(jnp.int32, (n_e, tr), 0)
    for g in range(N_GROUPS):
        blk = sel[g * gsz:(g + 1) * gsz, :]
        io = lax.broadcasted_iota(jnp.int32, (gsz, tr), 0) + g * gsz
        m1 = jnp.max(blk, axis=0, keepdims=True)
        i1 = jnp.min(jnp.where(blk == m1, io, n_e), axis=0, keepdims=True)
        m2 = jnp.max(jnp.where(io == i1, neg, blk), axis=0, keepdims=True)
        rows_ref[g:g + 1, :] = m1 + m2
    gs = rows_ref[...]
    io_g = lax.broadcasted_iota(jnp.int32, (N_GROUPS, tr), 0)
    gsel = jnp.zeros((N_GROUPS, tr), F32)
    for _ in range(TOPK_GROUPS):
        m = jnp.max(gs, axis=0, keepdims=True)
        gi = jnp.min(jnp.where(gs == m, io_g, N_GROUPS), axis=0, keepdims=True)
        hit = io_g == gi
        gsel = jnp.where(hit, 1.0, gsel)
        gs = jnp.where(hit, neg, gs)
    rows_ref[...] = gsel
    cur = jnp.concatenate(
        [jnp.where(rows_ref[g:g + 1, :] > 0.5, sel[g * gsz:(g + 1) * gsz, :], neg) for g in range(N_GROUPS)],
        axis=0)

    idxs = []
    selm = jnp.zeros((n_e, tr), F32)
    for k in range(TOP_K):
        m = jnp.max(cur, axis=0, keepdims=True)
        idx = jnp.min(jnp.where(cur == m, io_e, n_e), axis=0, keepdims=True)
        hit = io_e == idx
        rows_ref[k:k + 1, :] = jnp.sum(jnp.where(hit, scores, 0.0), axis=0, keepdims=True)
        cur = jnp.where(hit, neg, cur)
        selm = jnp.where(hit, 1.0, selm)
        eidx_ref[k:k + 1, :] = idx
        idxs.append(idx)

    r_io = lax.broadcasted_iota(jnp.int32, (tr, tr + 128), 0)
    c_io = lax.broadcasted_iota(jnp.int32, (tr, tr + 128), 1)
    before = jnp.where((r_io < c_io) | (c_io >= tr), 1.0, 0.0).astype(BF16)
    r_ext = _dot(selm.astype(BF16), before)
    rank_full = r_ext[:, :tr] + carry_ref[:, 0:1]
    for k, idx in enumerate(idxs):
        rk = jnp.sum(jnp.where(io_e == idx, rank_full, 0.0), axis=0, keepdims=True)
        rank_ref[k:k + 1, :] = rk.astype(jnp.int32)
    carry_ref[...] = carry_ref[...] + r_ext[:, tr:]
    cnt_ref[...] = carry_ref[...]

    wk = rows_ref[...]
    wn = wk / (jnp.sum(wk, axis=0, keepdims=True) + 1e-20) * ROUTED_SCALE
    eye = jnp.where(lax.broadcasted_iota(jnp.int32, (TOP_K, 128), 0)
                    == lax.broadcasted_iota(jnp.int32, (TOP_K, 128), 1), 1.0, 0.0).astype(BF16)
    w1, w2, w3 = _split3(wn)
    wcol_ref[...] = _dot_tn(w1, eye) + _dot_tn(w2, eye) + _dot_tn(w3, eye)


def _router(x1, mod, npre2, w_router, router_bias):
    t, d = x1.shape
    n_e = w_router.shape[1]
    tr = SEQ_TILE
    wrt = w_router.T
    wrh = wrt.astype(BF16)
    wrl = (wrt - wrh.astype(F32)).astype(BF16)
    rb = jnp.broadcast_to(router_bias.reshape(n_e, 1), (n_e, 128))
    args = (x1, mod, npre2.reshape(1, d), wrh, wrl, rb)
    in_specs = [pl.BlockSpec((tr, d), lambda i: (i, 0))] + [_const_spec(a.shape) for a in args[1:]]
    return pl.pallas_call(
        _router_kernel,
        out_shape=(jax.ShapeDtypeStruct((t, d // 2), jnp.uint32),
                   jax.ShapeDtypeStruct((TOP_K, t), jnp.int32),
                   jax.ShapeDtypeStruct((TOP_K, t), jnp.int32),
                   jax.ShapeDtypeStruct((t, 128), F32),
                   jax.ShapeDtypeStruct((n_e, 128), F32)),
        grid=(t // tr,),
        in_specs=in_specs,
        out_specs=(pl.BlockSpec((tr, d // 2), lambda i: (i, 0)),
                   pl.BlockSpec((TOP_K, tr), lambda i: (0, i)),
                   pl.BlockSpec((TOP_K, tr), lambda i: (0, i)),
                   pl.BlockSpec((tr, 128), lambda i: (i, 0)),
                   pl.BlockSpec((n_e, 128), lambda i: (0, 0))),
        scratch_shapes=[pltpu.VMEM((n_e, 128), F32), pltpu.VMEM((TOP_K, tr), F32)],
        compiler_params=pltpu.CompilerParams(dimension_semantics=("arbitrary",),
                                             vmem_limit_bytes=VMEM_LIMIT_BYTES),
        name="moe_router",
    )(*args)


def _dest_kernel(eidx_ref, rank_ref, cnt_ref, dest_ref):
    n_e = cnt_ref.shape[0]
    tt = eidx_ref.shape[1]
    cnt = cnt_ref[...]
    c_hi = jnp.floor(cnt * (1.0 / 128.0))
    c_lo = cnt - c_hi * 128.0
    r = lax.broadcasted_iota(jnp.int32, (n_e, n_e), 0)
    c = lax.broadcasted_iota(jnp.int32, (n_e, n_e), 1)
    lower = jnp.where(c < r, 1.0, 0.0).astype(BF16)
    starts = 128.0 * _dot(lower, c_hi.astype(BF16)) + _dot(lower, c_lo.astype(BF16))
    start_col = starts[:, 0:1]
    io_e = lax.broadcasted_iota(jnp.int32, (n_e, tt), 0)
    rows = []
    for k in range(TOP_K):
        hit = io_e == eidx_ref[k:k + 1, :]
        rows.append(jnp.sum(jnp.where(hit, start_col, 0.0), axis=0, keepdims=True))
    dest_ref[...] = jnp.concatenate(rows, axis=0).astype(jnp.int32) + rank_ref[...]


def _dest(eidx, rank, cnt):
    k, t = eidx.shape
    tt = 512
    return pl.pallas_call(
        _dest_kernel,
        out_shape=jax.ShapeDtypeStruct((k, t), jnp.int32),
        grid=(t // tt,),
        in_specs=[pl.BlockSpec((k, tt), lambda i: (0, i)),
                  pl.BlockSpec((k, tt), lambda i: (0, i)),
                  _const_spec(cnt.shape)],
        out_specs=pl.BlockSpec((k, tt), lambda i: (0, i)),
        compiler_params=pltpu.CompilerParams(dimension_semantics=("arbitrary",)),
        name="moe_dest",
    )(eidx, rank, cnt)


def _index_tile_copy(idx_hbm, idx_smem, sem, step, tile):
    slot = step % 2
    return pltpu.make_async_copy(idx_hbm.at[:, pl.ds(step * tile, tile)], idx_smem.at[slot], sem.at[slot])


def _dispatch_kernel(dest_hbm, hp_ref, xs_hbm, idx_smem, idx_sem, row_sem):
    td = hp_ref.shape[0]
    step = pl.program_id(0)
    nsteps = pl.num_programs(0)

    @pl.when(step == 0)
    def _():
        _index_tile_copy(dest_hbm, idx_smem, idx_sem, step, td).start()

    @pl.when(step + 1 < nsteps)
    def _():
        _index_tile_copy(dest_hbm, idx_smem, idx_sem, step + 1, td).start()

    _index_tile_copy(dest_hbm, idx_smem, idx_sem, step, td).wait()
    slot = step % 2

    def row_copy(t, dst):
        return pltpu.make_async_copy(hp_ref.at[pl.ds(t, 1), :], xs_hbm.at[pl.ds(dst, 1), :], row_sem)

    def body(t, carry):
        for k in range(TOP_K):
            row_copy(t, idx_smem[slot, k, t]).start()
        return carry

    lax.fori_loop(0, td, body, 0)
    for k in range(TOP_K):
        pltpu.make_async_copy(hp_ref, xs_hbm.at[pl.ds(0, td), :], row_sem).wait()


def _dispatch(dest, hp):
    t, m = hp.shape
    td = SEQ_TILE
    return pl.pallas_call(
        _dispatch_kernel,
        out_shape=jax.ShapeDtypeStruct((t * TOP_K, m), jnp.uint32),
        grid=(t // td,),
        in_specs=[pl.BlockSpec(memory_space=pl.ANY),
                  pl.BlockSpec((td, m), lambda i: (i, 0))],
        out_specs=pl.BlockSpec(memory_space=pl.ANY),
        scratch_shapes=[pltpu.SMEM((2, TOP_K, td), jnp.int32),
                        pltpu.SemaphoreType.DMA((2,)),
                        pltpu.SemaphoreType.DMA(())],
        compiler_params=pltpu.CompilerParams(dimension_semantics=("arbitrary",)),
        name="moe_dispatch",
    )(dest, hp)


EXPERT_BLOCK = 128


def _expert_kernel(vm_ref, ve_ref, vlo_ref, vhi_ref, vfirst_ref, vnew_ref,
                   xs_ref, wg_ref, wu_ref, wd_ref, y_ref, wgu16_ref, wd16_ref):
    v = pl.program_id(0)
    bm, half = xs_ref.shape
    ff = wg_ref.shape[2]

    @pl.when(vnew_ref[v] == 1)
    def _():
        wgu16_ref[:, 0:ff] = wg_ref[0].astype(BF16)
        wgu16_ref[:, ff:2 * ff] = wu_ref[0].astype(BF16)
        wd16_ref[...] = wd_ref[0].astype(BF16)

    @pl.when(vfirst_ref[v] == 1)
    def _():
        y_ref[...] = jnp.zeros_like(y_ref)

    lo = vlo_ref[v]
    hi = vhi_ref[v]

    @pl.when(hi > lo)
    def _():
        xa, xb = _unpack_bf16_pairs(xs_ref[...])
        hgu = (_dot(xa.astype(BF16), wgu16_ref[0:half, :])
               + _dot(xb.astype(BF16), wgu16_ref[half:2 * half, :]))
        act = _silu(hgu[:, :ff]) * hgu[:, ff:]
        y = _dot(act.astype(BF16), wd16_ref[...])
        rows = lax.broadcasted_iota(jnp.int32, (bm, half), 0)
        mine = (rows >= lo) & (rows < hi)
        y_ref[...] = jnp.where(mine, _pack_bf16_pairs(y), y_ref[...])


def _visit_table(counts, n_rows, bm):
    n_e = counts.shape[0]
    ends = jnp.cumsum(counts)
    starts = ends - counts
    first_blk = starts // bm
    nv_e = jnp.where(counts > 0, (ends - 1) // bm - first_blk + 1, 0)
    vcum = jnp.cumsum(nv_e)
    n_vis = vcum[-1]
    max_vis = n_rows // bm + n_e - 1
    v = jnp.arange(max_vis, dtype=jnp.int32)
    vv = jnp.minimum(v, n_vis - 1)
    e = jnp.searchsorted(vcum, vv, side='right').astype(jnp.int32)
    j = vv - (vcum[e] - nv_e[e])
    m = first_blk[e] + j
    lo = jnp.maximum(starts[e], m * bm) - m * bm
    hi = jnp.minimum(ends[e], (m + 1) * bm) - m * bm
    live = v < n_vis
    hi = jnp.where(live, hi, lo)
    first = jnp.where(live & (lo == 0), 1, 0)
    new = jnp.where(live & (j == 0), 1, 0)
    i32 = lambda a: a.astype(jnp.int32)
    return i32(m), i32(e), i32(lo), i32(hi), i32(first), i32(new)


def _experts(table, xs, w_gate_e, w_up_e, w_down_e):
    p, half = xs.shape
    n_e, d, ff = w_gate_e.shape
    bm = EXPERT_BLOCK
    n_vis = table[0].shape[0]
    grid_spec = pltpu.PrefetchScalarGridSpec(
        num_scalar_prefetch=6,
        grid=(n_vis,),
        in_specs=[pl.BlockSpec((bm, half), lambda v, vm, ve, *_: (vm[v], 0)),
                  pl.BlockSpec((1, d, ff), lambda v, vm, ve, *_: (ve[v], 0, 0)),
                  pl.BlockSpec((1, d, ff), lambda v, vm, ve, *_: (ve[v], 0, 0)),
                  pl.BlockSpec((1, ff, d), lambda v, vm, ve, *_: (ve[v], 0, 0))],
        out_specs=pl.BlockSpec((bm, half), lambda v, vm, ve, *_: (vm[v], 0)),
        scratch_shapes=[pltpu.VMEM((d, 2 * ff), BF16), pltpu.VMEM((ff, d), BF16)],
    )
    return pl.pallas_call(
        _expert_kernel,
        out_shape=jax.ShapeDtypeStruct((p, half), jnp.uint32),
        grid_spec=grid_spec,
        compiler_params=pltpu.CompilerParams(dimension_semantics=("arbitrary",),
                                             vmem_limit_bytes=VMEM_LIMIT_BYTES),
        name="moe_experts",
    )(*table, xs, w_gate_e, w_up_e, w_down_e)


def _combine_kernel(dest_hbm, y_hbm, wcol_ref, x1_ref, hp_ref, mod_ref, npost_ref,
                    wgus_ref, wds_ref, o_ref, idx_smem, ybuf_ref, idx_sem, row_sem):
    te, d = x1_ref.shape
    half = d // 2
    ff = wds_ref.shape[0]
    step = pl.program_id(0)
    nsteps = pl.num_programs(0)

    @pl.when(step == 0)
    def _():
        _index_tile_copy(dest_hbm, idx_smem, idx_sem, step, te).start()

    @pl.when(step + 1 < nsteps)
    def _():
        _index_tile_copy(dest_hbm, idx_smem, idx_sem, step + 1, te).start()

    _index_tile_copy(dest_hbm, idx_smem, idx_sem, step, te).wait()
    slot = step % 2

    def body(t, carry):
        for k in range(TOP_K):
            src = idx_smem[slot, k, t]
            pltpu.make_async_copy(y_hbm.at[pl.ds(src, 1), :], ybuf_ref.at[k, pl.ds(t, 1), :], row_sem).start()
        return carry

    lax.fori_loop(0, te, body, 0)

    xa, xb = _unpack_bf16_pairs(hp_ref[...])
    hgu = _dot(xa.astype(BF16), wgus_ref[0:half, :]) + _dot(xb.astype(BF16), wgus_ref[half:d, :])
    act = _silu(hgu[:, :ff]) * hgu[:, ff:]
    shared = _dot(act.astype(BF16), wds_ref[...])

    for k in range(TOP_K):
        pltpu.make_async_copy(y_hbm.at[pl.ds(0, te), :], ybuf_ref.at[k], row_sem).wait()

    acc_a = shared[:, :half]
    acc_b = shared[:, half:]
    for k in range(TOP_K):
        ya, yb = _unpack_bf16_pairs(ybuf_ref[k])
        wk = wcol_ref[:, k:k + 1]
        acc_a = acc_a + wk * ya
        acc_b = acc_b + wk * yb
    moe = jnp.concatenate([acc_a, acc_b], axis=-1)
    g2 = mod_ref[:, 5 * d:6 * d]
    o_ref[...] = x1_ref[...] + g2 * _rms(moe, npost_ref[...])


def _combine(dest, y, wcol, x1, hp, mod, npost2, w_gate_s, w_up_s, w_down_s):
    t, d = x1.shape
    half = d // 2
    te = SEQ_TILE
    wgus = jnp.concatenate([w_gate_s, w_up_s], axis=1).astype(BF16)
    wds = w_down_s.astype(BF16)
    return pl.pallas_call(
        _combine_kernel,
        out_shape=jax.ShapeDtypeStruct((t, d), F32),
        grid=(t // te,),
        in_specs=[pl.BlockSpec(memory_space=pl.ANY),
                  pl.BlockSpec(memory_space=pl.ANY),
                  pl.BlockSpec((te, 128), lambda i: (i, 0)),
                  pl.BlockSpec((te, d), lambda i: (i, 0)),
                  pl.BlockSpec((te, half), lambda i: (i, 0)),
                  _const_spec(mod.shape),
                  _const_spec((1, d)),
                  _const_spec(wgus.shape),
                  _const_spec(wds.shape)],
        out_specs=pl.BlockSpec((te, d), lambda i: (i, 0)),
        scratch_shapes=[pltpu.SMEM((2, TOP_K, te), jnp.int32),
                        pltpu.VMEM((TOP_K, te, half), jnp.uint32),
                        pltpu.SemaphoreType.DMA((2,)),
                        pltpu.SemaphoreType.DMA(())],
        compiler_params=pltpu.CompilerParams(dimension_semantics=("arbitrary",),
                                             vmem_limit_bytes=VMEM_LIMIT_BYTES),
        name="moe_combine",
    )(dest, y, wcol, x1, hp, mod, npost2.reshape(1, d), wgus, wds)


def kernel(x, c, w_ada, b_ada, norm_pre_mix, norm_post_mix, w_in, conv_w, w_conv_out, hgrn_lower_bounds, hgrn_norm_w, w_hgrn_out, w_o, norm_pre_ffn, norm_post_ffn, w_router, router_bias, w_gate_e, w_up_e, w_down_e, w_gate_s, w_up_s, w_down_s):
    bsz, seq, d = x.shape
    assert bsz == 1 and w_ada.shape[0] == 1
    mod = _ada_mod(c, w_ada[0], b_ada[0])
    x1 = _mixer(x.reshape(seq, d), mod, norm_pre_mix[0], norm_post_mix[0], w_in[0], conv_w[0],
                w_conv_out[0], hgrn_lower_bounds, hgrn_norm_w[0], w_hgrn_out[0], w_o[0])
    hp, eidx, rank, wcol, cnt = _router(x1, mod, norm_pre_ffn[0], w_router[0], router_bias[0])
    dest = _dest(eidx, rank, cnt)
    xs = _dispatch(dest, hp)
    counts = cnt[:, 0].astype(jnp.int32)
    table = _visit_table(counts, seq * TOP_K, EXPERT_BLOCK)
    y = _experts(table, xs, w_gate_e[0], w_up_e[0], w_down_e[0])
    out = _combine(dest, y, wcol, x1, hp, mod, norm_post_ffn[0], w_gate_s[0], w_up_s[0], w_down_s[0])
    return out.reshape(bsz, seq, d)
```

```python
import functools

import jax
import jax.numpy as jnp
from jax import lax
from jax.experimental import pallas as pl
from jax.experimental.pallas import tpu as pltpu

F32 = jnp.float32
BF16 = jnp.bfloat16

NORM_EPS = 1e-6
CONV_DIM = 512
CONV_WIDTH = 3
HGRN_HEADS = 4
HGRN_DK = 128
HGRN_QK = HGRN_HEADS * HGRN_DK
N_GROUPS = 8
TOPK_GROUPS = 4
TOP_K = 8
ROUTED_SCALE = 2.5

SEQ_TILE = 256
EXPERT_CHUNK = 256
ROW_ALIGN = 8
VMEM_LIMIT_BYTES = 56 * 1024 * 1024


def _dot(a, b):
    return jnp.dot(a, b, preferred_element_type=F32)


def _dot_nt(a, b):
    return lax.dot_general(a, b, (((1,), (1,)), ((), ())), preferred_element_type=F32)


def _dot_tn(a, b):
    return lax.dot_general(a, b, (((0,), (0,)), ((), ())), preferred_element_type=F32)


def _split3(x):
    hi = x.astype(BF16)
    r1 = x - hi.astype(F32)
    mid = r1.astype(BF16)
    lo = (r1 - mid.astype(F32)).astype(BF16)
    return hi, mid, lo


def _sigmoid(x):
    return 1.0 / (1.0 + jnp.exp(-x))


def _silu(x):
    return x * _sigmoid(x)


def _rms(x, w):
    ms = jnp.mean(x * x, axis=-1, keepdims=True)
    return x * lax.rsqrt(ms + NORM_EPS) * w


def _ada_kernel(c_ref, w_ref, b_ref, o_ref):
    cs = _silu(c_ref[...])
    h1, h2, h3 = _split3(cs)
    w1, w2, w3 = _split3(w_ref[...])
    acc = _dot(h1, w1) + (_dot(h1, w2) + _dot(h2, w1)) + (_dot(h1, w3) + _dot(h2, w2) + _dot(h3, w1))
    o_ref[...] = acc + b_ref[...]


def _ada_mod(c, w_ada, b_ada):
    d = c.shape[-1]
    n = w_ada.shape[-1]
    bn = 1024
    c8 = jnp.broadcast_to(c.reshape(1, d), (8, d))
    out = pl.pallas_call(
        _ada_kernel,
        out_shape=jax.ShapeDtypeStruct((8, n), F32),
        grid=(n // bn,),
        in_specs=[pl.BlockSpec((8, d), lambda j: (0, 0)),
                  pl.BlockSpec((d, bn), lambda j: (0, j)),
                  pl.BlockSpec((1, bn), lambda j: (0, j))],
        out_specs=pl.BlockSpec((8, bn), lambda j: (0, j)),
        compiler_params=pltpu.CompilerParams(dimension_semantics=("arbitrary",),
                                             vmem_limit_bytes=VMEM_LIMIT_BYTES),
        name="ada_mod",
    )(c8, w_ada, b_ada.reshape(1, n))
    return out[0:1]


def _level_reference(b, rolled, s, row):
    ts = b.shape[0]
    c = s // 2 - 1
    if s >= 16:
        pieces = []
        for blk in range(ts // s):
            r = blk * s + c
            pieces.append(jnp.broadcast_to(b[r:r + 1, :], (s, b.shape[1])))
        return pieces[0] if len(pieces) == 1 else jnp.concatenate(pieces, axis=0)
    pos = row & (s - 1)
    out = b
    for p in range(s):
        if p == c:
            continue
        out = jnp.where(pos == p, rolled[p - c], out)
    return out


def _mixer_kernel(x_ref, mod_ref, npre_ref, npost_ref, win_ref, convw_ref, wco_ref,
                  hlb_ref, hnw_ref, who_ref, wo_ref, o_ref, state_ref, ubuf_ref):
    ts, d = x_ref.shape
    step = pl.program_id(0)

    @pl.when(step == 0)
    def _():
        state_ref[...] = jnp.zeros_like(state_ref)
        ubuf_ref[0:8, :] = jnp.zeros((8, CONV_DIM), F32)

    x = x_ref[...]
    sh1 = mod_ref[:, 0:d]
    sc1 = mod_ref[:, d:2 * d]
    g1 = mod_ref[:, 2 * d:3 * d]
    h = (_rms(x, npre_ref[...]) * (1.0 + sc1) + sh1).astype(BF16)

    def proj(lo, width):
        return _dot(h, win_ref[:, lo:lo + width])

    c0 = 0
    cb = proj(c0, CONV_DIM)
    u = proj(c0 + CONV_DIM, CONV_DIM) * proj(c0 + 2 * CONV_DIM, CONV_DIM)
    ubuf_ref[8:8 + ts, :] = u
    conv = (ubuf_ref[6:6 + ts, :] * convw_ref[0:1, :]
            + ubuf_ref[7:7 + ts, :] * convw_ref[1:2, :]
            + u * convw_ref[2:3, :])
    ubuf_ref[0:8, :] = ubuf_ref[ts:ts + 8, :]
    y_a = _dot((cb * conv).astype(BF16), wco_ref[...])

    c1 = 3 * CONV_DIM
    q = _silu(proj(c1, HGRN_QK))
    hl = hlb_ref[...]
    hm = jnp.max(hl, axis=0, keepdims=True)
    he = jnp.exp(hl - hm)
    lb = he[0:1, :] / jnp.sum(he, axis=0, keepdims=True)
    fg = lb + (1.0 - lb) * _sigmoid(proj(c1 + HGRN_QK, HGRN_QK))
    k = 1.0 - fg
    g = jnp.log(fg)
    v = proj(c1 + 2 * HGRN_QK, HGRN_QK)
    gg = proj(c1 + 3 * HGRN_QK, HGRN_QK)

    row = lax.broadcasted_iota(jnp.int32, (ts, ts), 0)
    col = lax.broadcasted_iota(jnp.int32, (ts, ts), 1)
    tril = jnp.where(row >= col, 1.0, 0.0).astype(BF16)
    ghi, gmid, glo = _split3(g)
    b = _dot(tril, ghi) + _dot(tril, gmid) + _dot(tril, glo)
    b_last = b[ts - 1:ts, :]

    rowq = lax.broadcasted_iota(jnp.int32, (ts, HGRN_QK), 0)
    rolled = {sft: pltpu.roll(b, sft % ts, axis=0) for sft in (-3, -2, -1, 1, 2, 3, 4)}
    xor = row ^ col

    levels = []
    s = ts
    while s >= 2:
        levels.append(s)
        s //= 2
    qt, kt = [], []
    for s in levels:
        bref = _level_reference(b, rolled, s, rowq)
        e = jnp.exp(-jnp.abs(b - bref))
        upper = (rowq & (s // 2)) != 0
        qt.append(jnp.where(upper, q * e, 0.0).astype(BF16))
        kt.append(jnp.where(upper, 0.0, k * e).astype(BF16))
    q_in = (q * jnp.exp(b)).astype(BF16)
    k_out = (k * jnp.exp(b_last - b)).astype(BF16)
    v16 = v.astype(BF16)
    qk = q * k
    decay_last = jnp.exp(b_last)

    hnw = hnw_ref[...]
    outs = []
    for hd in range(HGRN_HEADS):
        sl = slice(hd * HGRN_DK, (hd + 1) * HGRN_DK)
        a = jnp.zeros((ts, ts), F32)
        for li in range(len(levels) - 1, -1, -1):
            s = levels[li]
            p = _dot_nt(qt[li][:, sl], kt[li][:, sl])
            a = jnp.where(xor >= s // 2, p, a)
        st = state_ref[hd]
        o_h = (_dot(a.astype(BF16), v16[:, sl])
               + jnp.sum(qk[:, sl], axis=-1, keepdims=True) * v[:, sl]
               + _dot_nt(q_in[:, sl], st.astype(BF16)))
        state_ref[hd] = st * decay_last[:, sl] + _dot_tn(v16[:, sl], k_out[:, sl])
        outs.append(_rms(o_h, hnw))
    o = jnp.concatenate(outs, axis=-1) * _silu(gg)
    y_b = _dot(o.astype(BF16), who_ref[...])

    c2 = c1 + 4 * HGRN_QK
    m = _sigmoid(proj(c2, d)) * y_a + _sigmoid(proj(c2 + d, d)) * y_b
    y = _dot(m.astype(BF16), wo_ref[...])
    o_ref[...] = x + g1 * _rms(y, npost_ref[...])


def _const_spec(shape):
    nd = len(shape)
    return pl.BlockSpec(shape, lambda i: (0,) * nd)


def _mixer(x2, mod, npre, npost, w_in, conv_w, w_conv_out, hlb, hnw, w_hgrn_out, w_o):
    t, d = x2.shape
    ts = SEQ_TILE
    args = (x2, mod, npre.reshape(1, d), npost.reshape(1, d), w_in.astype(BF16), conv_w,
            w_conv_out.astype(BF16), hlb, hnw.reshape(1, -1), w_hgrn_out.astype(BF16),
            w_o.astype(BF16))
    in_specs = [pl.BlockSpec((ts, d), lambda i: (i, 0))] + [_const_spec(a.shape) for a in args[1:]]
    return pl.pallas_call(
        _mixer_kernel,
        out_shape=jax.ShapeDtypeStruct((t, d), F32),
        grid=(t // ts,),
        in_specs=in_specs,
        out_specs=pl.BlockSpec((ts, d), lambda i: (i, 0)),
        scratch_shapes=[pltpu.VMEM((HGRN_HEADS, HGRN_DK, HGRN_DK), F32),
                        pltpu.VMEM((ts + 8, CONV_DIM), F32)],
        compiler_params=pltpu.CompilerParams(dimension_semantics=("arbitrary",),
                                             vmem_limit_bytes=VMEM_LIMIT_BYTES),
        name="token_mixer",
    )(*args)


def _pack_bf16_pairs(x):
    m = x.shape[1] // 2
    hi = lax.bitcast_convert_type(x[:, :m].astype(BF16).astype(F32), jnp.uint32)
    lo = lax.bitcast_convert_type(x[:, m:].astype(BF16).astype(F32), jnp.uint32)
    return hi | (lo >> 16)


def _unpack_bf16_pairs(p):
    hi = lax.bitcast_convert_type(p & jnp.uint32(0xFFFF0000), F32)
    lo = lax.bitcast_convert_type(p << 16, F32)
    return hi, lo


def _router_kernel(x1_ref, mod_ref, npre_ref, wrh_ref, wrl_ref, rb_ref,
                   hp_ref, eidx_ref, rank_ref, wcol_ref, cnt_ref, carry_ref, rows_ref):
    assert N_GROUPS == TOP_K == rows_ref.shape[0]
    tr, d = x1_ref.shape
    n_e = wrh_ref.shape[0]
    gsz = n_e // N_GROUPS
    neg = -jnp.inf

    @pl.when(pl.program_id(0) == 0)
    def _():
        carry_ref[...] = jnp.zeros_like(carry_ref)

    sh2 = mod_ref[:, 3 * d:4 * d]
    sc2 = mod_ref[:, 4 * d:5 * d]
    h2 = _rms(x1_ref[...], npre_ref[...]) * (1.0 + sc2) + sh2
    hp_ref[...] = _pack_bf16_pairs(h2)
    h_hi = h2.astype(BF16)
    h_lo = (h2 - h_hi.astype(F32)).astype(BF16)
    wrh = wrh_ref[...]
    logits = _dot_nt(wrh, h_hi) + (_dot_nt(wrh, h_lo) + _dot_nt(wrl_ref[...], h_hi))
    scores = _sigmoid(logits)
    sel = scores + rb_ref[:, 0:1]

    io_e = lax.broadcasted_iota(jnp.int32, (n_e, tr), 0)
    for g in range(N_GROUPS):
        blk = sel[g * gsz:(g + 1) * gsz, :]
        io = lax.broadcasted_iota(jnp.int32, (gsz, tr), 0) + g * gsz
        m1 = jnp.max(blk, axis=0, keepdims=True)
        i1 = jnp.min(jnp.where(blk == m1, io, n_e), axis=0, keepdims=True)
        m2 = jnp.max(jnp.where(io == i1, neg, blk), axis=0, keepdims=True)
        rows_ref[g:g + 1, :] = m1 + m2
    gs = rows_ref[...]
    io_g = lax.broadcasted_iota(jnp.int32, (N_GROUPS, tr), 0)
    gsel = jnp.zeros((N_GROUPS, tr), F32)
    for _ in range(TOPK_GROUPS):
        m = jnp.max(gs, axis=0, keepdims=True)
        gi = jnp.min(jnp.where(gs == m, io_g, N_GROUPS), axis=0, keepdims=True)
        hit = io_g == gi
        gsel = jnp.where(hit, 1.0, gsel)
        gs = jnp.where(hit, neg, gs)
    rows_ref[...] = gsel
    cur = jnp.concatenate(
        [jnp.where(rows_ref[g:g + 1, :] > 0.5, sel[g * gsz:(g + 1) * gsz, :], neg) for g in range(N_GROUPS)],
        axis=0)

    idxs = []
    selm = jnp.zeros((n_e, tr), F32)
    for k in range(TOP_K):
        m = jnp.max(cur, axis=0, keepdims=True)
        idx = jnp.min(jnp.where(cur == m, io_e, n_e), axis=0, keepdims=True)
        hit = io_e == idx
        rows_ref[k:k + 1, :] = jnp.sum(jnp.where(hit, scores, 0.0), axis=0, keepdims=True)
        cur = jnp.where(hit, neg, cur)
        selm = jnp.where(hit, 1.0, selm)
        eidx_ref[k:k + 1, :] = idx
        idxs.append(idx)

    r_io = lax.broadcasted_iota(jnp.int32, (tr, tr + 128), 0)
    c_io = lax.broadcasted_iota(jnp.int32, (tr, tr + 128), 1)
    before = jnp.where((r_io < c_io) | (c_io >= tr), 1.0, 0.0).astype(BF16)
    r_ext = _dot(selm.astype(BF16), before)
    rank_full = r_ext[:, :tr] + carry_ref[:, 0:1]
    for k, idx in enumerate(idxs):
        rk = jnp.sum(jnp.where(io_e == idx, rank_full, 0.0), axis=0, keepdims=True)
        rank_ref[k:k + 1, :] = rk.astype(jnp.int32)
    carry_ref[...] = carry_ref[...] + r_ext[:, tr:]
    cnt_ref[...] = carry_ref[...]

    wk = rows_ref[...]
    wn = wk / (jnp.sum(wk, axis=0, keepdims=True) + 1e-20) * ROUTED_SCALE
    eye = jnp.where(lax.broadcasted_iota(jnp.int32, (TOP_K, 128), 0)
                    == lax.broadcasted_iota(jnp.int32, (TOP_K, 128), 1), 1.0, 0.0).astype(BF16)
    w1, w2, w3 = _split3(wn)
    wcol_ref[...] = _dot_tn(w1, eye) + _dot_tn(w2, eye) + _dot_tn(w3, eye)


def _router(x1, mod, npre2, w_router, router_bias):
    t, d = x1.shape
    n_e = w_router.shape[1]
    tr = SEQ_TILE
    wrt = w_router.T
    wrh = wrt.astype(BF16)
    wrl = (wrt - wrh.astype(F32)).astype(BF16)
    rb = jnp.broadcast_to(router_bias.reshape(n_e, 1), (n_e, 128))
    args = (x1, mod, npre2.reshape(1, d), wrh, wrl, rb)
    in_specs = [pl.BlockSpec((tr, d), lambda i: (i, 0))] + [_const_spec(a.shape) for a in args[1:]]
    return pl.pallas_call(
        _router_kernel,
        out_shape=(jax.ShapeDtypeStruct((t, d // 2), jnp.uint32),
                   jax.ShapeDtypeStruct((TOP_K, t), jnp.int32),
                   jax.ShapeDtypeStruct((TOP_K, t), jnp.int32),
                   jax.ShapeDtypeStruct((t, 128), F32),
                   jax.ShapeDtypeStruct((n_e, 128), F32)),
        grid=(t // tr,),
        in_specs=in_specs,
        out_specs=(pl.BlockSpec((tr, d // 2), lambda i: (i, 0)),
                   pl.BlockSpec((TOP_K, tr), lambda i: (0, i)),
                   pl.BlockSpec((TOP_K, tr), lambda i: (0, i)),
                   pl.BlockSpec((tr, 128), lambda i: (i, 0)),
                   pl.BlockSpec((n_e, 128), lambda i: (0, 0))),
        scratch_shapes=[pltpu.VMEM((n_e, 128), F32), pltpu.VMEM((TOP_K, tr), F32)],
        compiler_params=pltpu.CompilerParams(dimension_semantics=("arbitrary",),
                                             vmem_limit_bytes=VMEM_LIMIT_BYTES),
        name="moe_router",
    )(*args)


def _dest_kernel(eidx_ref, rank_ref, cnt_ref, dest_ref, starts_ref):
    n_e = cnt_ref.shape[0]
    tt = eidx_ref.shape[1]
    cnt = jnp.floor((cnt_ref[...] + (ROW_ALIGN - 1)) * (1.0 / ROW_ALIGN)) * ROW_ALIGN
    c_hi = jnp.floor(cnt * (1.0 / 128.0))
    c_lo = cnt - c_hi * 128.0
    r = lax.broadcasted_iota(jnp.int32, (n_e, n_e), 0)
    c = lax.broadcasted_iota(jnp.int32, (n_e, n_e), 1)
    lower = jnp.where(c < r, 1.0, 0.0).astype(BF16)
    starts = 128.0 * _dot(lower, c_hi.astype(BF16)) + _dot(lower, c_lo.astype(BF16))
    starts_ref[...] = starts.astype(jnp.int32)
    start_col = starts[:, 0:1]
    io_e = lax.broadcasted_iota(jnp.int32, (n_e, tt), 0)
    rows = []
    for k in range(TOP_K):
        hit = io_e == eidx_ref[k:k + 1, :]
        rows.append(jnp.sum(jnp.where(hit, start_col, 0.0), axis=0, keepdims=True))
    dest_ref[...] = jnp.concatenate(rows, axis=0).astype(jnp.int32) + rank_ref[...]


def _dest(eidx, rank, cnt):
    k, t = eidx.shape
    tt = 512
    return pl.pallas_call(
        _dest_kernel,
        out_shape=(jax.ShapeDtypeStruct((k, t), jnp.int32),
                   jax.ShapeDtypeStruct(cnt.shape, jnp.int32)),
        grid=(t // tt,),
        in_specs=[pl.BlockSpec((k, tt), lambda i: (0, i)),
                  pl.BlockSpec((k, tt), lambda i: (0, i)),
                  _const_spec(cnt.shape)],
        out_specs=(pl.BlockSpec((k, tt), lambda i: (0, i)), _const_spec(cnt.shape)),
        compiler_params=pltpu.CompilerParams(dimension_semantics=("arbitrary",)),
        name="moe_dest",
    )(eidx, rank, cnt)


def _index_tile_copy(idx_hbm, idx_smem, sem, step, tile):
    slot = step % 2
    return pltpu.make_async_copy(idx_hbm.at[:, pl.ds(step * tile, tile)], idx_smem.at[slot], sem.at[slot])


def _dispatch_kernel(starts_ref, counts_ref, dest_hbm, hp_ref, xs_hbm,
                     idx_smem, zero_ref, idx_sem, row_sem, zero_sem):
    td = hp_ref.shape[0]
    n_e = starts_ref.shape[0]
    step = pl.program_id(0)
    nsteps = pl.num_programs(0)

    @pl.when(step == 0)
    def _():
        _index_tile_copy(dest_hbm, idx_smem, idx_sem, step, td).start()
        zero_ref[...] = jnp.zeros_like(zero_ref)

        def pad_copy(e):
            end = starts_ref[e] + counts_ref[e]
            tile = pl.multiple_of((end // ROW_ALIGN) * ROW_ALIGN, ROW_ALIGN)
            return pltpu.make_async_copy(zero_ref.at[pl.ds(0, ROW_ALIGN), :],
                                         xs_hbm.at[pl.ds(tile, ROW_ALIGN), :], zero_sem)

        def start_pad(e, carry):
            pad_copy(e).start()
            return carry

        def wait_pad(e, carry):
            pad_copy(e).wait()
            return carry

        lax.fori_loop(0, n_e - 1, start_pad, 0)
        lax.fori_loop(0, n_e - 1, wait_pad, 0)
        last_end = starts_ref[n_e - 1] + counts_ref[n_e - 1]
        tail_start = (last_end // ROW_ALIGN) * ROW_ALIGN
        piece = zero_ref.shape[0]
        n_rows = xs_hbm.shape[0]
        for j in range(pl.cdiv(n_e * ROW_ALIGN + piece, piece)):
            at = pl.multiple_of(jnp.minimum(tail_start + j * piece, n_rows - piece), ROW_ALIGN)
            tail = pltpu.make_async_copy(zero_ref, xs_hbm.at[pl.ds(at, piece), :], zero_sem)
            tail.start()
            tail.wait()

    @pl.when(step + 1 < nsteps)
    def _():
        _index_tile_copy(dest_hbm, idx_smem, idx_sem, step + 1, td).start()

    _index_tile_copy(dest_hbm, idx_smem, idx_sem, step, td).wait()
    slot = step % 2

    def row_copy(t, dst):
        return pltpu.make_async_copy(hp_ref.at[pl.ds(t, 1), :], xs_hbm.at[pl.ds(dst, 1), :], row_sem)

    def body(t, carry):
        for k in range(TOP_K):
            row_copy(t, idx_smem[slot, k, t]).start(priority=k % 2)
        return carry

    lax.fori_loop(0, td, body, 0)
    for k in range(TOP_K):
        pltpu.make_async_copy(hp_ref, xs_hbm.at[pl.ds(0, td), :], row_sem).wait()


def _dispatch(starts, counts, dest, hp):
    t, m = hp.shape
    n_e = starts.shape[0]
    td = SEQ_TILE
    n_rows = t * TOP_K + n_e * ROW_ALIGN + EXPERT_CHUNK + ROW_ALIGN
    grid_spec = pltpu.PrefetchScalarGridSpec(
        num_scalar_prefetch=2,
        grid=(t // td,),
        in_specs=[pl.BlockSpec(memory_space=pl.ANY),
                  pl.BlockSpec((td, m), lambda i, *_: (i, 0))],
        out_specs=pl.BlockSpec(memory_space=pl.ANY),
        scratch_shapes=[pltpu.SMEM((2, TOP_K, td), jnp.int32),
                        pltpu.VMEM((EXPERT_CHUNK + ROW_ALIGN, m), jnp.uint32),
                        pltpu.SemaphoreType.DMA((2,)),
                        pltpu.SemaphoreType.DMA(()),
                        pltpu.SemaphoreType.DMA(())],
    )
    return pl.pallas_call(
        _dispatch_kernel,
        out_shape=jax.ShapeDtypeStruct((n_rows, m), jnp.uint32),
        grid_spec=grid_spec,
        compiler_params=pltpu.CompilerParams(dimension_semantics=("arbitrary",)),
        name="moe_dispatch",
    )(starts, counts, dest, hp)


def _expert_kernel(starts_ref, counts_ref, xs_hbm, wg_ref, wu_ref, wd_ref, y_hbm,
                   wgu16_ref, wd16_ref, xbuf_ref, ybuf_ref, in_sem, out_sem):
    e = pl.program_id(0)
    ch, half = xbuf_ref.shape[1:]
    ff = wg_ref.shape[2]
    start = pl.multiple_of(starts_ref[e], ROW_ALIGN)
    n_chunks = (counts_ref[e] + (ch - 1)) // ch

    def in_copy(c, slot):
        return pltpu.make_async_copy(xs_hbm.at[pl.ds(start + c * ch, ch), :], xbuf_ref.at[slot], in_sem.at[slot])

    def out_copy(c, slot):
        return pltpu.make_async_copy(ybuf_ref.at[slot], y_hbm.at[pl.ds(start + c * ch, ch), :], out_sem.at[slot])

    @pl.when(n_chunks > 0)
    def _():
        in_copy(0, 0).start()
        wgu16_ref[:, 0:ff] = wg_ref[0].astype(BF16)
        wgu16_ref[:, ff:2 * ff] = wu_ref[0].astype(BF16)
        wd16_ref[...] = wd_ref[0].astype(BF16)

    def chunk(c, carry):
        slot = c % 2

        @pl.when(c + 1 < n_chunks)
        def _():
            in_copy(c + 1, 1 - slot).start()

        in_copy(c, slot).wait()

        @pl.when(c >= 2)
        def _():
            out_copy(c - 2, slot).wait()

        xa, xb = _unpack_bf16_pairs(xbuf_ref[slot])
        hgu = (_dot(xa.astype(BF16), wgu16_ref[0:half, :])
               + _dot(xb.astype(BF16), wgu16_ref[half:2 * half, :]))
        act = _silu(hgu[:, :ff]) * hgu[:, ff:]
        y = _dot(act.astype(BF16), wd16_ref[...])
        ybuf_ref[slot] = _pack_bf16_pairs(y)
        out_copy(c, slot).start()
        return carry

    lax.fori_loop(0, n_chunks, chunk, 0)

    @pl.when(n_chunks >= 2)
    def _():
        out_copy(n_chunks - 2, n_chunks % 2).wait()

    @pl.when(n_chunks >= 1)
    def _():
        out_copy(n_chunks - 1, (n_chunks - 1) % 2).wait()

    @pl.when(e == pl.num_programs(0) - 1)
    def _():
        n_rows = y_hbm.shape[0]
        max_tail = starts_ref.shape[0] * ROW_ALIGN + EXPERT_CHUNK + ROW_ALIGN
        used = start + ((counts_ref[e] + (ROW_ALIGN - 1)) // ROW_ALIGN) * ROW_ALIGN
        ybuf_ref[0] = jnp.zeros((ch, half), jnp.uint32)
        for j in range(pl.cdiv(max_tail, ch)):
            at = pl.multiple_of(jnp.minimum(used + j * ch, n_rows - ch), ROW_ALIGN)
            tail = pltpu.make_async_copy(ybuf_ref.at[0], y_hbm.at[pl.ds(at, ch), :], out_sem.at[0])
            tail.start()
            tail.wait()


def _experts(starts, counts, xs, w_gate_e, w_up_e, w_down_e):
    p, half = xs.shape
    n_e, d, ff = w_gate_e.shape
    ch = EXPERT_CHUNK
    grid_spec = pltpu.PrefetchScalarGridSpec(
        num_scalar_prefetch=2,
        grid=(n_e,),
        in_specs=[pl.BlockSpec(memory_space=pl.ANY),
                  pl.BlockSpec((1, d, ff), lambda e, *_: (e, 0, 0)),
                  pl.BlockSpec((1, d, ff), lambda e, *_: (e, 0, 0)),
                  pl.BlockSpec((1, ff, d), lambda e, *_: (e, 0, 0))],
        out_specs=pl.BlockSpec(memory_space=pl.ANY),
        scratch_shapes=[pltpu.VMEM((d, 2 * ff), BF16), pltpu.VMEM((ff, d), BF16),
                        pltpu.VMEM((2, ch, half), jnp.uint32), pltpu.VMEM((2, ch, half), jnp.uint32),
                        pltpu.SemaphoreType.DMA((2,)), pltpu.SemaphoreType.DMA((2,))],
    )
    return pl.pallas_call(
        _expert_kernel,
        out_shape=jax.ShapeDtypeStruct((p, half), jnp.uint32),
        grid_spec=grid_spec,
        compiler_params=pltpu.CompilerParams(dimension_semantics=("arbitrary",),
                                             vmem_limit_bytes=VMEM_LIMIT_BYTES),
        name="moe_experts",
    )(starts, counts, xs, w_gate_e, w_up_e, w_down_e)


def _combine_kernel(dest_hbm, y_hbm, wcol_ref, x1_ref, hp_ref, mod_ref, npost_ref,
                    wgus_ref, wds_ref, o_ref, idx_smem, ybuf_ref, idx_sem, row_sem):
    te, d = x1_ref.shape
    half = d // 2
    ff = wds_ref.shape[0]
    step = pl.program_id(0)
    nsteps = pl.num_programs(0)

    @pl.when(step == 0)
    def _():
        _index_tile_copy(dest_hbm, idx_smem, idx_sem, step, te).start()

    @pl.when(step + 1 < nsteps)
    def _():
        _index_tile_copy(dest_hbm, idx_smem, idx_sem, step + 1, te).start()

    _index_tile_copy(dest_hbm, idx_smem, idx_sem, step, te).wait()
    slot = step % 2

    def body(t, carry):
        for k in range(TOP_K):
            src = idx_smem[slot, k, t]
            pltpu.make_async_copy(y_hbm.at[pl.ds(src, 1), :], ybuf_ref.at[k, pl.ds(t, 1), :],
                                  row_sem).start(priority=k % 2)
        return carry

    lax.fori_loop(0, te, body, 0)

    xa, xb = _unpack_bf16_pairs(hp_ref[...])
    hgu = _dot(xa.astype(BF16), wgus_ref[0:half, :]) + _dot(xb.astype(BF16), wgus_ref[half:d, :])
    act = _silu(hgu[:, :ff]) * hgu[:, ff:]
    shared = _dot(act.astype(BF16), wds_ref[...])

    for k in range(TOP_K):
        pltpu.make_async_copy(y_hbm.at[pl.ds(0, te), :], ybuf_ref.at[k], row_sem).wait()

    acc_a = shared[:, :half]
    acc_b = shared[:, half:]
    for k in range(TOP_K):
        ya, yb = _unpack_bf16_pairs(ybuf_ref[k])
        wk = wcol_ref[:, k:k + 1]
        acc_a = acc_a + wk * ya
        acc_b = acc_b + wk * yb
    moe = jnp.concatenate([acc_a, acc_b], axis=-1)
    g2 = mod_ref[:, 5 * d:6 * d]
    o_ref[...] = x1_ref[...] + g2 * _rms(moe, npost_ref[...])


def _combine(dest, y, wcol, x1, hp, mod, npost2, w_gate_s, w_up_s, w_down_s):
    t, d = x1.shape
    half = d // 2
    te = SEQ_TILE
    wgus = jnp.concatenate([w_gate_s, w_up_s], axis=1).astype(BF16)
    wds = w_down_s.astype(BF16)
    return pl.pallas_call(
        _combine_kernel,
        out_shape=jax.ShapeDtypeStruct((t, d), F32),
        grid=(t // te,),
        in_specs=[pl.BlockSpec(memory_space=pl.ANY),
                  pl.BlockSpec(memory_space=pl.ANY),
                  pl.BlockSpec((te, 128), lambda i: (i, 0)),
                  pl.BlockSpec((te, d), lambda i: (i, 0)),
                  pl.BlockSpec((te, half), lambda i: (i, 0)),
                  _const_spec(mod.shape),
                  _const_spec((1, d)),
                  _const_spec(wgus.shape),
                  _const_spec(wds.shape)],
        out_specs=pl.BlockSpec((te, d), lambda i: (i, 0)),
        scratch_shapes=[pltpu.SMEM((2, TOP_K, te), jnp.int32),
                        pltpu.VMEM((TOP_K, te, half), jnp.uint32),
                        pltpu.SemaphoreType.DMA((2,)),
                        pltpu.SemaphoreType.DMA(())],
        compiler_params=pltpu.CompilerParams(dimension_semantics=("arbitrary",),
                                             vmem_limit_bytes=VMEM_LIMIT_BYTES),
        name="moe_combine",
    )(dest, y, wcol, x1, hp, mod, npost2.reshape(1, d), wgus, wds)


def kernel(x, c, w_ada, b_ada, norm_pre_mix, norm_post_mix, w_in, conv_w, w_conv_out, hgrn_lower_bounds, hgrn_norm_w, w_hgrn_out, w_o, norm_pre_ffn, norm_post_ffn, w_router, router_bias, w_gate_e, w_up_e, w_down_e, w_gate_s, w_up_s, w_down_s):
    bsz, seq, d = x.shape
    assert bsz == 1 and w_ada.shape[0] == 1
    mod = _ada_mod(c, w_ada[0], b_ada[0])
    x1 = _mixer(x.reshape(seq, d), mod, norm_pre_mix[0], norm_post_mix[0], w_in[0], conv_w[0],
                w_conv_out[0], hgrn_lower_bounds, hgrn_norm_w[0], w_hgrn_out[0], w_o[0])
    hp, eidx, rank, wcol, cnt = _router(x1, mod, norm_pre_ffn[0], w_router[0], router_bias[0])
    dest, starts = _dest(eidx, rank, cnt)
    starts = starts[:, 0]
    counts = cnt[:, 0].astype(jnp.int32)
    xs = _dispatch(starts, counts, dest, hp)
    y = _experts(starts, counts, xs, w_gate_e[0], w_up_e[0], w_down_e[0])
    out = _combine(dest, y, wcol, x1, hp, mod, norm_post_ffn[0], w_gate_s[0], w_up_s[0], w_down_s[0])
    return out.reshape(bsz, seq, d)
```

```python
import functools

import jax
import jax.numpy as jnp
from jax import lax
from jax.experimental import pallas as pl
from jax.experimental.pallas import tpu as pltpu
from jax.experimental.pallas import tpu_sc as plsc

F32 = jnp.float32
BF16 = jnp.bfloat16

NORM_EPS = 1e-6
CONV_DIM = 512
CONV_WIDTH = 3
HGRN_HEADS = 4
HGRN_DK = 128
HGRN_QK = HGRN_HEADS * HGRN_DK
N_GROUPS = 8
TOPK_GROUPS = 4
TOP_K = 8
ROUTED_SCALE = 2.5

SEQ_TILE = 256
EXPERT_CHUNK = 512
ROW_ALIGN = 8
VMEM_LIMIT_BYTES = 56 * 1024 * 1024


def _dot(a, b):
    return jnp.dot(a, b, preferred_element_type=F32)


def _dot_nt(a, b):
    return lax.dot_general(a, b, (((1,), (1,)), ((), ())), preferred_element_type=F32)


def _dot_tn(a, b):
    return lax.dot_general(a, b, (((0,), (0,)), ((), ())), preferred_element_type=F32)


def _split3(x):
    hi = x.astype(BF16)
    r1 = x - hi.astype(F32)
    mid = r1.astype(BF16)
    lo = (r1 - mid.astype(F32)).astype(BF16)
    return hi, mid, lo


def _sigmoid(x):
    return 1.0 / (1.0 + jnp.exp(-x))


def _silu(x):
    return x * _sigmoid(x)


def _rms(x, w):
    ms = jnp.mean(x * x, axis=-1, keepdims=True)
    return x * lax.rsqrt(ms + NORM_EPS) * w


def _ada_kernel(c_ref, w_ref, b_ref, o_ref):
    cs = _silu(c_ref[...])
    h1, h2, h3 = _split3(cs)
    w1, w2, w3 = _split3(w_ref[...])
    acc = _dot(h1, w1) + (_dot(h1, w2) + _dot(h2, w1)) + (_dot(h1, w3) + _dot(h2, w2) + _dot(h3, w1))
    o_ref[...] = acc + b_ref[...]


def _ada_mod(c, w_ada, b_ada):
    d = c.shape[-1]
    n = w_ada.shape[-1]
    bn = 1024
    c8 = jnp.broadcast_to(c.reshape(1, d), (8, d))
    out = pl.pallas_call(
        _ada_kernel,
        out_shape=jax.ShapeDtypeStruct((8, n), F32),
        grid=(n // bn,),
        in_specs=[pl.BlockSpec((8, d), lambda j: (0, 0)),
                  pl.BlockSpec((d, bn), lambda j: (0, j)),
                  pl.BlockSpec((1, bn), lambda j: (0, j))],
        out_specs=pl.BlockSpec((8, bn), lambda j: (0, j)),
        compiler_params=pltpu.CompilerParams(dimension_semantics=("arbitrary",),
                                             vmem_limit_bytes=VMEM_LIMIT_BYTES),
        name="ada_mod",
    )(c8, w_ada, b_ada.reshape(1, n))
    return out[0:1]


def _level_reference(b, rolled, s, row):
    ts = b.shape[0]
    c = s // 2 - 1
    if s >= 16:
        pieces = []
        for blk in range(ts // s):
            r = blk * s + c
            pieces.append(jnp.broadcast_to(b[r:r + 1, :], (s, b.shape[1])))
        return pieces[0] if len(pieces) == 1 else jnp.concatenate(pieces, axis=0)
    pos = row & (s - 1)
    out = b
    for p in range(s):
        if p == c:
            continue
        out = jnp.where(pos == p, rolled[p - c], out)
    return out


def _mixer_kernel(x_ref, mod_ref, npre_ref, npost_ref, win_ref, convw_ref, wco_ref,
                  hlb_ref, hnw_ref, who_ref, wo_ref, o_ref, state_ref, ubuf_ref):
    ts, d = x_ref.shape
    step = pl.program_id(0)

    @pl.when(step == 0)
    def _():
        state_ref[...] = jnp.zeros_like(state_ref)
        ubuf_ref[0:8, :] = jnp.zeros((8, CONV_DIM), F32)

    x = x_ref[...]
    sh1 = mod_ref[:, 0:d]
    sc1 = mod_ref[:, d:2 * d]
    g1 = mod_ref[:, 2 * d:3 * d]
    h = (_rms(x, npre_ref[...]) * (1.0 + sc1) + sh1).astype(BF16)

    def proj(lo, width):
        return _dot(h, win_ref[:, lo:lo + width])

    c0 = 0
    cb = proj(c0, CONV_DIM)
    u = proj(c0 + CONV_DIM, CONV_DIM) * proj(c0 + 2 * CONV_DIM, CONV_DIM)
    ubuf_ref[8:8 + ts, :] = u
    conv = (ubuf_ref[6:6 + ts, :] * convw_ref[0:1, :]
            + ubuf_ref[7:7 + ts, :] * convw_ref[1:2, :]
            + u * convw_ref[2:3, :])
    ubuf_ref[0:8, :] = ubuf_ref[ts:ts + 8, :]
    y_a = _dot((cb * conv).astype(BF16), wco_ref[...])

    c1 = 3 * CONV_DIM
    q = _silu(proj(c1, HGRN_QK))
    hl = hlb_ref[...]
    hm = jnp.max(hl, axis=0, keepdims=True)
    he = jnp.exp(hl - hm)
    lb = he[0:1, :] / jnp.sum(he, axis=0, keepdims=True)
    fg = lb + (1.0 - lb) * _sigmoid(proj(c1 + HGRN_QK, HGRN_QK))
    k = 1.0 - fg
    g = jnp.log(fg)
    v = proj(c1 + 2 * HGRN_QK, HGRN_QK)
    gg = proj(c1 + 3 * HGRN_QK, HGRN_QK)

    row = lax.broadcasted_iota(jnp.int32, (ts, ts), 0)
    col = lax.broadcasted_iota(jnp.int32, (ts, ts), 1)
    tril = jnp.where(row >= col, 1.0, 0.0).astype(BF16)
    ghi, gmid, glo = _split3(g)
    b = _dot(tril, ghi) + _dot(tril, gmid) + _dot(tril, glo)
    b_last = b[ts - 1:ts, :]

    rowq = lax.broadcasted_iota(jnp.int32, (ts, HGRN_QK), 0)
    rolled = {sft: pltpu.roll(b, sft % ts, axis=0) for sft in (-3, -2, -1, 1, 2, 3, 4)}
    xor = row ^ col

    levels = []
    s = ts
    while s >= 2:
        levels.append(s)
        s //= 2
    qt, kt = [], []
    for s in levels:
        bref = _level_reference(b, rolled, s, rowq)
        e = jnp.exp(-jnp.abs(b - bref))
        upper = (rowq & (s // 2)) != 0
        qt.append(jnp.where(upper, q * e, 0.0).astype(BF16))
        kt.append(jnp.where(upper, 0.0, k * e).astype(BF16))
    q_in = (q * jnp.exp(b)).astype(BF16)
    k_out = (k * jnp.exp(b_last - b)).astype(BF16)
    v16 = v.astype(BF16)
    qk = q * k
    decay_last = jnp.exp(b_last)

    hnw = hnw_ref[...]
    outs = []
    for hd in range(HGRN_HEADS):
        sl = slice(hd * HGRN_DK, (hd + 1) * HGRN_DK)
        a = jnp.zeros((ts, ts), F32)
        for li in range(len(levels) - 1, -1, -1):
            s = levels[li]
            p = _dot_nt(qt[li][:, sl], kt[li][:, sl])
            a = jnp.where(xor >= s // 2, p, a)
        st = state_ref[hd]
        o_h = (_dot(a.astype(BF16), v16[:, sl])
               + jnp.sum(qk[:, sl], axis=-1, keepdims=True) * v[:, sl]
               + _dot_nt(q_in[:, sl], st.astype(BF16)))
        state_ref[hd] = st * decay_last[:, sl] + _dot_tn(v16[:, sl], k_out[:, sl])
        outs.append(_rms(o_h, hnw))
    o = jnp.concatenate(outs, axis=-1) * _silu(gg)
    y_b = _dot(o.astype(BF16), who_ref[...])

    c2 = c1 + 4 * HGRN_QK
    m = _sigmoid(proj(c2, d)) * y_a + _sigmoid(proj(c2 + d, d)) * y_b
    y = _dot(m.astype(BF16), wo_ref[...])
    o_ref[...] = x + g1 * _rms(y, npost_ref[...])


def _const_spec(shape):
    nd = len(shape)
    return pl.BlockSpec(shape, lambda i: (0,) * nd)


def _mixer(x2, mod, npre, npost, w_in, conv_w, w_conv_out, hlb, hnw, w_hgrn_out, w_o):
    t, d = x2.shape
    ts = SEQ_TILE
    args = (x2, mod, npre.reshape(1, d), npost.reshape(1, d), w_in.astype(BF16), conv_w,
            w_conv_out.astype(BF16), hlb, hnw.reshape(1, -1), w_hgrn_out.astype(BF16),
            w_o.astype(BF16))
    in_specs = [pl.BlockSpec((ts, d), lambda i: (i, 0))] + [_const_spec(a.shape) for a in args[1:]]
    return pl.pallas_call(
        _mixer_kernel,
        out_shape=jax.ShapeDtypeStruct((t, d), F32),
        grid=(t // ts,),
        in_specs=in_specs,
        out_specs=pl.BlockSpec((ts, d), lambda i: (i, 0)),
        scratch_shapes=[pltpu.VMEM((HGRN_HEADS, HGRN_DK, HGRN_DK), F32),
                        pltpu.VMEM((ts + 8, CONV_DIM), F32)],
        compiler_params=pltpu.CompilerParams(dimension_semantics=("arbitrary",),
                                             vmem_limit_bytes=VMEM_LIMIT_BYTES),
        name="token_mixer",
    )(*args)


PACKED = jnp.int32


def _pack_bf16_pairs(x):
    m = x.shape[1] // 2
    hi = lax.bitcast_convert_type(x[:, :m].astype(BF16).astype(F32), jnp.uint32)
    lo = lax.bitcast_convert_type(x[:, m:].astype(BF16).astype(F32), jnp.uint32)
    return lax.bitcast_convert_type(hi | (lo >> 16), PACKED)


def _unpack_bf16_pairs(p):
    p = lax.bitcast_convert_type(p, jnp.uint32)
    hi = lax.bitcast_convert_type(p & jnp.uint32(0xFFFF0000), F32)
    lo = lax.bitcast_convert_type(p << 16, F32)
    return hi, lo


def _router_kernel(x1_ref, mod_ref, npre_ref, wrh_ref, wrl_ref, rb_ref,
                   hp_ref, eidx_ref, rank_ref, wcol_ref, cnt_ref, carry_ref, rows_ref):
    assert N_GROUPS == TOP_K == rows_ref.shape[0]
    tr, d = x1_ref.shape
    n_e = wrh_ref.shape[0]
    gsz = n_e // N_GROUPS
    neg = -jnp.inf

    @pl.when(pl.program_id(0) == 0)
    def _():
        carry_ref[...] = jnp.zeros_like(carry_ref)

    sh2 = mod_ref[:, 3 * d:4 * d]
    sc2 = mod_ref[:, 4 * d:5 * d]
    h2 = _rms(x1_ref[...], npre_ref[...]) * (1.0 + sc2) + sh2
    hp_ref[...] = _pack_bf16_pairs(h2)
    h_hi = h2.astype(BF16)
    h_lo = (h2 - h_hi.astype(F32)).astype(BF16)
    wrh = wrh_ref[...]
    logits = _dot_nt(wrh, h_hi) + (_dot_nt(wrh, h_lo) + _dot_nt(wrl_ref[...], h_hi))
    scores = _sigmoid(logits)
    sel = scores + rb_ref[:, 0:1]

    io_e = lax.broadcasted_iota(jnp.int32, (n_e, tr), 0)
    for g in range(N_GROUPS):
        blk = sel[g * gsz:(g + 1) * gsz, :]
        io = lax.broadcasted_iota(jnp.int32, (gsz, tr), 0) + g * gsz
        m1 = jnp.max(blk, axis=0, keepdims=True)
        i1 = jnp.min(jnp.where(blk == m1, io, n_e), axis=0, keepdims=True)
        m2 = jnp.max(jnp.where(io == i1, neg, blk), axis=0, keepdims=True)
        rows_ref[g:g + 1, :] = m1 + m2
    gs = rows_ref[...]
    io_g = lax.broadcasted_iota(jnp.int32, (N_GROUPS, tr), 0)
    gsel = jnp.zeros((N_GROUPS, tr), F32)
    for _ in range(TOPK_GROUPS):
        m = jnp.max(gs, axis=0, keepdims=True)
        gi = jnp.min(jnp.where(gs == m, io_g, N_GROUPS), axis=0, keepdims=True)
        hit = io_g == gi
        gsel = jnp.where(hit, 1.0, gsel)
        gs = jnp.where(hit, neg, gs)
    rows_ref[...] = gsel
    cur = jnp.concatenate(
        [jnp.where(rows_ref[g:g + 1, :] > 0.5, sel[g * gsz:(g + 1) * gsz, :], neg) for g in range(N_GROUPS)],
        axis=0)

    idxs = []
    selm = jnp.zeros((n_e, tr), F32)
    for k in range(TOP_K):
        m = jnp.max(cur, axis=0, keepdims=True)
        idx = jnp.min(jnp.where(cur == m, io_e, n_e), axis=0, keepdims=True)
        hit = io_e == idx
        rows_ref[k:k + 1, :] = jnp.sum(jnp.where(hit, scores, 0.0), axis=0, keepdims=True)
        cur = jnp.where(hit, neg, cur)
        selm = jnp.where(hit, 1.0, selm)
        eidx_ref[k:k + 1, :] = idx
        idxs.append(idx)

    r_io = lax.broadcasted_iota(jnp.int32, (tr, tr + 128), 0)
    c_io = lax.broadcasted_iota(jnp.int32, (tr, tr + 128), 1)
    before = jnp.where((r_io < c_io) | (c_io >= tr), 1.0, 0.0).astype(BF16)
    r_ext = _dot(selm.astype(BF16), before)
    rank_full = r_ext[:, :tr] + carry_ref[:, 0:1]
    for k, idx in enumerate(idxs):
        rk = jnp.sum(jnp.where(io_e == idx, rank_full, 0.0), axis=0, keepdims=True)
        rank_ref[k:k + 1, :] = rk.astype(jnp.int32)
    carry_ref[...] = carry_ref[...] + r_ext[:, tr:]
    cnt_ref[...] = carry_ref[...]

    wk = rows_ref[...]
    wn = wk / (jnp.sum(wk, axis=0, keepdims=True) + 1e-20) * ROUTED_SCALE
    eye = jnp.where(lax.broadcasted_iota(jnp.int32, (TOP_K, 128), 0)
                    == lax.broadcasted_iota(jnp.int32, (TOP_K, 128), 1), 1.0, 0.0).astype(BF16)
    w1, w2, w3 = _split3(wn)
    wcol_ref[...] = _dot_tn(w1, eye) + _dot_tn(w2, eye) + _dot_tn(w3, eye)


def _router(x1, mod, npre2, w_router, router_bias):
    t, d = x1.shape
    n_e = w_router.shape[1]
    tr = SEQ_TILE
    wrt = w_router.T
    wrh = wrt.astype(BF16)
    wrl = (wrt - wrh.astype(F32)).astype(BF16)
    rb = jnp.broadcast_to(router_bias.reshape(n_e, 1), (n_e, 128))
    args = (x1, mod, npre2.reshape(1, d), wrh, wrl, rb)
    in_specs = [pl.BlockSpec((tr, d), lambda i: (i, 0))] + [_const_spec(a.shape) for a in args[1:]]
    return pl.pallas_call(
        _router_kernel,
        out_shape=(jax.ShapeDtypeStruct((t, d // 2), PACKED),
                   jax.ShapeDtypeStruct((TOP_K, t), jnp.int32),
                   jax.ShapeDtypeStruct((TOP_K, t), jnp.int32),
                   jax.ShapeDtypeStruct((t, 128), F32),
                   jax.ShapeDtypeStruct((n_e, 128), F32)),
        grid=(t // tr,),
        in_specs=in_specs,
        out_specs=(pl.BlockSpec((tr, d // 2), lambda i: (i, 0)),
                   pl.BlockSpec((TOP_K, tr), lambda i: (0, i)),
                   pl.BlockSpec((TOP_K, tr), lambda i: (0, i)),
                   pl.BlockSpec((tr, 128), lambda i: (i, 0)),
                   pl.BlockSpec((n_e, 128), lambda i: (0, 0))),
        scratch_shapes=[pltpu.VMEM((n_e, 128), F32), pltpu.VMEM((TOP_K, tr), F32)],
        compiler_params=pltpu.CompilerParams(dimension_semantics=("arbitrary",),
                                             vmem_limit_bytes=VMEM_LIMIT_BYTES),
        name="moe_router",
    )(*args)


TAB_CHUNK_ROW, TAB_CHUNK_VALID, TAB_START, TAB_COUNT, TAB_CHUNK0, TAB_NCHUNK, TAB_TOTAL = range(7)
TAB_ROWS = 8


def _excl_cumsum_rows(lower, x):
    hi = jnp.floor(x * (1.0 / 128.0))
    lo = x - hi * 128.0
    return 128.0 * _dot(lower, hi.astype(BF16)) + _dot(lower, lo.astype(BF16))


def _dest_kernel(eidx_ref, rank_ref, cnt_ref, dest_ref, tab_ref):
    n_e = cnt_ref.shape[0]
    tt = eidx_ref.shape[1]
    n_g = tab_ref.shape[1]
    ch = float(EXPERT_CHUNK)
    cnt = jnp.floor((cnt_ref[...] + (ROW_ALIGN - 1)) * (1.0 / ROW_ALIGN)) * ROW_ALIGN
    r = lax.broadcasted_iota(jnp.int32, (n_e, n_e), 0)
    c = lax.broadcasted_iota(jnp.int32, (n_e, n_e), 1)
    lower = jnp.where(c < r, 1.0, 0.0).astype(BF16)
    starts = _excl_cumsum_rows(lower, cnt)
    start_col = starts[:, 0:1]

    @pl.when(pl.program_id(0) == 0)
    def _():
        nch = jnp.floor((cnt + (ch - 1.0)) * (1.0 / ch))
        chunk0 = _excl_cumsum_rows(lower, nch)
        cend_col = (chunk0 + nch)[:, 0:1]
        eye = r == c

        def as_row(col):
            return jnp.sum(jnp.where(eye, col, 0.0), axis=0, keepdims=True)

        g = lax.broadcasted_iota(jnp.int32, (n_e, n_g), 1).astype(F32)
        owner = jnp.sum(jnp.where(cend_col <= g, 1.0, 0.0), axis=0, keepdims=True)
        mine = lax.broadcasted_iota(jnp.int32, (n_e, n_g), 0).astype(F32) == owner
        g_row = lax.broadcasted_iota(jnp.int32, (1, n_g), 1).astype(F32)
        base = jnp.sum(jnp.where(mine, start_col - chunk0[:, 0:1] * ch, 0.0), axis=0, keepdims=True)
        left = jnp.sum(jnp.where(mine, cnt[:, 0:1] + chunk0[:, 0:1] * ch, 0.0), axis=0, keepdims=True)
        pad = jnp.zeros((1, n_g - n_e), F32)

        def wide(row):
            return jnp.concatenate([row, pad], axis=1)

        total = jnp.sum(nch[:, 0:1], axis=0, keepdims=True)
        tab_ref[...] = jnp.zeros(tab_ref.shape, jnp.int32)
        tab_ref[TAB_CHUNK_ROW:TAB_CHUNK_ROW + 1, :] = (base + g_row * ch).astype(jnp.int32)
        tab_ref[TAB_CHUNK_VALID:TAB_CHUNK_VALID + 1, :] = jnp.clip(left - g_row * ch, 0.0, ch).astype(jnp.int32)
        tab_ref[TAB_START:TAB_START + 1, :] = wide(as_row(start_col)).astype(jnp.int32)
        tab_ref[TAB_COUNT:TAB_COUNT + 1, :] = wide(as_row(cnt_ref[:, 0:1])).astype(jnp.int32)
        tab_ref[TAB_CHUNK0:TAB_CHUNK0 + 1, :] = wide(as_row(chunk0[:, 0:1])).astype(jnp.int32)
        tab_ref[TAB_NCHUNK:TAB_NCHUNK + 1, :] = wide(as_row(nch[:, 0:1])).astype(jnp.int32)
        tab_ref[TAB_TOTAL:TAB_TOTAL + 1, :] = jnp.broadcast_to(total, (1, n_g)).astype(jnp.int32)

    io_e = lax.broadcasted_iota(jnp.int32, (n_e, tt), 0)
    rows = []
    for k in range(TOP_K):
        hit = io_e == eidx_ref[k:k + 1, :]
        rows.append(jnp.sum(jnp.where(hit, start_col, 0.0), axis=0, keepdims=True))
    dest_ref[...] = jnp.concatenate(rows, axis=0).astype(jnp.int32) + rank_ref[...]


def _dest(eidx, rank, cnt):
    k, t = eidx.shape
    n_e = cnt.shape[0]
    tt = 512
    n_g = (t * k) // EXPERT_CHUNK + n_e
    return pl.pallas_call(
        _dest_kernel,
        out_shape=(jax.ShapeDtypeStruct((k, t), jnp.int32),
                   jax.ShapeDtypeStruct((TAB_ROWS, n_g), jnp.int32)),
        grid=(t // tt,),
        in_specs=[pl.BlockSpec((k, tt), lambda i: (0, i)),
                  pl.BlockSpec((k, tt), lambda i: (0, i)),
                  _const_spec(cnt.shape)],
        out_specs=(pl.BlockSpec((k, tt), lambda i: (0, i)), _const_spec((TAB_ROWS, n_g))),
        compiler_params=pltpu.CompilerParams(dimension_semantics=("arbitrary",)),
        name="moe_dest",
    )(eidx, rank, cnt)


SC_CORES = 2
SC_SUBCORES = 16
SC_WORKERS = SC_CORES * SC_SUBCORES
SC_ROWS = 128


def _sc_mesh():
    return plsc.VectorSubcoreMesh(core_axis_name="c", subcore_axis_name="s")


def _sc_worker_id():
    return lax.axis_index("s") * SC_CORES + lax.axis_index("c")


def _sc_scatter_rows(rows, dest_flat, n_out):
    t, w = rows.shape
    n_k = dest_flat.shape[0] // t
    per = t // SC_WORKERS
    assert per % SC_ROWS == 0

    @functools.partial(
        pl.kernel, mesh=_sc_mesh(), out_type=jax.ShapeDtypeStruct((n_out, w), rows.dtype),
        scratch_types=[pltpu.VMEM((SC_ROWS,), jnp.int32), pltpu.VMEM((SC_ROWS, w), rows.dtype),
                       pltpu.SemaphoreType.DMA],
        name="moe_dispatch_sc")
    def scatter(rows_hbm, idx_hbm, out_hbm, idx_v, rows_v, sem):
        base = _sc_worker_id() * per

        @pl.loop(0, per // SC_ROWS)
        def _(j):
            off = pl.multiple_of(base + j * SC_ROWS, ROW_ALIGN)
            pltpu.sync_copy(rows_hbm.at[pl.ds(off, SC_ROWS)], rows_v)
            for k in range(n_k):
                pltpu.sync_copy(idx_hbm.at[pl.ds(pl.multiple_of(k * t + off, ROW_ALIGN), SC_ROWS)], idx_v)
                pltpu.async_copy(rows_v, out_hbm.at[idx_v], sem).wait()

    return scatter(rows, dest_flat)


def _sc_gather_rows(table, idx):
    n = idx.shape[0]
    w = table.shape[1]
    per = n // SC_WORKERS
    assert per % SC_ROWS == 0

    @functools.partial(
        pl.kernel, mesh=_sc_mesh(), out_type=jax.ShapeDtypeStruct((n, w), table.dtype),
        scratch_types=[pltpu.VMEM((SC_ROWS,), jnp.int32), pltpu.VMEM((SC_ROWS, w), table.dtype),
                       pltpu.SemaphoreType.DMA],
        name="moe_gather_sc")
    def gather(table_hbm, idx_hbm, out_hbm, idx_v, rows_v, sem):
        base = _sc_worker_id() * per

        @pl.loop(0, per // SC_ROWS)
        def _(j):
            off = pl.multiple_of(base + j * SC_ROWS, ROW_ALIGN)
            pltpu.sync_copy(idx_hbm.at[pl.ds(off, SC_ROWS)], idx_v)
            pltpu.async_copy(table_hbm.at[idx_v], rows_v, sem).wait()
            pltpu.sync_copy(rows_v, out_hbm.at[pl.ds(off, SC_ROWS)])

    return gather(table, idx)


def _pad_kernel(tab_ref, xs_in, xs_hbm, zero_ref, sem, *, n_e):
    del xs_in
    zero_ref[...] = jnp.zeros_like(zero_ref)

    def pad_row_copy(row):
        return pltpu.make_async_copy(zero_ref.at[pl.ds(0, 1), :], xs_hbm.at[pl.ds(row, 1), :], sem)

    def pad_rows(e, carry):
        count = tab_ref[TAB_COUNT, e]
        end = tab_ref[TAB_START, e] + count
        n_pad = (ROW_ALIGN - count % ROW_ALIGN) % ROW_ALIGN
        for j in range(ROW_ALIGN - 1):
            pl.when(j < n_pad)(pad_row_copy(end + j).start)
        for j in range(ROW_ALIGN - 1):
            pl.when(j < n_pad)(pad_row_copy(end + j).wait)
        return carry

    lax.fori_loop(0, n_e, pad_rows, 0)
    last = tab_ref[TAB_COUNT, n_e - 1]
    used = tab_ref[TAB_START, n_e - 1] + ((last + (ROW_ALIGN - 1)) // ROW_ALIGN) * ROW_ALIGN
    piece = zero_ref.shape[0]
    n_rows = xs_hbm.shape[0]
    for j in range(pl.cdiv(n_e * ROW_ALIGN + EXPERT_CHUNK + ROW_ALIGN, piece)):
        at = pl.multiple_of(jnp.minimum(used + j * piece, n_rows - piece), ROW_ALIGN)
        tail = pltpu.make_async_copy(zero_ref, xs_hbm.at[pl.ds(at, piece), :], sem)
        tail.start()
        tail.wait()


def _sorted_rows(t, n_e):
    return t * TOP_K + n_e * ROW_ALIGN + EXPERT_CHUNK + ROW_ALIGN


def _dispatch(tab, dest, hp, n_e):
    t, m = hp.shape
    n_rows = _sorted_rows(t, n_e)
    xs = _sc_scatter_rows(hp, dest.reshape(-1), n_rows)
    grid_spec = pltpu.PrefetchScalarGridSpec(
        num_scalar_prefetch=1,
        grid=(1,),
        in_specs=[pl.BlockSpec(memory_space=pl.ANY)],
        out_specs=pl.BlockSpec(memory_space=pl.ANY),
        scratch_shapes=[pltpu.VMEM((EXPERT_CHUNK, m), PACKED), pltpu.SemaphoreType.DMA(())],
    )
    return pl.pallas_call(
        functools.partial(_pad_kernel, n_e=n_e),
        out_shape=jax.ShapeDtypeStruct((n_rows, m), PACKED),
        grid_spec=grid_spec,
        input_output_aliases={1: 0},
        compiler_params=pltpu.CompilerParams(dimension_semantics=("arbitrary",)),
        name="moe_pad",
    )(tab, xs)


X_LOOKAHEAD = 2
X_SLOTS = X_LOOKAHEAD + 1
Y_SLOTS = 2


def _expert_kernel(tab_ref, xs_hbm, wg_ref, wu_ref, wd_ref, y_hbm,
                   wgu16_ref, wd16_ref, xbuf_ref, ybuf_ref, in_sem, out_sem, *, n_e):
    e = pl.program_id(0)
    ch, half = xbuf_ref.shape[1:]
    ff = wg_ref.shape[2]
    n_total = tab_ref[TAB_TOTAL, 0]
    first = tab_ref[TAB_CHUNK0, e]
    n_chunks = tab_ref[TAB_NCHUNK, e]

    def in_copy(g):
        slot = g % X_SLOTS
        row = pl.multiple_of(tab_ref[TAB_CHUNK_ROW, g], ROW_ALIGN)
        return pltpu.make_async_copy(xs_hbm.at[pl.ds(row, ch), :], xbuf_ref.at[slot], in_sem.at[slot])

    def out_copies(g):
        slot = g % Y_SLOTS
        row = tab_ref[TAB_CHUNK_ROW, g]
        valid = tab_ref[TAB_CHUNK_VALID, g]
        yield valid >= ch, pltpu.make_async_copy(ybuf_ref.at[slot], y_hbm.at[pl.ds(pl.multiple_of(row, ROW_ALIGN), ch), :],
                                                 out_sem.at[slot])
        size = ch // 2
        while size >= ROW_ALIGN:
            off = pl.multiple_of((valid // (2 * size)) * (2 * size), ROW_ALIGN)
            cond = (valid < ch) & ((valid & size) != 0)
            yield cond, pltpu.make_async_copy(ybuf_ref.at[slot, pl.ds(off, size), :],
                                              y_hbm.at[pl.ds(pl.multiple_of(row + off, ROW_ALIGN), size), :],
                                              out_sem.at[slot])
            size //= 2

    def start_out(g):
        for cond, cp in out_copies(g):
            pl.when(cond)(cp.start)

    def wait_out(g):
        for cond, cp in out_copies(g):
            pl.when(cond)(cp.wait)

    @pl.when(e == 0)
    def _():
        for j in range(X_LOOKAHEAD):
            pl.when(j < n_total)(in_copy(j).start)

    @pl.when(n_chunks > 0)
    def _():
        wgu16_ref[:, 0:ff] = wg_ref[0].astype(BF16)
        wgu16_ref[:, ff:2 * ff] = wu_ref[0].astype(BF16)
        wd16_ref[...] = wd_ref[0].astype(BF16)

    def chunk(c, carry):
        g = first + c
        pl.when(g + X_LOOKAHEAD < n_total)(in_copy(g + X_LOOKAHEAD).start)
        in_copy(g).wait()
        pl.when(g >= Y_SLOTS)(lambda: wait_out(g - Y_SLOTS))
        xa, xb = _unpack_bf16_pairs(xbuf_ref[g % X_SLOTS])
        hgu = (_dot(xa.astype(BF16), wgu16_ref[0:half, :])
               + _dot(xb.astype(BF16), wgu16_ref[half:2 * half, :]))
        act = _silu(hgu[:, :ff]) * hgu[:, ff:]
        y = _dot(act.astype(BF16), wd16_ref[...])
        ybuf_ref[g % Y_SLOTS] = _pack_bf16_pairs(y)
        start_out(g)
        return carry

    lax.fori_loop(0, n_chunks, chunk, 0)

    @pl.when(e == n_e - 1)
    def _():
        for j in range(Y_SLOTS, 0, -1):
            pl.when(n_total >= j)(lambda j=j: wait_out(n_total - j))
        n_rows = y_hbm.shape[0]
        used = tab_ref[TAB_START, e] + ((tab_ref[TAB_COUNT, e] + (ROW_ALIGN - 1)) // ROW_ALIGN) * ROW_ALIGN
        ybuf_ref[0] = jnp.zeros((ch, half), PACKED)
        for j in range(pl.cdiv(n_e * ROW_ALIGN + EXPERT_CHUNK + ROW_ALIGN, ch)):
            at = pl.multiple_of(jnp.minimum(used + j * ch, n_rows - ch), ROW_ALIGN)
            tail = pltpu.make_async_copy(ybuf_ref.at[0], y_hbm.at[pl.ds(at, ch), :], out_sem.at[0])
            tail.start()
            tail.wait()


def _experts(tab, xs, w_gate_e, w_up_e, w_down_e):
    p, half = xs.shape
    n_e, d, ff = w_gate_e.shape
    ch = EXPERT_CHUNK
    grid_spec = pltpu.PrefetchScalarGridSpec(
        num_scalar_prefetch=1,
        grid=(n_e,),
        in_specs=[pl.BlockSpec(memory_space=pl.ANY),
                  pl.BlockSpec((1, d, ff), lambda e, *_: (e, 0, 0)),
                  pl.BlockSpec((1, d, ff), lambda e, *_: (e, 0, 0)),
                  pl.BlockSpec((1, ff, d), lambda e, *_: (e, 0, 0))],
        out_specs=pl.BlockSpec(memory_space=pl.ANY),
        scratch_shapes=[pltpu.VMEM((d, 2 * ff), BF16), pltpu.VMEM((ff, d), BF16),
                        pltpu.VMEM((X_SLOTS, ch, half), PACKED), pltpu.VMEM((Y_SLOTS, ch, half), PACKED),
                        pltpu.SemaphoreType.DMA((X_SLOTS,)), pltpu.SemaphoreType.DMA((Y_SLOTS,))],
    )
    return pl.pallas_call(
        functools.partial(_expert_kernel, n_e=n_e),
        out_shape=jax.ShapeDtypeStruct((p, half), PACKED),
        grid_spec=grid_spec,
        compiler_params=pltpu.CompilerParams(dimension_semantics=("arbitrary",),
                                             vmem_limit_bytes=VMEM_LIMIT_BYTES),
        name="moe_experts",
    )(tab, xs, w_gate_e, w_up_e, w_down_e)


def _combine_kernel(yg_ref, wcol_ref, x1_ref, hp_ref, mod_ref, npost_ref, wgus_ref, wds_ref, o_ref):
    te, d = x1_ref.shape
    half = d // 2
    ff = wds_ref.shape[0]

    xa, xb = _unpack_bf16_pairs(hp_ref[...])
    hgu = _dot(xa.astype(BF16), wgus_ref[0:half, :]) + _dot(xb.astype(BF16), wgus_ref[half:d, :])
    act = _silu(hgu[:, :ff]) * hgu[:, ff:]
    shared = _dot(act.astype(BF16), wds_ref[...])

    acc_a = shared[:, :half]
    acc_b = shared[:, half:]
    for k in range(TOP_K):
        ya, yb = _unpack_bf16_pairs(yg_ref[k])
        wk = wcol_ref[:, k:k + 1]
        acc_a = acc_a + wk * ya
        acc_b = acc_b + wk * yb
    moe = jnp.concatenate([acc_a, acc_b], axis=-1)
    g2 = mod_ref[:, 5 * d:6 * d]
    o_ref[...] = x1_ref[...] + g2 * _rms(moe, npost_ref[...])


def _combine(dest, y, wcol, x1, hp, mod, npost2, w_gate_s, w_up_s, w_down_s):
    t, d = x1.shape
    half = d // 2
    te = SEQ_TILE
    wgus = jnp.concatenate([w_gate_s, w_up_s], axis=1).astype(BF16)
    wds = w_down_s.astype(BF16)
    yg = _sc_gather_rows(y, dest.reshape(-1)).reshape(TOP_K, t, half)
    return pl.pallas_call(
        _combine_kernel,
        out_shape=jax.ShapeDtypeStruct((t, d), F32),
        grid=(t // te,),
        in_specs=[pl.BlockSpec((TOP_K, te, half), lambda i: (0, i, 0)),
                  pl.BlockSpec((te, 128), lambda i: (i, 0)),
                  pl.BlockSpec((te, d), lambda i: (i, 0)),
                  pl.BlockSpec((te, half), lambda i: (i, 0)),
                  _const_spec(mod.shape),
                  _const_spec((1, d)),
                  _const_spec(wgus.shape),
                  _const_spec(wds.shape)],
        out_specs=pl.BlockSpec((te, d), lambda i: (i, 0)),
        compiler_params=pltpu.CompilerParams(dimension_semantics=("arbitrary",),
                                             vmem_limit_bytes=VMEM_LIMIT_BYTES),
        name="moe_combine",
    )(yg, wcol, x1, hp, mod, npost2.reshape(1, d), wgus, wds)


def kernel(x, c, w_ada, b_ada, norm_pre_mix, norm_post_mix, w_in, conv_w, w_conv_out, hgrn_lower_bounds, hgrn_norm_w, w_hgrn_out, w_o, norm_pre_ffn, norm_post_ffn, w_router, router_bias, w_gate_e, w_up_e, w_down_e, w_gate_s, w_up_s, w_down_s):
    bsz, seq, d = x.shape
    assert bsz == 1 and w_ada.shape[0] == 1
    mod = _ada_mod(c, w_ada[0], b_ada[0])
    x1 = _mixer(x.reshape(seq, d), mod, norm_pre_mix[0], norm_post_mix[0], w_in[0], conv_w[0],
                w_conv_out[0], hgrn_lower_bounds, hgrn_norm_w[0], w_hgrn_out[0], w_o[0])
    hp, eidx, rank, wcol, cnt = _router(x1, mod, norm_pre_ffn[0], w_router[0], router_bias[0])
    dest, tab = _dest(eidx, rank, cnt)
    xs = _dispatch(tab, dest, hp, w_router.shape[-1])
    y = _experts(tab, xs, w_gate_e[0], w_up_e[0], w_down_e[0])
    out = _combine(dest, y, wcol, x1, hp, mod, norm_post_ffn[0], w_gate_s[0], w_up_s[0], w_down_s[0])
    return out.reshape(bsz, seq, d)
```

```python
import functools

import jax
import jax.numpy as jnp
from jax import lax
from jax.experimental import pallas as pl
from jax.experimental.pallas import tpu as pltpu
from jax.experimental.pallas import tpu_sc as plsc

F32 = jnp.float32
BF16 = jnp.bfloat16

NORM_EPS = 1e-6
CONV_DIM = 512
CONV_WIDTH = 3
HGRN_HEADS = 4
HGRN_DK = 128
HGRN_QK = HGRN_HEADS * HGRN_DK
N_GROUPS = 8
TOPK_GROUPS = 4
TOP_K = 8
ROUTED_SCALE = 2.5

SEQ_TILE = 256
EXPERT_CHUNK = 512
ROW_ALIGN = 8
VMEM_LIMIT_BYTES = 56 * 1024 * 1024


def _dot(a, b):
    return jnp.dot(a, b, preferred_element_type=F32)


def _dot_nt(a, b):
    return lax.dot_general(a, b, (((1,), (1,)), ((), ())), preferred_element_type=F32)


def _dot_tn(a, b):
    return lax.dot_general(a, b, (((0,), (0,)), ((), ())), preferred_element_type=F32)


def _split3(x):
    hi = x.astype(BF16)
    r1 = x - hi.astype(F32)
    mid = r1.astype(BF16)
    lo = (r1 - mid.astype(F32)).astype(BF16)
    return hi, mid, lo


def _sigmoid(x):
    return 1.0 / (1.0 + jnp.exp(-x))


def _silu(x):
    return x * _sigmoid(x)


def _rms(x, w):
    ms = jnp.mean(x * x, axis=-1, keepdims=True)
    return x * lax.rsqrt(ms + NORM_EPS) * w


def _ada_kernel(c_ref, w_ref, b_ref, o_ref):
    cs = _silu(c_ref[...])
    h1, h2, h3 = _split3(cs)
    w1, w2, w3 = _split3(w_ref[...])
    acc = _dot(h1, w1) + (_dot(h1, w2) + _dot(h2, w1)) + (_dot(h1, w3) + _dot(h2, w2) + _dot(h3, w1))
    o_ref[...] = acc + b_ref[...]


def _ada_mod(c, w_ada, b_ada):
    d = c.shape[-1]
    n = w_ada.shape[-1]
    bn = 1024
    c8 = jnp.broadcast_to(c.reshape(1, d), (8, d))
    out = pl.pallas_call(
        _ada_kernel,
        out_shape=jax.ShapeDtypeStruct((8, n), F32),
        grid=(n // bn,),
        in_specs=[pl.BlockSpec((8, d), lambda j: (0, 0)),
                  pl.BlockSpec((d, bn), lambda j: (0, j)),
                  pl.BlockSpec((1, bn), lambda j: (0, j))],
        out_specs=pl.BlockSpec((8, bn), lambda j: (0, j)),
        compiler_params=pltpu.CompilerParams(dimension_semantics=("arbitrary",),
                                             vmem_limit_bytes=VMEM_LIMIT_BYTES),
        name="ada_mod",
    )(c8, w_ada, b_ada.reshape(1, n))
    return out[0:1]


def _level_reference(b, rolled, s, row):
    ts = b.shape[0]
    c = s // 2 - 1
    if s >= 16:
        pieces = []
        for blk in range(ts // s):
            r = blk * s + c
            pieces.append(jnp.broadcast_to(b[r:r + 1, :], (s, b.shape[1])))
        return pieces[0] if len(pieces) == 1 else jnp.concatenate(pieces, axis=0)
    pos = row & (s - 1)
    out = b
    for p in range(s):
        if p == c:
            continue
        out = jnp.where(pos == p, rolled[p - c], out)
    return out


def _mixer_kernel(x_ref, mod_ref, npre_ref, npost_ref, win_ref, convw_ref, wco_ref,
                  hlb_ref, hnw_ref, who_ref, wo_ref, o_ref, state_ref, ubuf_ref):
    ts, d = x_ref.shape
    step = pl.program_id(0)

    @pl.when(step == 0)
    def _():
        state_ref[...] = jnp.zeros_like(state_ref)
        ubuf_ref[0:8, :] = jnp.zeros((8, CONV_DIM), F32)

    x = x_ref[...]
    sh1 = mod_ref[:, 0:d]
    sc1 = mod_ref[:, d:2 * d]
    g1 = mod_ref[:, 2 * d:3 * d]
    h = (_rms(x, npre_ref[...]) * (1.0 + sc1) + sh1).astype(BF16)

    def proj(lo, width):
        return _dot(h, win_ref[:, lo:lo + width])

    c0 = 0
    cb = proj(c0, CONV_DIM)
    u = proj(c0 + CONV_DIM, CONV_DIM) * proj(c0 + 2 * CONV_DIM, CONV_DIM)
    ubuf_ref[8:8 + ts, :] = u
    conv = (ubuf_ref[6:6 + ts, :] * convw_ref[0:1, :]
            + ubuf_ref[7:7 + ts, :] * convw_ref[1:2, :]
            + u * convw_ref[2:3, :])
    ubuf_ref[0:8, :] = ubuf_ref[ts:ts + 8, :]
    y_a = _dot((cb * conv).astype(BF16), wco_ref[...])

    c1 = 3 * CONV_DIM
    q = _silu(proj(c1, HGRN_QK))
    hl = hlb_ref[...]
    hm = jnp.max(hl, axis=0, keepdims=True)
    he = jnp.exp(hl - hm)
    lb = he[0:1, :] / jnp.sum(he, axis=0, keepdims=True)
    fg = lb + (1.0 - lb) * _sigmoid(proj(c1 + HGRN_QK, HGRN_QK))
    k = 1.0 - fg
    g = jnp.log(fg)
    v = proj(c1 + 2 * HGRN_QK, HGRN_QK)
    gg = proj(c1 + 3 * HGRN_QK, HGRN_QK)

    row = lax.broadcasted_iota(jnp.int32, (ts, ts), 0)
    col = lax.broadcasted_iota(jnp.int32, (ts, ts), 1)
    tril = jnp.where(row >= col, 1.0, 0.0).astype(BF16)
    ghi, gmid, glo = _split3(g)
    b = _dot(tril, ghi) + _dot(tril, gmid) + _dot(tril, glo)
    b_last = b[ts - 1:ts, :]

    rowq = lax.broadcasted_iota(jnp.int32, (ts, HGRN_QK), 0)
    rolled = {sft: pltpu.roll(b, sft % ts, axis=0) for sft in (-3, -2, -1, 1, 2, 3, 4)}
    xor = row ^ col

    levels = []
    s = ts
    while s >= 2:
        levels.append(s)
        s //= 2
    qt, kt = [], []
    for s in levels:
        bref = _level_reference(b, rolled, s, rowq)
        e = jnp.exp(-jnp.abs(b - bref))
        upper = (rowq & (s // 2)) != 0
        qt.append(jnp.where(upper, q * e, 0.0).astype(BF16))
        kt.append(jnp.where(upper, 0.0, k * e).astype(BF16))
    q_in = (q * jnp.exp(b)).astype(BF16)
    k_out = (k * jnp.exp(b_last - b)).astype(BF16)
    v16 = v.astype(BF16)
    qk = q * k
    decay_last = jnp.exp(b_last)

    hnw = hnw_ref[...]
    outs = []
    for hd in range(HGRN_HEADS):
        sl = slice(hd * HGRN_DK, (hd + 1) * HGRN_DK)
        a = jnp.zeros((ts, ts), F32)
        for li in range(len(levels) - 1, -1, -1):
            s = levels[li]
            p = _dot_nt(qt[li][:, sl], kt[li][:, sl])
            a = jnp.where(xor >= s // 2, p, a)
        st = state_ref[hd]
        o_h = (_dot(a.astype(BF16), v16[:, sl])
               + jnp.sum(qk[:, sl], axis=-1, keepdims=True) * v[:, sl]
               + _dot_nt(q_in[:, sl], st.astype(BF16)))
        state_ref[hd] = st * decay_last[:, sl] + _dot_tn(v16[:, sl], k_out[:, sl])
        outs.append(_rms(o_h, hnw))
    o = jnp.concatenate(outs, axis=-1) * _silu(gg)
    y_b = _dot(o.astype(BF16), who_ref[...])

    c2 = c1 + 4 * HGRN_QK
    m = _sigmoid(proj(c2, d)) * y_a + _sigmoid(proj(c2 + d, d)) * y_b
    y = _dot(m.astype(BF16), wo_ref[...])
    o_ref[...] = x + g1 * _rms(y, npost_ref[...])


def _const_spec(shape):
    nd = len(shape)
    return pl.BlockSpec(shape, lambda i: (0,) * nd)


def _mixer(x2, mod, npre, npost, w_in, conv_w, w_conv_out, hlb, hnw, w_hgrn_out, w_o):
    t, d = x2.shape
    ts = SEQ_TILE
    args = (x2, mod, npre.reshape(1, d), npost.reshape(1, d), w_in.astype(BF16), conv_w,
            w_conv_out.astype(BF16), hlb, hnw.reshape(1, -1), w_hgrn_out.astype(BF16),
            w_o.astype(BF16))
    in_specs = [pl.BlockSpec((ts, d), lambda i: (i, 0))] + [_const_spec(a.shape) for a in args[1:]]
    return pl.pallas_call(
        _mixer_kernel,
        out_shape=jax.ShapeDtypeStruct((t, d), F32),
        grid=(t // ts,),
        in_specs=in_specs,
        out_specs=pl.BlockSpec((ts, d), lambda i: (i, 0)),
        scratch_shapes=[pltpu.VMEM((HGRN_HEADS, HGRN_DK, HGRN_DK), F32),
                        pltpu.VMEM((ts + 8, CONV_DIM), F32)],
        compiler_params=pltpu.CompilerParams(dimension_semantics=("arbitrary",),
                                             vmem_limit_bytes=VMEM_LIMIT_BYTES),
        name="token_mixer",
    )(*args)


PACKED = jnp.int32


def _pack_bf16_pairs(x):
    m = x.shape[1] // 2
    hi = lax.bitcast_convert_type(x[:, :m].astype(BF16).astype(F32), jnp.uint32)
    lo = lax.bitcast_convert_type(x[:, m:].astype(BF16).astype(F32), jnp.uint32)
    return lax.bitcast_convert_type(hi | (lo >> 16), PACKED)


def _unpack_bf16_pairs(p):
    p = lax.bitcast_convert_type(p, jnp.uint32)
    hi = lax.bitcast_convert_type(p & jnp.uint32(0xFFFF0000), F32)
    lo = lax.bitcast_convert_type(p << 16, F32)
    return hi, lo


def _router_kernel(x1_ref, mod_ref, npre_ref, wrh_ref, wrl_ref, rb_ref,
                   hp_ref, eidx_ref, rank_ref, wcol_ref, cnt_ref, carry_ref, rows_ref):
    assert N_GROUPS == TOP_K == rows_ref.shape[0]
    tr, d = x1_ref.shape
    n_e = wrh_ref.shape[0]
    gsz = n_e // N_GROUPS
    neg = -jnp.inf

    @pl.when(pl.program_id(0) == 0)
    def _():
        carry_ref[...] = jnp.zeros_like(carry_ref)

    sh2 = mod_ref[:, 3 * d:4 * d]
    sc2 = mod_ref[:, 4 * d:5 * d]
    h2 = _rms(x1_ref[...], npre_ref[...]) * (1.0 + sc2) + sh2
    hp_ref[...] = _pack_bf16_pairs(h2)
    h_hi = h2.astype(BF16)
    h_lo = (h2 - h_hi.astype(F32)).astype(BF16)
    wrh = wrh_ref[...]
    logits = _dot_nt(wrh, h_hi) + (_dot_nt(wrh, h_lo) + _dot_nt(wrl_ref[...], h_hi))
    scores = _sigmoid(logits)
    sel = scores + rb_ref[:, 0:1]

    io_e = lax.broadcasted_iota(jnp.int32, (n_e, tr), 0)
    for g in range(N_GROUPS):
        blk = sel[g * gsz:(g + 1) * gsz, :]
        io = lax.broadcasted_iota(jnp.int32, (gsz, tr), 0) + g * gsz
        m1 = jnp.max(blk, axis=0, keepdims=True)
        i1 = jnp.min(jnp.where(blk == m1, io, n_e), axis=0, keepdims=True)
        m2 = jnp.max(jnp.where(io == i1, neg, blk), axis=0, keepdims=True)
        rows_ref[g:g + 1, :] = m1 + m2
    gs = rows_ref[...]
    io_g = lax.broadcasted_iota(jnp.int32, (N_GROUPS, tr), 0)
    gsel = jnp.zeros((N_GROUPS, tr), F32)
    for _ in range(TOPK_GROUPS):
        m = jnp.max(gs, axis=0, keepdims=True)
        gi = jnp.min(jnp.where(gs == m, io_g, N_GROUPS), axis=0, keepdims=True)
        hit = io_g == gi
        gsel = jnp.where(hit, 1.0, gsel)
        gs = jnp.where(hit, neg, gs)
    rows_ref[...] = gsel
    cur = jnp.concatenate(
        [jnp.where(rows_ref[g:g + 1, :] > 0.5, sel[g * gsz:(g + 1) * gsz, :], neg) for g in range(N_GROUPS)],
        axis=0)

    idxs = []
    selm = jnp.zeros((n_e, tr), F32)
    for k in range(TOP_K):
        m = jnp.max(cur, axis=0, keepdims=True)
        idx = jnp.min(jnp.where(cur == m, io_e, n_e), axis=0, keepdims=True)
        hit = io_e == idx
        rows_ref[k:k + 1, :] = jnp.sum(jnp.where(hit, scores, 0.0), axis=0, keepdims=True)
        cur = jnp.where(hit, neg, cur)
        selm = jnp.where(hit, 1.0, selm)
        eidx_ref[k:k + 1, :] = idx
        idxs.append(idx)

    r_io = lax.broadcasted_iota(jnp.int32, (tr, tr + 128), 0)
    c_io = lax.broadcasted_iota(jnp.int32, (tr, tr + 128), 1)
    before = jnp.where((r_io < c_io) | (c_io >= tr), 1.0, 0.0).astype(BF16)
    r_ext = _dot(selm.astype(BF16), before)
    rank_full = r_ext[:, :tr] + carry_ref[:, 0:1]
    for k, idx in enumerate(idxs):
        rk = jnp.sum(jnp.where(io_e == idx, rank_full, 0.0), axis=0, keepdims=True)
        rank_ref[k:k + 1, :] = rk.astype(jnp.int32)
    carry_ref[...] = carry_ref[...] + r_ext[:, tr:]
    cnt_ref[...] = carry_ref[...]

    wk = rows_ref[...]
    wn = wk / (jnp.sum(wk, axis=0, keepdims=True) + 1e-20) * ROUTED_SCALE
    eye = jnp.where(lax.broadcasted_iota(jnp.int32, (TOP_K, 128), 0)
                    == lax.broadcasted_iota(jnp.int32, (TOP_K, 128), 1), 1.0, 0.0).astype(BF16)
    w1, w2, w3 = _split3(wn)
    wcol_ref[...] = _dot_tn(w1, eye) + _dot_tn(w2, eye) + _dot_tn(w3, eye)


def _router(x1, mod, npre2, w_router, router_bias):
    t, d = x1.shape
    n_e = w_router.shape[1]
    tr = SEQ_TILE
    wrt = w_router.T
    wrh = wrt.astype(BF16)
    wrl = (wrt - wrh.astype(F32)).astype(BF16)
    rb = jnp.broadcast_to(router_bias.reshape(n_e, 1), (n_e, 128))
    args = (x1, mod, npre2.reshape(1, d), wrh, wrl, rb)
    in_specs = [pl.BlockSpec((tr, d), lambda i: (i, 0))] + [_const_spec(a.shape) for a in args[1:]]
    return pl.pallas_call(
        _router_kernel,
        out_shape=(jax.ShapeDtypeStruct((t, d // 2), PACKED),
                   jax.ShapeDtypeStruct((TOP_K, t), jnp.int32),
                   jax.ShapeDtypeStruct((TOP_K, t), jnp.int32),
                   jax.ShapeDtypeStruct((t, 128), F32),
                   jax.ShapeDtypeStruct((n_e, 128), F32)),
        grid=(t // tr,),
        in_specs=in_specs,
        out_specs=(pl.BlockSpec((tr, d // 2), lambda i: (i, 0)),
                   pl.BlockSpec((TOP_K, tr), lambda i: (0, i)),
                   pl.BlockSpec((TOP_K, tr), lambda i: (0, i)),
                   pl.BlockSpec((tr, 128), lambda i: (i, 0)),
                   pl.BlockSpec((n_e, 128), lambda i: (0, 0))),
        scratch_shapes=[pltpu.VMEM((n_e, 128), F32), pltpu.VMEM((TOP_K, tr), F32)],
        compiler_params=pltpu.CompilerParams(dimension_semantics=("arbitrary",),
                                             vmem_limit_bytes=VMEM_LIMIT_BYTES),
        name="moe_router",
    )(*args)


TAB_CHUNK_ROW, TAB_CHUNK_VALID, TAB_START, TAB_COUNT, TAB_CHUNK0, TAB_NCHUNK, TAB_TOTAL = range(7)
TAB_ROWS = 8


def _excl_cumsum_rows(lower, x):
    hi = jnp.floor(x * (1.0 / 128.0))
    lo = x - hi * 128.0
    return 128.0 * _dot(lower, hi.astype(BF16)) + _dot(lower, lo.astype(BF16))


def _dest_kernel(eidx_ref, rank_ref, cnt_ref, dest_ref, tab_ref):
    n_e = cnt_ref.shape[0]
    tt = eidx_ref.shape[1]
    n_g = tab_ref.shape[1]
    ch = float(EXPERT_CHUNK)
    cnt = jnp.floor((cnt_ref[...] + (ROW_ALIGN - 1)) * (1.0 / ROW_ALIGN)) * ROW_ALIGN
    r = lax.broadcasted_iota(jnp.int32, (n_e, n_e), 0)
    c = lax.broadcasted_iota(jnp.int32, (n_e, n_e), 1)
    lower = jnp.where(c < r, 1.0, 0.0).astype(BF16)
    starts = _excl_cumsum_rows(lower, cnt)
    start_col = starts[:, 0:1]

    @pl.when(pl.program_id(0) == 0)
    def _():
        nch = jnp.floor((cnt + (ch - 1.0)) * (1.0 / ch))
        chunk0 = _excl_cumsum_rows(lower, nch)
        cend_col = (chunk0 + nch)[:, 0:1]
        eye = r == c

        def as_row(col):
            return jnp.sum(jnp.where(eye, col, 0.0), axis=0, keepdims=True)

        g = lax.broadcasted_iota(jnp.int32, (n_e, n_g), 1).astype(F32)
        owner = jnp.sum(jnp.where(cend_col <= g, 1.0, 0.0), axis=0, keepdims=True)
        mine = lax.broadcasted_iota(jnp.int32, (n_e, n_g), 0).astype(F32) == owner
        g_row = lax.broadcasted_iota(jnp.int32, (1, n_g), 1).astype(F32)
        base = jnp.sum(jnp.where(mine, start_col - chunk0[:, 0:1] * ch, 0.0), axis=0, keepdims=True)
        left = jnp.sum(jnp.where(mine, cnt[:, 0:1] + chunk0[:, 0:1] * ch, 0.0), axis=0, keepdims=True)
        pad = jnp.zeros((1, n_g - n_e), F32)

        def wide(row):
            return jnp.concatenate([row, pad], axis=1)

        total = jnp.sum(nch[:, 0:1], axis=0, keepdims=True)
        tab_ref[...] = jnp.zeros(tab_ref.shape, jnp.int32)
        tab_ref[TAB_CHUNK_ROW:TAB_CHUNK_ROW + 1, :] = (base + g_row * ch).astype(jnp.int32)
        tab_ref[TAB_CHUNK_VALID:TAB_CHUNK_VALID + 1, :] = jnp.clip(left - g_row * ch, 0.0, ch).astype(jnp.int32)
        tab_ref[TAB_START:TAB_START + 1, :] = wide(as_row(start_col)).astype(jnp.int32)
        tab_ref[TAB_COUNT:TAB_COUNT + 1, :] = wide(as_row(cnt_ref[:, 0:1])).astype(jnp.int32)
        tab_ref[TAB_CHUNK0:TAB_CHUNK0 + 1, :] = wide(as_row(chunk0[:, 0:1])).astype(jnp.int32)
        tab_ref[TAB_NCHUNK:TAB_NCHUNK + 1, :] = wide(as_row(nch[:, 0:1])).astype(jnp.int32)
        tab_ref[TAB_TOTAL:TAB_TOTAL + 1, :] = jnp.broadcast_to(total, (1, n_g)).astype(jnp.int32)

    io_e = lax.broadcasted_iota(jnp.int32, (n_e, tt), 0)
    rows = []
    for k in range(TOP_K):
        hit = io_e == eidx_ref[k:k + 1, :]
        rows.append(jnp.sum(jnp.where(hit, start_col, 0.0), axis=0, keepdims=True))
    dest_ref[...] = jnp.concatenate(rows, axis=0).astype(jnp.int32) + rank_ref[...]


def _dest(eidx, rank, cnt):
    k, t = eidx.shape
    n_e = cnt.shape[0]
    tt = 512
    n_g = (t * k) // EXPERT_CHUNK + n_e
    return pl.pallas_call(
        _dest_kernel,
        out_shape=(jax.ShapeDtypeStruct((k, t), jnp.int32),
                   jax.ShapeDtypeStruct((TAB_ROWS, n_g), jnp.int32)),
        grid=(t // tt,),
        in_specs=[pl.BlockSpec((k, tt), lambda i: (0, i)),
                  pl.BlockSpec((k, tt), lambda i: (0, i)),
                  _const_spec(cnt.shape)],
        out_specs=(pl.BlockSpec((k, tt), lambda i: (0, i)), _const_spec((TAB_ROWS, n_g))),
        compiler_params=pltpu.CompilerParams(dimension_semantics=("arbitrary",)),
        name="moe_dest",
    )(eidx, rank, cnt)


SC_CORES = 2
SC_SUBCORES = 16
SC_WORKERS = SC_CORES * SC_SUBCORES
SC_ROWS = 128


def _sc_mesh():
    return plsc.VectorSubcoreMesh(core_axis_name="c", subcore_axis_name="s")


def _sc_worker_id():
    return lax.axis_index("s") * SC_CORES + lax.axis_index("c")


def _sc_scatter_rows(rows, dest_flat, n_out):
    t, w = rows.shape
    n_k = dest_flat.shape[0] // t
    r = SC_GATHER_ROWS
    per = t // SC_WORKERS
    n_steps = per // r
    assert per % (2 * r) == 0

    @functools.partial(
        pl.kernel, mesh=_sc_mesh(), out_type=jax.ShapeDtypeStruct((n_out, w), rows.dtype),
        scratch_types=[pltpu.VMEM((n_k, r), jnp.int32), pltpu.VMEM((2, r, w), rows.dtype),
                       pltpu.SemaphoreType.DMA((2,)), pltpu.SemaphoreType.DMA],
        name="moe_dispatch_sc")
    def scatter(rows_hbm, idx_hbm, out_hbm, idx_v, rows_v, l_sem, s_sem):
        base = _sc_worker_id() * per

        def load(j, b):
            return pltpu.make_async_copy(rows_hbm.at[pl.ds(pl.multiple_of(base + j * r, ROW_ALIGN), r)],
                                         rows_v.at[b], l_sem.at[b])

        load(0, 0).start()

        @pl.loop(0, n_steps, step=2)
        def _(j0):
            for b in range(2):
                j = j0 + b
                pl.when(j + 1 < n_steps)(load(j + 1, 1 - b).start)
                for k in range(n_k):
                    off = pl.multiple_of(k * t + base + j * r, ROW_ALIGN)
                    pltpu.sync_copy(idx_hbm.at[pl.ds(off, r)], idx_v.at[k])
                load(j, b).wait()
                copies = [pltpu.make_async_copy(rows_v.at[b], out_hbm.at[idx_v.at[k]], s_sem) for k in range(n_k)]
                for cp in copies:
                    cp.start()
                for cp in copies:
                    cp.wait()

    return scatter(rows, dest_flat)


SC_GATHER_ROWS = 64
COMBINE_SPLITS = 4


def _sc_gather_rows(table, idx):
    n = idx.shape[0]
    w = table.shape[1]
    r = SC_GATHER_ROWS
    per = n // SC_WORKERS
    n_steps = per // r
    assert per % (2 * r) == 0

    @functools.partial(
        pl.kernel, mesh=_sc_mesh(), out_type=jax.ShapeDtypeStruct((n, w), table.dtype),
        scratch_types=[pltpu.VMEM((2, r), jnp.int32), pltpu.VMEM((2, r, w), table.dtype),
                       pltpu.SemaphoreType.DMA((2,)), pltpu.SemaphoreType.DMA((2,))],
        name="moe_gather_sc")
    def gather(table_hbm, idx_hbm, out_hbm, idx_v, rows_v, g_sem, w_sem):
        base = _sc_worker_id() * per

        def at(j):
            return pl.ds(pl.multiple_of(base + j * r, ROW_ALIGN), r)

        def fetch(j, b):
            return pltpu.make_async_copy(table_hbm.at[idx_v.at[b]], rows_v.at[b], g_sem.at[b])

        def write(j, b):
            return pltpu.make_async_copy(rows_v.at[b], out_hbm.at[at(j)], w_sem.at[b])

        pltpu.sync_copy(idx_hbm.at[at(0)], idx_v.at[0])
        fetch(0, 0).start()

        @pl.loop(0, n_steps, step=2)
        def _(j0):
            for b in range(2):
                j = j0 + b

                @pl.when(j + 1 < n_steps)
                def _():
                    pl.when(j >= 1)(write(j - 1, 1 - b).wait)
                    pltpu.sync_copy(idx_hbm.at[at(j + 1)], idx_v.at[1 - b])
                    fetch(j + 1, 1 - b).start()

                fetch(j, b).wait()
                write(j, b).start()

        write(n_steps - 2, 0).wait()
        write(n_steps - 1, 1).wait()

    return gather(table, idx)


def _pad_kernel(tab_ref, xs_in, xs_hbm, zero_ref, sem, *, n_e):
    del xs_in
    zero_ref[...] = jnp.zeros_like(zero_ref)

    def pad_row_copy(row):
        return pltpu.make_async_copy(zero_ref.at[pl.ds(0, 1), :], xs_hbm.at[pl.ds(row, 1), :], sem)

    def pad_rows(e, carry, wait):
        count = tab_ref[TAB_COUNT, e]
        end = tab_ref[TAB_START, e] + count
        n_pad = (ROW_ALIGN - count % ROW_ALIGN) % ROW_ALIGN
        for j in range(ROW_ALIGN - 1):
            copy = pad_row_copy(end + j)
            pl.when(j < n_pad)(copy.wait if wait else copy.start)
        return carry

    lax.fori_loop(0, n_e, functools.partial(pad_rows, wait=False), 0)
    lax.fori_loop(0, n_e, functools.partial(pad_rows, wait=True), 0)
    last = tab_ref[TAB_COUNT, n_e - 1]
    used = tab_ref[TAB_START, n_e - 1] + ((last + (ROW_ALIGN - 1)) // ROW_ALIGN) * ROW_ALIGN
    piece = zero_ref.shape[0]
    n_rows = xs_hbm.shape[0]
    for j in range(pl.cdiv(n_e * ROW_ALIGN + EXPERT_CHUNK + ROW_ALIGN, piece)):
        at = pl.multiple_of(jnp.minimum(used + j * piece, n_rows - piece), ROW_ALIGN)
        tail = pltpu.make_async_copy(zero_ref, xs_hbm.at[pl.ds(at, piece), :], sem)
        tail.start()
        tail.wait()


def _sorted_rows(t, n_e):
    return t * TOP_K + n_e * ROW_ALIGN + EXPERT_CHUNK + ROW_ALIGN


def _dispatch(tab, dest, hp, n_e):
    t, m = hp.shape
    n_rows = _sorted_rows(t, n_e)
    xs = _sc_scatter_rows(hp, dest.reshape(-1), n_rows)
    grid_spec = pltpu.PrefetchScalarGridSpec(
        num_scalar_prefetch=1,
        grid=(1,),
        in_specs=[pl.BlockSpec(memory_space=pl.ANY)],
        out_specs=pl.BlockSpec(memory_space=pl.ANY),
        scratch_shapes=[pltpu.VMEM((EXPERT_CHUNK, m), PACKED), pltpu.SemaphoreType.DMA(())],
    )
    return pl.pallas_call(
        functools.partial(_pad_kernel, n_e=n_e),
        out_shape=jax.ShapeDtypeStruct((n_rows, m), PACKED),
        grid_spec=grid_spec,
        input_output_aliases={1: 0},
        compiler_params=pltpu.CompilerParams(dimension_semantics=("arbitrary",)),
        name="moe_pad",
    )(tab, xs)


X_LOOKAHEAD = 2
X_SLOTS = X_LOOKAHEAD + 1
Y_SLOTS = 2


def _expert_kernel(tab_ref, xs_hbm, wg_ref, wu_ref, wd_ref, y_hbm,
                   wgu16_ref, wd16_ref, xbuf_ref, ybuf_ref, in_sem, out_sem, *, n_e):
    e = pl.program_id(0)
    ch, half = xbuf_ref.shape[1:]
    ff = wg_ref.shape[2]
    n_total = tab_ref[TAB_TOTAL, 0]
    first = tab_ref[TAB_CHUNK0, e]
    n_chunks = tab_ref[TAB_NCHUNK, e]

    def in_copy(g):
        slot = g % X_SLOTS
        row = pl.multiple_of(tab_ref[TAB_CHUNK_ROW, g], ROW_ALIGN)
        return pltpu.make_async_copy(xs_hbm.at[pl.ds(row, ch), :], xbuf_ref.at[slot], in_sem.at[slot])

    def out_copies(g):
        slot = g % Y_SLOTS
        row = tab_ref[TAB_CHUNK_ROW, g]
        valid = tab_ref[TAB_CHUNK_VALID, g]
        yield valid >= ch, pltpu.make_async_copy(ybuf_ref.at[slot], y_hbm.at[pl.ds(pl.multiple_of(row, ROW_ALIGN), ch), :],
                                                 out_sem.at[slot])
        size = ch // 2
        while size >= ROW_ALIGN:
            off = pl.multiple_of((valid // (2 * size)) * (2 * size), ROW_ALIGN)
            cond = (valid < ch) & ((valid & size) != 0)
            yield cond, pltpu.make_async_copy(ybuf_ref.at[slot, pl.ds(off, size), :],
                                              y_hbm.at[pl.ds(pl.multiple_of(row + off, ROW_ALIGN), size), :],
                                              out_sem.at[slot])
            size //= 2

    def start_out(g):
        for cond, cp in out_copies(g):
            pl.when(cond)(cp.start)

    def wait_out(g):
        for cond, cp in out_copies(g):
            pl.when(cond)(cp.wait)

    @pl.when(e == 0)
    def _():
        for j in range(X_LOOKAHEAD):
            pl.when(j < n_total)(in_copy(j).start)

    @pl.when(n_chunks > 0)
    def _():
        wgu16_ref[:, 0:ff] = wg_ref[0].astype(BF16)
        wgu16_ref[:, ff:2 * ff] = wu_ref[0].astype(BF16)
        wd16_ref[...] = wd_ref[0].astype(BF16)

    def chunk(c, carry):
        g = first + c
        pl.when(g + X_LOOKAHEAD < n_total)(in_copy(g + X_LOOKAHEAD).start)
        in_copy(g).wait()
        pl.when(g >= Y_SLOTS)(lambda: wait_out(g - Y_SLOTS))
        xa, xb = _unpack_bf16_pairs(xbuf_ref[g % X_SLOTS])
        hgu = (_dot(xa.astype(BF16), wgu16_ref[0:half, :])
               + _dot(xb.astype(BF16), wgu16_ref[half:2 * half, :]))
        act = _silu(hgu[:, :ff]) * hgu[:, ff:]
        y = _dot(act.astype(BF16), wd16_ref[...])
        ybuf_ref[g % Y_SLOTS] = _pack_bf16_pairs(y)
        start_out(g)
        return carry

    lax.fori_loop(0, n_chunks, chunk, 0)

    @pl.when(e == n_e - 1)
    def _():
        for j in range(Y_SLOTS, 0, -1):
            pl.when(n_total >= j)(lambda j=j: wait_out(n_total - j))
        n_rows = y_hbm.shape[0]
        used = tab_ref[TAB_START, e] + ((tab_ref[TAB_COUNT, e] + (ROW_ALIGN - 1)) // ROW_ALIGN) * ROW_ALIGN
        ybuf_ref[0] = jnp.zeros((ch, half), PACKED)
        for j in range(pl.cdiv(n_e * ROW_ALIGN + EXPERT_CHUNK + ROW_ALIGN, ch)):
            at = pl.multiple_of(jnp.minimum(used + j * ch, n_rows - ch), ROW_ALIGN)
            tail = pltpu.make_async_copy(ybuf_ref.at[0], y_hbm.at[pl.ds(at, ch), :], out_sem.at[0])
            tail.start()
            tail.wait()


def _experts(tab, xs, w_gate_e, w_up_e, w_down_e):
    p, half = xs.shape
    n_e, d, ff = w_gate_e.shape
    ch = EXPERT_CHUNK
    grid_spec = pltpu.PrefetchScalarGridSpec(
        num_scalar_prefetch=1,
        grid=(n_e,),
        in_specs=[pl.BlockSpec(memory_space=pl.ANY),
                  pl.BlockSpec((1, d, ff), lambda e, *_: (e, 0, 0)),
                  pl.BlockSpec((1, d, ff), lambda e, *_: (e, 0, 0)),
                  pl.BlockSpec((1, ff, d), lambda e, *_: (e, 0, 0))],
        out_specs=pl.BlockSpec(memory_space=pl.ANY),
        scratch_shapes=[pltpu.VMEM((d, 2 * ff), BF16), pltpu.VMEM((ff, d), BF16),
                        pltpu.VMEM((X_SLOTS, ch, half), PACKED), pltpu.VMEM((Y_SLOTS, ch, half), PACKED),
                        pltpu.SemaphoreType.DMA((X_SLOTS,)), pltpu.SemaphoreType.DMA((Y_SLOTS,))],
    )
    return pl.pallas_call(
        functools.partial(_expert_kernel, n_e=n_e),
        out_shape=jax.ShapeDtypeStruct((p, half), PACKED),
        grid_spec=grid_spec,
        compiler_params=pltpu.CompilerParams(dimension_semantics=("arbitrary",),
                                             vmem_limit_bytes=VMEM_LIMIT_BYTES),
        name="moe_experts",
    )(tab, xs, w_gate_e, w_up_e, w_down_e)


def _combine_kernel(yg_ref, wcol_ref, x1_ref, hp_ref, mod_ref, npost_ref, wgus_ref, wds_ref, o_ref):
    te, d = x1_ref.shape
    half = d // 2
    ff = wds_ref.shape[0]

    xa, xb = _unpack_bf16_pairs(hp_ref[...])
    hgu = _dot(xa.astype(BF16), wgus_ref[0:half, :]) + _dot(xb.astype(BF16), wgus_ref[half:d, :])
    act = _silu(hgu[:, :ff]) * hgu[:, ff:]
    shared = _dot(act.astype(BF16), wds_ref[...])

    acc_a = shared[:, :half]
    acc_b = shared[:, half:]
    for k in range(TOP_K):
        ya, yb = _unpack_bf16_pairs(yg_ref[k])
        wk = wcol_ref[:, k:k + 1]
        acc_a = acc_a + wk * ya
        acc_b = acc_b + wk * yb
    moe = jnp.concatenate([acc_a, acc_b], axis=-1)
    g2 = mod_ref[:, 5 * d:6 * d]
    o_ref[...] = x1_ref[...] + g2 * _rms(moe, npost_ref[...])


def _combine(dest, y, wcol, x1, hp, mod, npost2, w_gate_s, w_up_s, w_down_s):
    t, d = x1.shape
    half = d // 2
    te = SEQ_TILE
    wgus = jnp.concatenate([w_gate_s, w_up_s], axis=1).astype(BF16)
    wds = w_down_s.astype(BF16)
    n_split = COMBINE_SPLITS if t % (COMBINE_SPLITS * 2 * SC_GATHER_ROWS * SC_WORKERS // TOP_K) == 0 else 1
    tq = t // n_split
    steps = tq // te
    out = None
    for q in range(n_split):
        dest_q = dest[:, q * tq:(q + 1) * tq]
        yg = _sc_gather_rows(y, dest_q.reshape(-1)).reshape(TOP_K, tq, half)
        row = lambda i, q=q: (q * steps + i, 0)
        in_specs = [pl.BlockSpec((TOP_K, te, half), lambda i: (0, i, 0)),
                    pl.BlockSpec((te, 128), row),
                    pl.BlockSpec((te, d), row),
                    pl.BlockSpec((te, half), row),
                    _const_spec(mod.shape),
                    _const_spec((1, d)),
                    _const_spec(wgus.shape),
                    _const_spec(wds.shape)]
        args = [yg, wcol, x1, hp, mod, npost2.reshape(1, d), wgus, wds]
        body = _combine_kernel
        aliases = {}
        if out is not None:
            in_specs.append(pl.BlockSpec(memory_space=pl.ANY))
            args.append(out)
            body = lambda *refs: _combine_kernel(*refs[:8], refs[9])
            aliases = {8: 0}
        out = pl.pallas_call(
            body,
            out_shape=jax.ShapeDtypeStruct((t, d), F32),
            grid=(steps,),
            in_specs=in_specs,
            out_specs=pl.BlockSpec((te, d), row),
            input_output_aliases=aliases,
            compiler_params=pltpu.CompilerParams(dimension_semantics=("arbitrary",),
                                                 vmem_limit_bytes=VMEM_LIMIT_BYTES),
            name="moe_combine",
        )(*args)
    return out


def kernel(x, c, w_ada, b_ada, norm_pre_mix, norm_post_mix, w_in, conv_w, w_conv_out, hgrn_lower_bounds, hgrn_norm_w, w_hgrn_out, w_o, norm_pre_ffn, norm_post_ffn, w_router, router_bias, w_gate_e, w_up_e, w_down_e, w_gate_s, w_up_s, w_down_s):
    bsz, seq, d = x.shape
    assert bsz == 1 and w_ada.shape[0] == 1
    mod = _ada_mod(c, w_ada[0], b_ada[0])
    x1 = _mixer(x.reshape(seq, d), mod, norm_pre_mix[0], norm_post_mix[0], w_in[0], conv_w[0],
                w_conv_out[0], hgrn_lower_bounds, hgrn_norm_w[0], w_hgrn_out[0], w_o[0])
    hp, eidx, rank, wcol, cnt = _router(x1, mod, norm_pre_ffn[0], w_router[0], router_bias[0])
    dest, tab = _dest(eidx, rank, cnt)
    xs = _dispatch(tab, dest, hp, w_router.shape[-1])
    y = _experts(tab, xs, w_gate_e[0], w_up_e[0], w_down_e[0])
    out = _combine(dest, y, wcol, x1, hp, mod, norm_post_ffn[0], w_gate_s[0], w_up_s[0], w_down_s[0])
    return out.reshape(bsz, seq, d)
```

```python
import functools

import jax
import jax.numpy as jnp
from jax import lax
from jax.experimental import pallas as pl
from jax.experimental.pallas import tpu as pltpu
from jax.experimental.pallas import tpu_sc as plsc

F32 = jnp.float32
BF16 = jnp.bfloat16

NORM_EPS = 1e-6
CONV_DIM = 512
CONV_WIDTH = 3
HGRN_HEADS = 4
HGRN_DK = 128
HGRN_QK = HGRN_HEADS * HGRN_DK
N_GROUPS = 8
TOPK_GROUPS = 4
TOP_K = 8
ROUTED_SCALE = 2.5

SEQ_TILE = 256
EXPERT_CHUNK = 512
EXPERT_ROW_STEPS = (128, 256, EXPERT_CHUNK)
ROW_ALIGN = 8
VMEM_LIMIT_BYTES = 56 * 1024 * 1024


def _dot(a, b):
    return jnp.dot(a, b, preferred_element_type=F32)


def _dot_nt(a, b):
    return lax.dot_general(a, b, (((1,), (1,)), ((), ())), preferred_element_type=F32)


def _dot_tn(a, b):
    return lax.dot_general(a, b, (((0,), (0,)), ((), ())), preferred_element_type=F32)


def _split3(x):
    hi = x.astype(BF16)
    r1 = x - hi.astype(F32)
    mid = r1.astype(BF16)
    lo = (r1 - mid.astype(F32)).astype(BF16)
    return hi, mid, lo


def _sigmoid(x):
    return 1.0 / (1.0 + jnp.exp(-x))


def _silu(x):
    return x * _sigmoid(x)


def _rms(x, w):
    ms = jnp.mean(x * x, axis=-1, keepdims=True)
    return x * lax.rsqrt(ms + NORM_EPS) * w


def _ada_kernel(c_ref, w_ref, b_ref, o_ref):
    cs = _silu(c_ref[...])
    h1, h2, h3 = _split3(cs)
    w1, w2, w3 = _split3(w_ref[...])
    acc = _dot(h1, w1) + (_dot(h1, w2) + _dot(h2, w1)) + (_dot(h1, w3) + _dot(h2, w2) + _dot(h3, w1))
    o_ref[...] = acc + b_ref[...]


def _ada_mod(c, w_ada, b_ada):
    d = c.shape[-1]
    n = w_ada.shape[-1]
    bn = 1024
    c8 = jnp.broadcast_to(c.reshape(1, d), (8, d))
    out = pl.pallas_call(
        _ada_kernel,
        out_shape=jax.ShapeDtypeStruct((8, n), F32),
        grid=(n // bn,),
        in_specs=[pl.BlockSpec((8, d), lambda j: (0, 0)),
                  pl.BlockSpec((d, bn), lambda j: (0, j)),
                  pl.BlockSpec((1, bn), lambda j: (0, j))],
        out_specs=pl.BlockSpec((8, bn), lambda j: (0, j)),
        compiler_params=pltpu.CompilerParams(dimension_semantics=("arbitrary",),
                                             vmem_limit_bytes=VMEM_LIMIT_BYTES),
        name="ada_mod",
    )(c8, w_ada, b_ada.reshape(1, n))
    return out[0:1]


def _level_reference(b, rolled, s, row):
    ts = b.shape[0]
    c = s // 2 - 1
    if s >= 16:
        pieces = []
        for blk in range(ts // s):
            r = blk * s + c
            pieces.append(jnp.broadcast_to(b[r:r + 1, :], (s, b.shape[1])))
        return pieces[0] if len(pieces) == 1 else jnp.concatenate(pieces, axis=0)
    pos = row & (s - 1)
    out = b
    for p in range(s):
        if p == c:
            continue
        out = jnp.where(pos == p, rolled[p - c], out)
    return out


def _cast_expert_weights(wg_ref, wu_ref, wd_ref, wgu16_ref, wd16_ref):
    ff = wg_ref.shape[2]
    for j in range(wg_ref.shape[0]):
        wgu16_ref[j, :, 0:ff] = wg_ref[j].astype(BF16)
        wgu16_ref[j, :, ff:2 * ff] = wu_ref[j].astype(BF16)
        wd16_ref[j] = wd_ref[j].astype(BF16)


def _mixer_kernel(x_ref, mod_ref, npre_ref, npost_ref, win_ref, convw_ref, wco_ref,
                  hlb_ref, hnw_ref, who_ref, wo_ref, wg_ref, wu_ref, wd_ref,
                  o_ref, wgu16_ref, wd16_ref, state_ref, ubuf_ref):
    ts, d = x_ref.shape
    step = pl.program_id(0)
    _cast_expert_weights(wg_ref, wu_ref, wd_ref, wgu16_ref, wd16_ref)

    @pl.when(step == 0)
    def _():
        state_ref[...] = jnp.zeros_like(state_ref)
        ubuf_ref[0:8, :] = jnp.zeros((8, CONV_DIM), F32)

    x = x_ref[...]
    sh1 = mod_ref[:, 0:d]
    sc1 = mod_ref[:, d:2 * d]
    g1 = mod_ref[:, 2 * d:3 * d]
    h = (_rms(x, npre_ref[...]) * (1.0 + sc1) + sh1).astype(BF16)

    def proj(lo, width):
        return _dot(h, win_ref[:, lo:lo + width])

    c0 = 0
    cb = proj(c0, CONV_DIM)
    u = proj(c0 + CONV_DIM, CONV_DIM) * proj(c0 + 2 * CONV_DIM, CONV_DIM)
    ubuf_ref[8:8 + ts, :] = u
    conv = (ubuf_ref[6:6 + ts, :] * convw_ref[0:1, :]
            + ubuf_ref[7:7 + ts, :] * convw_ref[1:2, :]
            + u * convw_ref[2:3, :])
    ubuf_ref[0:8, :] = ubuf_ref[ts:ts + 8, :]
    y_a = _dot((cb * conv).astype(BF16), wco_ref[...])

    c1 = 3 * CONV_DIM
    q = _silu(proj(c1, HGRN_QK))
    hl = hlb_ref[...]
    hm = jnp.max(hl, axis=0, keepdims=True)
    he = jnp.exp(hl - hm)
    lb = he[0:1, :] / jnp.sum(he, axis=0, keepdims=True)
    fg = lb + (1.0 - lb) * _sigmoid(proj(c1 + HGRN_QK, HGRN_QK))
    k = 1.0 - fg
    g = jnp.log(fg)
    v = proj(c1 + 2 * HGRN_QK, HGRN_QK)
    gg = proj(c1 + 3 * HGRN_QK, HGRN_QK)

    row = lax.broadcasted_iota(jnp.int32, (ts, ts), 0)
    col = lax.broadcasted_iota(jnp.int32, (ts, ts), 1)
    tril = jnp.where(row >= col, 1.0, 0.0).astype(BF16)
    ghi, gmid, glo = _split3(g)
    b = _dot(tril, ghi) + _dot(tril, gmid) + _dot(tril, glo)
    b_last = b[ts - 1:ts, :]

    rowq = lax.broadcasted_iota(jnp.int32, (ts, HGRN_QK), 0)
    rolled = {sft: pltpu.roll(b, sft % ts, axis=0) for sft in (-3, -2, -1, 1, 2, 3, 4)}
    xor = row ^ col

    levels = []
    s = ts
    while s >= 2:
        levels.append(s)
        s //= 2
    qt, kt = [], []
    for s in levels:
        bref = _level_reference(b, rolled, s, rowq)
        e = jnp.exp(-jnp.abs(b - bref))
        upper = (rowq & (s // 2)) != 0
        qt.append(jnp.where(upper, q * e, 0.0).astype(BF16))
        kt.append(jnp.where(upper, 0.0, k * e).astype(BF16))
    q_in = (q * jnp.exp(b)).astype(BF16)
    k_out = (k * jnp.exp(b_last - b)).astype(BF16)
    v16 = v.astype(BF16)
    qk = q * k
    decay_last = jnp.exp(b_last)

    hnw = hnw_ref[...]
    outs = []
    for hd in range(HGRN_HEADS):
        sl = slice(hd * HGRN_DK, (hd + 1) * HGRN_DK)
        a = jnp.zeros((ts, ts), F32)
        for li in range(len(levels) - 1, -1, -1):
            s = levels[li]
            p = _dot_nt(qt[li][:, sl], kt[li][:, sl])
            a = jnp.where(xor >= s // 2, p, a)
        st = state_ref[hd]
        o_h = (_dot(a.astype(BF16), v16[:, sl])
               + jnp.sum(qk[:, sl], axis=-1, keepdims=True) * v[:, sl]
               + _dot_nt(q_in[:, sl], st.astype(BF16)))
        state_ref[hd] = st * decay_last[:, sl] + _dot_tn(v16[:, sl], k_out[:, sl])
        outs.append(_rms(o_h, hnw))
    o = jnp.concatenate(outs, axis=-1) * _silu(gg)
    y_b = _dot(o.astype(BF16), who_ref[...])

    c2 = c1 + 4 * HGRN_QK
    m = _sigmoid(proj(c2, d)) * y_a + _sigmoid(proj(c2 + d, d)) * y_b
    y = _dot(m.astype(BF16), wo_ref[...])
    o_ref[...] = x + g1 * _rms(y, npost_ref[...])


def _const_spec(shape):
    nd = len(shape)
    return pl.BlockSpec(shape, lambda i: (0,) * nd)


def _expert_cast_specs(w_gate_e, n_steps, first_step):
    n_e, d, ff = w_gate_e.shape
    assert n_e % (2 * n_steps) == 0
    per = n_e // (2 * n_steps)
    at = lambda i, *_: (first_step + i, 0, 0)
    ins = [pl.BlockSpec((per, d, ff), at), pl.BlockSpec((per, d, ff), at), pl.BlockSpec((per, ff, d), at)]
    outs = [pl.BlockSpec((per, d, 2 * ff), at), pl.BlockSpec((per, ff, d), at)]
    shapes = [jax.ShapeDtypeStruct((n_e, d, 2 * ff), BF16), jax.ShapeDtypeStruct((n_e, ff, d), BF16)]
    return ins, outs, shapes


def _mixer(x2, mod, npre, npost, w_in, conv_w, w_conv_out, hlb, hnw, w_hgrn_out, w_o,
           w_gate_e, w_up_e, w_down_e):
    t, d = x2.shape
    ts = SEQ_TILE
    n = t // ts
    args = (x2, mod, npre.reshape(1, d), npost.reshape(1, d), w_in.astype(BF16), conv_w,
            w_conv_out.astype(BF16), hlb, hnw.reshape(1, -1), w_hgrn_out.astype(BF16),
            w_o.astype(BF16))
    cast_in, cast_out, cast_shapes = _expert_cast_specs(w_gate_e, n, 0)
    in_specs = ([pl.BlockSpec((ts, d), lambda i: (i, 0))] + [_const_spec(a.shape) for a in args[1:]]
                + cast_in)
    return pl.pallas_call(
        _mixer_kernel,
        out_shape=[jax.ShapeDtypeStruct((t, d), F32)] + cast_shapes,
        grid=(n,),
        in_specs=in_specs,
        out_specs=[pl.BlockSpec((ts, d), lambda i: (i, 0))] + cast_out,
        scratch_shapes=[pltpu.VMEM((HGRN_HEADS, HGRN_DK, HGRN_DK), F32),
                        pltpu.VMEM((ts + 8, CONV_DIM), F32)],
        compiler_params=pltpu.CompilerParams(dimension_semantics=("arbitrary",),
                                             vmem_limit_bytes=VMEM_LIMIT_BYTES),
        name="token_mixer",
    )(*args, w_gate_e, w_up_e, w_down_e)


PACKED = jnp.int32


def _pack_bf16_pairs(x):
    m = x.shape[1] // 2
    hi = lax.bitcast_convert_type(x[:, :m].astype(BF16).astype(F32), jnp.uint32)
    lo = lax.bitcast_convert_type(x[:, m:].astype(BF16).astype(F32), jnp.uint32)
    return lax.bitcast_convert_type(hi | (lo >> 16), PACKED)


def _unpack_bf16_pairs(p):
    p = lax.bitcast_convert_type(p, jnp.uint32)
    hi = lax.bitcast_convert_type(p & jnp.uint32(0xFFFF0000), F32)
    lo = lax.bitcast_convert_type(p << 16, F32)
    return hi, lo


def _router_kernel(x1_ref, mod_ref, npre_ref, wrh_ref, wrl_ref, rb_ref, wg_ref, wu_ref, wd_ref,
                   wgu16_in, wd16_in,
                   hp_ref, eidx_ref, rank_ref, wcol_ref, cnt_ref, wgu16_ref, wd16_ref, carry_ref, rows_ref):
    del wgu16_in, wd16_in
    assert N_GROUPS == TOP_K == rows_ref.shape[0]
    _cast_expert_weights(wg_ref, wu_ref, wd_ref, wgu16_ref, wd16_ref)
    tr, d = x1_ref.shape
    n_e = wrh_ref.shape[0]
    gsz = n_e // N_GROUPS
    neg = -jnp.inf

    @pl.when(pl.program_id(0) == 0)
    def _():
        carry_ref[...] = jnp.zeros_like(carry_ref)

    sh2 = mod_ref[:, 3 * d:4 * d]
    sc2 = mod_ref[:, 4 * d:5 * d]
    h2 = _rms(x1_ref[...], npre_ref[...]) * (1.0 + sc2) + sh2
    hp_ref[...] = _pack_bf16_pairs(h2)
    h_hi = h2.astype(BF16)
    h_lo = (h2 - h_hi.astype(F32)).astype(BF16)
    wrh = wrh_ref[...]
    logits = _dot_nt(wrh, h_hi) + (_dot_nt(wrh, h_lo) + _dot_nt(wrl_ref[...], h_hi))
    scores = _sigmoid(logits)
    sel = scores + rb_ref[:, 0:1]

    io_e = lax.broadcasted_iota(jnp.int32, (n_e, tr), 0)
    for g in range(N_GROUPS):
        blk = sel[g * gsz:(g + 1) * gsz, :]
        io = lax.broadcasted_iota(jnp.int32, (gsz, tr), 0) + g * gsz
        m1 = jnp.max(blk, axis=0, keepdims=True)
        i1 = jnp.min(jnp.where(blk == m1, io, n_e), axis=0, keepdims=True)
        m2 = jnp.max(jnp.where(io == i1, neg, blk), axis=0, keepdims=True)
        rows_ref[g:g + 1, :] = m1 + m2
    gs = rows_ref[...]
    io_g = lax.broadcasted_iota(jnp.int32, (N_GROUPS, tr), 0)
    gsel = jnp.zeros((N_GROUPS, tr), F32)
    for _ in range(TOPK_GROUPS):
        m = jnp.max(gs, axis=0, keepdims=True)
        gi = jnp.min(jnp.where(gs == m, io_g, N_GROUPS), axis=0, keepdims=True)
        hit = io_g == gi
        gsel = jnp.where(hit, 1.0, gsel)
        gs = jnp.where(hit, neg, gs)
    rows_ref[...] = gsel
    cur = jnp.concatenate(
        [jnp.where(rows_ref[g:g + 1, :] > 0.5, sel[g * gsz:(g + 1) * gsz, :], neg) for g in range(N_GROUPS)],
        axis=0)

    idxs = []
    selm = jnp.zeros((n_e, tr), F32)
    for k in range(TOP_K):
        m = jnp.max(cur, axis=0, keepdims=True)
        idx = jnp.min(jnp.where(cur == m, io_e, n_e), axis=0, keepdims=True)
        hit = io_e == idx
        rows_ref[k:k + 1, :] = jnp.sum(jnp.where(hit, scores, 0.0), axis=0, keepdims=True)
        cur = jnp.where(hit, neg, cur)
        selm = jnp.where(hit, 1.0, selm)
        eidx_ref[k:k + 1, :] = idx
        idxs.append(idx)

    r_io = lax.broadcasted_iota(jnp.int32, (tr, tr + 128), 0)
    c_io = lax.broadcasted_iota(jnp.int32, (tr, tr + 128), 1)
    before = jnp.where((r_io < c_io) | (c_io >= tr), 1.0, 0.0).astype(BF16)
    r_ext = _dot(selm.astype(BF16), before)
    rank_full = r_ext[:, :tr] + carry_ref[:, 0:1]
    for k, idx in enumerate(idxs):
        rk = jnp.sum(jnp.where(io_e == idx, rank_full, 0.0), axis=0, keepdims=True)
        rank_ref[k:k + 1, :] = rk.astype(jnp.int32)
    carry_ref[...] = carry_ref[...] + r_ext[:, tr:]
    cnt_ref[...] = carry_ref[...]

    wk = rows_ref[...]
    wn = wk / (jnp.sum(wk, axis=0, keepdims=True) + 1e-20) * ROUTED_SCALE
    eye = jnp.where(lax.broadcasted_iota(jnp.int32, (TOP_K, 128), 0)
                    == lax.broadcasted_iota(jnp.int32, (TOP_K, 128), 1), 1.0, 0.0).astype(BF16)
    w1, w2, w3 = _split3(wn)
    wcol_ref[...] = _dot_tn(w1, eye) + _dot_tn(w2, eye) + _dot_tn(w3, eye)


def _router(x1, mod, npre2, w_router, router_bias, w_gate_e, w_up_e, w_down_e, wgu16, wd16):
    t, d = x1.shape
    n_e = w_router.shape[1]
    tr = SEQ_TILE
    n = t // tr
    wrt = w_router.T
    wrh = wrt.astype(BF16)
    wrl = (wrt - wrh.astype(F32)).astype(BF16)
    rb = jnp.broadcast_to(router_bias.reshape(n_e, 1), (n_e, 128))
    args = (x1, mod, npre2.reshape(1, d), wrh, wrl, rb)
    cast_in, cast_out, cast_shapes = _expert_cast_specs(w_gate_e, n, n)
    in_specs = ([pl.BlockSpec((tr, d), lambda i: (i, 0))] + [_const_spec(a.shape) for a in args[1:]]
                + cast_in + [pl.BlockSpec(memory_space=pl.ANY)] * 2)
    n_in = len(in_specs)
    return pl.pallas_call(
        _router_kernel,
        out_shape=[jax.ShapeDtypeStruct((t, d // 2), PACKED),
                   jax.ShapeDtypeStruct((TOP_K, t), jnp.int32),
                   jax.ShapeDtypeStruct((TOP_K, t), jnp.int32),
                   jax.ShapeDtypeStruct((t, 128), F32),
                   jax.ShapeDtypeStruct((n_e, 128), F32)] + cast_shapes,
        grid=(n,),
        in_specs=in_specs,
        out_specs=[pl.BlockSpec((tr, d // 2), lambda i: (i, 0)),
                   pl.BlockSpec((TOP_K, tr), lambda i: (0, i)),
                   pl.BlockSpec((TOP_K, tr), lambda i: (0, i)),
                   pl.BlockSpec((tr, 128), lambda i: (i, 0)),
                   pl.BlockSpec((n_e, 128), lambda i: (0, 0))] + cast_out,
        input_output_aliases={n_in - 2: 5, n_in - 1: 6},
        scratch_shapes=[pltpu.VMEM((n_e, 128), F32), pltpu.VMEM((TOP_K, tr), F32)],
        compiler_params=pltpu.CompilerParams(dimension_semantics=("arbitrary",),
                                             vmem_limit_bytes=VMEM_LIMIT_BYTES),
        name="moe_router",
    )(*args, w_gate_e, w_up_e, w_down_e, wgu16, wd16)


TAB_CHUNK_ROW, TAB_CHUNK_VALID, TAB_START, TAB_COUNT, TAB_CHUNK0, TAB_NCHUNK, TAB_TOTAL = range(7)
TAB_ROWS = 8


def _excl_cumsum_rows(lower, x):
    hi = jnp.floor(x * (1.0 / 128.0))
    lo = x - hi * 128.0
    return 128.0 * _dot(lower, hi.astype(BF16)) + _dot(lower, lo.astype(BF16))


def _dest_kernel(eidx_ref, rank_ref, cnt_ref, dest_ref, tab_ref):
    n_e = cnt_ref.shape[0]
    tt = eidx_ref.shape[1]
    n_g = tab_ref.shape[1]
    ch = float(EXPERT_CHUNK)
    cnt = jnp.floor((cnt_ref[...] + (ROW_ALIGN - 1)) * (1.0 / ROW_ALIGN)) * ROW_ALIGN
    r = lax.broadcasted_iota(jnp.int32, (n_e, n_e), 0)
    c = lax.broadcasted_iota(jnp.int32, (n_e, n_e), 1)
    lower = jnp.where(c < r, 1.0, 0.0).astype(BF16)
    starts = _excl_cumsum_rows(lower, cnt)
    start_col = starts[:, 0:1]

    @pl.when(pl.program_id(0) == 0)
    def _():
        nch = jnp.floor((cnt + (ch - 1.0)) * (1.0 / ch))
        chunk0 = _excl_cumsum_rows(lower, nch)
        cend_col = (chunk0 + nch)[:, 0:1]
        eye = r == c

        def as_row(col):
            return jnp.sum(jnp.where(eye, col, 0.0), axis=0, keepdims=True)

        g = lax.broadcasted_iota(jnp.int32, (n_e, n_g), 1).astype(F32)
        owner = jnp.sum(jnp.where(cend_col <= g, 1.0, 0.0), axis=0, keepdims=True)
        mine = lax.broadcasted_iota(jnp.int32, (n_e, n_g), 0).astype(F32) == owner
        g_row = lax.broadcasted_iota(jnp.int32, (1, n_g), 1).astype(F32)
        base = jnp.sum(jnp.where(mine, start_col - chunk0[:, 0:1] * ch, 0.0), axis=0, keepdims=True)
        left = jnp.sum(jnp.where(mine, cnt[:, 0:1] + chunk0[:, 0:1] * ch, 0.0), axis=0, keepdims=True)
        pad = jnp.zeros((1, n_g - n_e), F32)

        def wide(row):
            return jnp.concatenate([row, pad], axis=1)

        total = jnp.sum(nch[:, 0:1], axis=0, keepdims=True)
        tab_ref[...] = jnp.zeros(tab_ref.shape, jnp.int32)
        tab_ref[TAB_CHUNK_ROW:TAB_CHUNK_ROW + 1, :] = (base + g_row * ch).astype(jnp.int32)
        tab_ref[TAB_CHUNK_VALID:TAB_CHUNK_VALID + 1, :] = jnp.clip(left - g_row * ch, 0.0, ch).astype(jnp.int32)
        tab_ref[TAB_START:TAB_START + 1, :] = wide(as_row(start_col)).astype(jnp.int32)
        tab_ref[TAB_COUNT:TAB_COUNT + 1, :] = wide(as_row(cnt_ref[:, 0:1])).astype(jnp.int32)
        tab_ref[TAB_CHUNK0:TAB_CHUNK0 + 1, :] = wide(as_row(chunk0[:, 0:1])).astype(jnp.int32)
        tab_ref[TAB_NCHUNK:TAB_NCHUNK + 1, :] = wide(as_row(nch[:, 0:1])).astype(jnp.int32)
        tab_ref[TAB_TOTAL:TAB_TOTAL + 1, :] = jnp.broadcast_to(total, (1, n_g)).astype(jnp.int32)

    io_e = lax.broadcasted_iota(jnp.int32, (n_e, tt), 0)
    rows = []
    for k in range(TOP_K):
        hit = io_e == eidx_ref[k:k + 1, :]
        rows.append(jnp.sum(jnp.where(hit, start_col, 0.0), axis=0, keepdims=True))
    dest_ref[...] = jnp.concatenate(rows, axis=0).astype(jnp.int32) + rank_ref[...]


def _dest(eidx, rank, cnt):
    k, t = eidx.shape
    n_e = cnt.shape[0]
    tt = 512
    n_g = (t * k) // EXPERT_CHUNK + n_e
    return pl.pallas_call(
        _dest_kernel,
        out_shape=(jax.ShapeDtypeStruct((k, t), jnp.int32),
                   jax.ShapeDtypeStruct((TAB_ROWS, n_g), jnp.int32)),
        grid=(t // tt,),
        in_specs=[pl.BlockSpec((k, tt), lambda i: (0, i)),
                  pl.BlockSpec((k, tt), lambda i: (0, i)),
                  _const_spec(cnt.shape)],
        out_specs=(pl.BlockSpec((k, tt), lambda i: (0, i)), _const_spec((TAB_ROWS, n_g))),
        compiler_params=pltpu.CompilerParams(dimension_semantics=("arbitrary",)),
        name="moe_dest",
    )(eidx, rank, cnt)


SC_CORES = 2
SC_SUBCORES = 16
SC_WORKERS = SC_CORES * SC_SUBCORES
SC_ROWS = 128


def _sc_mesh():
    return plsc.VectorSubcoreMesh(core_axis_name="c", subcore_axis_name="s")


def _sc_worker_id():
    return lax.axis_index("s") * SC_CORES + lax.axis_index("c")


def _sc_scatter_rows(rows, dest_flat, n_out):
    t, w = rows.shape
    n_k = dest_flat.shape[0] // t
    r = SC_GATHER_ROWS
    per = t // SC_WORKERS
    n_steps = per // r
    assert per % (2 * r) == 0

    @functools.partial(
        pl.kernel, mesh=_sc_mesh(), out_type=jax.ShapeDtypeStruct((n_out, w), rows.dtype),
        scratch_types=[pltpu.VMEM((n_k, r), jnp.int32), pltpu.VMEM((2, r, w), rows.dtype),
                       pltpu.SemaphoreType.DMA((2,)), pltpu.SemaphoreType.DMA],
        name="moe_dispatch_sc")
    def scatter(rows_hbm, idx_hbm, out_hbm, idx_v, rows_v, l_sem, s_sem):
        base = _sc_worker_id() * per

        def load(j, b):
            return pltpu.make_async_copy(rows_hbm.at[pl.ds(pl.multiple_of(base + j * r, ROW_ALIGN), r)],
                                         rows_v.at[b], l_sem.at[b])

        load(0, 0).start()

        @pl.loop(0, n_steps, step=2)
        def _(j0):
            for b in range(2):
                j = j0 + b
                pl.when(j + 1 < n_steps)(load(j + 1, 1 - b).start)
                for k in range(n_k):
                    off = pl.multiple_of(k * t + base + j * r, ROW_ALIGN)
                    pltpu.sync_copy(idx_hbm.at[pl.ds(off, r)], idx_v.at[k])
                load(j, b).wait()
                copies = [pltpu.make_async_copy(rows_v.at[b], out_hbm.at[idx_v.at[k]], s_sem) for k in range(n_k)]
                for cp in copies:
                    cp.start()
                for cp in copies:
                    cp.wait()

    return scatter(rows, dest_flat)


SC_GATHER_ROWS = 64
COMBINE_SPLITS = 4


def _sc_gather_rows(table, idx):
    n = idx.shape[0]
    w = table.shape[1]
    r = SC_GATHER_ROWS
    per = n // SC_WORKERS
    n_steps = per // r
    assert per % (2 * r) == 0

    @functools.partial(
        pl.kernel, mesh=_sc_mesh(), out_type=jax.ShapeDtypeStruct((n, w), table.dtype),
        scratch_types=[pltpu.VMEM((2, r), jnp.int32), pltpu.VMEM((2, r, w), table.dtype),
                       pltpu.SemaphoreType.DMA((2,)), pltpu.SemaphoreType.DMA((2,))],
        name="moe_gather_sc")
    def gather(table_hbm, idx_hbm, out_hbm, idx_v, rows_v, g_sem, w_sem):
        base = _sc_worker_id() * per

        def at(j):
            return pl.ds(pl.multiple_of(base + j * r, ROW_ALIGN), r)

        def fetch(j, b):
            return pltpu.make_async_copy(table_hbm.at[idx_v.at[b]], rows_v.at[b], g_sem.at[b])

        def write(j, b):
            return pltpu.make_async_copy(rows_v.at[b], out_hbm.at[at(j)], w_sem.at[b])

        pltpu.sync_copy(idx_hbm.at[at(0)], idx_v.at[0])
        fetch(0, 0).start()

        @pl.loop(0, n_steps, step=2)
        def _(j0):
            for b in range(2):
                j = j0 + b

                @pl.when(j + 1 < n_steps)
                def _():
                    pl.when(j >= 1)(write(j - 1, 1 - b).wait)
                    pltpu.sync_copy(idx_hbm.at[at(j + 1)], idx_v.at[1 - b])
                    fetch(j + 1, 1 - b).start()

                fetch(j, b).wait()
                write(j, b).start()

        write(n_steps - 2, 0).wait()
        write(n_steps - 1, 1).wait()

    return gather(table, idx)


def _pad_kernel(tab_ref, xs_in, xs_hbm, zero_ref, sem, *, n_e):
    del xs_in
    zero_ref[...] = jnp.zeros_like(zero_ref)

    def pad_row_copy(row):
        return pltpu.make_async_copy(zero_ref.at[pl.ds(0, 1), :], xs_hbm.at[pl.ds(row, 1), :], sem)

    def pad_rows(e, carry, wait):
        count = tab_ref[TAB_COUNT, e]
        end = tab_ref[TAB_START, e] + count
        n_pad = (ROW_ALIGN - count % ROW_ALIGN) % ROW_ALIGN
        for j in range(ROW_ALIGN - 1):
            copy = pad_row_copy(end + j)
            pl.when(j < n_pad)(copy.wait if wait else copy.start)
        return carry

    lax.fori_loop(0, n_e, functools.partial(pad_rows, wait=False), 0)
    lax.fori_loop(0, n_e, functools.partial(pad_rows, wait=True), 0)
    last = tab_ref[TAB_COUNT, n_e - 1]
    used = tab_ref[TAB_START, n_e - 1] + ((last + (ROW_ALIGN - 1)) // ROW_ALIGN) * ROW_ALIGN
    piece = zero_ref.shape[0]
    n_rows = xs_hbm.shape[0]
    for j in range(pl.cdiv(n_e * ROW_ALIGN + EXPERT_CHUNK + ROW_ALIGN, piece)):
        at = pl.multiple_of(jnp.minimum(used + j * piece, n_rows - piece), ROW_ALIGN)
        tail = pltpu.make_async_copy(zero_ref, xs_hbm.at[pl.ds(at, piece), :], sem)
        tail.start()
        tail.wait()


def _sorted_rows(t, n_e):
    return t * TOP_K + n_e * ROW_ALIGN + EXPERT_CHUNK + ROW_ALIGN


def _dispatch(tab, dest, hp, n_e):
    t, m = hp.shape
    n_rows = _sorted_rows(t, n_e)
    xs = _sc_scatter_rows(hp, dest.reshape(-1), n_rows)
    grid_spec = pltpu.PrefetchScalarGridSpec(
        num_scalar_prefetch=1,
        grid=(1,),
        in_specs=[pl.BlockSpec(memory_space=pl.ANY)],
        out_specs=pl.BlockSpec(memory_space=pl.ANY),
        scratch_shapes=[pltpu.VMEM((EXPERT_CHUNK, m), PACKED), pltpu.SemaphoreType.DMA(())],
    )
    return pl.pallas_call(
        functools.partial(_pad_kernel, n_e=n_e),
        out_shape=jax.ShapeDtypeStruct((n_rows, m), PACKED),
        grid_spec=grid_spec,
        input_output_aliases={1: 0},
        compiler_params=pltpu.CompilerParams(dimension_semantics=("arbitrary",)),
        name="moe_pad",
    )(tab, xs)


X_LOOKAHEAD = 2
X_SLOTS = X_LOOKAHEAD + 1
Y_SLOTS = 2


def _expert_kernel(tab_ref, xs_hbm, wgu_ref, wd_ref, y_hbm,
                   xbuf_ref, ybuf_ref, in_sem, out_sem, *, n_e):
    e = pl.program_id(0)
    ch, half = xbuf_ref.shape[1:]
    ff = wd_ref.shape[1]
    n_total = tab_ref[TAB_TOTAL, 0]
    first = tab_ref[TAB_CHUNK0, e]
    n_chunks = tab_ref[TAB_NCHUNK, e]

    def in_copy(g):
        slot = g % X_SLOTS
        row = pl.multiple_of(tab_ref[TAB_CHUNK_ROW, g], ROW_ALIGN)
        return pltpu.make_async_copy(xs_hbm.at[pl.ds(row, ch), :], xbuf_ref.at[slot], in_sem.at[slot])

    def out_copies(g):
        slot = g % Y_SLOTS
        row = tab_ref[TAB_CHUNK_ROW, g]
        valid = tab_ref[TAB_CHUNK_VALID, g]
        yield valid >= ch, pltpu.make_async_copy(ybuf_ref.at[slot], y_hbm.at[pl.ds(pl.multiple_of(row, ROW_ALIGN), ch), :],
                                                 out_sem.at[slot])
        size = ch // 2
        while size >= ROW_ALIGN:
            off = pl.multiple_of((valid // (2 * size)) * (2 * size), ROW_ALIGN)
            cond = (valid < ch) & ((valid & size) != 0)
            yield cond, pltpu.make_async_copy(ybuf_ref.at[slot, pl.ds(off, size), :],
                                              y_hbm.at[pl.ds(pl.multiple_of(row + off, ROW_ALIGN), size), :],
                                              out_sem.at[slot])
            size //= 2

    def start_out(g):
        for cond, cp in out_copies(g):
            pl.when(cond)(cp.start)

    def wait_out(g):
        for cond, cp in out_copies(g):
            pl.when(cond)(cp.wait)

    @pl.when(e == 0)
    def _():
        for j in range(X_LOOKAHEAD):
            pl.when(j < n_total)(in_copy(j).start)

    def chunk(c, carry):
        g = first + c
        pl.when(g + X_LOOKAHEAD < n_total)(in_copy(g + X_LOOKAHEAD).start)
        in_copy(g).wait()
        pl.when(g >= Y_SLOTS)(lambda: wait_out(g - Y_SLOTS))
        valid = tab_ref[TAB_CHUNK_VALID, g]

        def ffn(rows):
            xa, xb = _unpack_bf16_pairs(xbuf_ref[g % X_SLOTS, 0:rows, :])
            hgu = (_dot(xa.astype(BF16), wgu_ref[0, 0:half, :])
                   + _dot(xb.astype(BF16), wgu_ref[0, half:2 * half, :]))
            act = _silu(hgu[:, :ff]) * hgu[:, ff:]
            y = _dot(act.astype(BF16), wd_ref[0])
            ybuf_ref[g % Y_SLOTS, 0:rows, :] = _pack_bf16_pairs(y)

        lo = 0
        for rows in EXPERT_ROW_STEPS:
            pl.when((valid > lo) & (valid <= rows))(functools.partial(ffn, rows))
            lo = rows
        start_out(g)
        return carry

    lax.fori_loop(0, n_chunks, chunk, 0)

    @pl.when(e == n_e - 1)
    def _():
        for j in range(Y_SLOTS, 0, -1):
            pl.when(n_total >= j)(lambda j=j: wait_out(n_total - j))
        n_rows = y_hbm.shape[0]
        used = tab_ref[TAB_START, e] + ((tab_ref[TAB_COUNT, e] + (ROW_ALIGN - 1)) // ROW_ALIGN) * ROW_ALIGN
        ybuf_ref[0] = jnp.zeros((ch, half), PACKED)
        for j in range(pl.cdiv(n_e * ROW_ALIGN + EXPERT_CHUNK + ROW_ALIGN, ch)):
            at = pl.multiple_of(jnp.minimum(used + j * ch, n_rows - ch), ROW_ALIGN)
            tail = pltpu.make_async_copy(ybuf_ref.at[0], y_hbm.at[pl.ds(at, ch), :], out_sem.at[0])
            tail.start()
            tail.wait()


def _experts(tab, xs, wgu16, wd16):
    p, half = xs.shape
    n_e, ff, d = wd16.shape
    ch = EXPERT_CHUNK
    grid_spec = pltpu.PrefetchScalarGridSpec(
        num_scalar_prefetch=1,
        grid=(n_e,),
        in_specs=[pl.BlockSpec(memory_space=pl.ANY),
                  pl.BlockSpec((1, d, 2 * ff), lambda e, *_: (e, 0, 0)),
                  pl.BlockSpec((1, ff, d), lambda e, *_: (e, 0, 0))],
        out_specs=pl.BlockSpec(memory_space=pl.ANY),
        scratch_shapes=[pltpu.VMEM((X_SLOTS, ch, half), PACKED), pltpu.VMEM((Y_SLOTS, ch, half), PACKED),
                        pltpu.SemaphoreType.DMA((X_SLOTS,)), pltpu.SemaphoreType.DMA((Y_SLOTS,))],
    )
    return pl.pallas_call(
        functools.partial(_expert_kernel, n_e=n_e),
        out_shape=jax.ShapeDtypeStruct((p, half), PACKED),
        grid_spec=grid_spec,
        compiler_params=pltpu.CompilerParams(dimension_semantics=("arbitrary",),
                                             vmem_limit_bytes=VMEM_LIMIT_BYTES),
        name="moe_experts",
    )(tab, xs, wgu16, wd16)


def _combine_kernel(yg_ref, wcol_ref, x1_ref, hp_ref, mod_ref, npost_ref, wgus_ref, wds_ref, o_ref):
    te, d = x1_ref.shape
    half = d // 2
    ff = wds_ref.shape[0]

    xa, xb = _unpack_bf16_pairs(hp_ref[...])
    hgu = _dot(xa.astype(BF16), wgus_ref[0:half, :]) + _dot(xb.astype(BF16), wgus_ref[half:d, :])
    act = _silu(hgu[:, :ff]) * hgu[:, ff:]
    shared = _dot(act.astype(BF16), wds_ref[...])

    acc_a = shared[:, :half]
    acc_b = shared[:, half:]
    for k in range(TOP_K):
        ya, yb = _unpack_bf16_pairs(yg_ref[k])
        wk = wcol_ref[:, k:k + 1]
        acc_a = acc_a + wk * ya
        acc_b = acc_b + wk * yb
    moe = jnp.concatenate([acc_a, acc_b], axis=-1)
    g2 = mod_ref[:, 5 * d:6 * d]
    o_ref[...] = x1_ref[...] + g2 * _rms(moe, npost_ref[...])


def _combine(dest, y, wcol, x1, hp, mod, npost2, w_gate_s, w_up_s, w_down_s):
    t, d = x1.shape
    half = d // 2
    te = SEQ_TILE
    wgus = jnp.concatenate([w_gate_s, w_up_s], axis=1).astype(BF16)
    wds = w_down_s.astype(BF16)
    n_split = COMBINE_SPLITS if t % (COMBINE_SPLITS * 2 * SC_GATHER_ROWS * SC_WORKERS // TOP_K) == 0 else 1
    tq = t // n_split
    steps = tq // te
    out = None
    for q in range(n_split):
        dest_q = dest[:, q * tq:(q + 1) * tq]
        yg = _sc_gather_rows(y, dest_q.reshape(-1)).reshape(TOP_K, tq, half)
        row = lambda i, q=q: (q * steps + i, 0)
        in_specs = [pl.BlockSpec((TOP_K, te, half), lambda i: (0, i, 0)),
                    pl.BlockSpec((te, 128), row),
                    pl.BlockSpec((te, d), row),
                    pl.BlockSpec((te, half), row),
                    _const_spec(mod.shape),
                    _const_spec((1, d)),
                    _const_spec(wgus.shape),
                    _const_spec(wds.shape)]
        args = [yg, wcol, x1, hp, mod, npost2.reshape(1, d), wgus, wds]
        body = _combine_kernel
        aliases = {}
        if out is not None:
            in_specs.append(pl.BlockSpec(memory_space=pl.ANY))
            args.append(out)
            body = lambda *refs: _combine_kernel(*refs[:8], refs[9])
            aliases = {8: 0}
        out = pl.pallas_call(
            body,
            out_shape=jax.ShapeDtypeStruct((t, d), F32),
            grid=(steps,),
            in_specs=in_specs,
            out_specs=pl.BlockSpec((te, d), row),
            input_output_aliases=aliases,
            compiler_params=pltpu.CompilerParams(dimension_semantics=("arbitrary",),
                                                 vmem_limit_bytes=VMEM_LIMIT_BYTES),
            name="moe_combine",
        )(*args)
    return out


def kernel(x, c, w_ada, b_ada, norm_pre_mix, norm_post_mix, w_in, conv_w, w_conv_out, hgrn_lower_bounds, hgrn_norm_w, w_hgrn_out, w_o, norm_pre_ffn, norm_post_ffn, w_router, router_bias, w_gate_e, w_up_e, w_down_e, w_gate_s, w_up_s, w_down_s):
    bsz, seq, d = x.shape
    assert bsz == 1 and w_ada.shape[0] == 1
    mod = _ada_mod(c, w_ada[0], b_ada[0])
    expert_w = (w_gate_e[0], w_up_e[0], w_down_e[0])
    x1, wgu16, wd16 = _mixer(x.reshape(seq, d), mod, norm_pre_mix[0], norm_post_mix[0], w_in[0], conv_w[0],
                             w_conv_out[0], hgrn_lower_bounds, hgrn_norm_w[0], w_hgrn_out[0], w_o[0], *expert_w)
    hp, eidx, rank, wcol, cnt, wgu16, wd16 = _router(x1, mod, norm_pre_ffn[0], w_router[0], router_bias[0],
                                                     *expert_w, wgu16, wd16)
    dest, tab = _dest(eidx, rank, cnt)
    xs = _dispatch(tab, dest, hp, w_router.shape[-1])
    y = _experts(tab, xs, wgu16, wd16)
    out = _combine(dest, y, wcol, x1, hp, mod, norm_post_ffn[0], w_gate_s[0], w_up_s[0], w_down_s[0])
    return out.reshape(bsz, seq, d)
```

```python
import functools

import jax
import jax.numpy as jnp
from jax import lax
from jax.experimental import pallas as pl
from jax.experimental.pallas import tpu as pltpu
from jax.experimental.pallas import tpu_sc as plsc

F32 = jnp.float32
BF16 = jnp.bfloat16

NORM_EPS = 1e-6
CONV_DIM = 512
CONV_WIDTH = 3
HGRN_HEADS = 4
HGRN_DK = 128
HGRN_QK = HGRN_HEADS * HGRN_DK
N_GROUPS = 8
TOPK_GROUPS = 4
TOP_K = 8
ROUTED_SCALE = 2.5

SEQ_TILE = 256
EXPERT_CHUNK = 512
EXPERT_ROW_STEPS = (128, 256, EXPERT_CHUNK)
ROW_ALIGN = 8
VMEM_LIMIT_BYTES = 56 * 1024 * 1024


def _dot(a, b):
    return jnp.dot(a, b, preferred_element_type=F32)


def _dot_nt(a, b):
    return lax.dot_general(a, b, (((1,), (1,)), ((), ())), preferred_element_type=F32)


def _dot_tn(a, b):
    return lax.dot_general(a, b, (((0,), (0,)), ((), ())), preferred_element_type=F32)


def _split3(x):
    hi = x.astype(BF16)
    r1 = x - hi.astype(F32)
    mid = r1.astype(BF16)
    lo = (r1 - mid.astype(F32)).astype(BF16)
    return hi, mid, lo


def _sigmoid(x):
    return 1.0 / (1.0 + jnp.exp(-x))


def _silu(x):
    return x * _sigmoid(x)


def _rms(x, w):
    ms = jnp.mean(x * x, axis=-1, keepdims=True)
    return x * lax.rsqrt(ms + NORM_EPS) * w


def _ada_kernel(c_ref, w_ref, b_ref, o_ref):
    cs = _silu(c_ref[...])
    h1, h2, h3 = _split3(cs)
    w1, w2, w3 = _split3(w_ref[...])
    acc = _dot(h1, w1) + (_dot(h1, w2) + _dot(h2, w1)) + (_dot(h1, w3) + _dot(h2, w2) + _dot(h3, w1))
    o_ref[...] = acc + b_ref[...]


def _ada_mod(c, w_ada, b_ada):
    d = c.shape[-1]
    n = w_ada.shape[-1]
    bn = 1024
    c8 = jnp.broadcast_to(c.reshape(1, d), (8, d))
    out = pl.pallas_call(
        _ada_kernel,
        out_shape=jax.ShapeDtypeStruct((8, n), F32),
        grid=(n // bn,),
        in_specs=[pl.BlockSpec((8, d), lambda j: (0, 0)),
                  pl.BlockSpec((d, bn), lambda j: (0, j)),
                  pl.BlockSpec((1, bn), lambda j: (0, j))],
        out_specs=pl.BlockSpec((8, bn), lambda j: (0, j)),
        compiler_params=pltpu.CompilerParams(dimension_semantics=("arbitrary",),
                                             vmem_limit_bytes=VMEM_LIMIT_BYTES),
        name="ada_mod",
    )(c8, w_ada, b_ada.reshape(1, n))
    return out[0:1]


def _level_reference(b, rolled, s, row):
    ts = b.shape[0]
    c = s // 2 - 1
    if s >= 16:
        pieces = []
        for blk in range(ts // s):
            r = blk * s + c
            pieces.append(jnp.broadcast_to(b[r:r + 1, :], (s, b.shape[1])))
        return pieces[0] if len(pieces) == 1 else jnp.concatenate(pieces, axis=0)
    pos = row & (s - 1)
    out = b
    for p in range(s):
        if p == c:
            continue
        out = jnp.where(pos == p, rolled[p - c], out)
    return out


def _cast_expert_weights(wg_ref, wu_ref, wgu16_ref):
    ff = wg_ref.shape[2]
    for j in range(wg_ref.shape[0]):
        wgu16_ref[j, :, 0:ff] = wg_ref[j].astype(BF16)
        wgu16_ref[j, :, ff:2 * ff] = wu_ref[j].astype(BF16)


def _mixer_kernel(x_ref, mod_ref, npre_ref, npost_ref, win_ref, convw_ref, wco_ref,
                  hlb_ref, hnw_ref, who_ref, wo_ref, wg_ref, wu_ref,
                  o_ref, wgu16_ref, state_ref, ubuf_ref):
    ts, d = x_ref.shape
    step = pl.program_id(0)
    _cast_expert_weights(wg_ref, wu_ref, wgu16_ref)

    @pl.when(step == 0)
    def _():
        state_ref[...] = jnp.zeros_like(state_ref)
        ubuf_ref[0:8, :] = jnp.zeros((8, CONV_DIM), F32)

    x = x_ref[...]
    sh1 = mod_ref[:, 0:d]
    sc1 = mod_ref[:, d:2 * d]
    g1 = mod_ref[:, 2 * d:3 * d]
    h = (_rms(x, npre_ref[...]) * (1.0 + sc1) + sh1).astype(BF16)

    def proj(lo, width):
        return _dot(h, win_ref[:, lo:lo + width])

    c0 = 0
    cb = proj(c0, CONV_DIM)
    u = proj(c0 + CONV_DIM, CONV_DIM) * proj(c0 + 2 * CONV_DIM, CONV_DIM)
    ubuf_ref[8:8 + ts, :] = u
    conv = (ubuf_ref[6:6 + ts, :] * convw_ref[0:1, :]
            + ubuf_ref[7:7 + ts, :] * convw_ref[1:2, :]
            + u * convw_ref[2:3, :])
    ubuf_ref[0:8, :] = ubuf_ref[ts:ts + 8, :]
    y_a = _dot((cb * conv).astype(BF16), wco_ref[...])

    c1 = 3 * CONV_DIM
    q = _silu(proj(c1, HGRN_QK))
    hl = hlb_ref[...]
    hm = jnp.max(hl, axis=0, keepdims=True)
    he = jnp.exp(hl - hm)
    lb = he[0:1, :] / jnp.sum(he, axis=0, keepdims=True)
    fg = lb + (1.0 - lb) * _sigmoid(proj(c1 + HGRN_QK, HGRN_QK))
    k = 1.0 - fg
    g = jnp.log(fg)
    v = proj(c1 + 2 * HGRN_QK, HGRN_QK)
    gg = proj(c1 + 3 * HGRN_QK, HGRN_QK)

    row = lax.broadcasted_iota(jnp.int32, (ts, ts), 0)
    col = lax.broadcasted_iota(jnp.int32, (ts, ts), 1)
    tril = jnp.where(row >= col, 1.0, 0.0).astype(BF16)
    ghi, gmid, glo = _split3(g)
    b = _dot(tril, ghi) + _dot(tril, gmid) + _dot(tril, glo)
    b_last = b[ts - 1:ts, :]

    rowq = lax.broadcasted_iota(jnp.int32, (ts, HGRN_QK), 0)
    rolled = {sft: pltpu.roll(b, sft % ts, axis=0) for sft in (-3, -2, -1, 1, 2, 3, 4)}
    xor = row ^ col

    levels = []
    s = ts
    while s >= 2:
        levels.append(s)
        s //= 2
    qt, kt = [], []
    for s in levels:
        bref = _level_reference(b, rolled, s, rowq)
        e = jnp.exp(-jnp.abs(b - bref))
        upper = (rowq & (s // 2)) != 0
        qt.append(jnp.where(upper, q * e, 0.0).astype(BF16))
        kt.append(jnp.where(upper, 0.0, k * e).astype(BF16))
    q_in = (q * jnp.exp(b)).astype(BF16)
    k_out = (k * jnp.exp(b_last - b)).astype(BF16)
    v16 = v.astype(BF16)
    qk = q * k
    decay_last = jnp.exp(b_last)

    hnw = hnw_ref[...]
    outs = []
    for hd in range(HGRN_HEADS):
        sl = slice(hd * HGRN_DK, (hd + 1) * HGRN_DK)
        a = jnp.zeros((ts, ts), F32)
        for li in range(len(levels) - 1, -1, -1):
            s = levels[li]
            p = _dot_nt(qt[li][:, sl], kt[li][:, sl])
            a = jnp.where(xor >= s // 2, p, a)
        st = state_ref[hd]
        o_h = (_dot(a.astype(BF16), v16[:, sl])
               + jnp.sum(qk[:, sl], axis=-1, keepdims=True) * v[:, sl]
               + _dot_nt(q_in[:, sl], st.astype(BF16)))
        state_ref[hd] = st * decay_last[:, sl] + _dot_tn(v16[:, sl], k_out[:, sl])
        outs.append(_rms(o_h, hnw))
    o = jnp.concatenate(outs, axis=-1) * _silu(gg)
    y_b = _dot(o.astype(BF16), who_ref[...])

    c2 = c1 + 4 * HGRN_QK
    m = _sigmoid(proj(c2, d)) * y_a + _sigmoid(proj(c2 + d, d)) * y_b
    y = _dot(m.astype(BF16), wo_ref[...])
    o_ref[...] = x + g1 * _rms(y, npost_ref[...])


def _const_spec(shape):
    nd = len(shape)
    return pl.BlockSpec(shape, lambda i: (0,) * nd)


def _mixer(x2, mod, npre, npost, w_in, conv_w, w_conv_out, hlb, hnw, w_hgrn_out, w_o, w_gate_e, w_up_e):
    t, d = x2.shape
    ts = SEQ_TILE
    n = t // ts
    n_e, _, ff = w_gate_e.shape
    assert n_e % n == 0
    per = n_e // n
    args = (x2, mod, npre.reshape(1, d), npost.reshape(1, d), w_in.astype(BF16), conv_w,
            w_conv_out.astype(BF16), hlb, hnw.reshape(1, -1), w_hgrn_out.astype(BF16),
            w_o.astype(BF16))
    experts = lambda i: (i, 0, 0)
    in_specs = ([pl.BlockSpec((ts, d), lambda i: (i, 0))] + [_const_spec(a.shape) for a in args[1:]]
                + [pl.BlockSpec((per, d, ff), experts)] * 2)
    return pl.pallas_call(
        _mixer_kernel,
        out_shape=[jax.ShapeDtypeStruct((t, d), F32), jax.ShapeDtypeStruct((n_e, d, 2 * ff), BF16)],
        grid=(n,),
        in_specs=in_specs,
        out_specs=[pl.BlockSpec((ts, d), lambda i: (i, 0)), pl.BlockSpec((per, d, 2 * ff), experts)],
        scratch_shapes=[pltpu.VMEM((HGRN_HEADS, HGRN_DK, HGRN_DK), F32),
                        pltpu.VMEM((ts + 8, CONV_DIM), F32)],
        compiler_params=pltpu.CompilerParams(dimension_semantics=("arbitrary",),
                                             vmem_limit_bytes=VMEM_LIMIT_BYTES),
        name="token_mixer",
    )(*args, w_gate_e, w_up_e)


PACKED = jnp.int32


def _pack_bf16_pairs(x):
    m = x.shape[1] // 2
    hi = lax.bitcast_convert_type(x[:, :m].astype(BF16).astype(F32), jnp.uint32)
    lo = lax.bitcast_convert_type(x[:, m:].astype(BF16).astype(F32), jnp.uint32)
    return lax.bitcast_convert_type(hi | (lo >> 16), PACKED)


def _unpack_bf16_pairs(p):
    p = lax.bitcast_convert_type(p, jnp.uint32)
    hi = lax.bitcast_convert_type(p & jnp.uint32(0xFFFF0000), F32)
    lo = lax.bitcast_convert_type(p << 16, F32)
    return hi, lo


def _router_kernel(x1_ref, mod_ref, npre_ref, wrh_ref, wrl_ref, rb_ref,
                   hp_ref, eidx_ref, rank_ref, wcol_ref, cnt_ref, carry_ref, rows_ref):
    assert N_GROUPS == TOP_K == rows_ref.shape[0]
    tr, d = x1_ref.shape
    n_e = wrh_ref.shape[0]
    gsz = n_e // N_GROUPS
    neg = -jnp.inf

    @pl.when(pl.program_id(0) == 0)
    def _():
        carry_ref[...] = jnp.zeros_like(carry_ref)

    sh2 = mod_ref[:, 3 * d:4 * d]
    sc2 = mod_ref[:, 4 * d:5 * d]
    h2 = _rms(x1_ref[...], npre_ref[...]) * (1.0 + sc2) + sh2
    hp_ref[...] = _pack_bf16_pairs(h2)
    h_hi = h2.astype(BF16)
    h_lo = (h2 - h_hi.astype(F32)).astype(BF16)
    wrh = wrh_ref[...]
    logits = _dot_nt(wrh, h_hi) + (_dot_nt(wrh, h_lo) + _dot_nt(wrl_ref[...], h_hi))
    scores = _sigmoid(logits)
    sel = scores + rb_ref[:, 0:1]

    io_e = lax.broadcasted_iota(jnp.int32, (n_e, tr), 0)
    for g in range(N_GROUPS):
        blk = sel[g * gsz:(g + 1) * gsz, :]
        io = lax.broadcasted_iota(jnp.int32, (gsz, tr), 0) + g * gsz
        m1 = jnp.max(blk, axis=0, keepdims=True)
        i1 = jnp.min(jnp.where(blk == m1, io, n_e), axis=0, keepdims=True)
        m2 = jnp.max(jnp.where(io == i1, neg, blk), axis=0, keepdims=True)
        rows_ref[g:g + 1, :] = m1 + m2
    gs = rows_ref[...]
    io_g = lax.broadcasted_iota(jnp.int32, (N_GROUPS, tr), 0)
    gsel = jnp.zeros((N_GROUPS, tr), F32)
    for _ in range(TOPK_GROUPS):
        m = jnp.max(gs, axis=0, keepdims=True)
        gi = jnp.min(jnp.where(gs == m, io_g, N_GROUPS), axis=0, keepdims=True)
        hit = io_g == gi
        gsel = jnp.where(hit, 1.0, gsel)
        gs = jnp.where(hit, neg, gs)
    rows_ref[...] = gsel
    cur = jnp.concatenate(
        [jnp.where(rows_ref[g:g + 1, :] > 0.5, sel[g * gsz:(g + 1) * gsz, :], neg) for g in range(N_GROUPS)],
        axis=0)

    idxs = []
    selm = jnp.zeros((n_e, tr), F32)
    for k in range(TOP_K):
        m = jnp.max(cur, axis=0, keepdims=True)
        idx = jnp.min(jnp.where(cur == m, io_e, n_e), axis=0, keepdims=True)
        hit = io_e == idx
        rows_ref[k:k + 1, :] = jnp.sum(jnp.where(hit, scores, 0.0), axis=0, keepdims=True)
        cur = jnp.where(hit, neg, cur)
        selm = jnp.where(hit, 1.0, selm)
        eidx_ref[k:k + 1, :] = idx
        idxs.append(idx)

    r_io = lax.broadcasted_iota(jnp.int32, (tr, tr + 128), 0)
    c_io = lax.broadcasted_iota(jnp.int32, (tr, tr + 128), 1)
    before = jnp.where((r_io < c_io) | (c_io >= tr), 1.0, 0.0).astype(BF16)
    r_ext = _dot(selm.astype(BF16), before)
    rank_full = r_ext[:, :tr] + carry_ref[:, 0:1]
    for k, idx in enumerate(idxs):
        rk = jnp.sum(jnp.where(io_e == idx, rank_full, 0.0), axis=0, keepdims=True)
        rank_ref[k:k + 1, :] = rk.astype(jnp.int32)
    carry_ref[...] = carry_ref[...] + r_ext[:, tr:]
    cnt_ref[...] = carry_ref[...]

    wk = rows_ref[...]
    wn = wk / (jnp.sum(wk, axis=0, keepdims=True) + 1e-20) * ROUTED_SCALE
    eye = jnp.where(lax.broadcasted_iota(jnp.int32, (TOP_K, 128), 0)
                    == lax.broadcasted_iota(jnp.int32, (TOP_K, 128), 1), 1.0, 0.0).astype(BF16)
    w1, w2, w3 = _split3(wn)
    wcol_ref[...] = _dot_tn(w1, eye) + _dot_tn(w2, eye) + _dot_tn(w3, eye)


def _router(x1, mod, npre2, w_router, router_bias):
    t, d = x1.shape
    n_e = w_router.shape[1]
    tr = SEQ_TILE
    wrt = w_router.T
    wrh = wrt.astype(BF16)
    wrl = (wrt - wrh.astype(F32)).astype(BF16)
    rb = jnp.broadcast_to(router_bias.reshape(n_e, 1), (n_e, 128))
    args = (x1, mod, npre2.reshape(1, d), wrh, wrl, rb)
    in_specs = [pl.BlockSpec((tr, d), lambda i: (i, 0))] + [_const_spec(a.shape) for a in args[1:]]
    return pl.pallas_call(
        _router_kernel,
        out_shape=(jax.ShapeDtypeStruct((t, d // 2), PACKED),
                   jax.ShapeDtypeStruct((TOP_K, t), jnp.int32),
                   jax.ShapeDtypeStruct((TOP_K, t), jnp.int32),
                   jax.ShapeDtypeStruct((t, 128), F32),
                   jax.ShapeDtypeStruct((n_e, 128), F32)),
        grid=(t // tr,),
        in_specs=in_specs,
        out_specs=(pl.BlockSpec((tr, d // 2), lambda i: (i, 0)),
                   pl.BlockSpec((TOP_K, tr), lambda i: (0, i)),
                   pl.BlockSpec((TOP_K, tr), lambda i: (0, i)),
                   pl.BlockSpec((tr, 128), lambda i: (i, 0)),
                   pl.BlockSpec((n_e, 128), lambda i: (0, 0))),
        scratch_shapes=[pltpu.VMEM((n_e, 128), F32), pltpu.VMEM((TOP_K, tr), F32)],
        compiler_params=pltpu.CompilerParams(dimension_semantics=("arbitrary",),
                                             vmem_limit_bytes=VMEM_LIMIT_BYTES),
        name="moe_router",
    )(*args)


TAB_CHUNK_ROW, TAB_CHUNK_VALID, TAB_START, TAB_COUNT, TAB_CHUNK0, TAB_NCHUNK, TAB_TOTAL = range(7)
TAB_ROWS = 8


def _excl_cumsum_rows(lower, x):
    hi = jnp.floor(x * (1.0 / 128.0))
    lo = x - hi * 128.0
    return 128.0 * _dot(lower, hi.astype(BF16)) + _dot(lower, lo.astype(BF16))


def _dest_kernel(eidx_ref, rank_ref, cnt_ref, dest_ref, tab_ref):
    n_e = cnt_ref.shape[0]
    tt = eidx_ref.shape[1]
    n_g = tab_ref.shape[1]
    ch = float(EXPERT_CHUNK)
    cnt = jnp.floor((cnt_ref[...] + (ROW_ALIGN - 1)) * (1.0 / ROW_ALIGN)) * ROW_ALIGN
    r = lax.broadcasted_iota(jnp.int32, (n_e, n_e), 0)
    c = lax.broadcasted_iota(jnp.int32, (n_e, n_e), 1)
    lower = jnp.where(c < r, 1.0, 0.0).astype(BF16)
    starts = _excl_cumsum_rows(lower, cnt)
    start_col = starts[:, 0:1]

    @pl.when(pl.program_id(0) == 0)
    def _():
        nch = jnp.floor((cnt + (ch - 1.0)) * (1.0 / ch))
        chunk0 = _excl_cumsum_rows(lower, nch)
        cend_col = (chunk0 + nch)[:, 0:1]
        eye = r == c

        def as_row(col):
            return jnp.sum(jnp.where(eye, col, 0.0), axis=0, keepdims=True)

        g = lax.broadcasted_iota(jnp.int32, (n_e, n_g), 1).astype(F32)
        owner = jnp.sum(jnp.where(cend_col <= g, 1.0, 0.0), axis=0, keepdims=True)
        mine = lax.broadcasted_iota(jnp.int32, (n_e, n_g), 0).astype(F32) == owner
        g_row = lax.broadcasted_iota(jnp.int32, (1, n_g), 1).astype(F32)
        base = jnp.sum(jnp.where(mine, start_col - chunk0[:, 0:1] * ch, 0.0), axis=0, keepdims=True)
        left = jnp.sum(jnp.where(mine, cnt[:, 0:1] + chunk0[:, 0:1] * ch, 0.0), axis=0, keepdims=True)
        pad = jnp.zeros((1, n_g - n_e), F32)

        def wide(row):
            return jnp.concatenate([row, pad], axis=1)

        total = jnp.sum(nch[:, 0:1], axis=0, keepdims=True)
        tab_ref[...] = jnp.zeros(tab_ref.shape, jnp.int32)
        tab_ref[TAB_CHUNK_ROW:TAB_CHUNK_ROW + 1, :] = (base + g_row * ch).astype(jnp.int32)
        tab_ref[TAB_CHUNK_VALID:TAB_CHUNK_VALID + 1, :] = jnp.clip(left - g_row * ch, 0.0, ch).astype(jnp.int32)
        tab_ref[TAB_START:TAB_START + 1, :] = wide(as_row(start_col)).astype(jnp.int32)
        tab_ref[TAB_COUNT:TAB_COUNT + 1, :] = wide(as_row(cnt_ref[:, 0:1])).astype(jnp.int32)
        tab_ref[TAB_CHUNK0:TAB_CHUNK0 + 1, :] = wide(as_row(chunk0[:, 0:1])).astype(jnp.int32)
        tab_ref[TAB_NCHUNK:TAB_NCHUNK + 1, :] = wide(as_row(nch[:, 0:1])).astype(jnp.int32)
        tab_ref[TAB_TOTAL:TAB_TOTAL + 1, :] = jnp.broadcast_to(total, (1, n_g)).astype(jnp.int32)

    io_e = lax.broadcasted_iota(jnp.int32, (n_e, tt), 0)
    rows = []
    for k in range(TOP_K):
        hit = io_e == eidx_ref[k:k + 1, :]
        rows.append(jnp.sum(jnp.where(hit, start_col, 0.0), axis=0, keepdims=True))
    dest_ref[...] = jnp.concatenate(rows, axis=0).astype(jnp.int32) + rank_ref[...]


def _dest(eidx, rank, cnt):
    k, t = eidx.shape
    n_e = cnt.shape[0]
    tt = 512
    n_g = (t * k) // EXPERT_CHUNK + n_e
    return pl.pallas_call(
        _dest_kernel,
        out_shape=(jax.ShapeDtypeStruct((k, t), jnp.int32),
                   jax.ShapeDtypeStruct((TAB_ROWS, n_g), jnp.int32)),
        grid=(t // tt,),
        in_specs=[pl.BlockSpec((k, tt), lambda i: (0, i)),
                  pl.BlockSpec((k, tt), lambda i: (0, i)),
                  _const_spec(cnt.shape)],
        out_specs=(pl.BlockSpec((k, tt), lambda i: (0, i)), _const_spec((TAB_ROWS, n_g))),
        compiler_params=pltpu.CompilerParams(dimension_semantics=("arbitrary",)),
        name="moe_dest",
    )(eidx, rank, cnt)


SC_CORES = 2
SC_SUBCORES = 16
SC_WORKERS = SC_CORES * SC_SUBCORES
SC_ROWS = 128


def _sc_mesh():
    return plsc.VectorSubcoreMesh(core_axis_name="c", subcore_axis_name="s")


def _sc_worker_id():
    return lax.axis_index("s") * SC_CORES + lax.axis_index("c")


def _sc_scatter_rows(rows, dest_flat, n_out):
    t, w = rows.shape
    n_k = dest_flat.shape[0] // t
    r = SC_GATHER_ROWS
    per = t // SC_WORKERS
    n_steps = per // r
    assert per % (2 * r) == 0

    @functools.partial(
        pl.kernel, mesh=_sc_mesh(), out_type=jax.ShapeDtypeStruct((n_out, w), rows.dtype),
        scratch_types=[pltpu.VMEM((n_k, r), jnp.int32), pltpu.VMEM((2, r, w), rows.dtype),
                       pltpu.SemaphoreType.DMA((2,)), pltpu.SemaphoreType.DMA],
        name="moe_dispatch_sc")
    def scatter(rows_hbm, idx_hbm, out_hbm, idx_v, rows_v, l_sem, s_sem):
        base = _sc_worker_id() * per

        def load(j, b):
            return pltpu.make_async_copy(rows_hbm.at[pl.ds(pl.multiple_of(base + j * r, ROW_ALIGN), r)],
                                         rows_v.at[b], l_sem.at[b])

        load(0, 0).start()

        @pl.loop(0, n_steps, step=2)
        def _(j0):
            for b in range(2):
                j = j0 + b
                pl.when(j + 1 < n_steps)(load(j + 1, 1 - b).start)
                for k in range(n_k):
                    off = pl.multiple_of(k * t + base + j * r, ROW_ALIGN)
                    pltpu.sync_copy(idx_hbm.at[pl.ds(off, r)], idx_v.at[k])
                load(j, b).wait()
                copies = [pltpu.make_async_copy(rows_v.at[b], out_hbm.at[idx_v.at[k]], s_sem) for k in range(n_k)]
                for cp in copies:
                    cp.start()
                for cp in copies:
                    cp.wait()

    return scatter(rows, dest_flat)


SC_GATHER_ROWS = 64
COMBINE_SPLITS = 4


def _sc_gather_rows(table, idx):
    n = idx.shape[0]
    w = table.shape[1]
    r = SC_GATHER_ROWS
    per = n // SC_WORKERS
    n_steps = per // r
    assert per % (2 * r) == 0

    @functools.partial(
        pl.kernel, mesh=_sc_mesh(), out_type=jax.ShapeDtypeStruct((n, w), table.dtype),
        scratch_types=[pltpu.VMEM((2, r), jnp.int32), pltpu.VMEM((2, r, w), table.dtype),
                       pltpu.SemaphoreType.DMA((2,)), pltpu.SemaphoreType.DMA((2,))],
        name="moe_gather_sc")
    def gather(table_hbm, idx_hbm, out_hbm, idx_v, rows_v, g_sem, w_sem):
        base = _sc_worker_id() * per

        def at(j):
            return pl.ds(pl.multiple_of(base + j * r, ROW_ALIGN), r)

        def fetch(j, b):
            return pltpu.make_async_copy(table_hbm.at[idx_v.at[b]], rows_v.at[b], g_sem.at[b])

        def write(j, b):
            return pltpu.make_async_copy(rows_v.at[b], out_hbm.at[at(j)], w_sem.at[b])

        pltpu.sync_copy(idx_hbm.at[at(0)], idx_v.at[0])
        fetch(0, 0).start()

        @pl.loop(0, n_steps, step=2)
        def _(j0):
            for b in range(2):
                j = j0 + b

                @pl.when(j + 1 < n_steps)
                def _():
                    pl.when(j >= 1)(write(j - 1, 1 - b).wait)
                    pltpu.sync_copy(idx_hbm.at[at(j + 1)], idx_v.at[1 - b])
                    fetch(j + 1, 1 - b).start()

                fetch(j, b).wait()
                write(j, b).start()

        write(n_steps - 2, 0).wait()
        write(n_steps - 1, 1).wait()

    return gather(table, idx)


def _pad_kernel(tab_ref, xs_in, xs_hbm, zero_ref, sem, *, n_e):
    del xs_in
    zero_ref[...] = jnp.zeros_like(zero_ref)

    def pad_row_copy(row):
        return pltpu.make_async_copy(zero_ref.at[pl.ds(0, 1), :], xs_hbm.at[pl.ds(row, 1), :], sem)

    def pad_rows(e, carry, wait):
        count = tab_ref[TAB_COUNT, e]
        end = tab_ref[TAB_START, e] + count
        n_pad = (ROW_ALIGN - count % ROW_ALIGN) % ROW_ALIGN
        for j in range(ROW_ALIGN - 1):
            copy = pad_row_copy(end + j)
            pl.when(j < n_pad)(copy.wait if wait else copy.start)
        return carry

    lax.fori_loop(0, n_e, functools.partial(pad_rows, wait=False), 0)
    lax.fori_loop(0, n_e, functools.partial(pad_rows, wait=True), 0)
    last = tab_ref[TAB_COUNT, n_e - 1]
    used = tab_ref[TAB_START, n_e - 1] + ((last + (ROW_ALIGN - 1)) // ROW_ALIGN) * ROW_ALIGN
    piece = zero_ref.shape[0]
    n_rows = xs_hbm.shape[0]
    for j in range(pl.cdiv(n_e * ROW_ALIGN + EXPERT_CHUNK + ROW_ALIGN, piece)):
        at = pl.multiple_of(jnp.minimum(used + j * piece, n_rows - piece), ROW_ALIGN)
        tail = pltpu.make_async_copy(zero_ref, xs_hbm.at[pl.ds(at, piece), :], sem)
        tail.start()
        tail.wait()


def _sorted_rows(t, n_e):
    return t * TOP_K + n_e * ROW_ALIGN + EXPERT_CHUNK + ROW_ALIGN


def _dispatch(tab, dest, hp, n_e):
    t, m = hp.shape
    n_rows = _sorted_rows(t, n_e)
    xs = _sc_scatter_rows(hp, dest.reshape(-1), n_rows)
    grid_spec = pltpu.PrefetchScalarGridSpec(
        num_scalar_prefetch=1,
        grid=(1,),
        in_specs=[pl.BlockSpec(memory_space=pl.ANY)],
        out_specs=pl.BlockSpec(memory_space=pl.ANY),
        scratch_shapes=[pltpu.VMEM((EXPERT_CHUNK, m), PACKED), pltpu.SemaphoreType.DMA(())],
    )
    return pl.pallas_call(
        functools.partial(_pad_kernel, n_e=n_e),
        out_shape=jax.ShapeDtypeStruct((n_rows, m), PACKED),
        grid_spec=grid_spec,
        input_output_aliases={1: 0},
        compiler_params=pltpu.CompilerParams(dimension_semantics=("arbitrary",)),
        name="moe_pad",
    )(tab, xs)


X_LOOKAHEAD = 2
X_SLOTS = X_LOOKAHEAD + 1
Y_SLOTS = 2


def _expert_kernel(tab_ref, xs_hbm, wgu_ref, wd_ref, y_hbm,
                   wd16_ref, xbuf_ref, ybuf_ref, in_sem, out_sem, *, n_e):
    e = pl.program_id(0)
    ch, half = xbuf_ref.shape[1:]
    ff = wd_ref.shape[1]
    n_total = tab_ref[TAB_TOTAL, 0]
    first = tab_ref[TAB_CHUNK0, e]
    n_chunks = tab_ref[TAB_NCHUNK, e]

    def in_copy(g):
        slot = g % X_SLOTS
        row = pl.multiple_of(tab_ref[TAB_CHUNK_ROW, g], ROW_ALIGN)
        return pltpu.make_async_copy(xs_hbm.at[pl.ds(row, ch), :], xbuf_ref.at[slot], in_sem.at[slot])

    def out_copies(g):
        slot = g % Y_SLOTS
        row = tab_ref[TAB_CHUNK_ROW, g]
        valid = tab_ref[TAB_CHUNK_VALID, g]
        yield valid >= ch, pltpu.make_async_copy(ybuf_ref.at[slot], y_hbm.at[pl.ds(pl.multiple_of(row, ROW_ALIGN), ch), :],
                                                 out_sem.at[slot])
        size = ch // 2
        while size >= ROW_ALIGN:
            off = pl.multiple_of((valid // (2 * size)) * (2 * size), ROW_ALIGN)
            cond = (valid < ch) & ((valid & size) != 0)
            yield cond, pltpu.make_async_copy(ybuf_ref.at[slot, pl.ds(off, size), :],
                                              y_hbm.at[pl.ds(pl.multiple_of(row + off, ROW_ALIGN), size), :],
                                              out_sem.at[slot])
            size //= 2

    def start_out(g):
        for cond, cp in out_copies(g):
            pl.when(cond)(cp.start)

    def wait_out(g):
        for cond, cp in out_copies(g):
            pl.when(cond)(cp.wait)

    @pl.when(e == 0)
    def _():
        for j in range(X_LOOKAHEAD):
            pl.when(j < n_total)(in_copy(j).start)

    @pl.when(n_chunks > 0)
    def _():
        wd16_ref[...] = wd_ref[0].astype(BF16)

    def chunk(c, carry):
        g = first + c
        pl.when(g + X_LOOKAHEAD < n_total)(in_copy(g + X_LOOKAHEAD).start)
        in_copy(g).wait()
        pl.when(g >= Y_SLOTS)(lambda: wait_out(g - Y_SLOTS))
        valid = tab_ref[TAB_CHUNK_VALID, g]

        def ffn(rows):
            xa, xb = _unpack_bf16_pairs(xbuf_ref[g % X_SLOTS, 0:rows, :])
            hgu = (_dot(xa.astype(BF16), wgu_ref[0, 0:half, :])
                   + _dot(xb.astype(BF16), wgu_ref[0, half:2 * half, :]))
            act = _silu(hgu[:, :ff]) * hgu[:, ff:]
            y = _dot(act.astype(BF16), wd16_ref[...])
            ybuf_ref[g % Y_SLOTS, 0:rows, :] = _pack_bf16_pairs(y)

        lo = 0
        for rows in EXPERT_ROW_STEPS:
            pl.when((valid > lo) & (valid <= rows))(functools.partial(ffn, rows))
            lo = rows
        start_out(g)
        return carry

    lax.fori_loop(0, n_chunks, chunk, 0)

    @pl.when(e == n_e - 1)
    def _():
        for j in range(Y_SLOTS, 0, -1):
            pl.when(n_total >= j)(lambda j=j: wait_out(n_total - j))
        n_rows = y_hbm.shape[0]
        used = tab_ref[TAB_START, e] + ((tab_ref[TAB_COUNT, e] + (ROW_ALIGN - 1)) // ROW_ALIGN) * ROW_ALIGN
        ybuf_ref[0] = jnp.zeros((ch, half), PACKED)
        for j in range(pl.cdiv(n_e * ROW_ALIGN + EXPERT_CHUNK + ROW_ALIGN, ch)):
            at = pl.multiple_of(jnp.minimum(used + j * ch, n_rows - ch), ROW_ALIGN)
            tail = pltpu.make_async_copy(ybuf_ref.at[0], y_hbm.at[pl.ds(at, ch), :], out_sem.at[0])
            tail.start()
            tail.wait()


def _experts(tab, xs, wgu16, w_down_e):
    p, half = xs.shape
    n_e, ff, d = w_down_e.shape
    ch = EXPERT_CHUNK
    grid_spec = pltpu.PrefetchScalarGridSpec(
        num_scalar_prefetch=1,
        grid=(n_e,),
        in_specs=[pl.BlockSpec(memory_space=pl.ANY),
                  pl.BlockSpec((1, d, 2 * ff), lambda e, *_: (e, 0, 0)),
                  pl.BlockSpec((1, ff, d), lambda e, *_: (e, 0, 0))],
        out_specs=pl.BlockSpec(memory_space=pl.ANY),
        scratch_shapes=[pltpu.VMEM((ff, d), BF16),
                        pltpu.VMEM((X_SLOTS, ch, half), PACKED), pltpu.VMEM((Y_SLOTS, ch, half), PACKED),
                        pltpu.SemaphoreType.DMA((X_SLOTS,)), pltpu.SemaphoreType.DMA((Y_SLOTS,))],
    )
    return pl.pallas_call(
        functools.partial(_expert_kernel, n_e=n_e),
        out_shape=jax.ShapeDtypeStruct((p, half), PACKED),
        grid_spec=grid_spec,
        compiler_params=pltpu.CompilerParams(dimension_semantics=("arbitrary",),
                                             vmem_limit_bytes=VMEM_LIMIT_BYTES),
        name="moe_experts",
    )(tab, xs, wgu16, w_down_e)


def _combine_kernel(yg_ref, wcol_ref, x1_ref, hp_ref, mod_ref, npost_ref, wgus_ref, wds_ref, o_ref):
    te, d = x1_ref.shape
    half = d // 2
    ff = wds_ref.shape[0]

    xa, xb = _unpack_bf16_pairs(hp_ref[...])
    hgu = _dot(xa.astype(BF16), wgus_ref[0:half, :]) + _dot(xb.astype(BF16), wgus_ref[half:d, :])
    act = _silu(hgu[:, :ff]) * hgu[:, ff:]
    shared = _dot(act.astype(BF16), wds_ref[...])

    acc_a = shared[:, :half]
    acc_b = shared[:, half:]
    for k in range(TOP_K):
        ya, yb = _unpack_bf16_pairs(yg_ref[k])
        wk = wcol_ref[:, k:k + 1]
        acc_a = acc_a + wk * ya
        acc_b = acc_b + wk * yb
    moe = jnp.concatenate([acc_a, acc_b], axis=-1)
    g2 = mod_ref[:, 5 * d:6 * d]
    o_ref[...] = x1_ref[...] + g2 * _rms(moe, npost_ref[...])


def _combine(dest, y, wcol, x1, hp, mod, npost2, w_gate_s, w_up_s, w_down_s):
    t, d = x1.shape
    half = d // 2
    te = SEQ_TILE
    wgus = jnp.concatenate([w_gate_s, w_up_s], axis=1).astype(BF16)
    wds = w_down_s.astype(BF16)
    n_split = COMBINE_SPLITS if t % (COMBINE_SPLITS * 2 * SC_GATHER_ROWS * SC_WORKERS // TOP_K) == 0 else 1
    tq = t // n_split
    steps = tq // te
    out = None
    for q in range(n_split):
        dest_q = dest[:, q * tq:(q + 1) * tq]
        yg = _sc_gather_rows(y, dest_q.reshape(-1)).reshape(TOP_K, tq, half)
        row = lambda i, q=q: (q * steps + i, 0)
        in_specs = [pl.BlockSpec((TOP_K, te, half), lambda i: (0, i, 0)),
                    pl.BlockSpec((te, 128), row),
                    pl.BlockSpec((te, d), row),
                    pl.BlockSpec((te, half), row),
                    _const_spec(mod.shape),
                    _const_spec((1, d)),
                    _const_spec(wgus.shape),
                    _const_spec(wds.shape)]
        args = [yg, wcol, x1, hp, mod, npost2.reshape(1, d), wgus, wds]
        body = _combine_kernel
        aliases = {}
        if out is not None:
            in_specs.append(pl.BlockSpec(memory_space=pl.ANY))
            args.append(out)
            body = lambda *refs: _combine_kernel(*refs[:8], refs[9])
            aliases = {8: 0}
        out = pl.pallas_call(
            body,
            out_shape=jax.ShapeDtypeStruct((t, d), F32),
            grid=(steps,),
            in_specs=in_specs,
            out_specs=pl.BlockSpec((te, d), row),
            input_output_aliases=aliases,
            compiler_params=pltpu.CompilerParams(dimension_semantics=("arbitrary",),
                                                 vmem_limit_bytes=VMEM_LIMIT_BYTES),
            name="moe_combine",
        )(*args)
    return out


def kernel(x, c, w_ada, b_ada, norm_pre_mix, norm_post_mix, w_in, conv_w, w_conv_out, hgrn_lower_bounds, hgrn_norm_w, w_hgrn_out, w_o, norm_pre_ffn, norm_post_ffn, w_router, router_bias, w_gate_e, w_up_e, w_down_e, w_gate_s, w_up_s, w_down_s):
    bsz, seq, d = x.shape
    assert bsz == 1 and w_ada.shape[0] == 1
    mod = _ada_mod(c, w_ada[0], b_ada[0])
    x1, wgu16 = _mixer(x.reshape(seq, d), mod, norm_pre_mix[0], norm_post_mix[0], w_in[0], conv_w[0],
                       w_conv_out[0], hgrn_lower_bounds, hgrn_norm_w[0], w_hgrn_out[0], w_o[0],
                       w_gate_e[0], w_up_e[0])
    hp, eidx, rank, wcol, cnt = _router(x1, mod, norm_pre_ffn[0], w_router[0], router_bias[0])
    dest, tab = _dest(eidx, rank, cnt)
    xs = _dispatch(tab, dest, hp, w_router.shape[-1])
    y = _experts(tab, xs, wgu16, w_down_e[0])
    out = _combine(dest, y, wcol, x1, hp, mod, norm_post_ffn[0], w_gate_s[0], w_up_s[0], w_down_s[0])
    return out.reshape(bsz, seq, d)
```

```python
import functools

import jax
import jax.numpy as jnp
from jax import lax
from jax.experimental import pallas as pl
from jax.experimental.pallas import tpu as pltpu
from jax.experimental.pallas import tpu_sc as plsc

F32 = jnp.float32
BF16 = jnp.bfloat16

NORM_EPS = 1e-6
CONV_DIM = 512
CONV_WIDTH = 3
HGRN_HEADS = 4
HGRN_DK = 128
HGRN_QK = HGRN_HEADS * HGRN_DK
N_GROUPS = 8
TOPK_GROUPS = 4
TOP_K = 8
ROUTED_SCALE = 2.5

SEQ_TILE = 256
EXPERT_CHUNK = 512
EXPERT_ROW_STEPS = (128, 256, EXPERT_CHUNK)
ROW_ALIGN = 8
VMEM_LIMIT_BYTES = 56 * 1024 * 1024


def _dot(a, b):
    return jnp.dot(a, b, preferred_element_type=F32)


def _dot_nt(a, b):
    return lax.dot_general(a, b, (((1,), (1,)), ((), ())), preferred_element_type=F32)


def _dot_tn(a, b):
    return lax.dot_general(a, b, (((0,), (0,)), ((), ())), preferred_element_type=F32)


def _split3(x):
    hi = x.astype(BF16)
    r1 = x - hi.astype(F32)
    mid = r1.astype(BF16)
    lo = (r1 - mid.astype(F32)).astype(BF16)
    return hi, mid, lo


def _sigmoid(x):
    return 1.0 / (1.0 + jnp.exp(-x))


def _silu(x):
    return x * _sigmoid(x)


def _rms(x, w):
    ms = jnp.mean(x * x, axis=-1, keepdims=True)
    return x * lax.rsqrt(ms + NORM_EPS) * w


def _ada_kernel(c_ref, w_ref, b_ref, o_ref):
    cs = _silu(c_ref[...])
    h1, h2, h3 = _split3(cs)
    w1, w2, w3 = _split3(w_ref[...])
    acc = _dot(h1, w1) + (_dot(h1, w2) + _dot(h2, w1)) + (_dot(h1, w3) + _dot(h2, w2) + _dot(h3, w1))
    o_ref[...] = acc + b_ref[...]


def _ada_mod(c, w_ada, b_ada):
    d = c.shape[-1]
    n = w_ada.shape[-1]
    bn = 1024
    c8 = jnp.broadcast_to(c.reshape(1, d), (8, d))
    out = pl.pallas_call(
        _ada_kernel,
        out_shape=jax.ShapeDtypeStruct((8, n), F32),
        grid=(n // bn,),
        in_specs=[pl.BlockSpec((8, d), lambda j: (0, 0)),
                  pl.BlockSpec((d, bn), lambda j: (0, j)),
                  pl.BlockSpec((1, bn), lambda j: (0, j))],
        out_specs=pl.BlockSpec((8, bn), lambda j: (0, j)),
        compiler_params=pltpu.CompilerParams(dimension_semantics=("arbitrary",),
                                             vmem_limit_bytes=VMEM_LIMIT_BYTES),
        name="ada_mod",
    )(c8, w_ada, b_ada.reshape(1, n))
    return out[0:1]


def _level_reference(b, rolled, s, row):
    ts = b.shape[0]
    c = s // 2 - 1
    if s >= 16:
        pieces = []
        for blk in range(ts // s):
            r = blk * s + c
            pieces.append(jnp.broadcast_to(b[r:r + 1, :], (s, b.shape[1])))
        return pieces[0] if len(pieces) == 1 else jnp.concatenate(pieces, axis=0)
    pos = row & (s - 1)
    out = b
    for p in range(s):
        if p == c:
            continue
        out = jnp.where(pos == p, rolled[p - c], out)
    return out


def _cast_expert_weights(wg_ref, wu_ref, wgu16_ref):
    ff = wg_ref.shape[2]
    for j in range(wg_ref.shape[0]):
        wgu16_ref[j, :, 0:ff] = wg_ref[j].astype(BF16)
        wgu16_ref[j, :, ff:2 * ff] = wu_ref[j].astype(BF16)


def _mixer_kernel(x_ref, mod_ref, npre_ref, npost_ref, win_ref, convw_ref, wco_ref,
                  hlb_ref, hnw_ref, who_ref, wo_ref, wg_ref, wu_ref,
                  o_ref, wgu16_ref, state_ref, ubuf_ref):
    ts, d = x_ref.shape
    step = pl.program_id(0)
    _cast_expert_weights(wg_ref, wu_ref, wgu16_ref)

    @pl.when(step == 0)
    def _():
        state_ref[...] = jnp.zeros_like(state_ref)
        ubuf_ref[0:8, :] = jnp.zeros((8, CONV_DIM), F32)

    x = x_ref[...]
    sh1 = mod_ref[:, 0:d]
    sc1 = mod_ref[:, d:2 * d]
    g1 = mod_ref[:, 2 * d:3 * d]
    h = (_rms(x, npre_ref[...]) * (1.0 + sc1) + sh1).astype(BF16)

    def proj(lo, width):
        return _dot(h, win_ref[:, lo:lo + width])

    c0 = 0
    cb = proj(c0, CONV_DIM)
    u = proj(c0 + CONV_DIM, CONV_DIM) * proj(c0 + 2 * CONV_DIM, CONV_DIM)
    ubuf_ref[8:8 + ts, :] = u
    conv = (ubuf_ref[6:6 + ts, :] * convw_ref[0:1, :]
            + ubuf_ref[7:7 + ts, :] * convw_ref[1:2, :]
            + u * convw_ref[2:3, :])
    ubuf_ref[0:8, :] = ubuf_ref[ts:ts + 8, :]
    y_a = _dot((cb * conv).astype(BF16), wco_ref[...])

    c1 = 3 * CONV_DIM
    q = _silu(proj(c1, HGRN_QK))
    hl = hlb_ref[...]
    hm = jnp.max(hl, axis=0, keepdims=True)
    he = jnp.exp(hl - hm)
    lb = he[0:1, :] / jnp.sum(he, axis=0, keepdims=True)
    fg = lb + (1.0 - lb) * _sigmoid(proj(c1 + HGRN_QK, HGRN_QK))
    k = 1.0 - fg
    g = jnp.log(fg)
    v = proj(c1 + 2 * HGRN_QK, HGRN_QK)
    gg = proj(c1 + 3 * HGRN_QK, HGRN_QK)

    row = lax.broadcasted_iota(jnp.int32, (ts, ts), 0)
    col = lax.broadcasted_iota(jnp.int32, (ts, ts), 1)
    tril = jnp.where(row >= col, 1.0, 0.0).astype(BF16)
    ghi, gmid, glo = _split3(g)
    b = _dot(tril, ghi) + _dot(tril, gmid) + _dot(tril, glo)
    b_last = b[ts - 1:ts, :]

    rowq = lax.broadcasted_iota(jnp.int32, (ts, HGRN_QK), 0)
    rolled = {sft: pltpu.roll(b, sft % ts, axis=0) for sft in (-3, -2, -1, 1, 2, 3, 4)}
    xor = row ^ col

    levels = []
    s = ts
    while s >= 2:
        levels.append(s)
        s //= 2
    qt, kt = [], []
    for s in levels:
        bref = _level_reference(b, rolled, s, rowq)
        e = jnp.exp(-jnp.abs(b - bref))
        upper = (rowq & (s // 2)) != 0
        qt.append(jnp.where(upper, q * e, 0.0).astype(BF16))
        kt.append(jnp.where(upper, 0.0, k * e).astype(BF16))
    q_in = (q * jnp.exp(b)).astype(BF16)
    k_out = (k * jnp.exp(b_last - b)).astype(BF16)
    v16 = v.astype(BF16)
    qk = q * k
    decay_last = jnp.exp(b_last)

    hnw = hnw_ref[...]
    outs = []
    for hd in range(HGRN_HEADS):
        sl = slice(hd * HGRN_DK, (hd + 1) * HGRN_DK)
        a = jnp.zeros((ts, ts), F32)
        for li in range(len(levels) - 1, -1, -1):
            s = levels[li]
            p = _dot_nt(qt[li][:, sl], kt[li][:, sl])
            a = jnp.where(xor >= s // 2, p, a)
        st = state_ref[hd]
        o_h = (_dot(a.astype(BF16), v16[:, sl])
               + jnp.sum(qk[:, sl], axis=-1, keepdims=True) * v[:, sl]
               + _dot_nt(q_in[:, sl], st.astype(BF16)))
        state_ref[hd] = st * decay_last[:, sl] + _dot_tn(v16[:, sl], k_out[:, sl])
        outs.append(_rms(o_h, hnw))
    o = jnp.concatenate(outs, axis=-1) * _silu(gg)
    y_b = _dot(o.astype(BF16), who_ref[...])

    c2 = c1 + 4 * HGRN_QK
    m = _sigmoid(proj(c2, d)) * y_a + _sigmoid(proj(c2 + d, d)) * y_b
    y = _dot(m.astype(BF16), wo_ref[...])
    o_ref[...] = x + g1 * _rms(y, npost_ref[...])


def _const_spec(shape):
    nd = len(shape)
    return pl.BlockSpec(shape, lambda i: (0,) * nd)


def _mixer(x2, mod, npre, npost, w_in, conv_w, w_conv_out, hlb, hnw, w_hgrn_out, w_o, w_gate_e, w_up_e):
    t, d = x2.shape
    ts = SEQ_TILE
    n = t // ts
    n_e, _, ff = w_gate_e.shape
    assert n_e % n == 0
    per = n_e // n
    args = (x2, mod, npre.reshape(1, d), npost.reshape(1, d), w_in.astype(BF16), conv_w,
            w_conv_out.astype(BF16), hlb, hnw.reshape(1, -1), w_hgrn_out.astype(BF16),
            w_o.astype(BF16))
    experts = lambda i: (i, 0, 0)
    in_specs = ([pl.BlockSpec((ts, d), lambda i: (i, 0))] + [_const_spec(a.shape) for a in args[1:]]
                + [pl.BlockSpec((per, d, ff), experts)] * 2)
    return pl.pallas_call(
        _mixer_kernel,
        out_shape=[jax.ShapeDtypeStruct((t, d), F32), jax.ShapeDtypeStruct((n_e, d, 2 * ff), BF16)],
        grid=(n,),
        in_specs=in_specs,
        out_specs=[pl.BlockSpec((ts, d), lambda i: (i, 0)), pl.BlockSpec((per, d, 2 * ff), experts)],
        scratch_shapes=[pltpu.VMEM((HGRN_HEADS, HGRN_DK, HGRN_DK), F32),
                        pltpu.VMEM((ts + 8, CONV_DIM), F32)],
        compiler_params=pltpu.CompilerParams(dimension_semantics=("arbitrary",),
                                             vmem_limit_bytes=VMEM_LIMIT_BYTES),
        name="token_mixer",
    )(*args, w_gate_e, w_up_e)


PACKED = jnp.int32


def _pack_bf16_pairs(x):
    m = x.shape[1] // 2
    hi = lax.bitcast_convert_type(x[:, :m].astype(BF16).astype(F32), jnp.uint32)
    lo = lax.bitcast_convert_type(x[:, m:].astype(BF16).astype(F32), jnp.uint32)
    return lax.bitcast_convert_type(hi | (lo >> 16), PACKED)


def _unpack_bf16_pairs(p):
    p = lax.bitcast_convert_type(p, jnp.uint32)
    hi = lax.bitcast_convert_type(p & jnp.uint32(0xFFFF0000), F32)
    lo = lax.bitcast_convert_type(p << 16, F32)
    return hi, lo


def _router_kernel(x1_ref, mod_ref, npre_ref, wrh_ref, wrl_ref, rb_ref,
                   hp_ref, eidx_ref, rank_ref, wcol_ref, cnt_ref, carry_ref, rows_ref):
    assert N_GROUPS == TOP_K == rows_ref.shape[0]
    tr, d = x1_ref.shape
    n_e = wrh_ref.shape[0]
    gsz = n_e // N_GROUPS
    neg = -jnp.inf

    @pl.when(pl.program_id(0) == 0)
    def _():
        carry_ref[...] = jnp.zeros_like(carry_ref)

    sh2 = mod_ref[:, 3 * d:4 * d]
    sc2 = mod_ref[:, 4 * d:5 * d]
    h2 = _rms(x1_ref[...], npre_ref[...]) * (1.0 + sc2) + sh2
    hp_ref[...] = _pack_bf16_pairs(h2)
    h_hi = h2.astype(BF16)
    h_lo = (h2 - h_hi.astype(F32)).astype(BF16)
    wrh = wrh_ref[...]
    logits = _dot_nt(wrh, h_hi) + (_dot_nt(wrh, h_lo) + _dot_nt(wrl_ref[...], h_hi))
    scores = _sigmoid(logits)
    sel = scores + rb_ref[:, 0:1]

    io_e = lax.broadcasted_iota(jnp.int32, (n_e, tr), 0)
    for g in range(N_GROUPS):
        blk = sel[g * gsz:(g + 1) * gsz, :]
        io = lax.broadcasted_iota(jnp.int32, (gsz, tr), 0) + g * gsz
        m1 = jnp.max(blk, axis=0, keepdims=True)
        i1 = jnp.min(jnp.where(blk == m1, io, n_e), axis=0, keepdims=True)
        m2 = jnp.max(jnp.where(io == i1, neg, blk), axis=0, keepdims=True)
        rows_ref[g:g + 1, :] = m1 + m2
    gs = rows_ref[...]
    io_g = lax.broadcasted_iota(jnp.int32, (N_GROUPS, tr), 0)
    gsel = jnp.zeros((N_GROUPS, tr), F32)
    for _ in range(TOPK_GROUPS):
        m = jnp.max(gs, axis=0, keepdims=True)
        gi = jnp.min(jnp.where(gs == m, io_g, N_GROUPS), axis=0, keepdims=True)
        hit = io_g == gi
        gsel = jnp.where(hit, 1.0, gsel)
        gs = jnp.where(hit, neg, gs)
    rows_ref[...] = gsel
    cur = jnp.concatenate(
        [jnp.where(rows_ref[g:g + 1, :] > 0.5, sel[g * gsz:(g + 1) * gsz, :], neg) for g in range(N_GROUPS)],
        axis=0)

    idxs = []
    selm = jnp.zeros((n_e, tr), F32)
    for k in range(TOP_K):
        m = jnp.max(cur, axis=0, keepdims=True)
        idx = jnp.min(jnp.where(cur == m, io_e, n_e), axis=0, keepdims=True)
        hit = io_e == idx
        rows_ref[k:k + 1, :] = jnp.sum(jnp.where(hit, scores, 0.0), axis=0, keepdims=True)
        cur = jnp.where(hit, neg, cur)
        selm = jnp.where(hit, 1.0, selm)
        eidx_ref[k:k + 1, :] = idx
        idxs.append(idx)

    r_io = lax.broadcasted_iota(jnp.int32, (tr, tr + 128), 0)
    c_io = lax.broadcasted_iota(jnp.int32, (tr, tr + 128), 1)
    before = jnp.where((r_io < c_io) | (c_io >= tr), 1.0, 0.0).astype(BF16)
    r_ext = _dot(selm.astype(BF16), before)
    rank_full = r_ext[:, :tr] + carry_ref[:, 0:1]
    for k, idx in enumerate(idxs):
        rk = jnp.sum(jnp.where(io_e == idx, rank_full, 0.0), axis=0, keepdims=True)
        rank_ref[k:k + 1, :] = rk.astype(jnp.int32)
    carry_ref[...] = carry_ref[...] + r_ext[:, tr:]
    cnt_ref[...] = carry_ref[...]

    wk = rows_ref[...]
    wn = wk / (jnp.sum(wk, axis=0, keepdims=True) + 1e-20) * ROUTED_SCALE
    eye = jnp.where(lax.broadcasted_iota(jnp.int32, (TOP_K, 128), 0)
                    == lax.broadcasted_iota(jnp.int32, (TOP_K, 128), 1), 1.0, 0.0).astype(BF16)
    w1, w2, w3 = _split3(wn)
    wcol_ref[...] = _dot_tn(w1, eye) + _dot_tn(w2, eye) + _dot_tn(w3, eye)


def _router(x1, mod, npre2, w_router, router_bias):
    t, d = x1.shape
    n_e = w_router.shape[1]
    tr = SEQ_TILE
    wrt = w_router.T
    wrh = wrt.astype(BF16)
    wrl = (wrt - wrh.astype(F32)).astype(BF16)
    rb = jnp.broadcast_to(router_bias.reshape(n_e, 1), (n_e, 128))
    args = (x1, mod, npre2.reshape(1, d), wrh, wrl, rb)
    in_specs = [pl.BlockSpec((tr, d), lambda i: (i, 0))] + [_const_spec(a.shape) for a in args[1:]]
    return pl.pallas_call(
        _router_kernel,
        out_shape=(jax.ShapeDtypeStruct((t, d // 2), PACKED),
                   jax.ShapeDtypeStruct((TOP_K, t), jnp.int32),
                   jax.ShapeDtypeStruct((TOP_K, t), jnp.int32),
                   jax.ShapeDtypeStruct((t, 128), F32),
                   jax.ShapeDtypeStruct((n_e, 128), F32)),
        grid=(t // tr,),
        in_specs=in_specs,
        out_specs=(pl.BlockSpec((tr, d // 2), lambda i: (i, 0)),
                   pl.BlockSpec((TOP_K, tr), lambda i: (0, i)),
                   pl.BlockSpec((TOP_K, tr), lambda i: (0, i)),
                   pl.BlockSpec((tr, 128), lambda i: (i, 0)),
                   pl.BlockSpec((n_e, 128), lambda i: (0, 0))),
        scratch_shapes=[pltpu.VMEM((n_e, 128), F32), pltpu.VMEM((TOP_K, tr), F32)],
        compiler_params=pltpu.CompilerParams(dimension_semantics=("arbitrary",),
                                             vmem_limit_bytes=VMEM_LIMIT_BYTES),
        name="moe_router",
    )(*args)


TAB_CHUNK_ROW, TAB_CHUNK_VALID, TAB_START, TAB_COUNT, TAB_CHUNK0, TAB_NCHUNK, TAB_TOTAL = range(7)
TAB_ROWS = 8


def _excl_cumsum_rows(lower, x):
    hi = jnp.floor(x * (1.0 / 128.0))
    lo = x - hi * 128.0
    return 128.0 * _dot(lower, hi.astype(BF16)) + _dot(lower, lo.astype(BF16))


def _dest_kernel(eidx_ref, rank_ref, cnt_ref, dest_ref, tab_ref):
    n_e = cnt_ref.shape[0]
    tt = eidx_ref.shape[1]
    n_g = tab_ref.shape[1]
    ch = float(EXPERT_CHUNK)
    cnt = jnp.floor((cnt_ref[...] + (ROW_ALIGN - 1)) * (1.0 / ROW_ALIGN)) * ROW_ALIGN
    r = lax.broadcasted_iota(jnp.int32, (n_e, n_e), 0)
    c = lax.broadcasted_iota(jnp.int32, (n_e, n_e), 1)
    lower = jnp.where(c < r, 1.0, 0.0).astype(BF16)
    starts = _excl_cumsum_rows(lower, cnt)
    start_col = starts[:, 0:1]

    @pl.when(pl.program_id(0) == 0)
    def _():
        nch = jnp.floor((cnt + (ch - 1.0)) * (1.0 / ch))
        chunk0 = _excl_cumsum_rows(lower, nch)
        cend_col = (chunk0 + nch)[:, 0:1]
        eye = r == c

        def as_row(col):
            return jnp.sum(jnp.where(eye, col, 0.0), axis=0, keepdims=True)

        g = lax.broadcasted_iota(jnp.int32, (n_e, n_g), 1).astype(F32)
        owner = jnp.sum(jnp.where(cend_col <= g, 1.0, 0.0), axis=0, keepdims=True)
        mine = lax.broadcasted_iota(jnp.int32, (n_e, n_g), 0).astype(F32) == owner
        g_row = lax.broadcasted_iota(jnp.int32, (1, n_g), 1).astype(F32)
        base = jnp.sum(jnp.where(mine, start_col - chunk0[:, 0:1] * ch, 0.0), axis=0, keepdims=True)
        left = jnp.sum(jnp.where(mine, cnt[:, 0:1] + chunk0[:, 0:1] * ch, 0.0), axis=0, keepdims=True)
        pad = jnp.zeros((1, n_g - n_e), F32)

        def wide(row):
            return jnp.concatenate([row, pad], axis=1)

        total = jnp.sum(nch[:, 0:1], axis=0, keepdims=True)
        tab_ref[...] = jnp.zeros(tab_ref.shape, jnp.int32)
        tab_ref[TAB_CHUNK_ROW:TAB_CHUNK_ROW + 1, :] = (base + g_row * ch).astype(jnp.int32)
        tab_ref[TAB_CHUNK_VALID:TAB_CHUNK_VALID + 1, :] = jnp.clip(left - g_row * ch, 0.0, ch).astype(jnp.int32)
        tab_ref[TAB_START:TAB_START + 1, :] = wide(as_row(start_col)).astype(jnp.int32)
        tab_ref[TAB_COUNT:TAB_COUNT + 1, :] = wide(as_row(cnt_ref[:, 0:1])).astype(jnp.int32)
        tab_ref[TAB_CHUNK0:TAB_CHUNK0 + 1, :] = wide(as_row(chunk0[:, 0:1])).astype(jnp.int32)
        tab_ref[TAB_NCHUNK:TAB_NCHUNK + 1, :] = wide(as_row(nch[:, 0:1])).astype(jnp.int32)
        tab_ref[TAB_TOTAL:TAB_TOTAL + 1, :] = jnp.broadcast_to(total, (1, n_g)).astype(jnp.int32)

    io_e = lax.broadcasted_iota(jnp.int32, (n_e, tt), 0)
    rows = []
    for k in range(TOP_K):
        hit = io_e == eidx_ref[k:k + 1, :]
        rows.append(jnp.sum(jnp.where(hit, start_col, 0.0), axis=0, keepdims=True))
    dest_ref[...] = jnp.concatenate(rows, axis=0).astype(jnp.int32) + rank_ref[...]


def _dest(eidx, rank, cnt):
    k, t = eidx.shape
    n_e = cnt.shape[0]
    tt = 512
    n_g = (t * k) // EXPERT_CHUNK + n_e
    return pl.pallas_call(
        _dest_kernel,
        out_shape=(jax.ShapeDtypeStruct((k, t), jnp.int32),
                   jax.ShapeDtypeStruct((TAB_ROWS, n_g), jnp.int32)),
        grid=(t // tt,),
        in_specs=[pl.BlockSpec((k, tt), lambda i: (0, i)),
                  pl.BlockSpec((k, tt), lambda i: (0, i)),
                  _const_spec(cnt.shape)],
        out_specs=(pl.BlockSpec((k, tt), lambda i: (0, i)), _const_spec((TAB_ROWS, n_g))),
        compiler_params=pltpu.CompilerParams(dimension_semantics=("arbitrary",)),
        name="moe_dest",
    )(eidx, rank, cnt)


SC_CORES = 2
SC_SUBCORES = 16
SC_WORKERS = SC_CORES * SC_SUBCORES
SC_ROWS = 128


def _sc_mesh():
    return plsc.VectorSubcoreMesh(core_axis_name="c", subcore_axis_name="s")


def _sc_worker_id():
    return lax.axis_index("s") * SC_CORES + lax.axis_index("c")


def _sc_scatter_rows(rows, dest_flat, n_out):
    t, w = rows.shape
    n_k = dest_flat.shape[0] // t
    r = SC_GATHER_ROWS
    per = t // SC_WORKERS
    n_steps = per // r
    assert per % (2 * r) == 0

    @functools.partial(
        pl.kernel, mesh=_sc_mesh(), out_type=jax.ShapeDtypeStruct((n_out, w), rows.dtype),
        scratch_types=[pltpu.VMEM((n_k, r), jnp.int32), pltpu.VMEM((2, r, w), rows.dtype),
                       pltpu.SemaphoreType.DMA((2,)), pltpu.SemaphoreType.DMA],
        name="moe_dispatch_sc")
    def scatter(rows_hbm, idx_hbm, out_hbm, idx_v, rows_v, l_sem, s_sem):
        base = _sc_worker_id() * per

        def load(j, b):
            return pltpu.make_async_copy(rows_hbm.at[pl.ds(pl.multiple_of(base + j * r, ROW_ALIGN), r)],
                                         rows_v.at[b], l_sem.at[b])

        load(0, 0).start()

        @pl.loop(0, n_steps, step=2)
        def _(j0):
            for b in range(2):
                j = j0 + b
                pl.when(j + 1 < n_steps)(load(j + 1, 1 - b).start)
                for k in range(n_k):
                    off = pl.multiple_of(k * t + base + j * r, ROW_ALIGN)
                    pltpu.sync_copy(idx_hbm.at[pl.ds(off, r)], idx_v.at[k])
                load(j, b).wait()
                copies = [pltpu.make_async_copy(rows_v.at[b], out_hbm.at[idx_v.at[k]], s_sem) for k in range(n_k)]
                for cp in copies:
                    cp.start()
                for cp in copies:
                    cp.wait()

    return scatter(rows, dest_flat)


SC_GATHER_ROWS = 64
COMBINE_SPLITS = 4
COMBINE_TILE = 512


def _sc_gather_rows(table, idx):
    n = idx.shape[0]
    w = table.shape[1]
    r = SC_GATHER_ROWS
    per = n // SC_WORKERS
    n_steps = per // r
    assert per % (2 * r) == 0

    @functools.partial(
        pl.kernel, mesh=_sc_mesh(), out_type=jax.ShapeDtypeStruct((n, w), table.dtype),
        scratch_types=[pltpu.VMEM((2, r), jnp.int32), pltpu.VMEM((2, r, w), table.dtype),
                       pltpu.SemaphoreType.DMA((2,)), pltpu.SemaphoreType.DMA((2,))],
        name="moe_gather_sc")
    def gather(table_hbm, idx_hbm, out_hbm, idx_v, rows_v, g_sem, w_sem):
        base = _sc_worker_id() * per

        def at(j):
            return pl.ds(pl.multiple_of(base + j * r, ROW_ALIGN), r)

        def fetch(j, b):
            return pltpu.make_async_copy(table_hbm.at[idx_v.at[b]], rows_v.at[b], g_sem.at[b])

        def write(j, b):
            return pltpu.make_async_copy(rows_v.at[b], out_hbm.at[at(j)], w_sem.at[b])

        pltpu.sync_copy(idx_hbm.at[at(0)], idx_v.at[0])
        fetch(0, 0).start()

        @pl.loop(0, n_steps, step=2)
        def _(j0):
            for b in range(2):
                j = j0 + b

                @pl.when(j + 1 < n_steps)
                def _():
                    pl.when(j >= 1)(write(j - 1, 1 - b).wait)
                    pltpu.sync_copy(idx_hbm.at[at(j + 1)], idx_v.at[1 - b])
                    fetch(j + 1, 1 - b).start()

                fetch(j, b).wait()
                write(j, b).start()

        write(n_steps - 2, 0).wait()
        write(n_steps - 1, 1).wait()

    return gather(table, idx)


def _pad_kernel(tab_ref, xs_in, xs_hbm, zero_ref, sem, *, n_e):
    del xs_in
    zero_ref[...] = jnp.zeros_like(zero_ref)

    def pad_row_copy(row):
        return pltpu.make_async_copy(zero_ref.at[pl.ds(0, 1), :], xs_hbm.at[pl.ds(row, 1), :], sem)

    def pad_rows(e, carry, wait):
        count = tab_ref[TAB_COUNT, e]
        end = tab_ref[TAB_START, e] + count
        n_pad = (ROW_ALIGN - count % ROW_ALIGN) % ROW_ALIGN
        for j in range(ROW_ALIGN - 1):
            copy = pad_row_copy(end + j)
            pl.when(j < n_pad)(copy.wait if wait else copy.start)
        return carry

    lax.fori_loop(0, n_e, functools.partial(pad_rows, wait=False), 0)
    lax.fori_loop(0, n_e, functools.partial(pad_rows, wait=True), 0)
    last = tab_ref[TAB_COUNT, n_e - 1]
    used = tab_ref[TAB_START, n_e - 1] + ((last + (ROW_ALIGN - 1)) // ROW_ALIGN) * ROW_ALIGN
    piece = zero_ref.shape[0]
    n_rows = xs_hbm.shape[0]
    for j in range(pl.cdiv(n_e * ROW_ALIGN + EXPERT_CHUNK + ROW_ALIGN, piece)):
        at = pl.multiple_of(jnp.minimum(used + j * piece, n_rows - piece), ROW_ALIGN)
        tail = pltpu.make_async_copy(zero_ref, xs_hbm.at[pl.ds(at, piece), :], sem)
        tail.start()
        tail.wait()


def _sorted_rows(t, n_e):
    return t * TOP_K + n_e * ROW_ALIGN + EXPERT_CHUNK + ROW_ALIGN


def _dispatch(tab, dest, hp, n_e):
    t, m = hp.shape
    n_rows = _sorted_rows(t, n_e)
    xs = _sc_scatter_rows(hp, dest.reshape(-1), n_rows)
    grid_spec = pltpu.PrefetchScalarGridSpec(
        num_scalar_prefetch=1,
        grid=(1,),
        in_specs=[pl.BlockSpec(memory_space=pl.ANY)],
        out_specs=pl.BlockSpec(memory_space=pl.ANY),
        scratch_shapes=[pltpu.VMEM((EXPERT_CHUNK, m), PACKED), pltpu.SemaphoreType.DMA(())],
    )
    return pl.pallas_call(
        functools.partial(_pad_kernel, n_e=n_e),
        out_shape=jax.ShapeDtypeStruct((n_rows, m), PACKED),
        grid_spec=grid_spec,
        input_output_aliases={1: 0},
        compiler_params=pltpu.CompilerParams(dimension_semantics=("arbitrary",)),
        name="moe_pad",
    )(tab, xs)


X_LOOKAHEAD = 2
X_SLOTS = X_LOOKAHEAD + 1
Y_SLOTS = 2


def _expert_kernel(tab_ref, xs_hbm, wgu_ref, wd_ref, y_hbm,
                   wd16_ref, xbuf_ref, ybuf_ref, in_sem, out_sem, *, n_e):
    e = pl.program_id(0)
    ch, half = xbuf_ref.shape[1:]
    ff = wd_ref.shape[1]
    n_total = tab_ref[TAB_TOTAL, 0]
    first = tab_ref[TAB_CHUNK0, e]
    n_chunks = tab_ref[TAB_NCHUNK, e]

    def in_copy(g):
        slot = g % X_SLOTS
        row = pl.multiple_of(tab_ref[TAB_CHUNK_ROW, g], ROW_ALIGN)
        return pltpu.make_async_copy(xs_hbm.at[pl.ds(row, ch), :], xbuf_ref.at[slot], in_sem.at[slot])

    def out_copies(g):
        slot = g % Y_SLOTS
        row = tab_ref[TAB_CHUNK_ROW, g]
        valid = tab_ref[TAB_CHUNK_VALID, g]
        yield valid >= ch, pltpu.make_async_copy(ybuf_ref.at[slot], y_hbm.at[pl.ds(pl.multiple_of(row, ROW_ALIGN), ch), :],
                                                 out_sem.at[slot])
        size = ch // 2
        while size >= ROW_ALIGN:
            off = pl.multiple_of((valid // (2 * size)) * (2 * size), ROW_ALIGN)
            cond = (valid < ch) & ((valid & size) != 0)
            yield cond, pltpu.make_async_copy(ybuf_ref.at[slot, pl.ds(off, size), :],
                                              y_hbm.at[pl.ds(pl.multiple_of(row + off, ROW_ALIGN), size), :],
                                              out_sem.at[slot])
            size //= 2

    def start_out(g):
        for cond, cp in out_copies(g):
            pl.when(cond)(cp.start)

    def wait_out(g):
        for cond, cp in out_copies(g):
            pl.when(cond)(cp.wait)

    @pl.when(e == 0)
    def _():
        for j in range(X_LOOKAHEAD):
            pl.when(j < n_total)(in_copy(j).start)

    @pl.when(n_chunks > 0)
    def _():
        wd16_ref[...] = wd_ref[0].astype(BF16)

    def chunk(c, carry):
        g = first + c
        pl.when(g + X_LOOKAHEAD < n_total)(in_copy(g + X_LOOKAHEAD).start)
        in_copy(g).wait()
        pl.when(g >= Y_SLOTS)(lambda: wait_out(g - Y_SLOTS))
        valid = tab_ref[TAB_CHUNK_VALID, g]

        def ffn(rows):
            xa, xb = _unpack_bf16_pairs(xbuf_ref[g % X_SLOTS, 0:rows, :])
            hgu = (_dot(xa.astype(BF16), wgu_ref[0, 0:half, :])
                   + _dot(xb.astype(BF16), wgu_ref[0, half:2 * half, :]))
            act = _silu(hgu[:, :ff]) * hgu[:, ff:]
            y = _dot(act.astype(BF16), wd16_ref[...])
            ybuf_ref[g % Y_SLOTS, 0:rows, :] = _pack_bf16_pairs(y)

        lo = 0
        for rows in EXPERT_ROW_STEPS:
            pl.when((valid > lo) & (valid <= rows))(functools.partial(ffn, rows))
            lo = rows
        start_out(g)
        return carry

    lax.fori_loop(0, n_chunks, chunk, 0)

    @pl.when(e == n_e - 1)
    def _():
        for j in range(Y_SLOTS, 0, -1):
            pl.when(n_total >= j)(lambda j=j: wait_out(n_total - j))
        n_rows = y_hbm.shape[0]
        used = tab_ref[TAB_START, e] + ((tab_ref[TAB_COUNT, e] + (ROW_ALIGN - 1)) // ROW_ALIGN) * ROW_ALIGN
        ybuf_ref[0] = jnp.zeros((ch, half), PACKED)
        for j in range(pl.cdiv(n_e * ROW_ALIGN + EXPERT_CHUNK + ROW_ALIGN, ch)):
            at = pl.multiple_of(jnp.minimum(used + j * ch, n_rows - ch), ROW_ALIGN)
            tail = pltpu.make_async_copy(ybuf_ref.at[0], y_hbm.at[pl.ds(at, ch), :], out_sem.at[0])
            tail.start()
            tail.wait()


def _experts(tab, xs, wgu16, w_down_e):
    p, half = xs.shape
    n_e, ff, d = w_down_e.shape
    ch = EXPERT_CHUNK
    grid_spec = pltpu.PrefetchScalarGridSpec(
        num_scalar_prefetch=1,
        grid=(n_e,),
        in_specs=[pl.BlockSpec(memory_space=pl.ANY),
                  pl.BlockSpec((1, d, 2 * ff), lambda e, *_: (e, 0, 0)),
                  pl.BlockSpec((1, ff, d), lambda e, *_: (e, 0, 0))],
        out_specs=pl.BlockSpec(memory_space=pl.ANY),
        scratch_shapes=[pltpu.VMEM((ff, d), BF16),
                        pltpu.VMEM((X_SLOTS, ch, half), PACKED), pltpu.VMEM((Y_SLOTS, ch, half), PACKED),
                        pltpu.SemaphoreType.DMA((X_SLOTS,)), pltpu.SemaphoreType.DMA((Y_SLOTS,))],
    )
    return pl.pallas_call(
        functools.partial(_expert_kernel, n_e=n_e),
        out_shape=jax.ShapeDtypeStruct((p, half), PACKED),
        grid_spec=grid_spec,
        compiler_params=pltpu.CompilerParams(dimension_semantics=("arbitrary",),
                                             vmem_limit_bytes=VMEM_LIMIT_BYTES),
        name="moe_experts",
    )(tab, xs, wgu16, w_down_e)


def _combine_kernel(yg_ref, wcol_ref, x1_ref, hp_ref, mod_ref, npost_ref, wgus_ref, wds_ref, o_ref):
    te, d = x1_ref.shape
    half = d // 2
    ff = wds_ref.shape[0]

    xa, xb = _unpack_bf16_pairs(hp_ref[...])
    hgu = _dot(xa.astype(BF16), wgus_ref[0:half, :]) + _dot(xb.astype(BF16), wgus_ref[half:d, :])
    act = _silu(hgu[:, :ff]) * hgu[:, ff:]
    shared = _dot(act.astype(BF16), wds_ref[...])

    acc_a = shared[:, :half]
    acc_b = shared[:, half:]
    for k in range(TOP_K):
        ya, yb = _unpack_bf16_pairs(yg_ref[k])
        wk = wcol_ref[:, k:k + 1]
        acc_a = acc_a + wk * ya
        acc_b = acc_b + wk * yb
    moe = jnp.concatenate([acc_a, acc_b], axis=-1)
    g2 = mod_ref[:, 5 * d:6 * d]
    o_ref[...] = x1_ref[...] + g2 * _rms(moe, npost_ref[...])


def _combine(dest, y, wcol, x1, hp, mod, npost2, w_gate_s, w_up_s, w_down_s):
    t, d = x1.shape
    half = d // 2
    te = COMBINE_TILE if t % (COMBINE_SPLITS * COMBINE_TILE) == 0 else SEQ_TILE
    wgus =jnp.concatenate([w_gate_s, w_up_s], axis=1).astype(BF16)
    wds = w_down_s.astype(BF16)
    n_split = COMBINE_SPLITS if t % (COMBINE_SPLITS * 2 * SC_GATHER_ROWS * SC_WORKERS // TOP_K) == 0 else 1
    tq = t // n_split
    steps = tq // te
    out = None
    for q in range(n_split):
        dest_q = dest[:, q * tq:(q + 1) * tq]
        yg = _sc_gather_rows(y, dest_q.reshape(-1)).reshape(TOP_K, tq, half)
        row = lambda i, q=q: (q * steps + i, 0)
        in_specs = [pl.BlockSpec((TOP_K, te, half), lambda i: (0, i, 0)),
                    pl.BlockSpec((te, 128), row),
                    pl.BlockSpec((te, d), row),
                    pl.BlockSpec((te, half), row),
                    _const_spec(mod.shape),
                    _const_spec((1, d)),
                    _const_spec(wgus.shape),
                    _const_spec(wds.shape)]
        args = [yg, wcol, x1, hp, mod, npost2.reshape(1, d), wgus, wds]
        body = _combine_kernel
        aliases = {}
        if out is not None:
            in_specs.append(pl.BlockSpec(memory_space=pl.ANY))
            args.append(out)
            body = lambda *refs: _combine_kernel(*refs[:8], refs[9])
            aliases = {8: 0}
        out = pl.pallas_call(
            body,
            out_shape=jax.ShapeDtypeStruct((t, d), F32),
            grid=(steps,),
            in_specs=in_specs,
            out_specs=pl.BlockSpec((te, d), row),
            input_output_aliases=aliases,
            compiler_params=pltpu.CompilerParams(dimension_semantics=("arbitrary",),
                                                 vmem_limit_bytes=VMEM_LIMIT_BYTES),
            name="moe_combine",
        )(*args)
    return out


def kernel(x, c, w_ada, b_ada, norm_pre_mix, norm_post_mix, w_in, conv_w, w_conv_out, hgrn_lower_bounds, hgrn_norm_w, w_hgrn_out, w_o, norm_pre_ffn, norm_post_ffn, w_router, router_bias, w_gate_e, w_up_e, w_down_e, w_gate_s, w_up_s, w_down_s):
    bsz, seq, d = x.shape
    assert bsz == 1 and w_ada.shape[0] == 1
    mod = _ada_mod(c, w_ada[0], b_ada[0])
    x1, wgu16 = _mixer(x.reshape(seq, d), mod, norm_pre_mix[0], norm_post_mix[0], w_in[0], conv_w[0],
                       w_conv_out[0], hgrn_lower_bounds, hgrn_norm_w[0], w_hgrn_out[0], w_o[0],
                       w_gate_e[0], w_up_e[0])
    hp, eidx, rank, wcol, cnt = _router(x1, mod, norm_pre_ffn[0], w_router[0], router_bias[0])
    dest, tab = _dest(eidx, rank, cnt)
    xs = _dispatch(tab, dest, hp, w_router.shape[-1])
    y = _experts(tab, xs, wgu16, w_down_e[0])
    out = _combine(dest, y, wcol, x1, hp, mod, norm_post_ffn[0], w_gate_s[0], w_up_s[0], w_down_s[0])
    return out.reshape(bsz, seq, d)
```

```python
import functools

import jax
import jax.numpy as jnp
from jax import lax
from jax.experimental import pallas as pl
from jax.experimental.pallas import tpu as pltpu
from jax.experimental.pallas import tpu_sc as plsc

F32 = jnp.float32
BF16 = jnp.bfloat16

NORM_EPS = 1e-6
CONV_DIM = 512
CONV_WIDTH = 3
HGRN_HEADS = 4
HGRN_DK = 128
HGRN_QK = HGRN_HEADS * HGRN_DK
N_GROUPS = 8
TOPK_GROUPS = 4
TOP_K = 8
ROUTED_SCALE = 2.5

SEQ_TILE = 256
EXPERT_CHUNK = 512
EXPERT_ROW_STEPS = (128, 256, EXPERT_CHUNK)
ROW_ALIGN = 8
VMEM_LIMIT_BYTES = 56 * 1024 * 1024


def _dot(a, b):
    return jnp.dot(a, b, preferred_element_type=F32)


def _dot_nt(a, b):
    return lax.dot_general(a, b, (((1,), (1,)), ((), ())), preferred_element_type=F32)


def _dot_tn(a, b):
    return lax.dot_general(a, b, (((0,), (0,)), ((), ())), preferred_element_type=F32)


def _split3(x):
    hi = x.astype(BF16)
    r1 = x - hi.astype(F32)
    mid = r1.astype(BF16)
    lo = (r1 - mid.astype(F32)).astype(BF16)
    return hi, mid, lo


def _sigmoid(x):
    return 1.0 / (1.0 + jnp.exp(-x))


def _silu(x):
    return x * _sigmoid(x)


def _rms(x, w):
    ms = jnp.mean(x * x, axis=-1, keepdims=True)
    return x * lax.rsqrt(ms + NORM_EPS) * w


def _ada_kernel(c_ref, w_ref, b_ref, o_ref):
    cs = _silu(c_ref[...])
    h1, h2, h3 = _split3(cs)
    w1, w2, w3 = _split3(w_ref[...])
    acc = _dot(h1, w1) + (_dot(h1, w2) + _dot(h2, w1)) + (_dot(h1, w3) + _dot(h2, w2) + _dot(h3, w1))
    o_ref[...] = acc + b_ref[...]


def _ada_mod(c, w_ada, b_ada):
    d = c.shape[-1]
    n = w_ada.shape[-1]
    bn = 1024
    c8 = jnp.broadcast_to(c.reshape(1, d), (8, d))
    out = pl.pallas_call(
        _ada_kernel,
        out_shape=jax.ShapeDtypeStruct((8, n), F32),
        grid=(n // bn,),
        in_specs=[pl.BlockSpec((8, d), lambda j: (0, 0)),
                  pl.BlockSpec((d, bn), lambda j: (0, j)),
                  pl.BlockSpec((1, bn), lambda j: (0, j))],
        out_specs=pl.BlockSpec((8, bn), lambda j: (0, j)),
        compiler_params=pltpu.CompilerParams(dimension_semantics=("arbitrary",),
                                             vmem_limit_bytes=VMEM_LIMIT_BYTES),
        name="ada_mod",
    )(c8, w_ada, b_ada.reshape(1, n))
    return out[0:1]


def _level_reference(b, rolled, s, row):
    ts = b.shape[0]
    c = s // 2 - 1
    if s >= 16:
        pieces = []
        for blk in range(ts // s):
            r = blk * s + c
            pieces.append(jnp.broadcast_to(b[r:r + 1, :], (s, b.shape[1])))
        return pieces[0] if len(pieces) == 1 else jnp.concatenate(pieces, axis=0)
    pos = row & (s - 1)
    out = b
    for p in range(s):
        if p == c:
            continue
        out = jnp.where(pos == p, rolled[p - c], out)
    return out


def _cast_expert_weights(wg_ref, wu_ref, wgu16_ref):
    ff = wg_ref.shape[2]
    for j in range(wg_ref.shape[0]):
        wgu16_ref[j, :, 0:ff] = wg_ref[j].astype(BF16)
        wgu16_ref[j, :, ff:2 * ff] = wu_ref[j].astype(BF16)


def _mixer_kernel(x_ref, mod_ref, npre_ref, npost_ref, win_ref, convw_ref, wco_ref,
                  hlb_ref, hnw_ref, who_ref, wo_ref, wg_ref, wu_ref,
                  o_ref, wgu16_ref, state_ref, ubuf_ref):
    ts, d = x_ref.shape
    step = pl.program_id(0)
    _cast_expert_weights(wg_ref, wu_ref, wgu16_ref)

    @pl.when(step == 0)
    def _():
        state_ref[...] = jnp.zeros_like(state_ref)
        ubuf_ref[0:8, :] = jnp.zeros((8, CONV_DIM), F32)

    x = x_ref[...]
    sh1 = mod_ref[:, 0:d]
    sc1 = mod_ref[:, d:2 * d]
    g1 = mod_ref[:, 2 * d:3 * d]
    h = (_rms(x, npre_ref[...]) * (1.0 + sc1) + sh1).astype(BF16)

    def proj(lo, width):
        return _dot(h, win_ref[:, lo:lo + width])

    c0 = 0
    cb = proj(c0, CONV_DIM)
    u = proj(c0 + CONV_DIM, CONV_DIM) * proj(c0 + 2 * CONV_DIM, CONV_DIM)
    ubuf_ref[8:8 + ts, :] = u
    conv = (ubuf_ref[6:6 + ts, :] * convw_ref[0:1, :]
            + ubuf_ref[7:7 + ts, :] * convw_ref[1:2, :]
            + u * convw_ref[2:3, :])
    ubuf_ref[0:8, :] = ubuf_ref[ts:ts + 8, :]
    y_a = _dot((cb * conv).astype(BF16), wco_ref[...])

    c1 = 3 * CONV_DIM
    q = _silu(proj(c1, HGRN_QK))
    hl = hlb_ref[...]
    hm = jnp.max(hl, axis=0, keepdims=True)
    he = jnp.exp(hl - hm)
    lb = he[0:1, :] / jnp.sum(he, axis=0, keepdims=True)
    fg = lb + (1.0 - lb) * _sigmoid(proj(c1 + HGRN_QK, HGRN_QK))
    k = 1.0 - fg
    g = jnp.log(fg)
    v = proj(c1 + 2 * HGRN_QK, HGRN_QK)
    gg = proj(c1 + 3 * HGRN_QK, HGRN_QK)

    row = lax.broadcasted_iota(jnp.int32, (ts, ts), 0)
    col = lax.broadcasted_iota(jnp.int32, (ts, ts), 1)
    tril = jnp.where(row >= col, 1.0, 0.0).astype(BF16)
    ghi, gmid, glo = _split3(g)
    b = _dot(tril, ghi) + _dot(tril, gmid) + _dot(tril, glo)
    b_last = b[ts - 1:ts, :]

    rowq = lax.broadcasted_iota(jnp.int32, (ts, HGRN_QK), 0)
    rolled = {sft: pltpu.roll(b, sft % ts, axis=0) for sft in (-3, -2, -1, 1, 2, 3, 4)}
    xor = row ^ col

    levels = []
    s = ts
    while s >= 2:
        levels.append(s)
        s //= 2
    qt, kt = [], []
    for s in levels:
        bref = _level_reference(b, rolled, s, rowq)
        e = jnp.exp(-jnp.abs(b - bref))
        upper = (rowq & (s // 2)) != 0
        qt.append(jnp.where(upper, q * e, 0.0).astype(BF16))
        kt.append(jnp.where(upper, 0.0, k * e).astype(BF16))
    q_in = (q * jnp.exp(b)).astype(BF16)
    k_out = (k * jnp.exp(b_last - b)).astype(BF16)
    v16 = v.astype(BF16)
    qk = q * k
    decay_last = jnp.exp(b_last)

    hnw = hnw_ref[...]
    outs = []
    for hd in range(HGRN_HEADS):
        sl = slice(hd * HGRN_DK, (hd + 1) * HGRN_DK)
        a = jnp.zeros((ts, ts), F32)
        for li in range(len(levels) - 1, -1, -1):
            s = levels[li]
            p = _dot_nt(qt[li][:, sl], kt[li][:, sl])
            a = jnp.where(xor >= s // 2, p, a)
        st = state_ref[hd]
        o_h = (_dot(a.astype(BF16), v16[:, sl])
               + jnp.sum(qk[:, sl], axis=-1, keepdims=True) * v[:, sl]
               + _dot_nt(q_in[:, sl], st.astype(BF16)))
        state_ref[hd] = st * decay_last[:, sl] + _dot_tn(v16[:, sl], k_out[:, sl])
        outs.append(_rms(o_h, hnw))
    o = jnp.concatenate(outs, axis=-1) * _silu(gg)
    y_b = _dot(o.astype(BF16), who_ref[...])

    c2 = c1 + 4 * HGRN_QK
    m = _sigmoid(proj(c2, d)) * y_a + _sigmoid(proj(c2 + d, d)) * y_b
    y = _dot(m.astype(BF16), wo_ref[...])
    o_ref[...] = x + g1 * _rms(y, npost_ref[...])


def _const_spec(shape):
    nd = len(shape)
    return pl.BlockSpec(shape, lambda i: (0,) * nd)


def _mixer(x2, mod, npre, npost, w_in, conv_w, w_conv_out, hlb, hnw, w_hgrn_out, w_o, w_gate_e, w_up_e):
    t, d = x2.shape
    ts = SEQ_TILE
    n = t // ts
    n_e, _, ff = w_gate_e.shape
    assert n_e % n == 0
    per = n_e // n
    args = (x2, mod, npre.reshape(1, d), npost.reshape(1, d), w_in.astype(BF16), conv_w,
            w_conv_out.astype(BF16), hlb, hnw.reshape(1, -1), w_hgrn_out.astype(BF16),
            w_o.astype(BF16))
    experts = lambda i: (i, 0, 0)
    in_specs = ([pl.BlockSpec((ts, d), lambda i: (i, 0))] + [_const_spec(a.shape) for a in args[1:]]
                + [pl.BlockSpec((per, d, ff), experts)] * 2)
    return pl.pallas_call(
        _mixer_kernel,
        out_shape=[jax.ShapeDtypeStruct((t, d), F32), jax.ShapeDtypeStruct((n_e, d, 2 * ff), BF16)],
        grid=(n,),
        in_specs=in_specs,
        out_specs=[pl.BlockSpec((ts, d), lambda i: (i, 0)), pl.BlockSpec((per, d, 2 * ff), experts)],
        scratch_shapes=[pltpu.VMEM((HGRN_HEADS, HGRN_DK, HGRN_DK), F32),
                        pltpu.VMEM((ts + 8, CONV_DIM), F32)],
        compiler_params=pltpu.CompilerParams(dimension_semantics=("arbitrary",),
                                             vmem_limit_bytes=VMEM_LIMIT_BYTES),
        name="token_mixer",
    )(*args, w_gate_e, w_up_e)


PACKED = jnp.int32


def _pack_bf16_pairs(x):
    m = x.shape[1] // 2
    hi = lax.bitcast_convert_type(x[:, :m].astype(BF16).astype(F32), jnp.uint32)
    lo = lax.bitcast_convert_type(x[:, m:].astype(BF16).astype(F32), jnp.uint32)
    return lax.bitcast_convert_type(hi | (lo >> 16), PACKED)


def _unpack_bf16_pairs(p):
    p = lax.bitcast_convert_type(p, jnp.uint32)
    hi = lax.bitcast_convert_type(p & jnp.uint32(0xFFFF0000), F32)
    lo = lax.bitcast_convert_type(p << 16, F32)
    return hi, lo


def _router_kernel(x1_ref, mod_ref, npre_ref, wrh_ref, wrl_ref, rb_ref,
                   hp_ref, eidx_ref, rank_ref, wcol_ref, cnt_ref, carry_ref, rows_ref):
    assert N_GROUPS == TOP_K == rows_ref.shape[0]
    tr, d = x1_ref.shape
    n_e = wrh_ref.shape[0]
    gsz = n_e // N_GROUPS
    neg = -jnp.inf

    @pl.when(pl.program_id(0) == 0)
    def _():
        carry_ref[...] = jnp.zeros_like(carry_ref)

    sh2 = mod_ref[:, 3 * d:4 * d]
    sc2 = mod_ref[:, 4 * d:5 * d]
    h2 = _rms(x1_ref[...], npre_ref[...]) * (1.0 + sc2) + sh2
    hp_ref[...] = _pack_bf16_pairs(h2)
    h_hi = h2.astype(BF16)
    h_lo = (h2 - h_hi.astype(F32)).astype(BF16)
    wrh = wrh_ref[...]
    logits = _dot_nt(wrh, h_hi) + (_dot_nt(wrh, h_lo) + _dot_nt(wrl_ref[...], h_hi))
    scores = _sigmoid(logits)
    sel = scores + rb_ref[:, 0:1]

    io_e = lax.broadcasted_iota(jnp.int32, (n_e, tr), 0)
    for g in range(N_GROUPS):
        blk = sel[g * gsz:(g + 1) * gsz, :]
        io = lax.broadcasted_iota(jnp.int32, (gsz, tr), 0) + g * gsz
        m1 = jnp.max(blk, axis=0, keepdims=True)
        i1 = jnp.min(jnp.where(blk == m1, io, n_e), axis=0, keepdims=True)
        m2 = jnp.max(jnp.where(io == i1, neg, blk), axis=0, keepdims=True)
        rows_ref[g:g + 1, :] = m1 + m2
    gs = rows_ref[...]
    io_g = lax.broadcasted_iota(jnp.int32, (N_GROUPS, tr), 0)
    gsel = jnp.zeros((N_GROUPS, tr), F32)
    for _ in range(TOPK_GROUPS):
        m = jnp.max(gs, axis=0, keepdims=True)
        gi = jnp.min(jnp.where(gs == m, io_g, N_GROUPS), axis=0, keepdims=True)
        hit = io_g == gi
        gsel = jnp.where(hit, 1.0, gsel)
        gs = jnp.where(hit, neg, gs)
    rows_ref[...] = gsel
    cur = jnp.concatenate(
        [jnp.where(rows_ref[g:g + 1, :] > 0.5, sel[g * gsz:(g + 1) * gsz, :], neg) for g in range(N_GROUPS)],
        axis=0)

    idxs = []
    selm = jnp.zeros((n_e, tr), F32)
    for k in range(TOP_K):
        m = jnp.max(cur, axis=0, keepdims=True)
        idx = jnp.min(jnp.where(cur == m, io_e, n_e), axis=0, keepdims=True)
        hit = io_e == idx
        rows_ref[k:k + 1, :] = jnp.sum(jnp.where(hit, scores, 0.0), axis=0, keepdims=True)
        cur = jnp.where(hit, neg, cur)
        selm = jnp.where(hit, 1.0, selm)
        eidx_ref[k:k + 1, :] = idx
        idxs.append(idx)

    r_io = lax.broadcasted_iota(jnp.int32, (tr, tr + 128), 0)
    c_io = lax.broadcasted_iota(jnp.int32, (tr, tr + 128), 1)
    before = jnp.where((r_io < c_io) | (c_io >= tr), 1.0, 0.0).astype(BF16)
    r_ext = _dot(selm.astype(BF16), before)
    rank_full = r_ext[:, :tr] + carry_ref[:, 0:1]
    for k, idx in enumerate(idxs):
        rk = jnp.sum(jnp.where(io_e == idx, rank_full, 0.0), axis=0, keepdims=True)
        rank_ref[k:k + 1, :] = rk.astype(jnp.int32)
    carry_ref[...] = carry_ref[...] + r_ext[:, tr:]
    cnt_ref[...] = carry_ref[...]

    wk = rows_ref[...]
    wn = wk / (jnp.sum(wk, axis=0, keepdims=True) + 1e-20) * ROUTED_SCALE
    eye = jnp.where(lax.broadcasted_iota(jnp.int32, (TOP_K, 128), 0)
                    == lax.broadcasted_iota(jnp.int32, (TOP_K, 128), 1), 1.0, 0.0).astype(BF16)
    w1, w2, w3 = _split3(wn)
    wcol_ref[...] = _dot_tn(w1, eye) + _dot_tn(w2, eye) + _dot_tn(w3, eye)


def _router(x1, mod, npre2, w_router, router_bias):
    t, d = x1.shape
    n_e = w_router.shape[1]
    tr = SEQ_TILE
    wrt = w_router.T
    wrh = wrt.astype(BF16)
    wrl = (wrt - wrh.astype(F32)).astype(BF16)
    rb = jnp.broadcast_to(router_bias.reshape(n_e, 1), (n_e, 128))
    args = (x1, mod, npre2.reshape(1, d), wrh, wrl, rb)
    in_specs = [pl.BlockSpec((tr, d), lambda i: (i, 0))] + [_const_spec(a.shape) for a in args[1:]]
    return pl.pallas_call(
        _router_kernel,
        out_shape=(jax.ShapeDtypeStruct((t, d // 2), PACKED),
                   jax.ShapeDtypeStruct((TOP_K, t), jnp.int32),
                   jax.ShapeDtypeStruct((TOP_K, t), jnp.int32),
                   jax.ShapeDtypeStruct((t, 128), F32),
                   jax.ShapeDtypeStruct((n_e, 128), F32)),
        grid=(t // tr,),
        in_specs=in_specs,
        out_specs=(pl.BlockSpec((tr, d // 2), lambda i: (i, 0)),
                   pl.BlockSpec((TOP_K, tr), lambda i: (0, i)),
                   pl.BlockSpec((TOP_K, tr), lambda i: (0, i)),
                   pl.BlockSpec((tr, 128), lambda i: (i, 0)),
                   pl.BlockSpec((n_e, 128), lambda i: (0, 0))),
        scratch_shapes=[pltpu.VMEM((n_e, 128), F32), pltpu.VMEM((TOP_K, tr), F32)],
        compiler_params=pltpu.CompilerParams(dimension_semantics=("arbitrary",),
                                             vmem_limit_bytes=VMEM_LIMIT_BYTES),
        name="moe_router",
    )(*args)


TAB_CHUNK_ROW, TAB_CHUNK_VALID, TAB_START, TAB_COUNT, TAB_CHUNK0, TAB_NCHUNK, TAB_TOTAL = range(7)
TAB_ROWS = 8


def _excl_cumsum_rows(lower, x):
    hi = jnp.floor(x * (1.0 / 128.0))
    lo = x - hi * 128.0
    return 128.0 * _dot(lower, hi.astype(BF16)) + _dot(lower, lo.astype(BF16))


def _dest_kernel(eidx_ref, rank_ref, cnt_ref, dest_ref, tab_ref):
    n_e = cnt_ref.shape[0]
    tt = eidx_ref.shape[1]
    n_g = tab_ref.shape[1]
    ch = float(EXPERT_CHUNK)
    cnt = jnp.floor((cnt_ref[...] + (ROW_ALIGN - 1)) * (1.0 / ROW_ALIGN)) * ROW_ALIGN
    r = lax.broadcasted_iota(jnp.int32, (n_e, n_e), 0)
    c = lax.broadcasted_iota(jnp.int32, (n_e, n_e), 1)
    lower = jnp.where(c < r, 1.0, 0.0).astype(BF16)
    starts = _excl_cumsum_rows(lower, cnt)
    start_col = starts[:, 0:1]

    @pl.when(pl.program_id(0) == 0)
    def _():
        nch = jnp.floor((cnt + (ch - 1.0)) * (1.0 / ch))
        chunk0 = _excl_cumsum_rows(lower, nch)
        cend_col = (chunk0 + nch)[:, 0:1]
        eye = r == c

        def as_row(col):
            return jnp.sum(jnp.where(eye, col, 0.0), axis=0, keepdims=True)

        g = lax.broadcasted_iota(jnp.int32, (n_e, n_g), 1).astype(F32)
        owner = jnp.sum(jnp.where(cend_col <= g, 1.0, 0.0), axis=0, keepdims=True)
        mine = lax.broadcasted_iota(jnp.int32, (n_e, n_g), 0).astype(F32) == owner
        g_row = lax.broadcasted_iota(jnp.int32, (1, n_g), 1).astype(F32)
        base = jnp.sum(jnp.where(mine, start_col - chunk0[:, 0:1] * ch, 0.0), axis=0, keepdims=True)
        left = jnp.sum(jnp.where(mine, cnt[:, 0:1] + chunk0[:, 0:1] * ch, 0.0), axis=0, keepdims=True)
        pad = jnp.zeros((1, n_g - n_e), F32)

        def wide(row):
            return jnp.concatenate([row, pad], axis=1)

        total = jnp.sum(nch[:, 0:1], axis=0, keepdims=True)
        tab_ref[...] = jnp.zeros(tab_ref.shape, jnp.int32)
        tab_ref[TAB_CHUNK_ROW:TAB_CHUNK_ROW + 1, :] = (base + g_row * ch).astype(jnp.int32)
        tab_ref[TAB_CHUNK_VALID:TAB_CHUNK_VALID + 1, :] = jnp.clip(left - g_row * ch, 0.0, ch).astype(jnp.int32)
        tab_ref[TAB_START:TAB_START + 1, :] = wide(as_row(start_col)).astype(jnp.int32)
        tab_ref[TAB_COUNT:TAB_COUNT + 1, :] = wide(as_row(cnt_ref[:, 0:1])).astype(jnp.int32)
        tab_ref[TAB_CHUNK0:TAB_CHUNK0 + 1, :] = wide(as_row(chunk0[:, 0:1])).astype(jnp.int32)
        tab_ref[TAB_NCHUNK:TAB_NCHUNK + 1, :] = wide(as_row(nch[:, 0:1])).astype(jnp.int32)
        tab_ref[TAB_TOTAL:TAB_TOTAL + 1, :] = jnp.broadcast_to(total, (1, n_g)).astype(jnp.int32)

    io_e = lax.broadcasted_iota(jnp.int32, (n_e, tt), 0)
    rows = []
    for k in range(TOP_K):
        hit = io_e == eidx_ref[k:k + 1, :]
        rows.append(jnp.sum(jnp.where(hit, start_col, 0.0), axis=0, keepdims=True))
    dest_ref[...] = jnp.concatenate(rows, axis=0).astype(jnp.int32) + rank_ref[...]


def _dest(eidx, rank, cnt):
    k, t = eidx.shape
    n_e = cnt.shape[0]
    tt = 512
    n_g = (t * k) // EXPERT_CHUNK + n_e
    return pl.pallas_call(
        _dest_kernel,
        out_shape=(jax.ShapeDtypeStruct((k, t), jnp.int32),
                   jax.ShapeDtypeStruct((TAB_ROWS, n_g), jnp.int32)),
        grid=(t // tt,),
        in_specs=[pl.BlockSpec((k, tt), lambda i: (0, i)),
                  pl.BlockSpec((k, tt), lambda i: (0, i)),
                  _const_spec(cnt.shape)],
        out_specs=(pl.BlockSpec((k, tt), lambda i: (0, i)), _const_spec((TAB_ROWS, n_g))),
        compiler_params=pltpu.CompilerParams(dimension_semantics=("arbitrary",)),
        name="moe_dest",
    )(eidx, rank, cnt)


SC_CORES = 2
SC_SUBCORES = 16
SC_WORKERS = SC_CORES * SC_SUBCORES
SC_ROWS = 128


def _sc_mesh():
    return plsc.VectorSubcoreMesh(core_axis_name="c", subcore_axis_name="s")


def _sc_worker_id():
    return lax.axis_index("s") * SC_CORES + lax.axis_index("c")


def _sc_scatter_rows(rows, dest_flat, n_out):
    t, w = rows.shape
    n_k = dest_flat.shape[0] // t
    r = SC_GATHER_ROWS
    per = t // SC_WORKERS
    n_steps = per // r
    assert per % (2 * r) == 0

    @functools.partial(
        pl.kernel, mesh=_sc_mesh(), out_type=jax.ShapeDtypeStruct((n_out, w), rows.dtype),
        scratch_types=[pltpu.VMEM((n_k, r), jnp.int32), pltpu.VMEM((2, r, w), rows.dtype),
                       pltpu.SemaphoreType.DMA((2,)), pltpu.SemaphoreType.DMA],
        name="moe_dispatch_sc")
    def scatter(rows_hbm, idx_hbm, out_hbm, idx_v, rows_v, l_sem, s_sem):
        base = _sc_worker_id() * per

        def load(j, b):
            return pltpu.make_async_copy(rows_hbm.at[pl.ds(pl.multiple_of(base + j * r, ROW_ALIGN), r)],
                                         rows_v.at[b], l_sem.at[b])

        load(0, 0).start()

        @pl.loop(0, n_steps, step=2)
        def _(j0):
            for b in range(2):
                j = j0 + b
                pl.when(j + 1 < n_steps)(load(j + 1, 1 - b).start)
                for k in range(n_k):
                    off = pl.multiple_of(k * t + base + j * r, ROW_ALIGN)
                    pltpu.sync_copy(idx_hbm.at[pl.ds(off, r)], idx_v.at[k])
                load(j, b).wait()
                copies = [pltpu.make_async_copy(rows_v.at[b], out_hbm.at[idx_v.at[k]], s_sem) for k in range(n_k)]
                for cp in copies:
                    cp.start()
                for cp in copies:
                    cp.wait()

    return scatter(rows, dest_flat)


SC_GATHER_ROWS = 64
COMBINE_SPLITS = 4


def _sc_gather_rows(table, idx):
    n = idx.shape[0]
    w = table.shape[1]
    r = SC_GATHER_ROWS
    per = n // SC_WORKERS
    n_steps = per // r
    assert per % (2 * r) == 0

    @functools.partial(
        pl.kernel, mesh=_sc_mesh(), out_type=jax.ShapeDtypeStruct((n, w), table.dtype),
        scratch_types=[pltpu.VMEM((2, r), jnp.int32), pltpu.VMEM((2, r, w), table.dtype),
                       pltpu.SemaphoreType.DMA((2,)), pltpu.SemaphoreType.DMA((2,))],
        name="moe_gather_sc")
    def gather(table_hbm, idx_hbm, out_hbm, idx_v, rows_v, g_sem, w_sem):
        base = _sc_worker_id() * per

        def at(j):
            return pl.ds(pl.multiple_of(base + j * r, ROW_ALIGN), r)

        def fetch(j, b):
            return pltpu.make_async_copy(table_hbm.at[idx_v.at[b]], rows_v.at[b], g_sem.at[b])

        def write(j, b):
            return pltpu.make_async_copy(rows_v.at[b], out_hbm.at[at(j)], w_sem.at[b])

        pltpu.sync_copy(idx_hbm.at[at(0)], idx_v.at[0])
        fetch(0, 0).start()

        @pl.loop(0, n_steps, step=2)
        def _(j0):
            for b in range(2):
                j = j0 + b

                @pl.when(j + 1 < n_steps)
                def _():
                    pl.when(j >= 1)(write(j - 1, 1 - b).wait)
                    pltpu.sync_copy(idx_hbm.at[at(j + 1)], idx_v.at[1 - b])
                    fetch(j + 1, 1 - b).start()

                fetch(j, b).wait()
                write(j, b).start()

        write(n_steps - 2, 0).wait()
        write(n_steps - 1, 1).wait()

    return gather(table, idx)


def _pad_kernel(tab_ref, xs_in, xs_hbm, zero_ref, sem, *, n_e):
    del xs_in
    zero_ref[...] = jnp.zeros_like(zero_ref)

    def pad_row_copy(row):
        return pltpu.make_async_copy(zero_ref.at[pl.ds(0, 1), :], xs_hbm.at[pl.ds(row, 1), :], sem)

    def pad_rows(e, carry, wait):
        count = tab_ref[TAB_COUNT, e]
        end = tab_ref[TAB_START, e] + count
        n_pad = (ROW_ALIGN - count % ROW_ALIGN) % ROW_ALIGN
        for j in range(ROW_ALIGN - 1):
            copy = pad_row_copy(end + j)
            pl.when(j < n_pad)(copy.wait if wait else copy.start)
        return carry

    lax.fori_loop(0, n_e, functools.partial(pad_rows, wait=False), 0)
    lax.fori_loop(0, n_e, functools.partial(pad_rows, wait=True), 0)
    last = tab_ref[TAB_COUNT, n_e - 1]
    used = tab_ref[TAB_START, n_e - 1] + ((last + (ROW_ALIGN - 1)) // ROW_ALIGN) * ROW_ALIGN
    piece = zero_ref.shape[0]
    n_rows = xs_hbm.shape[0]
    for j in range(pl.cdiv(n_e * ROW_ALIGN + EXPERT_CHUNK + ROW_ALIGN, piece)):
        at = pl.multiple_of(jnp.minimum(used + j * piece, n_rows - piece), ROW_ALIGN)
        tail = pltpu.make_async_copy(zero_ref, xs_hbm.at[pl.ds(at, piece), :], sem)
        tail.start()
        tail.wait()


def _sorted_rows(t, n_e):
    return t * TOP_K + n_e * ROW_ALIGN + EXPERT_CHUNK + ROW_ALIGN


def _dispatch(tab, dest, hp, n_e):
    t, m = hp.shape
    n_rows = _sorted_rows(t, n_e)
    xs = _sc_scatter_rows(hp, dest.reshape(-1), n_rows)
    grid_spec = pltpu.PrefetchScalarGridSpec(
        num_scalar_prefetch=1,
        grid=(1,),
        in_specs=[pl.BlockSpec(memory_space=pl.ANY)],
        out_specs=pl.BlockSpec(memory_space=pl.ANY),
        scratch_shapes=[pltpu.VMEM((EXPERT_CHUNK, m), PACKED), pltpu.SemaphoreType.DMA(())],
    )
    return pl.pallas_call(
        functools.partial(_pad_kernel, n_e=n_e),
        out_shape=jax.ShapeDtypeStruct((n_rows, m), PACKED),
        grid_spec=grid_spec,
        input_output_aliases={1: 0},
        compiler_params=pltpu.CompilerParams(dimension_semantics=("arbitrary",)),
        name="moe_pad",
    )(tab, xs)


X_LOOKAHEAD = 2
X_SLOTS = X_LOOKAHEAD + 1
Y_SLOTS = 2


def _expert_kernel(tab_ref, xs_hbm, wgu_ref, wd_ref, y_hbm,
                   wd16_ref, xbuf_ref, ybuf_ref, in_sem, out_sem, *, n_e):
    e = pl.program_id(0)
    ch, half = xbuf_ref.shape[1:]
    ff = wd_ref.shape[1]
    n_total = tab_ref[TAB_TOTAL, 0]
    first = tab_ref[TAB_CHUNK0, e]
    n_chunks = tab_ref[TAB_NCHUNK, e]

    def in_copy(g):
        slot = g % X_SLOTS
        row = pl.multiple_of(tab_ref[TAB_CHUNK_ROW, g], ROW_ALIGN)
        return pltpu.make_async_copy(xs_hbm.at[pl.ds(row, ch), :], xbuf_ref.at[slot], in_sem.at[slot])

    def out_copies(g):
        slot = g % Y_SLOTS
        row = tab_ref[TAB_CHUNK_ROW, g]
        valid = tab_ref[TAB_CHUNK_VALID, g]
        yield valid >= ch, pltpu.make_async_copy(ybuf_ref.at[slot], y_hbm.at[pl.ds(pl.multiple_of(row, ROW_ALIGN), ch), :],
                                                 out_sem.at[slot])
        size = ch // 2
        while size >= ROW_ALIGN:
            off = pl.multiple_of((valid // (2 * size)) * (2 * size), ROW_ALIGN)
            cond = (valid < ch) & ((valid & size) != 0)
            yield cond, pltpu.make_async_copy(ybuf_ref.at[slot, pl.ds(off, size), :],
                                              y_hbm.at[pl.ds(pl.multiple_of(row + off, ROW_ALIGN), size), :],
                                              out_sem.at[slot])
            size //= 2

    def start_out(g):
        for cond, cp in out_copies(g):
            pl.when(cond)(cp.start)

    def wait_out(g):
        for cond, cp in out_copies(g):
            pl.when(cond)(cp.wait)

    @pl.when(e == 0)
    def _():
        for j in range(X_LOOKAHEAD):
            pl.when(j < n_total)(in_copy(j).start)

    @pl.when(n_chunks > 0)
    def _():
        wd16_ref[...] = wd_ref[0].astype(BF16)

    def chunk(c, carry):
        g = first + c
        pl.when(g + X_LOOKAHEAD < n_total)(in_copy(g + X_LOOKAHEAD).start)
        in_copy(g).wait()
        pl.when(g >= Y_SLOTS)(lambda: wait_out(g - Y_SLOTS))
        valid = tab_ref[TAB_CHUNK_VALID, g]

        def ffn(rows):
            xa, xb = _unpack_bf16_pairs(xbuf_ref[g % X_SLOTS, 0:rows, :])
            hgu = (_dot(xa.astype(BF16), wgu_ref[0, 0:half, :])
                   + _dot(xb.astype(BF16), wgu_ref[0, half:2 * half, :]))
            act = _silu(hgu[:, :ff]) * hgu[:, ff:]
            y = _dot(act.astype(BF16), wd16_ref[...])
            ybuf_ref[g % Y_SLOTS, 0:rows, :] = _pack_bf16_pairs(y)

        lo = 0
        for rows in EXPERT_ROW_STEPS:
            pl.when((valid > lo) & (valid <= rows))(functools.partial(ffn, rows))
            lo = rows
        start_out(g)
        return carry

    lax.fori_loop(0, n_chunks, chunk, 0)

    @pl.when(e == n_e - 1)
    def _():
        for j in range(Y_SLOTS, 0, -1):
            pl.when(n_total >= j)(lambda j=j: wait_out(n_total - j))
        n_rows = y_hbm.shape[0]
        used = tab_ref[TAB_START, e] + ((tab_ref[TAB_COUNT, e] + (ROW_ALIGN - 1)) // ROW_ALIGN) * ROW_ALIGN
        ybuf_ref[0] = jnp.zeros((ch, half), PACKED)
        for j in range(pl.cdiv(n_e * ROW_ALIGN + EXPERT_CHUNK + ROW_ALIGN, ch)):
            at = pl.multiple_of(jnp.minimum(used + j * ch, n_rows - ch), ROW_ALIGN)
            tail = pltpu.make_async_copy(ybuf_ref.at[0], y_hbm.at[pl.ds(at, ch), :], out_sem.at[0])
            tail.start()
            tail.wait()


def _experts(tab, xs, wgu16, w_down_e):
    p, half = xs.shape
    n_e, ff, d = w_down_e.shape
    ch = EXPERT_CHUNK
    grid_spec = pltpu.PrefetchScalarGridSpec(
        num_scalar_prefetch=1,
        grid=(n_e,),
        in_specs=[pl.BlockSpec(memory_space=pl.ANY),
                  pl.BlockSpec((1, d, 2 * ff), lambda e, *_: (e, 0, 0)),
                  pl.BlockSpec((1, ff, d), lambda e, *_: (e, 0, 0))],
        out_specs=pl.BlockSpec(memory_space=pl.ANY),
        scratch_shapes=[pltpu.VMEM((ff, d), BF16),
                        pltpu.VMEM((X_SLOTS, ch, half), PACKED), pltpu.VMEM((Y_SLOTS, ch, half), PACKED),
                        pltpu.SemaphoreType.DMA((X_SLOTS,)), pltpu.SemaphoreType.DMA((Y_SLOTS,))],
    )
    return pl.pallas_call(
        functools.partial(_expert_kernel, n_e=n_e),
        out_shape=jax.ShapeDtypeStruct((p, half), PACKED),
        grid_spec=grid_spec,
        compiler_params=pltpu.CompilerParams(dimension_semantics=("arbitrary",),
                                             vmem_limit_bytes=VMEM_LIMIT_BYTES),
        name="moe_experts",
    )(tab, xs, wgu16, w_down_e)


def _combine_kernel(yg_ref, wcol_ref, x1_ref, hp_ref, mod_ref, npost_ref, wgus_ref, wds_ref, o_ref):
    te, d = x1_ref.shape
    half = d // 2
    ff = wds_ref.shape[0]

    xa, xb = _unpack_bf16_pairs(hp_ref[...])
    hgu = _dot(xa.astype(BF16), wgus_ref[0:half, :]) + _dot(xb.astype(BF16), wgus_ref[half:d, :])
    act = _silu(hgu[:, :ff]) * hgu[:, ff:]
    shared = _dot(act.astype(BF16), wds_ref[...])

    acc_a = shared[:, :half]
    acc_b = shared[:, half:]
    for k in range(TOP_K):
        ya, yb = _unpack_bf16_pairs(yg_ref[k])
        wk = wcol_ref[:, k:k + 1]
        acc_a = acc_a + wk * ya
        acc_b = acc_b + wk * yb
    moe = jnp.concatenate([acc_a, acc_b], axis=-1)
    g2 = mod_ref[:, 5 * d:6 * d]
    o_ref[...] = x1_ref[...] + g2 * _rms(moe, npost_ref[...])


def _combine(dest, y, wcol, x1, hp, mod, npost2, w_gate_s, w_up_s, w_down_s):
    t, d = x1.shape
    half = d // 2
    te = SEQ_TILE
    wgus = jnp.concatenate([w_gate_s, w_up_s], axis=1).astype(BF16)
    wds = w_down_s.astype(BF16)
    n_split = COMBINE_SPLITS if t % (COMBINE_SPLITS * 2 * SC_GATHER_ROWS * SC_WORKERS // TOP_K) == 0 else 1
    tq = t // n_split
    steps = tq // te
    stream = x1
    for q in range(n_split):
        dest_q = dest[:, q * tq:(q + 1) * tq]
        yg = _sc_gather_rows(y, dest_q.reshape(-1)).reshape(TOP_K, tq, half)
        row = lambda i, q=q: (q * steps + i, 0)
        stream = pl.pallas_call(
            _combine_kernel,
            out_shape=jax.ShapeDtypeStruct((t, d), F32),
            grid=(steps,),
            in_specs=[pl.BlockSpec((TOP_K, te, half), lambda i: (0, i, 0)),
                      pl.BlockSpec((te, 128), row),
                      pl.BlockSpec((te, d), row),
                      pl.BlockSpec((te, half), row),
                      _const_spec(mod.shape),
                      _const_spec((1, d)),
                      _const_spec(wgus.shape),
                      _const_spec(wds.shape)],
            out_specs=pl.BlockSpec((te, d), row),
            input_output_aliases={2: 0},
            compiler_params=pltpu.CompilerParams(dimension_semantics=("arbitrary",),
                                                 vmem_limit_bytes=VMEM_LIMIT_BYTES),
            name="moe_combine",
        )(yg, wcol, stream, hp, mod, npost2.reshape(1, d), wgus, wds)
    return stream


def kernel(x, c, w_ada, b_ada, norm_pre_mix, norm_post_mix, w_in, conv_w, w_conv_out, hgrn_lower_bounds, hgrn_norm_w, w_hgrn_out, w_o, norm_pre_ffn, norm_post_ffn, w_router, router_bias, w_gate_e, w_up_e, w_down_e, w_gate_s, w_up_s, w_down_s):
    bsz, seq, d = x.shape
    assert bsz == 1 and w_ada.shape[0] == 1
    mod = _ada_mod(c, w_ada[0], b_ada[0])
    x1, wgu16 = _mixer(x.reshape(seq, d), mod, norm_pre_mix[0], norm_post_mix[0], w_in[0], conv_w[0],
                       w_conv_out[0], hgrn_lower_bounds, hgrn_norm_w[0], w_hgrn_out[0], w_o[0],
                       w_gate_e[0], w_up_e[0])
    hp, eidx, rank, wcol, cnt = _router(x1, mod, norm_pre_ffn[0], w_router[0], router_bias[0])
    dest, tab = _dest(eidx, rank, cnt)
    xs = _dispatch(tab, dest, hp, w_router.shape[-1])
    y = _experts(tab, xs, wgu16, w_down_e[0])
    out = _combine(dest, y, wcol, x1, hp, mod, norm_post_ffn[0], w_gate_s[0], w_up_s[0], w_down_s[0])
    return out.reshape(bsz, seq, d)
```

```python
import functools

import jax
import jax.numpy as jnp
from jax import lax
from jax.experimental import pallas as pl
from jax.experimental.pallas import tpu as pltpu
from jax.experimental.pallas import tpu_sc as plsc

F32 = jnp.float32
BF16 = jnp.bfloat16

NORM_EPS = 1e-6
CONV_DIM = 512
HGRN_HEADS = 4
HGRN_DK = 128
HGRN_QK = HGRN_HEADS * HGRN_DK
N_GROUPS = 8
TOPK_GROUPS = 4
TOP_K = 8
ROUTED_SCALE = 2.5

SEQ_TILE = 256
EXPERT_CHUNK = 512
EXPERT_ROW_STEPS = (128, 256, EXPERT_CHUNK)
ROW_ALIGN = 8
VMEM_LIMIT_BYTES = 56 * 1024 * 1024


def _dot(a, b):
    return jnp.dot(a, b, preferred_element_type=F32)


def _dot_nt(a, b):
    return lax.dot_general(a, b, (((1,), (1,)), ((), ())), preferred_element_type=F32)


def _dot_tn(a, b):
    return lax.dot_general(a, b, (((0,), (0,)), ((), ())), preferred_element_type=F32)


def _split3(x):
    hi = x.astype(BF16)
    r1 = x - hi.astype(F32)
    mid = r1.astype(BF16)
    lo = (r1 - mid.astype(F32)).astype(BF16)
    return hi, mid, lo


def _sigmoid(x):
    return 1.0 / (1.0 + jnp.exp(-x))


def _silu(x):
    return x * _sigmoid(x)


def _rms(x, w):
    ms = jnp.mean(x * x, axis=-1, keepdims=True)
    return x * lax.rsqrt(ms + NORM_EPS) * w


def _ada_kernel(c_ref, w_ref, b_ref, o_ref):
    cs = _silu(c_ref[...])
    h1, h2, h3 = _split3(cs)
    w1, w2, w3 = _split3(w_ref[...])
    acc = _dot(h1, w1) + (_dot(h1, w2) + _dot(h2, w1)) + (_dot(h1, w3) + _dot(h2, w2) + _dot(h3, w1))
    o_ref[...] = acc + b_ref[...]


def _ada_mod(c, w_ada, b_ada):
    d = c.shape[-1]
    n = w_ada.shape[-1]
    bn = 1024
    c8 = jnp.broadcast_to(c.reshape(1, d), (8, d))
    out = pl.pallas_call(
        _ada_kernel,
        out_shape=jax.ShapeDtypeStruct((8, n), F32),
        grid=(n // bn,),
        in_specs=[pl.BlockSpec((8, d), lambda j: (0, 0)),
                  pl.BlockSpec((d, bn), lambda j: (0, j)),
                  pl.BlockSpec((1, bn), lambda j: (0, j))],
        out_specs=pl.BlockSpec((8, bn), lambda j: (0, j)),
        compiler_params=pltpu.CompilerParams(dimension_semantics=("arbitrary",),
                                             vmem_limit_bytes=VMEM_LIMIT_BYTES),
        name="ada_mod",
    )(c8, w_ada, b_ada.reshape(1, n))
    return out[0:1]


def _level_reference(b, rolled, s, row):
    ts = b.shape[0]
    c = s // 2 - 1
    if s >= 16:
        pieces = []
        for blk in range(ts // s):
            r = blk * s + c
            pieces.append(jnp.broadcast_to(b[r:r + 1, :], (s, b.shape[1])))
        return pieces[0] if len(pieces) == 1 else jnp.concatenate(pieces, axis=0)
    pos = row & (s - 1)
    out = b
    for p in range(s):
        if p == c:
            continue
        out = jnp.where(pos == p, rolled[p - c], out)
    return out


def _cast_expert_weights(wg_ref, wu_ref, wgu16_ref):
    ff = wg_ref.shape[2]
    for j in range(wg_ref.shape[0]):
        wgu16_ref[j, :, 0:ff] = wg_ref[j].astype(BF16)
        wgu16_ref[j, :, ff:2 * ff] = wu_ref[j].astype(BF16)


def _mixer_kernel(x_ref, mod_ref, npre_ref, npost_ref, win_ref, convw_ref, wco_ref,
                  hlb_ref, hnw_ref, who_ref, wo_ref, wg_ref, wu_ref,
                  o_ref, wgu16_ref, state_ref, ubuf_ref):
    ts, d = x_ref.shape
    step = pl.program_id(0)
    _cast_expert_weights(wg_ref, wu_ref, wgu16_ref)

    @pl.when(step == 0)
    def _():
        state_ref[...] = jnp.zeros_like(state_ref)
        ubuf_ref[0:8, :] = jnp.zeros((8, CONV_DIM), F32)

    x = x_ref[...]
    sh1 = mod_ref[:, 0:d]
    sc1 = mod_ref[:, d:2 * d]
    g1 = mod_ref[:, 2 * d:3 * d]
    h = (_rms(x, npre_ref[...]) * (1.0 + sc1) + sh1).astype(BF16)

    def proj(lo, width):
        return _dot(h, win_ref[:, lo:lo + width])

    c0 = 0
    cb = proj(c0, CONV_DIM)
    u = proj(c0 + CONV_DIM, CONV_DIM) * proj(c0 + 2 * CONV_DIM, CONV_DIM)
    ubuf_ref[8:8 + ts, :] = u
    conv = (ubuf_ref[6:6 + ts, :] * convw_ref[0:1, :]
            + ubuf_ref[7:7 + ts, :] * convw_ref[1:2, :]
            + u * convw_ref[2:3, :])
    ubuf_ref[0:8, :] = ubuf_ref[ts:ts + 8, :]
    y_a = _dot((cb * conv).astype(BF16), wco_ref[...])

    c1 = 3 * CONV_DIM
    q = _silu(proj(c1, HGRN_QK))
    hl = hlb_ref[...]
    hm = jnp.max(hl, axis=0, keepdims=True)
    he = jnp.exp(hl - hm)
    lb = he[0:1, :] / jnp.sum(he, axis=0, keepdims=True)
    fg = lb + (1.0 - lb) * _sigmoid(proj(c1 + HGRN_QK, HGRN_QK))
    k = 1.0 - fg
    g = jnp.log(fg)
    v = proj(c1 + 2 * HGRN_QK, HGRN_QK)
    gg = proj(c1 + 3 * HGRN_QK, HGRN_QK)

    row = lax.broadcasted_iota(jnp.int32, (ts, ts), 0)
    col = lax.broadcasted_iota(jnp.int32, (ts, ts), 1)
    tril = jnp.where(row >= col, 1.0, 0.0).astype(BF16)
    ghi, gmid, glo = _split3(g)
    b = _dot(tril, ghi) + _dot(tril, gmid) + _dot(tril, glo)
    b_last = b[ts - 1:ts, :]

    rowq = lax.broadcasted_iota(jnp.int32, (ts, HGRN_QK), 0)
    rolled = {sft: pltpu.roll(b, sft % ts, axis=0) for sft in (-3, -2, -1, 1, 2, 3, 4)}
    xor = row ^ col

    levels = []
    s = ts
    while s >= 2:
        levels.append(s)
        s //= 2
    qt, kt = [], []
    for s in levels:
        bref = _level_reference(b, rolled, s, rowq)
        e = jnp.exp(-jnp.abs(b - bref))
        upper = (rowq & (s // 2)) != 0
        qt.append(jnp.where(upper, q * e, 0.0).astype(BF16))
        kt.append(jnp.where(upper, 0.0, k * e).astype(BF16))
    q_in = (q * jnp.exp(b)).astype(BF16)
    k_out = (k * jnp.exp(b_last - b)).astype(BF16)
    v16 = v.astype(BF16)
    qk = q * k
    decay_last = jnp.exp(b_last)

    hnw = hnw_ref[...]
    outs = []
    for hd in range(HGRN_HEADS):
        sl = slice(hd * HGRN_DK, (hd + 1) * HGRN_DK)
        a = jnp.zeros((ts, ts), F32)
        for li in range(len(levels) - 1, -1, -1):
            s = levels[li]
            p = _dot_nt(qt[li][:, sl], kt[li][:, sl])
            a = jnp.where(xor >= s // 2, p, a)
        st = state_ref[hd]
        o_h = (_dot(a.astype(BF16), v16[:, sl])
               + jnp.sum(qk[:, sl], axis=-1, keepdims=True) * v[:, sl]
               + _dot_nt(q_in[:, sl], st.astype(BF16)))
        state_ref[hd] = st * decay_last[:, sl] + _dot_tn(v16[:, sl], k_out[:, sl])
        outs.append(_rms(o_h, hnw))
    o = jnp.concatenate(outs, axis=-1) * _silu(gg)
    y_b = _dot(o.astype(BF16), who_ref[...])

    c2 = c1 + 4 * HGRN_QK
    m = _sigmoid(proj(c2, d)) * y_a + _sigmoid(proj(c2 + d, d)) * y_b
    y = _dot(m.astype(BF16), wo_ref[...])
    o_ref[...] = x + g1 * _rms(y, npost_ref[...])


def _const_spec(shape):
    nd = len(shape)
    return pl.BlockSpec(shape, lambda i: (0,) * nd)


def _mixer(x2, mod, npre, npost, w_in, conv_w, w_conv_out, hlb, hnw, w_hgrn_out, w_o, w_gate_e, w_up_e):
    t, d = x2.shape
    ts = SEQ_TILE
    n = t // ts
    n_e, _, ff = w_gate_e.shape
    assert n_e % n == 0 and conv_w.shape == (3, CONV_DIM)
    per = n_e // n
    args = (x2, mod, npre.reshape(1, d), npost.reshape(1, d), w_in.astype(BF16), conv_w,
            w_conv_out.astype(BF16), hlb, hnw.reshape(1, -1), w_hgrn_out.astype(BF16),
            w_o.astype(BF16))
    experts = lambda i: (i, 0, 0)
    in_specs = ([pl.BlockSpec((ts, d), lambda i: (i, 0))] + [_const_spec(a.shape) for a in args[1:]]
                + [pl.BlockSpec((per, d, ff), experts)] * 2)
    return pl.pallas_call(
        _mixer_kernel,
        out_shape=[jax.ShapeDtypeStruct((t, d), F32), jax.ShapeDtypeStruct((n_e, d, 2 * ff), BF16)],
        grid=(n,),
        in_specs=in_specs,
        out_specs=[pl.BlockSpec((ts, d), lambda i: (i, 0)), pl.BlockSpec((per, d, 2 * ff), experts)],
        scratch_shapes=[pltpu.VMEM((HGRN_HEADS, HGRN_DK, HGRN_DK), F32),
                        pltpu.VMEM((ts + 8, CONV_DIM), F32)],
        compiler_params=pltpu.CompilerParams(dimension_semantics=("arbitrary",),
                                             vmem_limit_bytes=VMEM_LIMIT_BYTES),
        name="token_mixer",
    )(*args, w_gate_e, w_up_e)


PACKED = jnp.int32


def _pack_bf16_pairs(x):
    m = x.shape[1] // 2
    hi = lax.bitcast_convert_type(x[:, :m].astype(BF16).astype(F32), jnp.uint32)
    lo = lax.bitcast_convert_type(x[:, m:].astype(BF16).astype(F32), jnp.uint32)
    return lax.bitcast_convert_type(hi | (lo >> 16), PACKED)


def _unpack_bf16_pairs(p):
    p = lax.bitcast_convert_type(p, jnp.uint32)
    hi = lax.bitcast_convert_type(p & jnp.uint32(0xFFFF0000), F32)
    lo = lax.bitcast_convert_type(p << 16, F32)
    return hi, lo


def _router_kernel(x1_ref, mod_ref, npre_ref, wrh_ref, wrl_ref, rb_ref,
                   hp_ref, eidx_ref, rank_ref, wcol_ref, cnt_ref, carry_ref, rows_ref):
    assert N_GROUPS == TOP_K == rows_ref.shape[0]
    tr, d = x1_ref.shape
    n_e = wrh_ref.shape[0]
    gsz = n_e // N_GROUPS
    neg = -jnp.inf

    @pl.when(pl.program_id(0) == 0)
    def _():
        carry_ref[...] = jnp.zeros_like(carry_ref)

    sh2 = mod_ref[:, 3 * d:4 * d]
    sc2 = mod_ref[:, 4 * d:5 * d]
    h2 = _rms(x1_ref[...], npre_ref[...]) * (1.0 + sc2) + sh2
    hp_ref[...] = _pack_bf16_pairs(h2)
    h_hi = h2.astype(BF16)
    h_lo = (h2 - h_hi.astype(F32)).astype(BF16)
    wrh = wrh_ref[...]
    logits = _dot_nt(wrh, h_hi) + (_dot_nt(wrh, h_lo) + _dot_nt(wrl_ref[...], h_hi))
    scores = _sigmoid(logits)
    sel = scores + rb_ref[:, 0:1]

    io_e = lax.broadcasted_iota(jnp.int32, (n_e, tr), 0)
    for g in range(N_GROUPS):
        blk = sel[g * gsz:(g + 1) * gsz, :]
        io = lax.broadcasted_iota(jnp.int32, (gsz, tr), 0) + g * gsz
        m1 = jnp.max(blk, axis=0, keepdims=True)
        i1 = jnp.min(jnp.where(blk == m1, io, n_e), axis=0, keepdims=True)
        m2 = jnp.max(jnp.where(io == i1, neg, blk), axis=0, keepdims=True)
        rows_ref[g:g + 1, :] = m1 + m2
    gs = rows_ref[...]
    io_g = lax.broadcasted_iota(jnp.int32, (N_GROUPS, tr), 0)
    gsel = jnp.zeros((N_GROUPS, tr), F32)
    for _ in range(TOPK_GROUPS):
        m = jnp.max(gs, axis=0, keepdims=True)
        gi = jnp.min(jnp.where(gs == m, io_g, N_GROUPS), axis=0, keepdims=True)
        hit = io_g == gi
        gsel = jnp.where(hit, 1.0, gsel)
        gs = jnp.where(hit, neg, gs)
    rows_ref[...] = gsel
    cur = jnp.concatenate(
        [jnp.where(rows_ref[g:g + 1, :] > 0.5, sel[g * gsz:(g + 1) * gsz, :], neg) for g in range(N_GROUPS)],
        axis=0)

    idxs = []
    selm = jnp.zeros((n_e, tr), F32)
    for k in range(TOP_K):
        m = jnp.max(cur, axis=0, keepdims=True)
        idx = jnp.min(jnp.where(cur == m, io_e, n_e), axis=0, keepdims=True)
        hit = io_e == idx
        rows_ref[k:k + 1, :] = jnp.sum(jnp.where(hit, scores, 0.0), axis=0, keepdims=True)
        cur = jnp.where(hit, neg, cur)
        selm = jnp.where(hit, 1.0, selm)
        eidx_ref[k:k + 1, :] = idx
        idxs.append(idx)

    r_io = lax.broadcasted_iota(jnp.int32, (tr, tr + 128), 0)
    c_io = lax.broadcasted_iota(jnp.int32, (tr, tr + 128), 1)
    before = jnp.where((r_io < c_io) | (c_io >= tr), 1.0, 0.0).astype(BF16)
    r_ext = _dot(selm.astype(BF16), before)
    rank_full = r_ext[:, :tr] + carry_ref[:, 0:1]
    for k, idx in enumerate(idxs):
        rk = jnp.sum(jnp.where(io_e == idx, rank_full, 0.0), axis=0, keepdims=True)
        rank_ref[k:k + 1, :] = rk.astype(jnp.int32)
    carry_ref[...] = carry_ref[...] + r_ext[:, tr:]
    cnt_ref[...] = carry_ref[...]

    wk = rows_ref[...]
    wn = wk / (jnp.sum(wk, axis=0, keepdims=True) + 1e-20) * ROUTED_SCALE
    eye = jnp.where(lax.broadcasted_iota(jnp.int32, (TOP_K, 128), 0)
                    == lax.broadcasted_iota(jnp.int32, (TOP_K, 128), 1), 1.0, 0.0).astype(BF16)
    w1, w2, w3 = _split3(wn)
    wcol_ref[...] = _dot_tn(w1, eye) + _dot_tn(w2, eye) + _dot_tn(w3, eye)


def _router(x1, mod, npre2, w_router, router_bias):
    t, d = x1.shape
    n_e = w_router.shape[1]
    tr = SEQ_TILE
    wrt = w_router.T
    wrh = wrt.astype(BF16)
    wrl = (wrt - wrh.astype(F32)).astype(BF16)
    rb = jnp.broadcast_to(router_bias.reshape(n_e, 1), (n_e, 128))
    args = (x1, mod, npre2.reshape(1, d), wrh, wrl, rb)
    in_specs = [pl.BlockSpec((tr, d), lambda i: (i, 0))] + [_const_spec(a.shape) for a in args[1:]]
    return pl.pallas_call(
        _router_kernel,
        out_shape=(jax.ShapeDtypeStruct((t, d // 2), PACKED),
                   jax.ShapeDtypeStruct((TOP_K, t), jnp.int32),
                   jax.ShapeDtypeStruct((TOP_K, t), jnp.int32),
                   jax.ShapeDtypeStruct((t, 128), F32),
                   jax.ShapeDtypeStruct((n_e, 128), F32)),
        grid=(t // tr,),
        in_specs=in_specs,
        out_specs=(pl.BlockSpec((tr, d // 2), lambda i: (i, 0)),
                   pl.BlockSpec((TOP_K, tr), lambda i: (0, i)),
                   pl.BlockSpec((TOP_K, tr), lambda i: (0, i)),
                   pl.BlockSpec((tr, 128), lambda i: (i, 0)),
                   pl.BlockSpec((n_e, 128), lambda i: (0, 0))),
        scratch_shapes=[pltpu.VMEM((n_e, 128), F32), pltpu.VMEM((TOP_K, tr), F32)],
        compiler_params=pltpu.CompilerParams(dimension_semantics=("arbitrary",),
                                             vmem_limit_bytes=VMEM_LIMIT_BYTES),
        name="moe_router",
    )(*args)


TAB_CHUNK_ROW, TAB_CHUNK_VALID, TAB_START, TAB_COUNT, TAB_CHUNK0, TAB_NCHUNK, TAB_TOTAL = range(7)
TAB_ROWS = 8


def _excl_cumsum_rows(lower, x):
    hi = jnp.floor(x * (1.0 / 128.0))
    lo = x - hi * 128.0
    return 128.0 * _dot(lower, hi.astype(BF16)) + _dot(lower, lo.astype(BF16))


def _dest_kernel(eidx_ref, rank_ref, cnt_ref, dest_ref, tab_ref):
    n_e = cnt_ref.shape[0]
    tt = eidx_ref.shape[1]
    n_g = tab_ref.shape[1]
    ch = float(EXPERT_CHUNK)
    cnt = jnp.floor((cnt_ref[...] + (ROW_ALIGN - 1)) * (1.0 / ROW_ALIGN)) * ROW_ALIGN
    r = lax.broadcasted_iota(jnp.int32, (n_e, n_e), 0)
    c = lax.broadcasted_iota(jnp.int32, (n_e, n_e), 1)
    lower = jnp.where(c < r, 1.0, 0.0).astype(BF16)
    starts = _excl_cumsum_rows(lower, cnt)
    start_col = starts[:, 0:1]

    @pl.when(pl.program_id(0) == 0)
    def _():
        nch = jnp.floor((cnt + (ch - 1.0)) * (1.0 / ch))
        chunk0 = _excl_cumsum_rows(lower, nch)
        cend_col = (chunk0 + nch)[:, 0:1]
        eye = r == c

        def as_row(col):
            return jnp.sum(jnp.where(eye, col, 0.0), axis=0, keepdims=True)

        g = lax.broadcasted_iota(jnp.int32, (n_e, n_g), 1).astype(F32)
        owner = jnp.sum(jnp.where(cend_col <= g, 1.0, 0.0), axis=0, keepdims=True)
        mine = lax.broadcasted_iota(jnp.int32, (n_e, n_g), 0).astype(F32) == owner
        g_row = lax.broadcasted_iota(jnp.int32, (1, n_g), 1).astype(F32)
        base = jnp.sum(jnp.where(mine, start_col - chunk0[:, 0:1] * ch, 0.0), axis=0, keepdims=True)
        left = jnp.sum(jnp.where(mine, cnt[:, 0:1] + chunk0[:, 0:1] * ch, 0.0), axis=0, keepdims=True)
        pad = jnp.zeros((1, n_g - n_e), F32)

        def wide(row):
            return jnp.concatenate([row, pad], axis=1)

        total = jnp.sum(nch[:, 0:1], axis=0, keepdims=True)
        tab_ref[...] = jnp.zeros(tab_ref.shape, jnp.int32)
        tab_ref[TAB_CHUNK_ROW:TAB_CHUNK_ROW + 1, :] = (base + g_row * ch).astype(jnp.int32)
        tab_ref[TAB_CHUNK_VALID:TAB_CHUNK_VALID + 1, :] = jnp.clip(left - g_row * ch, 0.0, ch).astype(jnp.int32)
        tab_ref[TAB_START:TAB_START + 1, :] = wide(as_row(start_col)).astype(jnp.int32)
        tab_ref[TAB_COUNT:TAB_COUNT + 1, :] = wide(as_row(cnt_ref[:, 0:1])).astype(jnp.int32)
        tab_ref[TAB_CHUNK0:TAB_CHUNK0 + 1, :] = wide(as_row(chunk0[:, 0:1])).astype(jnp.int32)
        tab_ref[TAB_NCHUNK:TAB_NCHUNK + 1, :] = wide(as_row(nch[:, 0:1])).astype(jnp.int32)
        tab_ref[TAB_TOTAL:TAB_TOTAL + 1, :] = jnp.broadcast_to(total, (1, n_g)).astype(jnp.int32)

    io_e = lax.broadcasted_iota(jnp.int32, (n_e, tt), 0)
    rows = []
    for k in range(TOP_K):
        hit = io_e == eidx_ref[k:k + 1, :]
        rows.append(jnp.sum(jnp.where(hit, start_col, 0.0), axis=0, keepdims=True))
    dest_ref[...] = jnp.concatenate(rows, axis=0).astype(jnp.int32) + rank_ref[...]


def _dest(eidx, rank, cnt):
    k, t = eidx.shape
    n_e = cnt.shape[0]
    tt = 512
    n_g = (t * k) // EXPERT_CHUNK + n_e
    return pl.pallas_call(
        _dest_kernel,
        out_shape=(jax.ShapeDtypeStruct((k, t), jnp.int32),
                   jax.ShapeDtypeStruct((TAB_ROWS, n_g), jnp.int32)),
        grid=(t // tt,),
        in_specs=[pl.BlockSpec((k, tt), lambda i: (0, i)),
                  pl.BlockSpec((k, tt), lambda i: (0, i)),
                  _const_spec(cnt.shape)],
        out_specs=(pl.BlockSpec((k, tt), lambda i: (0, i)), _const_spec((TAB_ROWS, n_g))),
        compiler_params=pltpu.CompilerParams(dimension_semantics=("arbitrary",)),
        name="moe_dest",
    )(eidx, rank, cnt)


SC_CORES = 2
SC_SUBCORES = 16
SC_WORKERS = SC_CORES * SC_SUBCORES
SC_GATHER_ROWS = 64
COMBINE_SPLITS = 8


def _sc_mesh():
    return plsc.VectorSubcoreMesh(core_axis_name="c", subcore_axis_name="s")


def _sc_worker_id():
    return lax.axis_index("s") * SC_CORES + lax.axis_index("c")


def _sc_scatter_rows(rows, dest_flat, n_out):
    t, w = rows.shape
    n_k = dest_flat.shape[0] // t
    r = SC_GATHER_ROWS
    per = t // SC_WORKERS
    n_steps = per // r
    assert per % (2 * r) == 0

    @functools.partial(
        pl.kernel, mesh=_sc_mesh(), out_type=jax.ShapeDtypeStruct((n_out, w), rows.dtype),
        scratch_types=[pltpu.VMEM((n_k, r), jnp.int32), pltpu.VMEM((2, r, w), rows.dtype),
                       pltpu.SemaphoreType.DMA((2,)), pltpu.SemaphoreType.DMA],
        name="moe_dispatch_sc")
    def scatter(rows_hbm, idx_hbm, out_hbm, idx_v, rows_v, l_sem, s_sem):
        base = _sc_worker_id() * per

        def load(j, b):
            return pltpu.make_async_copy(rows_hbm.at[pl.ds(pl.multiple_of(base + j * r, ROW_ALIGN), r)],
                                         rows_v.at[b], l_sem.at[b])

        load(0, 0).start()

        @pl.loop(0, n_steps, step=2)
        def _(j0):
            for b in range(2):
                j = j0 + b
                pl.when(j + 1 < n_steps)(load(j + 1, 1 - b).start)
                for k in range(n_k):
                    off = pl.multiple_of(k * t + base + j * r, ROW_ALIGN)
                    pltpu.sync_copy(idx_hbm.at[pl.ds(off, r)], idx_v.at[k])
                load(j, b).wait()
                copies = [pltpu.make_async_copy(rows_v.at[b], out_hbm.at[idx_v.at[k]], s_sem) for k in range(n_k)]
                for cp in copies:
                    cp.start()
                for cp in copies:
                    cp.wait()

    return scatter(rows, dest_flat)


def _sc_gather_rows(table, idx):
    n = idx.shape[0]
    w = table.shape[1]
    r = SC_GATHER_ROWS
    per = n // SC_WORKERS
    n_steps = per // r
    assert per % (2 * r) == 0

    @functools.partial(
        pl.kernel, mesh=_sc_mesh(), out_type=jax.ShapeDtypeStruct((n, w), table.dtype),
        scratch_types=[pltpu.VMEM((2, r), jnp.int32), pltpu.VMEM((2, r, w), table.dtype),
                       pltpu.SemaphoreType.DMA((2,)), pltpu.SemaphoreType.DMA((2,))],
        name="moe_gather_sc")
    def gather(table_hbm, idx_hbm, out_hbm, idx_v, rows_v, g_sem, w_sem):
        base = _sc_worker_id() * per

        def at(j):
            return pl.ds(pl.multiple_of(base + j * r, ROW_ALIGN), r)

        def fetch(j, b):
            return pltpu.make_async_copy(table_hbm.at[idx_v.at[b]], rows_v.at[b], g_sem.at[b])

        def write(j, b):
            return pltpu.make_async_copy(rows_v.at[b], out_hbm.at[at(j)], w_sem.at[b])

        pltpu.sync_copy(idx_hbm.at[at(0)], idx_v.at[0])
        fetch(0, 0).start()

        @pl.loop(0, n_steps, step=2)
        def _(j0):
            for b in range(2):
                j = j0 + b

                @pl.when(j + 1 < n_steps)
                def _():
                    pl.when(j >= 1)(write(j - 1, 1 - b).wait)
                    pltpu.sync_copy(idx_hbm.at[at(j + 1)], idx_v.at[1 - b])
                    fetch(j + 1, 1 - b).start()

                fetch(j, b).wait()
                write(j, b).start()

        write(n_steps - 2, 0).wait()
        write(n_steps - 1, 1).wait()

    return gather(table, idx)


def _pad_kernel(tab_ref, xs_in, xs_hbm, zero_ref, sem, *, n_e):
    del xs_in
    zero_ref[...] = jnp.zeros_like(zero_ref)

    def pad_row_copy(row):
        return pltpu.make_async_copy(zero_ref.at[pl.ds(0, 1), :], xs_hbm.at[pl.ds(row, 1), :], sem)

    def pad_rows(e, carry, wait):
        count = tab_ref[TAB_COUNT, e]
        end = tab_ref[TAB_START, e] + count
        n_pad = (ROW_ALIGN - count % ROW_ALIGN) % ROW_ALIGN
        for j in range(ROW_ALIGN - 1):
            copy = pad_row_copy(end + j)
            pl.when(j < n_pad)(copy.wait if wait else copy.start)
        return carry

    lax.fori_loop(0, n_e, functools.partial(pad_rows, wait=False), 0)
    lax.fori_loop(0, n_e, functools.partial(pad_rows, wait=True), 0)
    last = tab_ref[TAB_COUNT, n_e - 1]
    used = tab_ref[TAB_START, n_e - 1] + ((last + (ROW_ALIGN - 1)) // ROW_ALIGN) * ROW_ALIGN
    piece = zero_ref.shape[0]
    n_rows = xs_hbm.shape[0]
    for j in range(pl.cdiv(n_e * ROW_ALIGN + EXPERT_CHUNK + ROW_ALIGN, piece)):
        at = pl.multiple_of(jnp.minimum(used + j * piece, n_rows - piece), ROW_ALIGN)
        tail = pltpu.make_async_copy(zero_ref, xs_hbm.at[pl.ds(at, piece), :], sem)
        tail.start()
        tail.wait()


def _sorted_rows(t, n_e):
    return t * TOP_K + n_e * ROW_ALIGN + EXPERT_CHUNK + ROW_ALIGN


def _dispatch(tab, dest, hp, n_e):
    t, m = hp.shape
    n_rows = _sorted_rows(t, n_e)
    xs = _sc_scatter_rows(hp, dest.reshape(-1), n_rows)
    grid_spec = pltpu.PrefetchScalarGridSpec(
        num_scalar_prefetch=1,
        grid=(1,),
        in_specs=[pl.BlockSpec(memory_space=pl.ANY)],
        out_specs=pl.BlockSpec(memory_space=pl.ANY),
        scratch_shapes=[pltpu.VMEM((EXPERT_CHUNK, m), PACKED), pltpu.SemaphoreType.DMA(())],
    )
    return pl.pallas_call(
        functools.partial(_pad_kernel, n_e=n_e),
        out_shape=jax.ShapeDtypeStruct((n_rows, m), PACKED),
        grid_spec=grid_spec,
        input_output_aliases={1: 0},
        compiler_params=pltpu.CompilerParams(dimension_semantics=("arbitrary",)),
        name="moe_pad",
    )(tab, xs)


X_LOOKAHEAD = 2
X_SLOTS = X_LOOKAHEAD + 1
Y_SLOTS = 2


def _expert_kernel(tab_ref, xs_hbm, wgu_ref, wd_ref, y_hbm,
                   wd16_ref, xbuf_ref, ybuf_ref, in_sem, out_sem, *, n_e):
    e = pl.program_id(0)
    ch, half = xbuf_ref.shape[1:]
    ff = wd_ref.shape[1]
    n_total = tab_ref[TAB_TOTAL, 0]
    first = tab_ref[TAB_CHUNK0, e]
    n_chunks = tab_ref[TAB_NCHUNK, e]

    def in_copy(g):
        slot = g % X_SLOTS
        row = pl.multiple_of(tab_ref[TAB_CHUNK_ROW, g], ROW_ALIGN)
        return pltpu.make_async_copy(xs_hbm.at[pl.ds(row, ch), :], xbuf_ref.at[slot], in_sem.at[slot])

    def out_copies(g):
        slot = g % Y_SLOTS
        row = tab_ref[TAB_CHUNK_ROW, g]
        valid = tab_ref[TAB_CHUNK_VALID, g]
        yield valid >= ch, pltpu.make_async_copy(ybuf_ref.at[slot], y_hbm.at[pl.ds(pl.multiple_of(row, ROW_ALIGN), ch), :],
                                                 out_sem.at[slot])
        size = ch // 2
        while size >= ROW_ALIGN:
            off = pl.multiple_of((valid // (2 * size)) * (2 * size), ROW_ALIGN)
            cond = (valid < ch) & ((valid & size) != 0)
            yield cond, pltpu.make_async_copy(ybuf_ref.at[slot, pl.ds(off, size), :],
                                              y_hbm.at[pl.ds(pl.multiple_of(row + off, ROW_ALIGN), size), :],
                                              out_sem.at[slot])
            size //= 2

    def start_out(g):
        for cond, cp in out_copies(g):
            pl.when(cond)(cp.start)

    def wait_out(g):
        for cond, cp in out_copies(g):
            pl.when(cond)(cp.wait)

    @pl.when(e == 0)
    def _():
        for j in range(X_LOOKAHEAD):
            pl.when(j < n_total)(in_copy(j).start)

    @pl.when(n_chunks > 0)
    def _():
        wd16_ref[...] = wd_ref[0].astype(BF16)

    def chunk(c, carry):
        g = first + c
        pl.when(g + X_LOOKAHEAD < n_total)(in_copy(g + X_LOOKAHEAD).start)
        in_copy(g).wait()
        pl.when(g >= Y_SLOTS)(lambda: wait_out(g - Y_SLOTS))
        valid = tab_ref[TAB_CHUNK_VALID, g]

        def ffn(rows):
            xa, xb = _unpack_bf16_pairs(xbuf_ref[g % X_SLOTS, 0:rows, :])
            hgu = (_dot(xa.astype(BF16), wgu_ref[0, 0:half, :])
                   + _dot(xb.astype(BF16), wgu_ref[0, half:2 * half, :]))
            act = _silu(hgu[:, :ff]) * hgu[:, ff:]
            y = _dot(act.astype(BF16), wd16_ref[...])
            ybuf_ref[g % Y_SLOTS, 0:rows, :] = _pack_bf16_pairs(y)

        lo = 0
        for rows in EXPERT_ROW_STEPS:
            pl.when((valid > lo) & (valid <= rows))(functools.partial(ffn, rows))
            lo = rows
        start_out(g)
        return carry

    lax.fori_loop(0, n_chunks, chunk, 0)

    @pl.when(e == n_e - 1)
    def _():
        for j in range(Y_SLOTS, 0, -1):
            pl.when(n_total >= j)(lambda j=j: wait_out(n_total - j))
        n_rows = y_hbm.shape[0]
        used = tab_ref[TAB_START, e] + ((tab_ref[TAB_COUNT, e] + (ROW_ALIGN - 1)) // ROW_ALIGN) * ROW_ALIGN
        ybuf_ref[0] = jnp.zeros((ch, half), PACKED)
        for j in range(pl.cdiv(n_e * ROW_ALIGN + EXPERT_CHUNK + ROW_ALIGN, ch)):
            at = pl.multiple_of(jnp.minimum(used + j * ch, n_rows - ch), ROW_ALIGN)
            tail = pltpu.make_async_copy(ybuf_ref.at[0], y_hbm.at[pl.ds(at, ch), :], out_sem.at[0])
            tail.start()
            tail.wait()


def _experts(tab, xs, wgu16, w_down_e):
    p, half = xs.shape
    n_e, ff, d = w_down_e.shape
    ch = EXPERT_CHUNK
    grid_spec = pltpu.PrefetchScalarGridSpec(
        num_scalar_prefetch=1,
        grid=(n_e,),
        in_specs=[pl.BlockSpec(memory_space=pl.ANY),
                  pl.BlockSpec((1, d, 2 * ff), lambda e, *_: (e, 0, 0)),
                  pl.BlockSpec((1, ff, d), lambda e, *_: (e, 0, 0))],
        out_specs=pl.BlockSpec(memory_space=pl.ANY),
        scratch_shapes=[pltpu.VMEM((ff, d), BF16),
                        pltpu.VMEM((X_SLOTS, ch, half), PACKED), pltpu.VMEM((Y_SLOTS, ch, half), PACKED),
                        pltpu.SemaphoreType.DMA((X_SLOTS,)), pltpu.SemaphoreType.DMA((Y_SLOTS,))],
    )
    return pl.pallas_call(
        functools.partial(_expert_kernel, n_e=n_e),
        out_shape=jax.ShapeDtypeStruct((p, half), PACKED),
        grid_spec=grid_spec,
        compiler_params=pltpu.CompilerParams(dimension_semantics=("arbitrary",),
                                             vmem_limit_bytes=VMEM_LIMIT_BYTES),
        name="moe_experts",
    )(tab, xs, wgu16, w_down_e)


def _combine_kernel(yg_ref, wcol_ref, x1_ref, mod_ref, npre_ref, npost_ref, wgus_ref, wds_ref, o_ref):
    te, d = x1_ref.shape
    half = d // 2
    ff = wds_ref.shape[0]

    x1 = x1_ref[...]
    h2 = _rms(x1, npre_ref[...]) * (1.0 + mod_ref[:, 4 * d:5 * d]) + mod_ref[:, 3 * d:4 * d]
    hgu = _dot(h2.astype(BF16), wgus_ref[...])
    act = _silu(hgu[:, :ff]) * hgu[:, ff:]
    shared = _dot(act.astype(BF16), wds_ref[...])

    acc_a = shared[:, :half]
    acc_b = shared[:, half:]
    for k in range(TOP_K):
        ya, yb = _unpack_bf16_pairs(yg_ref[k])
        wk = wcol_ref[:, k:k + 1]
        acc_a = acc_a + wk * ya
        acc_b = acc_b + wk * yb
    moe = jnp.concatenate([acc_a, acc_b], axis=-1)
    g2 = mod_ref[:, 5 * d:6 * d]
    o_ref[...] = x1 + g2 * _rms(moe, npost_ref[...])


def _combine(dest, y, wcol, x1, mod, npre2, npost2, w_gate_s, w_up_s, w_down_s):
    t, d = x1.shape
    half = d // 2
    te = SEQ_TILE
    wgus = jnp.concatenate([w_gate_s, w_up_s], axis=1).astype(BF16)
    wds = w_down_s.astype(BF16)
    n_split = COMBINE_SPLITS if t % (COMBINE_SPLITS * 2 * SC_GATHER_ROWS * SC_WORKERS // TOP_K) == 0 else 1
    tq = t // n_split
    steps = tq // te
    stream = x1
    for q in range(n_split):
        dest_q = dest[:, q * tq:(q + 1) * tq]
        yg = _sc_gather_rows(y, dest_q.reshape(-1)).reshape(TOP_K, tq, half)
        row = lambda i, q=q: (q * steps + i, 0)
        stream = pl.pallas_call(
            _combine_kernel,
            out_shape=jax.ShapeDtypeStruct((t, d), F32),
            grid=(steps,),
            in_specs=[pl.BlockSpec((TOP_K, te, half), lambda i: (0, i, 0)),
                      pl.BlockSpec((te, 128), row),
                      pl.BlockSpec((te, d), row),
                      _const_spec(mod.shape),
                      _const_spec((1, d)),
                      _const_spec((1, d)),
                      _const_spec(wgus.shape),
                      _const_spec(wds.shape)],
            out_specs=pl.BlockSpec((te, d), row),
            input_output_aliases={2: 0},
            compiler_params=pltpu.CompilerParams(dimension_semantics=("arbitrary",),
                                                 vmem_limit_bytes=VMEM_LIMIT_BYTES),
            name="moe_combine",
        )(yg, wcol, stream, mod, npre2.reshape(1, d), npost2.reshape(1, d), wgus, wds)
    return stream


def kernel(x, c, w_ada, b_ada, norm_pre_mix, norm_post_mix, w_in, conv_w, w_conv_out, hgrn_lower_bounds, hgrn_norm_w, w_hgrn_out, w_o, norm_pre_ffn, norm_post_ffn, w_router, router_bias, w_gate_e, w_up_e, w_down_e, w_gate_s, w_up_s, w_down_s):
    bsz, seq, d = x.shape
    assert bsz == 1 and w_ada.shape[0] == 1
    mod = _ada_mod(c, w_ada[0], b_ada[0])
    x1, wgu16 = _mixer(x.reshape(seq, d), mod, norm_pre_mix[0], norm_post_mix[0], w_in[0], conv_w[0],
                       w_conv_out[0], hgrn_lower_bounds, hgrn_norm_w[0], w_hgrn_out[0], w_o[0],
                       w_gate_e[0], w_up_e[0])
    hp, eidx, rank, wcol, cnt = _router(x1, mod, norm_pre_ffn[0], w_router[0], router_bias[0])
    dest, tab = _dest(eidx, rank, cnt)
    xs = _dispatch(tab, dest, hp, w_router.shape[-1])
    y = _experts(tab, xs, wgu16, w_down_e[0])
    out = _combine(dest, y, wcol, x1, mod, norm_pre_ffn[0], norm_post_ffn[0], w_gate_s[0], w_up_s[0], w_down_s[0])
    return out.reshape(bsz, seq, d)
```

```python
import functools

import jax
import jax.numpy as jnp
from jax import lax
from jax.experimental import pallas as pl
from jax.experimental.pallas import tpu as pltpu
from jax.experimental.pallas import tpu_sc as plsc

F32 = jnp.float32
BF16 = jnp.bfloat16

NORM_EPS = 1e-6
CONV_DIM = 512
HGRN_HEADS = 4
HGRN_DK = 128
HGRN_QK = HGRN_HEADS * HGRN_DK
N_GROUPS = 8
TOPK_GROUPS = 4
TOP_K = 8
ROUTED_SCALE = 2.5

SEQ_TILE = 256
EXPERT_CHUNK = 512
EXPERT_ROW_STEPS = (128, 256, EXPERT_CHUNK)
ROW_ALIGN = 8
VMEM_LIMIT_BYTES = 56 * 1024 * 1024


def _dot(a, b):
    return jnp.dot(a, b, preferred_element_type=F32)


def _dot_nt(a, b):
    return lax.dot_general(a, b, (((1,), (1,)), ((), ())), preferred_element_type=F32)


def _dot_tn(a, b):
    return lax.dot_general(a, b, (((0,), (0,)), ((), ())), preferred_element_type=F32)


def _split3(x):
    hi = x.astype(BF16)
    r1 = x - hi.astype(F32)
    mid = r1.astype(BF16)
    lo = (r1 - mid.astype(F32)).astype(BF16)
    return hi, mid, lo


def _sigmoid(x):
    return 1.0 / (1.0 + jnp.exp(-x))


def _silu(x):
    return x * _sigmoid(x)


def _rms(x, w):
    ms = jnp.mean(x * x, axis=-1, keepdims=True)
    return x * lax.rsqrt(ms + NORM_EPS) * w


def _ada_kernel(c_ref, w_ref, b_ref, o_ref):
    cs = _silu(c_ref[...])
    h1, h2, h3 = _split3(cs)
    w1, w2, w3 = _split3(w_ref[...])
    acc = _dot(h1, w1) + (_dot(h1, w2) + _dot(h2, w1)) + (_dot(h1, w3) + _dot(h2, w2) + _dot(h3, w1))
    o_ref[...] = acc + b_ref[...]


def _ada_mod(c, w_ada, b_ada):
    d = c.shape[-1]
    n = w_ada.shape[-1]
    bn = 1024
    c8 = jnp.broadcast_to(c.reshape(1, d), (8, d))
    out = pl.pallas_call(
        _ada_kernel,
        out_shape=jax.ShapeDtypeStruct((8, n), F32),
        grid=(n // bn,),
        in_specs=[pl.BlockSpec((8, d), lambda j: (0, 0)),
                  pl.BlockSpec((d, bn), lambda j: (0, j)),
                  pl.BlockSpec((1, bn), lambda j: (0, j))],
        out_specs=pl.BlockSpec((8, bn), lambda j: (0, j)),
        compiler_params=pltpu.CompilerParams(dimension_semantics=("arbitrary",),
                                             vmem_limit_bytes=VMEM_LIMIT_BYTES),
        name="ada_mod",
    )(c8, w_ada, b_ada.reshape(1, n))
    return out[0:1]


def _level_reference(b, rolled, s, row):
    ts = b.shape[0]
    c = s // 2 - 1
    if s >= 16:
        pieces = []
        for blk in range(ts // s):
            r = blk * s + c
            pieces.append(jnp.broadcast_to(b[r:r + 1, :], (s, b.shape[1])))
        return pieces[0] if len(pieces) == 1 else jnp.concatenate(pieces, axis=0)
    pos = row & (s - 1)
    out = b
    for p in range(s):
        if p == c:
            continue
        out = jnp.where(pos == p, rolled[p - c], out)
    return out


def _cast_expert_weights(wg_ref, wu_ref, wgu16_ref):
    ff = wg_ref.shape[2]
    for j in range(wg_ref.shape[0]):
        wgu16_ref[j, :, 0:ff] = wg_ref[j].astype(BF16)
        wgu16_ref[j, :, ff:2 * ff] = wu_ref[j].astype(BF16)


def _mixer_kernel(x_ref, mod_ref, npre_ref, npost_ref, win_ref, convw_ref, wco_ref,
                  hlb_ref, hnw_ref, who_ref, wo_ref, wg_ref, wu_ref,
                  o_ref, wgu16_ref, state_ref, ubuf_ref):
    ts, d = x_ref.shape
    step = pl.program_id(0)
    _cast_expert_weights(wg_ref, wu_ref, wgu16_ref)

    @pl.when(step == 0)
    def _():
        state_ref[...] = jnp.zeros_like(state_ref)
        ubuf_ref[0:8, :] = jnp.zeros((8, CONV_DIM), F32)

    x = x_ref[...]
    sh1 = mod_ref[:, 0:d]
    sc1 = mod_ref[:, d:2 * d]
    g1 = mod_ref[:, 2 * d:3 * d]
    h = (_rms(x, npre_ref[...]) * (1.0 + sc1) + sh1).astype(BF16)

    def proj(lo, width):
        return _dot(h, win_ref[:, lo:lo + width])

    c0 = 0
    cb = proj(c0, CONV_DIM)
    u = proj(c0 + CONV_DIM, CONV_DIM) * proj(c0 + 2 * CONV_DIM, CONV_DIM)
    ubuf_ref[8:8 + ts, :] = u
    conv = (ubuf_ref[6:6 + ts, :] * convw_ref[0:1, :]
            + ubuf_ref[7:7 + ts, :] * convw_ref[1:2, :]
            + u * convw_ref[2:3, :])
    ubuf_ref[0:8, :] = ubuf_ref[ts:ts + 8, :]
    y_a = _dot((cb * conv).astype(BF16), wco_ref[...])

    c1 = 3 * CONV_DIM
    q = _silu(proj(c1, HGRN_QK))
    hl = hlb_ref[...]
    hm = jnp.max(hl, axis=0, keepdims=True)
    he = jnp.exp(hl - hm)
    lb = he[0:1, :] / jnp.sum(he, axis=0, keepdims=True)
    fg = lb + (1.0 - lb) * _sigmoid(proj(c1 + HGRN_QK, HGRN_QK))
    k = 1.0 - fg
    g = jnp.log(fg)
    v = proj(c1 + 2 * HGRN_QK, HGRN_QK)
    gg = proj(c1 + 3 * HGRN_QK, HGRN_QK)

    row = lax.broadcasted_iota(jnp.int32, (ts, ts), 0)
    col = lax.broadcasted_iota(jnp.int32, (ts, ts), 1)
    tril = jnp.where(row >= col, 1.0, 0.0).astype(BF16)
    ghi, gmid, glo = _split3(g)
    b = _dot(tril, ghi) + _dot(tril, gmid) + _dot(tril, glo)
    b_last = b[ts - 1:ts, :]

    rowq = lax.broadcasted_iota(jnp.int32, (ts, HGRN_QK), 0)
    rolled = {sft: pltpu.roll(b, sft % ts, axis=0) for sft in (-3, -2, -1, 1, 2, 3, 4)}
    xor = row ^ col

    levels = []
    s = ts
    while s >= 2:
        levels.append(s)
        s //= 2
    qt, kt = [], []
    for s in levels:
        bref = _level_reference(b, rolled, s, rowq)
        e = jnp.exp(-jnp.abs(b - bref))
        upper = (rowq & (s // 2)) != 0
        qt.append(jnp.where(upper, q * e, 0.0).astype(BF16))
        kt.append(jnp.where(upper, 0.0, k * e).astype(BF16))
    q_in = (q * jnp.exp(b)).astype(BF16)
    k_out = (k * jnp.exp(b_last - b)).astype(BF16)
    v16 = v.astype(BF16)
    qk = q * k
    decay_last = jnp.exp(b_last)

    hnw = hnw_ref[...]
    outs = []
    for hd in range(HGRN_HEADS):
        sl = slice(hd * HGRN_DK, (hd + 1) * HGRN_DK)
        a = jnp.zeros((ts, ts), F32)
        for li in range(len(levels) - 1, -1, -1):
            s = levels[li]
            p = _dot_nt(qt[li][:, sl], kt[li][:, sl])
            a = jnp.where(xor >= s // 2, p, a)
        st = state_ref[hd]
        o_h = (_dot(a.astype(BF16), v16[:, sl])
               + jnp.sum(qk[:, sl], axis=-1, keepdims=True) * v[:, sl]
               + _dot_nt(q_in[:, sl], st.astype(BF16)))
        state_ref[hd] = st * decay_last[:, sl] + _dot_tn(v16[:, sl], k_out[:, sl])
        outs.append(_rms(o_h, hnw))
    o = jnp.concatenate(outs, axis=-1) * _silu(gg)
    y_b = _dot(o.astype(BF16), who_ref[...])

    c2 = c1 + 4 * HGRN_QK
    m = _sigmoid(proj(c2, d)) * y_a + _sigmoid(proj(c2 + d, d)) * y_b
    y = _dot(m.astype(BF16), wo_ref[...])
    o_ref[...] = x + g1 * _rms(y, npost_ref[...])


def _const_spec(shape):
    nd = len(shape)
    return pl.BlockSpec(shape, lambda i: (0,) * nd)


def _mixer(x2, mod, npre, npost, w_in, conv_w, w_conv_out, hlb, hnw, w_hgrn_out, w_o, w_gate_e, w_up_e):
    t, d = x2.shape
    ts = SEQ_TILE
    n = t // ts
    n_e, _, ff = w_gate_e.shape
    assert n_e % n == 0 and conv_w.shape == (3, CONV_DIM)
    per = n_e // n
    args = (x2, mod, npre.reshape(1, d), npost.reshape(1, d), w_in.astype(BF16), conv_w,
            w_conv_out.astype(BF16), hlb, hnw.reshape(1, -1), w_hgrn_out.astype(BF16),
            w_o.astype(BF16))
    experts = lambda i: (i, 0, 0)
    in_specs = ([pl.BlockSpec((ts, d), lambda i: (i, 0))] + [_const_spec(a.shape) for a in args[1:]]
                + [pl.BlockSpec((per, d, ff), experts)] * 2)
    return pl.pallas_call(
        _mixer_kernel,
        out_shape=[jax.ShapeDtypeStruct((t, d), F32), jax.ShapeDtypeStruct((n_e, d, 2 * ff), BF16)],
        grid=(n,),
        in_specs=in_specs,
        out_specs=[pl.BlockSpec((ts, d), lambda i: (i, 0)), pl.BlockSpec((per, d, 2 * ff), experts)],
        scratch_shapes=[pltpu.VMEM((HGRN_HEADS, HGRN_DK, HGRN_DK), F32),
                        pltpu.VMEM((ts + 8, CONV_DIM), F32)],
        compiler_params=pltpu.CompilerParams(dimension_semantics=("arbitrary",),
                                             vmem_limit_bytes=VMEM_LIMIT_BYTES),
        name="token_mixer",
    )(*args, w_gate_e, w_up_e)


PACKED = jnp.int32


def _pack_bf16_pairs(x):
    m = x.shape[1] // 2
    hi = lax.bitcast_convert_type(x[:, :m].astype(BF16).astype(F32), jnp.uint32)
    lo = lax.bitcast_convert_type(x[:, m:].astype(BF16).astype(F32), jnp.uint32)
    return lax.bitcast_convert_type(hi | (lo >> 16), PACKED)


def _unpack_bf16_pairs(p):
    p = lax.bitcast_convert_type(p, jnp.uint32)
    hi = lax.bitcast_convert_type(p & jnp.uint32(0xFFFF0000), F32)
    lo = lax.bitcast_convert_type(p << 16, F32)
    return hi, lo


def _router_kernel(x1_ref, mod_ref, npre_ref, wrh_ref, wrl_ref, rb_ref,
                   hp_ref, eidx_ref, rank_ref, wcol_ref, cnt_ref, carry_ref, rows_ref):
    assert N_GROUPS == TOP_K == rows_ref.shape[0]
    tr, d = x1_ref.shape
    n_e = wrh_ref.shape[0]
    gsz = n_e // N_GROUPS
    neg = -jnp.inf

    @pl.when(pl.program_id(0) == 0)
    def _():
        carry_ref[...] = jnp.zeros_like(carry_ref)

    sh2 = mod_ref[:, 3 * d:4 * d]
    sc2 = mod_ref[:, 4 * d:5 * d]
    h2 = _rms(x1_ref[...], npre_ref[...]) * (1.0 + sc2) + sh2
    hp_ref[...] = _pack_bf16_pairs(h2)
    h_hi = h2.astype(BF16)
    h_lo = (h2 - h_hi.astype(F32)).astype(BF16)
    wrh = wrh_ref[...]
    logits = _dot_nt(wrh, h_hi) + (_dot_nt(wrh, h_lo) + _dot_nt(wrl_ref[...], h_hi))
    scores = _sigmoid(logits)
    sel = scores + rb_ref[:, 0:1]

    io_e = lax.broadcasted_iota(jnp.int32, (n_e, tr), 0)
    for g in range(N_GROUPS):
        blk = sel[g * gsz:(g + 1) * gsz, :]
        io = lax.broadcasted_iota(jnp.int32, (gsz, tr), 0) + g * gsz
        m1 = jnp.max(blk, axis=0, keepdims=True)
        i1 = jnp.min(jnp.where(blk == m1, io, n_e), axis=0, keepdims=True)
        m2 = jnp.max(jnp.where(io == i1, neg, blk), axis=0, keepdims=True)
        rows_ref[g:g + 1, :] = m1 + m2
    gs = rows_ref[...]
    io_g = lax.broadcasted_iota(jnp.int32, (N_GROUPS, tr), 0)
    gsel = jnp.zeros((N_GROUPS, tr), F32)
    for _ in range(TOPK_GROUPS):
        m = jnp.max(gs, axis=0, keepdims=True)
        gi = jnp.min(jnp.where(gs == m, io_g, N_GROUPS), axis=0, keepdims=True)
        hit = io_g == gi
        gsel = jnp.where(hit, 1.0, gsel)
        gs = jnp.where(hit, neg, gs)
    rows_ref[...] = gsel
    cur = jnp.concatenate(
        [jnp.where(rows_ref[g:g + 1, :] > 0.5, sel[g * gsz:(g + 1) * gsz, :], neg) for g in range(N_GROUPS)],
        axis=0)

    idxs = []
    selm = jnp.zeros((n_e, tr), F32)
    for k in range(TOP_K):
        m = jnp.max(cur, axis=0, keepdims=True)
        idx = jnp.min(jnp.where(cur == m, io_e, n_e), axis=0, keepdims=True)
        hit = io_e == idx
        rows_ref[k:k + 1, :] = jnp.sum(jnp.where(hit, scores, 0.0), axis=0, keepdims=True)
        cur = jnp.where(hit, neg, cur)
        selm = jnp.where(hit, 1.0, selm)
        eidx_ref[k:k + 1, :] = idx
        idxs.append(idx)

    r_io = lax.broadcasted_iota(jnp.int32, (tr, tr + 128), 0)
    c_io = lax.broadcasted_iota(jnp.int32, (tr, tr + 128), 1)
    before = jnp.where((r_io < c_io) | (c_io >= tr), 1.0, 0.0).astype(BF16)
    r_ext = _dot(selm.astype(BF16), before)
    rank_full = r_ext[:, :tr] + carry_ref[:, 0:1]
    for k, idx in enumerate(idxs):
        rk = jnp.sum(jnp.where(io_e == idx, rank_full, 0.0), axis=0, keepdims=True)
        rank_ref[k:k + 1, :] = rk.astype(jnp.int32)
    carry_ref[...] = carry_ref[...] + r_ext[:, tr:]
    cnt_ref[...] = carry_ref[...]

    wk = rows_ref[...]
    wn = wk / (jnp.sum(wk, axis=0, keepdims=True) + 1e-20) * ROUTED_SCALE
    eye = jnp.where(lax.broadcasted_iota(jnp.int32, (TOP_K, 128), 0)
                    == lax.broadcasted_iota(jnp.int32, (TOP_K, 128), 1), 1.0, 0.0).astype(BF16)
    w1, w2, w3 = _split3(wn)
    wcol_ref[...] = _dot_tn(w1, eye) + _dot_tn(w2, eye) + _dot_tn(w3, eye)


def _router(x1, mod, npre2, w_router, router_bias):
    t, d = x1.shape
    n_e = w_router.shape[1]
    tr = SEQ_TILE
    wrt = w_router.T
    wrh = wrt.astype(BF16)
    wrl = (wrt - wrh.astype(F32)).astype(BF16)
    rb = jnp.broadcast_to(router_bias.reshape(n_e, 1), (n_e, 128))
    args = (x1, mod, npre2.reshape(1, d), wrh, wrl, rb)
    in_specs = [pl.BlockSpec((tr, d), lambda i: (i, 0))] + [_const_spec(a.shape) for a in args[1:]]
    return pl.pallas_call(
        _router_kernel,
        out_shape=(jax.ShapeDtypeStruct((t, d // 2), PACKED),
                   jax.ShapeDtypeStruct((TOP_K, t), jnp.int32),
                   jax.ShapeDtypeStruct((TOP_K, t), jnp.int32),
                   jax.ShapeDtypeStruct((t, 128), F32),
                   jax.ShapeDtypeStruct((n_e, 128), F32)),
        grid=(t // tr,),
        in_specs=in_specs,
        out_specs=(pl.BlockSpec((tr, d // 2), lambda i: (i, 0)),
                   pl.BlockSpec((TOP_K, tr), lambda i: (0, i)),
                   pl.BlockSpec((TOP_K, tr), lambda i: (0, i)),
                   pl.BlockSpec((tr, 128), lambda i: (i, 0)),
                   pl.BlockSpec((n_e, 128), lambda i: (0, 0))),
        scratch_shapes=[pltpu.VMEM((n_e, 128), F32), pltpu.VMEM((TOP_K, tr), F32)],
        compiler_params=pltpu.CompilerParams(dimension_semantics=("arbitrary",),
                                             vmem_limit_bytes=VMEM_LIMIT_BYTES),
        name="moe_router",
    )(*args)


TAB_CHUNK_ROW, TAB_CHUNK_VALID, TAB_START, TAB_COUNT, TAB_CHUNK0, TAB_NCHUNK, TAB_TOTAL = range(7)
TAB_ROWS = 8


def _excl_cumsum_rows(lower, x):
    hi = jnp.floor(x * (1.0 / 128.0))
    lo = x - hi * 128.0
    return 128.0 * _dot(lower, hi.astype(BF16)) + _dot(lower, lo.astype(BF16))


def _dest_kernel(eidx_ref, rank_ref, cnt_ref, dest_ref, tab_ref):
    n_e = cnt_ref.shape[0]
    tt = eidx_ref.shape[1]
    n_g = tab_ref.shape[1]
    ch = float(EXPERT_CHUNK)
    cnt = jnp.floor((cnt_ref[...] + (ROW_ALIGN - 1)) * (1.0 / ROW_ALIGN)) * ROW_ALIGN
    r = lax.broadcasted_iota(jnp.int32, (n_e, n_e), 0)
    c = lax.broadcasted_iota(jnp.int32, (n_e, n_e), 1)
    lower = jnp.where(c < r, 1.0, 0.0).astype(BF16)
    starts = _excl_cumsum_rows(lower, cnt)
    start_col = starts[:, 0:1]

    @pl.when(pl.program_id(0) == 0)
    def _():
        nch = jnp.floor((cnt + (ch - 1.0)) * (1.0 / ch))
        chunk0 = _excl_cumsum_rows(lower, nch)
        cend_col = (chunk0 + nch)[:, 0:1]
        eye = r == c

        def as_row(col):
            return jnp.sum(jnp.where(eye, col, 0.0), axis=0, keepdims=True)

        g = lax.broadcasted_iota(jnp.int32, (n_e, n_g), 1).astype(F32)
        owner = jnp.sum(jnp.where(cend_col <= g, 1.0, 0.0), axis=0, keepdims=True)
        mine = lax.broadcasted_iota(jnp.int32, (n_e, n_g), 0).astype(F32) == owner
        g_row = lax.broadcasted_iota(jnp.int32, (1, n_g), 1).astype(F32)
        base = jnp.sum(jnp.where(mine, start_col - chunk0[:, 0:1] * ch, 0.0), axis=0, keepdims=True)
        left = jnp.sum(jnp.where(mine, cnt[:, 0:1] + chunk0[:, 0:1] * ch, 0.0), axis=0, keepdims=True)
        pad = jnp.zeros((1, n_g - n_e), F32)

        def wide(row):
            return jnp.concatenate([row, pad], axis=1)

        total = jnp.sum(nch[:, 0:1], axis=0, keepdims=True)
        tab_ref[...] = jnp.zeros(tab_ref.shape, jnp.int32)
        tab_ref[TAB_CHUNK_ROW:TAB_CHUNK_ROW + 1, :] = (base + g_row * ch).astype(jnp.int32)
        tab_ref[TAB_CHUNK_VALID:TAB_CHUNK_VALID + 1, :] = jnp.clip(left - g_row * ch, 0.0, ch).astype(jnp.int32)
        tab_ref[TAB_START:TAB_START + 1, :] = wide(as_row(start_col)).astype(jnp.int32)
        tab_ref[TAB_COUNT:TAB_COUNT + 1, :] = wide(as_row(cnt_ref[:, 0:1])).astype(jnp.int32)
        tab_ref[TAB_CHUNK0:TAB_CHUNK0 + 1, :] = wide(as_row(chunk0[:, 0:1])).astype(jnp.int32)
        tab_ref[TAB_NCHUNK:TAB_NCHUNK + 1, :] = wide(as_row(nch[:, 0:1])).astype(jnp.int32)
        tab_ref[TAB_TOTAL:TAB_TOTAL + 1, :] = jnp.broadcast_to(total, (1, n_g)).astype(jnp.int32)

    io_e = lax.broadcasted_iota(jnp.int32, (n_e, tt), 0)
    rows = []
    for k in range(TOP_K):
        hit = io_e == eidx_ref[k:k + 1, :]
        rows.append(jnp.sum(jnp.where(hit, start_col, 0.0), axis=0, keepdims=True))
    dest_ref[...] = jnp.concatenate(rows, axis=0).astype(jnp.int32) + rank_ref[...]


def _dest(eidx, rank, cnt):
    k, t = eidx.shape
    n_e = cnt.shape[0]
    tt = 512
    n_g = (t * k) // EXPERT_CHUNK + n_e
    return pl.pallas_call(
        _dest_kernel,
        out_shape=(jax.ShapeDtypeStruct((k, t), jnp.int32),
                   jax.ShapeDtypeStruct((TAB_ROWS, n_g), jnp.int32)),
        grid=(t // tt,),
        in_specs=[pl.BlockSpec((k, tt), lambda i: (0, i)),
                  pl.BlockSpec((k, tt), lambda i: (0, i)),
                  _const_spec(cnt.shape)],
        out_specs=(pl.BlockSpec((k, tt), lambda i: (0, i)), _const_spec((TAB_ROWS, n_g))),
        compiler_params=pltpu.CompilerParams(dimension_semantics=("arbitrary",)),
        name="moe_dest",
    )(eidx, rank, cnt)


SC_CORES = 2
SC_SUBCORES = 16
SC_WORKERS = SC_CORES * SC_SUBCORES
SC_GATHER_ROWS = 64
COMBINE_SPLITS = 4


def _sc_mesh():
    return plsc.VectorSubcoreMesh(core_axis_name="c", subcore_axis_name="s")


def _sc_worker_id():
    return lax.axis_index("s") * SC_CORES + lax.axis_index("c")


def _sc_scatter_rows(rows, dest_flat, n_out):
    t, w = rows.shape
    n_k = dest_flat.shape[0] // t
    r = SC_GATHER_ROWS
    per = t // SC_WORKERS
    n_steps = per // r
    assert per % (2 * r) == 0

    @functools.partial(
        pl.kernel, mesh=_sc_mesh(), out_type=jax.ShapeDtypeStruct((n_out, w), rows.dtype),
        scratch_types=[pltpu.VMEM((n_k, r), jnp.int32), pltpu.VMEM((2, r, w), rows.dtype),
                       pltpu.SemaphoreType.DMA((2,)), pltpu.SemaphoreType.DMA],
        name="moe_dispatch_sc")
    def scatter(rows_hbm, idx_hbm, out_hbm, idx_v, rows_v, l_sem, s_sem):
        base = _sc_worker_id() * per

        def load(j, b):
            return pltpu.make_async_copy(rows_hbm.at[pl.ds(pl.multiple_of(base + j * r, ROW_ALIGN), r)],
                                         rows_v.at[b], l_sem.at[b])

        load(0, 0).start()

        @pl.loop(0, n_steps, step=2)
        def _(j0):
            for b in range(2):
                j = j0 + b
                pl.when(j + 1 < n_steps)(load(j + 1, 1 - b).start)
                for k in range(n_k):
                    off = pl.multiple_of(k * t + base + j * r, ROW_ALIGN)
                    pltpu.sync_copy(idx_hbm.at[pl.ds(off, r)], idx_v.at[k])
                load(j, b).wait()
                copies = [pltpu.make_async_copy(rows_v.at[b], out_hbm.at[idx_v.at[k]], s_sem) for k in range(n_k)]
                for cp in copies:
                    cp.start()
                for cp in copies:
                    cp.wait()

    return scatter(rows, dest_flat)


def _sc_gather_rows(table, idx):
    n = idx.shape[0]
    w = table.shape[1]
    r = SC_GATHER_ROWS
    per = n // SC_WORKERS
    n_steps = per // r
    assert per % (2 * r) == 0

    @functools.partial(
        pl.kernel, mesh=_sc_mesh(), out_type=jax.ShapeDtypeStruct((n, w), table.dtype),
        scratch_types=[pltpu.VMEM((2, r), jnp.int32), pltpu.VMEM((2, r, w), table.dtype),
                       pltpu.SemaphoreType.DMA((2,)), pltpu.SemaphoreType.DMA((2,))],
        name="moe_gather_sc")
    def gather(table_hbm, idx_hbm, out_hbm, idx_v, rows_v, g_sem, w_sem):
        base = _sc_worker_id() * per

        def at(j):
            return pl.ds(pl.multiple_of(base + j * r, ROW_ALIGN), r)

        def fetch(j, b):
            return pltpu.make_async_copy(table_hbm.at[idx_v.at[b]], rows_v.at[b], g_sem.at[b])

        def write(j, b):
            return pltpu.make_async_copy(rows_v.at[b], out_hbm.at[at(j)], w_sem.at[b])

        pltpu.sync_copy(idx_hbm.at[at(0)], idx_v.at[0])
        fetch(0, 0).start()

        @pl.loop(0, n_steps, step=2)
        def _(j0):
            for b in range(2):
                j = j0 + b

                @pl.when(j + 1 < n_steps)
                def _():
                    pl.when(j >= 1)(write(j - 1, 1 - b).wait)
                    pltpu.sync_copy(idx_hbm.at[at(j + 1)], idx_v.at[1 - b])
                    fetch(j + 1, 1 - b).start()

                fetch(j, b).wait()
                write(j, b).start()

        write(n_steps - 2, 0).wait()
        write(n_steps - 1, 1).wait()

    return gather(table, idx)


def _pad_kernel(tab_ref, xs_in, xs_hbm, zero_ref, sem, *, n_e):
    del xs_in
    zero_ref[...] = jnp.zeros_like(zero_ref)

    def pad_row_copy(row):
        return pltpu.make_async_copy(zero_ref.at[pl.ds(0, 1), :], xs_hbm.at[pl.ds(row, 1), :], sem)

    def pad_rows(e, carry, wait):
        count = tab_ref[TAB_COUNT, e]
        end = tab_ref[TAB_START, e] + count
        n_pad = (ROW_ALIGN - count % ROW_ALIGN) % ROW_ALIGN
        for j in range(ROW_ALIGN - 1):
            copy = pad_row_copy(end + j)
            pl.when(j < n_pad)(copy.wait if wait else copy.start)
        return carry

    lax.fori_loop(0, n_e, functools.partial(pad_rows, wait=False), 0)
    lax.fori_loop(0, n_e, functools.partial(pad_rows, wait=True), 0)
    last = tab_ref[TAB_COUNT, n_e - 1]
    used = tab_ref[TAB_START, n_e - 1] + ((last + (ROW_ALIGN - 1)) // ROW_ALIGN) * ROW_ALIGN
    piece = zero_ref.shape[0]
    n_rows = xs_hbm.shape[0]
    for j in range(pl.cdiv(n_e * ROW_ALIGN + EXPERT_CHUNK + ROW_ALIGN, piece)):
        at = pl.multiple_of(jnp.minimum(used + j * piece, n_rows - piece), ROW_ALIGN)
        tail = pltpu.make_async_copy(zero_ref, xs_hbm.at[pl.ds(at, piece), :], sem)
        tail.start()
        tail.wait()


def _sorted_rows(t, n_e):
    return t * TOP_K + n_e * ROW_ALIGN + EXPERT_CHUNK + ROW_ALIGN


def _dispatch(tab, dest, hp, n_e):
    t, m = hp.shape
    n_rows = _sorted_rows(t, n_e)
    xs = _sc_scatter_rows(hp, dest.reshape(-1), n_rows)
    grid_spec = pltpu.PrefetchScalarGridSpec(
        num_scalar_prefetch=1,
        grid=(1,),
        in_specs=[pl.BlockSpec(memory_space=pl.ANY)],
        out_specs=pl.BlockSpec(memory_space=pl.ANY),
        scratch_shapes=[pltpu.VMEM((EXPERT_CHUNK, m), PACKED), pltpu.SemaphoreType.DMA(())],
    )
    return pl.pallas_call(
        functools.partial(_pad_kernel, n_e=n_e),
        out_shape=jax.ShapeDtypeStruct((n_rows, m), PACKED),
        grid_spec=grid_spec,
        input_output_aliases={1: 0},
        compiler_params=pltpu.CompilerParams(dimension_semantics=("arbitrary",)),
        name="moe_pad",
    )(tab, xs)


X_LOOKAHEAD = 2
X_SLOTS = X_LOOKAHEAD + 1
Y_SLOTS = 2


def _expert_kernel(tab_ref, xs_hbm, wgu_ref, wd_ref, y_hbm,
                   wd16_ref, xbuf_ref, ybuf_ref, in_sem, out_sem, *, n_e):
    e = pl.program_id(0)
    ch, half = xbuf_ref.shape[1:]
    ff = wd_ref.shape[1]
    n_total = tab_ref[TAB_TOTAL, 0]
    first = tab_ref[TAB_CHUNK0, e]
    n_chunks = tab_ref[TAB_NCHUNK, e]

    def in_copy(g):
        slot = g % X_SLOTS
        row = pl.multiple_of(tab_ref[TAB_CHUNK_ROW, g], ROW_ALIGN)
        return pltpu.make_async_copy(xs_hbm.at[pl.ds(row, ch), :], xbuf_ref.at[slot], in_sem.at[slot])

    def out_copies(g):
        slot = g % Y_SLOTS
        row = tab_ref[TAB_CHUNK_ROW, g]
        valid = tab_ref[TAB_CHUNK_VALID, g]
        yield valid >= ch, pltpu.make_async_copy(ybuf_ref.at[slot], y_hbm.at[pl.ds(pl.multiple_of(row, ROW_ALIGN), ch), :],
                                                 out_sem.at[slot])
        size = ch // 2
        while size >= ROW_ALIGN:
            off = pl.multiple_of((valid // (2 * size)) * (2 * size), ROW_ALIGN)
            cond = (valid < ch) & ((valid & size) != 0)
            yield cond, pltpu.make_async_copy(ybuf_ref.at[slot, pl.ds(off, size), :],
                                              y_hbm.at[pl.ds(pl.multiple_of(row + off, ROW_ALIGN), size), :],
                                              out_sem.at[slot])
            size //= 2

    def start_out(g):
        for cond, cp in out_copies(g):
            pl.when(cond)(cp.start)

    def wait_out(g):
        for cond, cp in out_copies(g):
            pl.when(cond)(cp.wait)

    @pl.when(e == 0)
    def _():
        for j in range(X_LOOKAHEAD):
            pl.when(j < n_total)(in_copy(j).start)

    @pl.when(n_chunks > 0)
    def _():
        wd16_ref[...] = wd_ref[0].astype(BF16)

    def chunk(c, carry):
        g = first + c
        pl.when(g + X_LOOKAHEAD < n_total)(in_copy(g + X_LOOKAHEAD).start)
        in_copy(g).wait()
        pl.when(g >= Y_SLOTS)(lambda: wait_out(g - Y_SLOTS))
        valid = tab_ref[TAB_CHUNK_VALID, g]

        def ffn(rows):
            xa, xb = _unpack_bf16_pairs(xbuf_ref[g % X_SLOTS, 0:rows, :])
            hgu = (_dot(xa.astype(BF16), wgu_ref[0, 0:half, :])
                   + _dot(xb.astype(BF16), wgu_ref[0, half:2 * half, :]))
            act = _silu(hgu[:, :ff]) * hgu[:, ff:]
            y = _dot(act.astype(BF16), wd16_ref[...])
            ybuf_ref[g % Y_SLOTS, 0:rows, :] = _pack_bf16_pairs(y)

        lo = 0
        for rows in EXPERT_ROW_STEPS:
            pl.when((valid > lo) & (valid <= rows))(functools.partial(ffn, rows))
            lo = rows
        start_out(g)
        return carry

    lax.fori_loop(0, n_chunks, chunk, 0)

    @pl.when(e == n_e - 1)
    def _():
        for j in range(Y_SLOTS, 0, -1):
            pl.when(n_total >= j)(lambda j=j: wait_out(n_total - j))
        n_rows = y_hbm.shape[0]
        used = tab_ref[TAB_START, e] + ((tab_ref[TAB_COUNT, e] + (ROW_ALIGN - 1)) // ROW_ALIGN) * ROW_ALIGN
        ybuf_ref[0] = jnp.zeros((ch, half), PACKED)
        for j in range(pl.cdiv(n_e * ROW_ALIGN + EXPERT_CHUNK + ROW_ALIGN, ch)):
            at = pl.multiple_of(jnp.minimum(used + j * ch, n_rows - ch), ROW_ALIGN)
            tail = pltpu.make_async_copy(ybuf_ref.at[0], y_hbm.at[pl.ds(at, ch), :], out_sem.at[0])
            tail.start()
            tail.wait()


def _experts(tab, xs, wgu16, w_down_e):
    p, half = xs.shape
    n_e, ff, d = w_down_e.shape
    ch = EXPERT_CHUNK
    grid_spec = pltpu.PrefetchScalarGridSpec(
        num_scalar_prefetch=1,
        grid=(n_e,),
        in_specs=[pl.BlockSpec(memory_space=pl.ANY),
                  pl.BlockSpec((1, d, 2 * ff), lambda e, *_: (e, 0, 0)),
                  pl.BlockSpec((1, ff, d), lambda e, *_: (e, 0, 0))],
        out_specs=pl.BlockSpec(memory_space=pl.ANY),
        scratch_shapes=[pltpu.VMEM((ff, d), BF16),
                        pltpu.VMEM((X_SLOTS, ch, half), PACKED), pltpu.VMEM((Y_SLOTS, ch, half), PACKED),
                        pltpu.SemaphoreType.DMA((X_SLOTS,)), pltpu.SemaphoreType.DMA((Y_SLOTS,))],
    )
    return pl.pallas_call(
        functools.partial(_expert_kernel, n_e=n_e),
        out_shape=jax.ShapeDtypeStruct((p, half), PACKED),
        grid_spec=grid_spec,
        compiler_params=pltpu.CompilerParams(dimension_semantics=("arbitrary",),
                                             vmem_limit_bytes=VMEM_LIMIT_BYTES),
        name="moe_experts",
    )(tab, xs, wgu16, w_down_e)


def _combine_kernel(yg_ref, wcol_ref, x1_ref, mod_ref, npre_ref, npost_ref, wgus_ref, wds_ref, o_ref):
    te, d = x1_ref.shape
    half = d // 2
    ff = wds_ref.shape[0]

    x1 = x1_ref[...]
    h2 = _rms(x1, npre_ref[...]) * (1.0 + mod_ref[:, 4 * d:5 * d]) + mod_ref[:, 3 * d:4 * d]
    hgu = _dot(h2.astype(BF16), wgus_ref[...])
    act = _silu(hgu[:, :ff]) * hgu[:, ff:]
    shared = _dot(act.astype(BF16), wds_ref[...])

    acc_a = shared[:, :half]
    acc_b = shared[:, half:]
    for k in range(TOP_K):
        ya, yb = _unpack_bf16_pairs(yg_ref[k])
        wk = wcol_ref[:, k:k + 1]
        acc_a = acc_a + wk * ya
        acc_b = acc_b + wk * yb
    moe = jnp.concatenate([acc_a, acc_b], axis=-1)
    g2 = mod_ref[:, 5 * d:6 * d]
    o_ref[...] = x1 + g2 * _rms(moe, npost_ref[...])


def _combine(dest, y, wcol, x1, mod, npre2, npost2, w_gate_s, w_up_s, w_down_s):
    t, d = x1.shape
    half = d // 2
    te = SEQ_TILE
    wgus = jnp.concatenate([w_gate_s, w_up_s], axis=1).astype(BF16)
    wds = w_down_s.astype(BF16)
    n_split = COMBINE_SPLITS if t % (COMBINE_SPLITS * 2 * SC_GATHER_ROWS * SC_WORKERS // TOP_K) == 0 else 1
    tq = t // n_split
    steps = tq // te
    stream = x1
    for q in range(n_split):
        dest_q = dest[:, q * tq:(q + 1) * tq]
        yg = _sc_gather_rows(y, dest_q.reshape(-1)).reshape(TOP_K, tq, half)
        row = lambda i, q=q: (q * steps + i, 0)
        stream = pl.pallas_call(
            _combine_kernel,
            out_shape=jax.ShapeDtypeStruct((t, d), F32),
            grid=(steps,),
            in_specs=[pl.BlockSpec((TOP_K, te, half), lambda i: (0, i, 0)),
                      pl.BlockSpec((te, 128), row),
                      pl.BlockSpec((te, d), row),
                      _const_spec(mod.shape),
                      _const_spec((1, d)),
                      _const_spec((1, d)),
                      _const_spec(wgus.shape),
                      _const_spec(wds.shape)],
            out_specs=pl.BlockSpec((te, d), row),
            input_output_aliases={2: 0},
            compiler_params=pltpu.CompilerParams(dimension_semantics=("arbitrary",),
                                                 vmem_limit_bytes=VMEM_LIMIT_BYTES),
            name="moe_combine",
        )(yg, wcol, stream, mod, npre2.reshape(1, d), npost2.reshape(1, d), wgus, wds)
    return stream


def kernel(x, c, w_ada, b_ada, norm_pre_mix, norm_post_mix, w_in, conv_w, w_conv_out, hgrn_lower_bounds, hgrn_norm_w, w_hgrn_out, w_o, norm_pre_ffn, norm_post_ffn, w_router, router_bias, w_gate_e, w_up_e, w_down_e, w_gate_s, w_up_s, w_down_s):
    bsz, seq, d = x.shape
    assert bsz == 1 and w_ada.shape[0] == 1
    mod = _ada_mod(c, w_ada[0], b_ada[0])
    x1, wgu16 = _mixer(x.reshape(seq, d), mod, norm_pre_mix[0], norm_post_mix[0], w_in[0], conv_w[0],
                       w_conv_out[0], hgrn_lower_bounds, hgrn_norm_w[0], w_hgrn_out[0], w_o[0],
                       w_gate_e[0], w_up_e[0])
    hp, eidx, rank, wcol, cnt = _router(x1, mod, norm_pre_ffn[0], w_router[0], router_bias[0])
    dest, tab = _dest(eidx, rank, cnt)
    xs = _dispatch(tab, dest, hp, w_router.shape[-1])
    y = _experts(tab, xs, wgu16, w_down_e[0])
    out = _combine(dest, y, wcol, x1, mod, norm_pre_ffn[0], norm_post_ffn[0], w_gate_s[0], w_up_s[0], w_down_s[0])
    return out.reshape(bsz, seq, d)
```

```python
import functools

import jax
import jax.numpy as jnp
from jax import lax
from jax.experimental import pallas as pl
from jax.experimental.pallas import tpu as pltpu
from jax.experimental.pallas import tpu_sc as plsc

F32 = jnp.float32
BF16 = jnp.bfloat16

NORM_EPS = 1e-6
CONV_DIM = 512
HGRN_HEADS = 4
HGRN_DK = 128
HGRN_QK = HGRN_HEADS * HGRN_DK
N_GROUPS = 8
TOPK_GROUPS = 4
TOP_K = 8
ROUTED_SCALE = 2.5

SEQ_TILE = 256
EXPERT_CHUNK = 512
EXPERT_ROW_STEPS = (128, 256, EXPERT_CHUNK)
ROW_ALIGN = 8
VMEM_LIMIT_BYTES = 56 * 1024 * 1024


def _dot(a, b):
    return jnp.dot(a, b, preferred_element_type=F32)


def _dot_nt(a, b):
    return lax.dot_general(a, b, (((1,), (1,)), ((), ())), preferred_element_type=F32)


def _dot_tn(a, b):
    return lax.dot_general(a, b, (((0,), (0,)), ((), ())), preferred_element_type=F32)


def _split3(x):
    hi = x.astype(BF16)
    r1 = x - hi.astype(F32)
    mid = r1.astype(BF16)
    lo = (r1 - mid.astype(F32)).astype(BF16)
    return hi, mid, lo


def _sigmoid(x):
    return 1.0 / (1.0 + jnp.exp(-x))


def _silu(x):
    return x * _sigmoid(x)


def _rms(x, w):
    ms = jnp.mean(x * x, axis=-1, keepdims=True)
    return x * lax.rsqrt(ms + NORM_EPS) * w


def _ada_kernel(c_ref, w_ref, b_ref, o_ref):
    cs = _silu(c_ref[...])
    h1, h2, h3 = _split3(cs)
    w1, w2, w3 = _split3(w_ref[...])
    acc = _dot(h1, w1) + (_dot(h1, w2) + _dot(h2, w1)) + (_dot(h1, w3) + _dot(h2, w2) + _dot(h3, w1))
    o_ref[...] = acc + b_ref[...]


def _ada_mod(c, w_ada, b_ada):
    d = c.shape[-1]
    n = w_ada.shape[-1]
    bn = 1024
    c8 = jnp.broadcast_to(c.reshape(1, d), (8, d))
    out = pl.pallas_call(
        _ada_kernel,
        out_shape=jax.ShapeDtypeStruct((8, n), F32),
        grid=(n // bn,),
        in_specs=[pl.BlockSpec((8, d), lambda j: (0, 0)),
                  pl.BlockSpec((d, bn), lambda j: (0, j)),
                  pl.BlockSpec((1, bn), lambda j: (0, j))],
        out_specs=pl.BlockSpec((8, bn), lambda j: (0, j)),
        compiler_params=pltpu.CompilerParams(dimension_semantics=("arbitrary",),
                                             vmem_limit_bytes=VMEM_LIMIT_BYTES),
        name="ada_mod",
    )(c8, w_ada, b_ada.reshape(1, n))
    return out[0:1]


def _level_reference(b, rolled, s, row):
    ts = b.shape[0]
    c = s // 2 - 1
    if s >= 16:
        pieces = []
        for blk in range(ts // s):
            r = blk * s + c
            pieces.append(jnp.broadcast_to(b[r:r + 1, :], (s, b.shape[1])))
        return pieces[0] if len(pieces) == 1 else jnp.concatenate(pieces, axis=0)
    pos = row & (s - 1)
    out = b
    for p in range(s):
        if p == c:
            continue
        out = jnp.where(pos == p, rolled[p - c], out)
    return out


def _cast_expert_weights(wg_ref, wu_ref, wgu16_ref):
    ff = wg_ref.shape[2]
    for j in range(wg_ref.shape[0]):
        wgu16_ref[j, :, 0:ff] = wg_ref[j].astype(BF16)
        wgu16_ref[j, :, ff:2 * ff] = wu_ref[j].astype(BF16)


def _mixer_kernel(x_ref, mod_ref, npre_ref, npost_ref, win_ref, convw_ref, wco_ref,
                  hlb_ref, hnw_ref, who_ref, wo_ref, wg_ref, wu_ref,
                  o_ref, wgu16_ref, state_ref, ubuf_ref):
    ts, d = x_ref.shape
    step = pl.program_id(0)
    _cast_expert_weights(wg_ref, wu_ref, wgu16_ref)

    @pl.when(step == 0)
    def _():
        state_ref[...] = jnp.zeros_like(state_ref)
        ubuf_ref[0:8, :] = jnp.zeros((8, CONV_DIM), F32)

    x = x_ref[...]
    sh1 = mod_ref[:, 0:d]
    sc1 = mod_ref[:, d:2 * d]
    g1 = mod_ref[:, 2 * d:3 * d]
    h = (_rms(x, npre_ref[...]) * (1.0 + sc1) + sh1).astype(BF16)

    def proj(lo, width):
        return _dot(h, win_ref[:, lo:lo + width])

    c0 = 0
    cb = proj(c0, CONV_DIM)
    u = proj(c0 + CONV_DIM, CONV_DIM) * proj(c0 + 2 * CONV_DIM, CONV_DIM)
    ubuf_ref[8:8 + ts, :] = u
    conv = (ubuf_ref[6:6 + ts, :] * convw_ref[0:1, :]
            + ubuf_ref[7:7 + ts, :] * convw_ref[1:2, :]
            + u * convw_ref[2:3, :])
    ubuf_ref[0:8, :] = ubuf_ref[ts:ts + 8, :]
    y_a = _dot((cb * conv).astype(BF16), wco_ref[...])

    c1 = 3 * CONV_DIM
    q = _silu(proj(c1, HGRN_QK))
    hl = hlb_ref[...]
    hm = jnp.max(hl, axis=0, keepdims=True)
    he = jnp.exp(hl - hm)
    lb = he[0:1, :] / jnp.sum(he, axis=0, keepdims=True)
    fg = lb + (1.0 - lb) * _sigmoid(proj(c1 + HGRN_QK, HGRN_QK))
    k = 1.0 - fg
    g = jnp.log(fg)
    v = proj(c1 + 2 * HGRN_QK, HGRN_QK)
    gg = proj(c1 + 3 * HGRN_QK, HGRN_QK)

    row = lax.broadcasted_iota(jnp.int32, (ts, ts), 0)
    col = lax.broadcasted_iota(jnp.int32, (ts, ts), 1)
    tril = jnp.where(row >= col, 1.0, 0.0).astype(BF16)
    ghi, gmid, glo = _split3(g)
    b = _dot(tril, ghi) + _dot(tril, gmid) + _dot(tril, glo)
    b_last = b[ts - 1:ts, :]

    rowq = lax.broadcasted_iota(jnp.int32, (ts, HGRN_QK), 0)
    rolled = {sft: pltpu.roll(b, sft % ts, axis=0) for sft in (-3, -2, -1, 1, 2, 3, 4)}
    xor = row ^ col

    levels = []
    s = ts
    while s >= 2:
        levels.append(s)
        s //= 2
    qt, kt = [], []
    for s in levels:
        bref = _level_reference(b, rolled, s, rowq)
        e = jnp.exp(-jnp.abs(b - bref))
        upper = (rowq & (s // 2)) != 0
        qt.append(jnp.where(upper, q * e, 0.0).astype(BF16))
        kt.append(jnp.where(upper, 0.0, k * e).astype(BF16))
    q_in = (q * jnp.exp(b)).astype(BF16)
    k_out = (k * jnp.exp(b_last - b)).astype(BF16)
    v16 = v.astype(BF16)
    qk = q * k
    decay_last = jnp.exp(b_last)

    hnw = hnw_ref[...]
    outs = []
    for hd in range(HGRN_HEADS):
        sl = slice(hd * HGRN_DK, (hd + 1) * HGRN_DK)
        a = jnp.zeros((ts, ts), F32)
        for li in range(len(levels) - 1, -1, -1):
            s = levels[li]
            p = _dot_nt(qt[li][:, sl], kt[li][:, sl])
            a = jnp.where(xor >= s // 2, p, a)
        st = state_ref[hd]
        o_h = (_dot(a.astype(BF16), v16[:, sl])
               + jnp.sum(qk[:, sl], axis=-1, keepdims=True) * v[:, sl]
               + _dot_nt(q_in[:, sl], st.astype(BF16)))
        state_ref[hd] = st * decay_last[:, sl] + _dot_tn(v16[:, sl], k_out[:, sl])
        outs.append(_rms(o_h, hnw))
    o = jnp.concatenate(outs, axis=-1) * _silu(gg)
    y_b = _dot(o.astype(BF16), who_ref[...])

    c2 = c1 + 4 * HGRN_QK
    m = _sigmoid(proj(c2, d)) * y_a + _sigmoid(proj(c2 + d, d)) * y_b
    y = _dot(m.astype(BF16), wo_ref[...])
    o_ref[...] = x + g1 * _rms(y, npost_ref[...])


def _const_spec(shape):
    nd = len(shape)
    return pl.BlockSpec(shape, lambda i: (0,) * nd)


def _mixer(x2, mod, npre, npost, w_in, conv_w, w_conv_out, hlb, hnw, w_hgrn_out, w_o, w_gate_e, w_up_e):
    t, d = x2.shape
    ts = SEQ_TILE
    n = t // ts
    n_e, _, ff = w_gate_e.shape
    assert n_e % n == 0 and conv_w.shape == (3, CONV_DIM)
    per = n_e // n
    args = (x2, mod, npre.reshape(1, d), npost.reshape(1, d), w_in.astype(BF16), conv_w,
            w_conv_out.astype(BF16), hlb, hnw.reshape(1, -1), w_hgrn_out.astype(BF16),
            w_o.astype(BF16))
    experts = lambda i: (i, 0, 0)
    in_specs = ([pl.BlockSpec((ts, d), lambda i: (i, 0))] + [_const_spec(a.shape) for a in args[1:]]
                + [pl.BlockSpec((per, d, ff), experts)] * 2)
    return pl.pallas_call(
        _mixer_kernel,
        out_shape=[jax.ShapeDtypeStruct((t, d), F32), jax.ShapeDtypeStruct((n_e, d, 2 * ff), BF16)],
        grid=(n,),
        in_specs=in_specs,
        out_specs=[pl.BlockSpec((ts, d), lambda i: (i, 0)), pl.BlockSpec((per, d, 2 * ff), experts)],
        scratch_shapes=[pltpu.VMEM((HGRN_HEADS, HGRN_DK, HGRN_DK), F32),
                        pltpu.VMEM((ts + 8, CONV_DIM), F32)],
        compiler_params=pltpu.CompilerParams(dimension_semantics=("arbitrary",),
                                             vmem_limit_bytes=VMEM_LIMIT_BYTES),
        name="token_mixer",
    )(*args, w_gate_e, w_up_e)


PACKED = jnp.int32


def _pack_bf16_pairs(x):
    m = x.shape[1] // 2
    hi = lax.bitcast_convert_type(x[:, :m].astype(BF16).astype(F32), jnp.uint32)
    lo = lax.bitcast_convert_type(x[:, m:].astype(BF16).astype(F32), jnp.uint32)
    return lax.bitcast_convert_type(hi | (lo >> 16), PACKED)


def _unpack_bf16_pairs(p):
    p = lax.bitcast_convert_type(p, jnp.uint32)
    hi = lax.bitcast_convert_type(p & jnp.uint32(0xFFFF0000), F32)
    lo = lax.bitcast_convert_type(p << 16, F32)
    return hi, lo


def _router_kernel(x1_ref, mod_ref, npre_ref, wrh_ref, wrl_ref, rb_ref,
                   hp_ref, eidx_ref, rank_ref, wcol_ref, cnt_ref, carry_ref, rows_ref):
    assert N_GROUPS == TOP_K == rows_ref.shape[0]
    tr, d = x1_ref.shape
    n_e = wrh_ref.shape[0]
    gsz = n_e // N_GROUPS
    neg = -jnp.inf

    @pl.when(pl.program_id(0) == 0)
    def _():
        carry_ref[...] = jnp.zeros_like(carry_ref)

    sh2 = mod_ref[:, 3 * d:4 * d]
    sc2 = mod_ref[:, 4 * d:5 * d]
    h2 = _rms(x1_ref[...], npre_ref[...]) * (1.0 + sc2) + sh2
    hp_ref[...] = _pack_bf16_pairs(h2)
    h_hi = h2.astype(BF16)
    h_lo = (h2 - h_hi.astype(F32)).astype(BF16)
    wrh = wrh_ref[...]
    logits = _dot_nt(wrh, h_hi) + (_dot_nt(wrh, h_lo) + _dot_nt(wrl_ref[...], h_hi))
    scores = _sigmoid(logits)
    sel = scores + rb_ref[:, 0:1]

    io_e = lax.broadcasted_iota(jnp.int32, (n_e, tr), 0)
    for g in range(N_GROUPS):
        blk = sel[g * gsz:(g + 1) * gsz, :]
        io = lax.broadcasted_iota(jnp.int32, (gsz, tr), 0) + g * gsz
        m1 = jnp.max(blk, axis=0, keepdims=True)
        i1 = jnp.min(jnp.where(blk == m1, io, n_e), axis=0, keepdims=True)
        m2 = jnp.max(jnp.where(io == i1, neg, blk), axis=0, keepdims=True)
        rows_ref[g:g + 1, :] = m1 + m2
    gs = rows_ref[...]
    io_g = lax.broadcasted_iota(jnp.int32, (N_GROUPS, tr), 0)
    gsel = jnp.zeros((N_GROUPS, tr), F32)
    for _ in range(TOPK_GROUPS):
        m = jnp.max(gs, axis=0, keepdims=True)
        gi = jnp.min(jnp.where(gs == m, io_g, N_GROUPS), axis=0, keepdims=True)
        hit = io_g == gi
        gsel = jnp.where(hit, 1.0, gsel)
        gs = jnp.where(hit, neg, gs)
    rows_ref[...] = gsel
    cur = jnp.concatenate(
        [jnp.where(rows_ref[g:g + 1, :] > 0.5, sel[g * gsz:(g + 1) * gsz, :], neg) for g in range(N_GROUPS)],
        axis=0)

    idxs = []
    selm = jnp.zeros((n_e, tr), F32)
    for k in range(TOP_K):
        m = jnp.max(cur, axis=0, keepdims=True)
        idx = jnp.min(jnp.where(cur == m, io_e, n_e), axis=0, keepdims=True)
        hit = io_e == idx
        rows_ref[k:k + 1, :] = jnp.sum(jnp.where(hit, scores, 0.0), axis=0, keepdims=True)
        cur = jnp.where(hit, neg, cur)
        selm = jnp.where(hit, 1.0, selm)
        eidx_ref[k:k + 1, :] = idx
        idxs.append(idx)

    r_io = lax.broadcasted_iota(jnp.int32, (tr, tr + 128), 0)
    c_io = lax.broadcasted_iota(jnp.int32, (tr, tr + 128), 1)
    before = jnp.where((r_io < c_io) | (c_io >= tr), 1.0, 0.0).astype(BF16)
    r_ext = _dot(selm.astype(BF16), before)
    rank_full = r_ext[:, :tr] + carry_ref[:, 0:1]
    for k, idx in enumerate(idxs):
        rk = jnp.sum(jnp.where(io_e == idx, rank_full, 0.0), axis=0, keepdims=True)
        rank_ref[k:k + 1, :] = rk.astype(jnp.int32)
    carry_ref[...] = carry_ref[...] + r_ext[:, tr:]
    cnt_ref[...] = carry_ref[...]

    wk = rows_ref[...]
    wn = wk / (jnp.sum(wk, axis=0, keepdims=True) + 1e-20) * ROUTED_SCALE
    eye = jnp.where(lax.broadcasted_iota(jnp.int32, (TOP_K, 128), 0)
                    == lax.broadcasted_iota(jnp.int32, (TOP_K, 128), 1), 1.0, 0.0).astype(BF16)
    w1, w2, w3 = _split3(wn)
    wcol_ref[...] = _dot_tn(w1, eye) + _dot_tn(w2, eye) + _dot_tn(w3, eye)


def _router(x1, mod, npre2, w_router, router_bias):
    t, d = x1.shape
    n_e = w_router.shape[1]
    tr = SEQ_TILE
    wrt = w_router.T
    wrh = wrt.astype(BF16)
    wrl = (wrt - wrh.astype(F32)).astype(BF16)
    rb = jnp.broadcast_to(router_bias.reshape(n_e, 1), (n_e, 128))
    args = (x1, mod, npre2.reshape(1, d), wrh, wrl, rb)
    in_specs = [pl.BlockSpec((tr, d), lambda i: (i, 0))] + [_const_spec(a.shape) for a in args[1:]]
    return pl.pallas_call(
        _router_kernel,
        out_shape=(jax.ShapeDtypeStruct((t, d // 2), PACKED),
                   jax.ShapeDtypeStruct((TOP_K, t), jnp.int32),
                   jax.ShapeDtypeStruct((TOP_K, t), jnp.int32),
                   jax.ShapeDtypeStruct((t, 128), F32),
                   jax.ShapeDtypeStruct((n_e, 128), F32)),
        grid=(t // tr,),
        in_specs=in_specs,
        out_specs=(pl.BlockSpec((tr, d // 2), lambda i: (i, 0)),
                   pl.BlockSpec((TOP_K, tr), lambda i: (0, i)),
                   pl.BlockSpec((TOP_K, tr), lambda i: (0, i)),
                   pl.BlockSpec((tr, 128), lambda i: (i, 0)),
                   pl.BlockSpec((n_e, 128), lambda i: (0, 0))),
        scratch_shapes=[pltpu.VMEM((n_e, 128), F32), pltpu.VMEM((TOP_K, tr), F32)],
        compiler_params=pltpu.CompilerParams(dimension_semantics=("arbitrary",),
                                             vmem_limit_bytes=VMEM_LIMIT_BYTES),
        name="moe_router",
    )(*args)


TAB_CHUNK_ROW, TAB_CHUNK_VALID, TAB_START, TAB_COUNT, TAB_CHUNK0, TAB_NCHUNK, TAB_TOTAL = range(7)
TAB_ROWS = 8


def _excl_cumsum_rows(lower, x):
    hi = jnp.floor(x * (1.0 / 128.0))
    lo = x - hi * 128.0
    return 128.0 * _dot(lower, hi.astype(BF16)) + _dot(lower, lo.astype(BF16))


def _dest_kernel(eidx_ref, rank_ref, cnt_ref, dest_ref, tab_ref, pad_ref, *, n_rows):
    n_e = cnt_ref.shape[0]
    tt = eidx_ref.shape[1]
    n_g = tab_ref.shape[1]
    ch = float(EXPERT_CHUNK)
    cnt = jnp.floor((cnt_ref[...] + (ROW_ALIGN - 1)) * (1.0 / ROW_ALIGN)) * ROW_ALIGN
    r = lax.broadcasted_iota(jnp.int32, (n_e, n_e), 0)
    c = lax.broadcasted_iota(jnp.int32, (n_e, n_e), 1)
    lower = jnp.where(c < r, 1.0, 0.0).astype(BF16)
    starts = _excl_cumsum_rows(lower, cnt)
    start_col = starts[:, 0:1]

    @pl.when(pl.program_id(0) == 0)
    def _():
        nch = jnp.floor((cnt + (ch - 1.0)) * (1.0 / ch))
        chunk0 = _excl_cumsum_rows(lower, nch)
        cend_col = (chunk0 + nch)[:, 0:1]
        eye = r == c

        def as_row(col):
            return jnp.sum(jnp.where(eye, col, 0.0), axis=0, keepdims=True)

        g = lax.broadcasted_iota(jnp.int32, (n_e, n_g), 1).astype(F32)
        owner = jnp.sum(jnp.where(cend_col <= g, 1.0, 0.0), axis=0, keepdims=True)
        mine = lax.broadcasted_iota(jnp.int32, (n_e, n_g), 0).astype(F32) == owner
        g_row = lax.broadcasted_iota(jnp.int32, (1, n_g), 1).astype(F32)
        base = jnp.sum(jnp.where(mine, start_col - chunk0[:, 0:1] * ch, 0.0), axis=0, keepdims=True)
        left = jnp.sum(jnp.where(mine, cnt[:, 0:1] + chunk0[:, 0:1] * ch, 0.0), axis=0, keepdims=True)
        pad = jnp.zeros((1, n_g - n_e), F32)

        def wide(row):
            return jnp.concatenate([row, pad], axis=1)

        total = jnp.sum(nch[:, 0:1], axis=0, keepdims=True)
        tab_ref[...] = jnp.zeros(tab_ref.shape, jnp.int32)
        tab_ref[TAB_CHUNK_ROW:TAB_CHUNK_ROW + 1, :] = (base + g_row * ch).astype(jnp.int32)
        tab_ref[TAB_CHUNK_VALID:TAB_CHUNK_VALID + 1, :] = jnp.clip(left - g_row * ch, 0.0, ch).astype(jnp.int32)
        tab_ref[TAB_START:TAB_START + 1, :] = wide(as_row(start_col)).astype(jnp.int32)
        tab_ref[TAB_COUNT:TAB_COUNT + 1, :] = wide(as_row(cnt_ref[:, 0:1])).astype(jnp.int32)
        tab_ref[TAB_CHUNK0:TAB_CHUNK0 + 1, :] = wide(as_row(chunk0[:, 0:1])).astype(jnp.int32)
        tab_ref[TAB_NCHUNK:TAB_NCHUNK + 1, :] = wide(as_row(nch[:, 0:1])).astype(jnp.int32)
        tab_ref[TAB_TOTAL:TAB_TOTAL + 1, :] = jnp.broadcast_to(total, (1, n_g)).astype(jnp.int32)

        cols = pad_ref.shape[1]
        tail_cols = cols - n_e
        j_io = lax.broadcasted_iota(jnp.int32, (ROW_ALIGN, cols), 0).astype(F32)
        c_io = lax.broadcasted_iota(jnp.int32, (ROW_ALIGN, cols), 1).astype(F32)
        spare = float(n_rows) + j_io * cols + c_io
        zeros_tail = jnp.zeros((1, tail_cols), F32)
        raw_col = cnt_ref[:, 0:1]
        end_w = jnp.concatenate([as_row(start_col + raw_col), zeros_tail], axis=1)
        npad_w = jnp.concatenate([as_row(cnt[:, 0:1] - raw_col), zeros_tail], axis=1)
        used = jnp.sum(cnt[:, 0:1], axis=0, keepdims=True)
        after = used + j_io * tail_cols + (c_io - n_e)
        target = jnp.where(c_io < n_e,
                           jnp.where(j_io < npad_w, end_w + j_io, spare),
                           jnp.where(after < float(n_rows), after, spare))
        pad_ref[...] = target.astype(jnp.int32)

    io_e = lax.broadcasted_iota(jnp.int32, (n_e, tt), 0)
    rows = []
    for k in range(TOP_K):
        hit = io_e == eidx_ref[k:k + 1, :]
        rows.append(jnp.sum(jnp.where(hit, start_col, 0.0), axis=0, keepdims=True))
    dest_ref[...] = jnp.concatenate(rows, axis=0).astype(jnp.int32) + rank_ref[...]


def _sorted_rows(t, n_e):
    return t * TOP_K + n_e * ROW_ALIGN + EXPERT_CHUNK + ROW_ALIGN


def _pad_list_cols(t, n_e):
    entries = n_e * ROW_ALIGN + (_sorted_rows(t, n_e) - t * TOP_K)
    per_round = SC_WORKERS * SC_GATHER_ROWS
    return pl.cdiv(entries, per_round) * per_round // ROW_ALIGN


def _dest(eidx, rank, cnt):
    k, t = eidx.shape
    n_e = cnt.shape[0]
    tt = 512
    n_g = (t * k) // EXPERT_CHUNK + n_e
    cols = _pad_list_cols(t, n_e)
    return pl.pallas_call(
        functools.partial(_dest_kernel, n_rows=_sorted_rows(t, n_e)),
        out_shape=(jax.ShapeDtypeStruct((k, t), jnp.int32),
                   jax.ShapeDtypeStruct((TAB_ROWS, n_g), jnp.int32),
                   jax.ShapeDtypeStruct((ROW_ALIGN, cols), jnp.int32)),
        grid=(t // tt,),
        in_specs=[pl.BlockSpec((k, tt), lambda i: (0, i)),
                  pl.BlockSpec((k, tt), lambda i: (0, i)),
                  _const_spec(cnt.shape)],
        out_specs=(pl.BlockSpec((k, tt), lambda i: (0, i)), _const_spec((TAB_ROWS, n_g)),
                   _const_spec((ROW_ALIGN, cols))),
        compiler_params=pltpu.CompilerParams(dimension_semantics=("arbitrary",)),
        name="moe_dest",
    )(eidx, rank, cnt)


SC_CORES = 2
SC_SUBCORES = 16
SC_WORKERS = SC_CORES * SC_SUBCORES
SC_GATHER_ROWS = 64
COMBINE_SPLITS = 4


def _sc_mesh():
    return plsc.VectorSubcoreMesh(core_axis_name="c", subcore_axis_name="s")


def _sc_worker_id():
    return lax.axis_index("s") * SC_CORES + lax.axis_index("c")


def _sc_scatter_rows(rows, dest_flat, zero_list, n_out):
    t, w = rows.shape
    n_k = dest_flat.shape[0] // t
    r = SC_GATHER_ROWS
    per = t // SC_WORKERS
    n_steps = per // r
    zero_rounds = zero_list.shape[0] // (SC_WORKERS * r)
    assert per % (2 * r) == 0 and zero_list.shape[0] % (SC_WORKERS * r) == 0
    zeros = jnp.zeros((r, w), rows.dtype)

    @functools.partial(
        pl.kernel, mesh=_sc_mesh(), out_type=jax.ShapeDtypeStruct((n_out, w), rows.dtype),
        scratch_types=[pltpu.VMEM((n_k, r), jnp.int32), pltpu.VMEM((2, r, w), rows.dtype),
                       pltpu.SemaphoreType.DMA((2,)), pltpu.SemaphoreType.DMA],
        name="moe_dispatch_sc")
    def scatter(rows_hbm, idx_hbm, zero_idx_hbm, zeros_hbm, out_hbm, idx_v, rows_v, l_sem, s_sem):
        worker = _sc_worker_id()
        base = worker * per

        pltpu.sync_copy(zeros_hbm, rows_v.at[0])
        for z in range(zero_rounds):
            off = pl.multiple_of((worker * zero_rounds + z) * r, ROW_ALIGN)
            pltpu.sync_copy(zero_idx_hbm.at[pl.ds(off, r)], idx_v.at[0])
            pltpu.async_copy(rows_v.at[0], out_hbm.at[idx_v.at[0]], s_sem).wait()


        def load(j, b):
            return pltpu.make_async_copy(rows_hbm.at[pl.ds(pl.multiple_of(base + j * r, ROW_ALIGN), r)],
                                         rows_v.at[b], l_sem.at[b])

        load(0, 0).start()

        @pl.loop(0, n_steps, step=2)
        def _(j0):
            for b in range(2):
                j = j0 + b
                pl.when(j + 1 < n_steps)(load(j + 1, 1 - b).start)
                for k in range(n_k):
                    off = pl.multiple_of(k * t + base + j * r, ROW_ALIGN)
                    pltpu.sync_copy(idx_hbm.at[pl.ds(off, r)], idx_v.at[k])
                load(j, b).wait()
                copies = [pltpu.make_async_copy(rows_v.at[b], out_hbm.at[idx_v.at[k]], s_sem) for k in range(n_k)]
                for cp in copies:
                    cp.start()
                for cp in copies:
                    cp.wait()

    return scatter(rows, dest_flat, zero_list, zeros)


def _sc_gather_rows(table, idx):
    n = idx.shape[0]
    w = table.shape[1]
    r = SC_GATHER_ROWS
    per = n // SC_WORKERS
    n_steps = per // r
    assert per % (2 * r) == 0

    @functools.partial(
        pl.kernel, mesh=_sc_mesh(), out_type=jax.ShapeDtypeStruct((n, w), table.dtype),
        scratch_types=[pltpu.VMEM((2, r), jnp.int32), pltpu.VMEM((2, r, w), table.dtype),
                       pltpu.SemaphoreType.DMA((2,)), pltpu.SemaphoreType.DMA((2,))],
        name="moe_gather_sc")
    def gather(table_hbm, idx_hbm, out_hbm, idx_v, rows_v, g_sem, w_sem):
        base = _sc_worker_id() * per

        def at(j):
            return pl.ds(pl.multiple_of(base + j * r, ROW_ALIGN), r)

        def fetch(j, b):
            return pltpu.make_async_copy(table_hbm.at[idx_v.at[b]], rows_v.at[b], g_sem.at[b])

        def write(j, b):
            return pltpu.make_async_copy(rows_v.at[b], out_hbm.at[at(j)], w_sem.at[b])

        pltpu.sync_copy(idx_hbm.at[at(0)], idx_v.at[0])
        fetch(0, 0).start()

        @pl.loop(0, n_steps, step=2)
        def _(j0):
            for b in range(2):
                j = j0 + b

                @pl.when(j + 1 < n_steps)
                def _():
                    pl.when(j >= 1)(write(j - 1, 1 - b).wait)
                    pltpu.sync_copy(idx_hbm.at[at(j + 1)], idx_v.at[1 - b])
                    fetch(j + 1, 1 - b).start()

                fetch(j, b).wait()
                write(j, b).start()

        write(n_steps - 2, 0).wait()
        write(n_steps - 1, 1).wait()

    return gather(table, idx)


def _dispatch(dest, pad_list, hp, n_e):
    t = hp.shape[0]
    n_out = _sorted_rows(t, n_e) + pad_list.size
    return _sc_scatter_rows(hp, dest.reshape(-1), pad_list.reshape(-1), n_out)


X_LOOKAHEAD = 2
X_SLOTS = X_LOOKAHEAD + 1
Y_SLOTS = 2


def _expert_kernel(tab_ref, xs_hbm, wgu_ref, wd_ref, y_hbm,
                   wd16_ref, xbuf_ref, ybuf_ref, in_sem, out_sem, *, n_e):
    e = pl.program_id(0)
    ch, half = xbuf_ref.shape[1:]
    ff = wd_ref.shape[1]
    n_total = tab_ref[TAB_TOTAL, 0]
    first = tab_ref[TAB_CHUNK0, e]
    n_chunks = tab_ref[TAB_NCHUNK, e]

    def in_copy(g):
        slot = g % X_SLOTS
        row = pl.multiple_of(tab_ref[TAB_CHUNK_ROW, g], ROW_ALIGN)
        return pltpu.make_async_copy(xs_hbm.at[pl.ds(row, ch), :], xbuf_ref.at[slot], in_sem.at[slot])

    def out_copies(g):
        slot = g % Y_SLOTS
        row = tab_ref[TAB_CHUNK_ROW, g]
        valid = tab_ref[TAB_CHUNK_VALID, g]
        yield valid >= ch, pltpu.make_async_copy(ybuf_ref.at[slot], y_hbm.at[pl.ds(pl.multiple_of(row, ROW_ALIGN), ch), :],
                                                 out_sem.at[slot])
        size = ch // 2
        while size >= ROW_ALIGN:
            off = pl.multiple_of((valid // (2 * size)) * (2 * size), ROW_ALIGN)
            cond = (valid < ch) & ((valid & size) != 0)
            yield cond, pltpu.make_async_copy(ybuf_ref.at[slot, pl.ds(off, size), :],
                                              y_hbm.at[pl.ds(pl.multiple_of(row + off, ROW_ALIGN), size), :],
                                              out_sem.at[slot])
            size //= 2

    def start_out(g):
        for cond, cp in out_copies(g):
            pl.when(cond)(cp.start)

    def wait_out(g):
        for cond, cp in out_copies(g):
            pl.when(cond)(cp.wait)

    @pl.when(e == 0)
    def _():
        for j in range(X_LOOKAHEAD):
            pl.when(j < n_total)(in_copy(j).start)

    @pl.when(n_chunks > 0)
    def _():
        wd16_ref[...] = wd_ref[0].astype(BF16)

    def chunk(c, carry):
        g = first + c
        pl.when(g + X_LOOKAHEAD < n_total)(in_copy(g + X_LOOKAHEAD).start)
        in_copy(g).wait()
        pl.when(g >= Y_SLOTS)(lambda: wait_out(g - Y_SLOTS))
        valid = tab_ref[TAB_CHUNK_VALID, g]

        def ffn(rows):
            xa, xb = _unpack_bf16_pairs(xbuf_ref[g % X_SLOTS, 0:rows, :])
            hgu = (_dot(xa.astype(BF16), wgu_ref[0, 0:half, :])
                   + _dot(xb.astype(BF16), wgu_ref[0, half:2 * half, :]))
            act = _silu(hgu[:, :ff]) * hgu[:, ff:]
            y = _dot(act.astype(BF16), wd16_ref[...])
            ybuf_ref[g % Y_SLOTS, 0:rows, :] = _pack_bf16_pairs(y)

        lo = 0
        for rows in EXPERT_ROW_STEPS:
            pl.when((valid > lo) & (valid <= rows))(functools.partial(ffn, rows))
            lo = rows
        start_out(g)
        return carry

    lax.fori_loop(0, n_chunks, chunk, 0)

    @pl.when(e == n_e - 1)
    def _():
        for j in range(Y_SLOTS, 0, -1):
            pl.when(n_total >= j)(lambda j=j: wait_out(n_total - j))
        n_rows = y_hbm.shape[0]
        used = tab_ref[TAB_START, e] + ((tab_ref[TAB_COUNT, e] + (ROW_ALIGN - 1)) // ROW_ALIGN) * ROW_ALIGN
        ybuf_ref[0] = jnp.zeros((ch, half), PACKED)
        for j in range(pl.cdiv(n_e * ROW_ALIGN + EXPERT_CHUNK + ROW_ALIGN, ch)):
            at = pl.multiple_of(jnp.minimum(used + j * ch, n_rows - ch), ROW_ALIGN)
            tail = pltpu.make_async_copy(ybuf_ref.at[0], y_hbm.at[pl.ds(at, ch), :], out_sem.at[0])
            tail.start()
            tail.wait()


def _experts(tab, xs, wgu16, w_down_e, p):
    half = xs.shape[1]
    n_e, ff, d = w_down_e.shape
    ch = EXPERT_CHUNK
    grid_spec = pltpu.PrefetchScalarGridSpec(
        num_scalar_prefetch=1,
        grid=(n_e,),
        in_specs=[pl.BlockSpec(memory_space=pl.ANY),
                  pl.BlockSpec((1, d, 2 * ff), lambda e, *_: (e, 0, 0)),
                  pl.BlockSpec((1, ff, d), lambda e, *_: (e, 0, 0))],
        out_specs=pl.BlockSpec(memory_space=pl.ANY),
        scratch_shapes=[pltpu.VMEM((ff, d), BF16),
                        pltpu.VMEM((X_SLOTS, ch, half), PACKED), pltpu.VMEM((Y_SLOTS, ch, half), PACKED),
                        pltpu.SemaphoreType.DMA((X_SLOTS,)), pltpu.SemaphoreType.DMA((Y_SLOTS,))],
    )
    return pl.pallas_call(
        functools.partial(_expert_kernel, n_e=n_e),
        out_shape=jax.ShapeDtypeStruct((p, half), PACKED),
        grid_spec=grid_spec,
        compiler_params=pltpu.CompilerParams(dimension_semantics=("arbitrary",),
                                             vmem_limit_bytes=VMEM_LIMIT_BYTES),
        name="moe_experts",
    )(tab, xs, wgu16, w_down_e)


def _combine_kernel(yg_ref, wcol_ref, x1_ref, mod_ref, npre_ref, npost_ref, wgus_ref, wds_ref, o_ref):
    te, d = x1_ref.shape
    half = d // 2
    ff = wds_ref.shape[0]

    x1 = x1_ref[...]
    h2 = _rms(x1, npre_ref[...]) * (1.0 + mod_ref[:, 4 * d:5 * d]) + mod_ref[:, 3 * d:4 * d]
    hgu = _dot(h2.astype(BF16), wgus_ref[...])
    act = _silu(hgu[:, :ff]) * hgu[:, ff:]
    shared = _dot(act.astype(BF16), wds_ref[...])

    acc_a = shared[:, :half]
    acc_b = shared[:, half:]
    for k in range(TOP_K):
        ya, yb = _unpack_bf16_pairs(yg_ref[k])
        wk = wcol_ref[:, k:k + 1]
        acc_a = acc_a + wk * ya
        acc_b = acc_b + wk * yb
    moe = jnp.concatenate([acc_a, acc_b], axis=-1)
    g2 = mod_ref[:, 5 * d:6 * d]
    o_ref[...] = x1 + g2 * _rms(moe, npost_ref[...])


def _combine(dest, y, wcol, x1, mod, npre2, npost2, w_gate_s, w_up_s, w_down_s):
    t, d = x1.shape
    half = d // 2
    te = SEQ_TILE
    wgus = jnp.concatenate([w_gate_s, w_up_s], axis=1).astype(BF16)
    wds = w_down_s.astype(BF16)
    n_split = COMBINE_SPLITS if t % (COMBINE_SPLITS * 2 * SC_GATHER_ROWS * SC_WORKERS // TOP_K) == 0 else 1
    tq = t // n_split
    steps = tq // te
    stream = x1
    for q in range(n_split):
        dest_q = dest[:, q * tq:(q + 1) * tq]
        yg = _sc_gather_rows(y, dest_q.reshape(-1)).reshape(TOP_K, tq, half)
        row = lambda i, q=q: (q * steps + i, 0)
        stream = pl.pallas_call(
            _combine_kernel,
            out_shape=jax.ShapeDtypeStruct((t, d), F32),
            grid=(steps,),
            in_specs=[pl.BlockSpec((TOP_K, te, half), lambda i: (0, i, 0)),
                      pl.BlockSpec((te, 128), row),
                      pl.BlockSpec((te, d), row),
                      _const_spec(mod.shape),
                      _const_spec((1, d)),
                      _const_spec((1, d)),
                      _const_spec(wgus.shape),
                      _const_spec(wds.shape)],
            out_specs=pl.BlockSpec((te, d), row),
            input_output_aliases={2: 0},
            compiler_params=pltpu.CompilerParams(dimension_semantics=("arbitrary",),
                                                 vmem_limit_bytes=VMEM_LIMIT_BYTES),
            name="moe_combine",
        )(yg, wcol, stream, mod, npre2.reshape(1, d), npost2.reshape(1, d), wgus, wds)
    return stream


def kernel(x, c, w_ada, b_ada, norm_pre_mix, norm_post_mix, w_in, conv_w, w_conv_out, hgrn_lower_bounds, hgrn_norm_w, w_hgrn_out, w_o, norm_pre_ffn, norm_post_ffn, w_router, router_bias, w_gate_e, w_up_e, w_down_e, w_gate_s, w_up_s, w_down_s):
    bsz, seq, d = x.shape
    assert bsz == 1 and w_ada.shape[0] == 1
    mod = _ada_mod(c, w_ada[0], b_ada[0])
    x1, wgu16 = _mixer(x.reshape(seq, d), mod, norm_pre_mix[0], norm_post_mix[0], w_in[0], conv_w[0],
                       w_conv_out[0], hgrn_lower_bounds, hgrn_norm_w[0], w_hgrn_out[0], w_o[0],
                       w_gate_e[0], w_up_e[0])
    hp, eidx, rank, wcol, cnt = _router(x1, mod, norm_pre_ffn[0], w_router[0], router_bias[0])
    n_e = w_router.shape[-1]
    dest, tab, pad_list = _dest(eidx, rank, cnt)
    xs = _dispatch(dest, pad_list, hp, n_e)
    y = _experts(tab, xs, wgu16, w_down_e[0], _sorted_rows(seq, n_e))
    out = _combine(dest, y, wcol, x1, mod, norm_pre_ffn[0], norm_post_ffn[0], w_gate_s[0], w_up_s[0], w_down_s[0])
    return out.reshape(bsz, seq, d)
```

```python
import functools

import jax
import jax.numpy as jnp
from jax import lax
from jax.experimental import pallas as pl
from jax.experimental.pallas import tpu as pltpu
from jax.experimental.pallas import tpu_sc as plsc

F32 = jnp.float32
BF16 = jnp.bfloat16

NORM_EPS = 1e-6
CONV_DIM = 512
HGRN_HEADS = 4
HGRN_DK = 128
HGRN_QK = HGRN_HEADS * HGRN_DK
N_GROUPS = 8
TOPK_GROUPS = 4
TOP_K = 8
ROUTED_SCALE = 2.5

SEQ_TILE = 256
EXPERT_CHUNK = 512
EXPERT_ROW_STEPS = (128, 256, EXPERT_CHUNK)
ROW_ALIGN = 8
VMEM_LIMIT_BYTES = 56 * 1024 * 1024


def _dot(a, b):
    return jnp.dot(a, b, preferred_element_type=F32)


def _dot_nt(a, b):
    return lax.dot_general(a, b, (((1,), (1,)), ((), ())), preferred_element_type=F32)


def _dot_tn(a, b):
    return lax.dot_general(a, b, (((0,), (0,)), ((), ())), preferred_element_type=F32)


def _split3(x):
    hi = x.astype(BF16)
    r1 = x - hi.astype(F32)
    mid = r1.astype(BF16)
    lo = (r1 - mid.astype(F32)).astype(BF16)
    return hi, mid, lo


def _sigmoid(x):
    return 1.0 / (1.0 + jnp.exp(-x))


def _silu(x):
    return x * _sigmoid(x)


def _rms(x, w):
    ms = jnp.mean(x * x, axis=-1, keepdims=True)
    return x * lax.rsqrt(ms + NORM_EPS) * w


def _ada_kernel(c_ref, w_ref, b_ref, o_ref):
    cs = _silu(c_ref[...])
    h1, h2, h3 = _split3(cs)
    w1, w2, w3 = _split3(w_ref[...])
    acc = _dot(h1, w1) + (_dot(h1, w2) + _dot(h2, w1)) + (_dot(h1, w3) + _dot(h2, w2) + _dot(h3, w1))
    o_ref[...] = acc + b_ref[...]


def _ada_mod(c, w_ada, b_ada):
    d = c.shape[-1]
    n = w_ada.shape[-1]
    bn = 1024
    c8 = jnp.broadcast_to(c.reshape(1, d), (8, d))
    out = pl.pallas_call(
        _ada_kernel,
        out_shape=jax.ShapeDtypeStruct((8, n), F32),
        grid=(n // bn,),
        in_specs=[pl.BlockSpec((8, d), lambda j: (0, 0)),
                  pl.BlockSpec((d, bn), lambda j: (0, j)),
                  pl.BlockSpec((1, bn), lambda j: (0, j))],
        out_specs=pl.BlockSpec((8, bn), lambda j: (0, j)),
        compiler_params=pltpu.CompilerParams(dimension_semantics=("arbitrary",),
                                             vmem_limit_bytes=VMEM_LIMIT_BYTES),
        name="ada_mod",
    )(c8, w_ada, b_ada.reshape(1, n))
    return out[0:1]


def _level_reference(b, rolled, s, row):
    ts = b.shape[0]
    c = s // 2 - 1
    if s >= 16:
        pieces = []
        for blk in range(ts // s):
            r = blk * s + c
            pieces.append(jnp.broadcast_to(b[r:r + 1, :], (s, b.shape[1])))
        return pieces[0] if len(pieces) == 1 else jnp.concatenate(pieces, axis=0)
    pos = row & (s - 1)
    out = b
    for p in range(s):
        if p == c:
            continue
        out = jnp.where(pos == p, rolled[p - c], out)
    return out


def _cast_expert_weights(wg_ref, wu_ref, wgu16_ref):
    ff = wg_ref.shape[2]
    for j in range(wg_ref.shape[0]):
        wgu16_ref[j, :, 0:ff] = wg_ref[j].astype(BF16)
        wgu16_ref[j, :, ff:2 * ff] = wu_ref[j].astype(BF16)


def _mixer_kernel(x_ref, mod_ref, npre_ref, npost_ref, win_ref, convw_ref, wco_ref,
                  hlb_ref, hnw_ref, who_ref, wo_ref, wg_ref, wu_ref,
                  o_ref, wgu16_ref, state_ref, ubuf_ref):
    ts, d = x_ref.shape
    step = pl.program_id(0)
    _cast_expert_weights(wg_ref, wu_ref, wgu16_ref)

    @pl.when(step == 0)
    def _():
        state_ref[...] = jnp.zeros_like(state_ref)
        ubuf_ref[0:8, :] = jnp.zeros((8, CONV_DIM), F32)

    x = x_ref[...]
    sh1 = mod_ref[:, 0:d]
    sc1 = mod_ref[:, d:2 * d]
    g1 = mod_ref[:, 2 * d:3 * d]
    h = (_rms(x, npre_ref[...]) * (1.0 + sc1) + sh1).astype(BF16)

    def proj(lo, width):
        return _dot(h, win_ref[:, lo:lo + width])

    c0 = 0
    cb = proj(c0, CONV_DIM)
    u = proj(c0 + CONV_DIM, CONV_DIM) * proj(c0 + 2 * CONV_DIM, CONV_DIM)
    ubuf_ref[8:8 + ts, :] = u
    conv = (ubuf_ref[6:6 + ts, :] * convw_ref[0:1, :]
            + ubuf_ref[7:7 + ts, :] * convw_ref[1:2, :]
            + u * convw_ref[2:3, :])
    ubuf_ref[0:8, :] = ubuf_ref[ts:ts + 8, :]
    y_a = _dot((cb * conv).astype(BF16), wco_ref[...])

    c1 = 3 * CONV_DIM
    q = _silu(proj(c1, HGRN_QK))
    hl = hlb_ref[...]
    hm = jnp.max(hl, axis=0, keepdims=True)
    he = jnp.exp(hl - hm)
    lb = he[0:1, :] / jnp.sum(he, axis=0, keepdims=True)
    fg = lb + (1.0 - lb) * _sigmoid(proj(c1 + HGRN_QK, HGRN_QK))
    k = 1.0 - fg
    g = jnp.log(fg)
    v = proj(c1 + 2 * HGRN_QK, HGRN_QK)
    gg = proj(c1 + 3 * HGRN_QK, HGRN_QK)

    row = lax.broadcasted_iota(jnp.int32, (ts, ts), 0)
    col = lax.broadcasted_iota(jnp.int32, (ts, ts), 1)
    tril = jnp.where(row >= col, 1.0, 0.0).astype(BF16)
    ghi, gmid, glo = _split3(g)
    b = _dot(tril, ghi) + _dot(tril, gmid) + _dot(tril, glo)
    b_last = b[ts - 1:ts, :]

    rowq = lax.broadcasted_iota(jnp.int32, (ts, HGRN_QK), 0)
    rolled = {sft: pltpu.roll(b, sft % ts, axis=0) for sft in (-3, -2, -1, 1, 2, 3, 4)}
    xor = row ^ col

    levels = []
    s = ts
    while s >= 2:
        levels.append(s)
        s //= 2
    qt, kt = [], []
    for s in levels:
        bref = _level_reference(b, rolled, s, rowq)
        e = jnp.exp(-jnp.abs(b - bref))
        upper = (rowq & (s // 2)) != 0
        qt.append(jnp.where(upper, q * e, 0.0).astype(BF16))
        kt.append(jnp.where(upper, 0.0, k * e).astype(BF16))
    q_in = (q * jnp.exp(b)).astype(BF16)
    k_out = (k * jnp.exp(b_last - b)).astype(BF16)
    v16 = v.astype(BF16)
    qk = q * k
    decay_last = jnp.exp(b_last)

    hnw = hnw_ref[...]
    outs = []
    for hd in range(HGRN_HEADS):
        sl = slice(hd * HGRN_DK, (hd + 1) * HGRN_DK)
        a = jnp.zeros((ts, ts), F32)
        for li in range(len(levels) - 1, -1, -1):
            s = levels[li]
            p = _dot_nt(qt[li][:, sl], kt[li][:, sl])
            a = jnp.where(xor >= s // 2, p, a)
        st = state_ref[hd]
        o_h = (_dot(a.astype(BF16), v16[:, sl])
               + jnp.sum(qk[:, sl], axis=-1, keepdims=True) * v[:, sl]
               + _dot_nt(q_in[:, sl], st.astype(BF16)))
        state_ref[hd] = st * decay_last[:, sl] + _dot_tn(v16[:, sl], k_out[:, sl])
        outs.append(_rms(o_h, hnw))
    o = jnp.concatenate(outs, axis=-1) * _silu(gg)
    y_b = _dot(o.astype(BF16), who_ref[...])

    c2 = c1 + 4 * HGRN_QK
    m = _sigmoid(proj(c2, d)) * y_a + _sigmoid(proj(c2 + d, d)) * y_b
    y = _dot(m.astype(BF16), wo_ref[...])
    o_ref[...] = x + g1 * _rms(y, npost_ref[...])


def _const_spec(shape):
    nd = len(shape)
    return pl.BlockSpec(shape, lambda i: (0,) * nd)


def _mixer(x2, mod, npre, npost, w_in, conv_w, w_conv_out, hlb, hnw, w_hgrn_out, w_o, w_gate_e, w_up_e):
    t, d = x2.shape
    ts = SEQ_TILE
    n = t // ts
    n_e, _, ff = w_gate_e.shape
    assert n_e % n == 0 and conv_w.shape == (3, CONV_DIM)
    per = n_e // n
    args = (x2, mod, npre.reshape(1, d), npost.reshape(1, d), w_in.astype(BF16), conv_w,
            w_conv_out.astype(BF16), hlb, hnw.reshape(1, -1), w_hgrn_out.astype(BF16),
            w_o.astype(BF16))
    experts = lambda i: (i, 0, 0)
    in_specs = ([pl.BlockSpec((ts, d), lambda i: (i, 0))] + [_const_spec(a.shape) for a in args[1:]]
                + [pl.BlockSpec((per, d, ff), experts)] * 2)
    return pl.pallas_call(
        _mixer_kernel,
        out_shape=[jax.ShapeDtypeStruct((t, d), F32), jax.ShapeDtypeStruct((n_e, d, 2 * ff), BF16)],
        grid=(n,),
        in_specs=in_specs,
        out_specs=[pl.BlockSpec((ts, d), lambda i: (i, 0)), pl.BlockSpec((per, d, 2 * ff), experts)],
        scratch_shapes=[pltpu.VMEM((HGRN_HEADS, HGRN_DK, HGRN_DK), F32),
                        pltpu.VMEM((ts + 8, CONV_DIM), F32)],
        compiler_params=pltpu.CompilerParams(dimension_semantics=("arbitrary",),
                                             vmem_limit_bytes=VMEM_LIMIT_BYTES),
        name="token_mixer",
    )(*args, w_gate_e, w_up_e)


PACKED = jnp.int32


def _pack_bf16_pairs(x):
    m = x.shape[1] // 2
    hi = lax.bitcast_convert_type(x[:, :m].astype(BF16).astype(F32), jnp.uint32)
    lo = lax.bitcast_convert_type(x[:, m:].astype(BF16).astype(F32), jnp.uint32)
    return lax.bitcast_convert_type(hi | (lo >> 16), PACKED)


def _unpack_bf16_pairs(p):
    p = lax.bitcast_convert_type(p, jnp.uint32)
    hi = lax.bitcast_convert_type(p & jnp.uint32(0xFFFF0000), F32)
    lo = lax.bitcast_convert_type(p << 16, F32)
    return hi, lo


def _router_kernel(x1_ref, mod_ref, npre_ref, wrh_ref, wrl_ref, rb_ref,
                   hp_ref, eidx_ref, rank_ref, wcol_ref, cnt_ref, carry_ref, rows_ref):
    assert N_GROUPS == TOP_K == rows_ref.shape[0]
    tr, d = x1_ref.shape
    n_e = wrh_ref.shape[0]
    gsz = n_e // N_GROUPS
    neg = -jnp.inf

    @pl.when(pl.program_id(0) == 0)
    def _():
        carry_ref[...] = jnp.zeros_like(carry_ref)

    sh2 = mod_ref[:, 3 * d:4 * d]
    sc2 = mod_ref[:, 4 * d:5 * d]
    h2 = _rms(x1_ref[...], npre_ref[...]) * (1.0 + sc2) + sh2
    hp_ref[...] = _pack_bf16_pairs(h2)
    h_hi = h2.astype(BF16)
    h_lo = (h2 - h_hi.astype(F32)).astype(BF16)
    wrh = wrh_ref[...]
    logits = _dot_nt(wrh, h_hi) + (_dot_nt(wrh, h_lo) + _dot_nt(wrl_ref[...], h_hi))
    scores = _sigmoid(logits)
    sel = scores + rb_ref[:, 0:1]

    io_e = lax.broadcasted_iota(jnp.int32, (n_e, tr), 0)
    for g in range(N_GROUPS):
        blk = sel[g * gsz:(g + 1) * gsz, :]
        io = lax.broadcasted_iota(jnp.int32, (gsz, tr), 0) + g * gsz
        m1 = jnp.max(blk, axis=0, keepdims=True)
        i1 = jnp.min(jnp.where(blk == m1, io, n_e), axis=0, keepdims=True)
        m2 = jnp.max(jnp.where(io == i1, neg, blk), axis=0, keepdims=True)
        rows_ref[g:g + 1, :] = m1 + m2
    gs = rows_ref[...]
    io_g = lax.broadcasted_iota(jnp.int32, (N_GROUPS, tr), 0)
    gsel = jnp.zeros((N_GROUPS, tr), F32)
    for _ in range(TOPK_GROUPS):
        m = jnp.max(gs, axis=0, keepdims=True)
        gi = jnp.min(jnp.where(gs == m, io_g, N_GROUPS), axis=0, keepdims=True)
        hit = io_g == gi
        gsel = jnp.where(hit, 1.0, gsel)
        gs = jnp.where(hit, neg, gs)
    rows_ref[...] = gsel
    cur = jnp.concatenate(
        [jnp.where(rows_ref[g:g + 1, :] > 0.5, sel[g * gsz:(g + 1) * gsz, :], neg) for g in range(N_GROUPS)],
        axis=0)

    idxs = []
    selm = jnp.zeros((n_e, tr), F32)
    for k in range(TOP_K):
        m = jnp.max(cur, axis=0, keepdims=True)
        idx = jnp.min(jnp.where(cur == m, io_e, n_e), axis=0, keepdims=True)
        hit = io_e == idx
        rows_ref[k:k + 1, :] = jnp.sum(jnp.where(hit, scores, 0.0), axis=0, keepdims=True)
        cur = jnp.where(hit, neg, cur)
        selm = jnp.where(hit, 1.0, selm)
        eidx_ref[k:k + 1, :] = idx
        idxs.append(idx)

    r_io = lax.broadcasted_iota(jnp.int32, (tr, tr + 128), 0)
    c_io = lax.broadcasted_iota(jnp.int32, (tr, tr + 128), 1)
    before = jnp.where((r_io < c_io) | (c_io >= tr), 1.0, 0.0).astype(BF16)
    r_ext = _dot(selm.astype(BF16), before)
    rank_full = r_ext[:, :tr] + carry_ref[:, 0:1]
    for k, idx in enumerate(idxs):
        rk = jnp.sum(jnp.where(io_e == idx, rank_full, 0.0), axis=0, keepdims=True)
        rank_ref[k:k + 1, :] = rk.astype(jnp.int32)
    carry_ref[...] = carry_ref[...] + r_ext[:, tr:]
    cnt_ref[...] = carry_ref[...]

    wk = rows_ref[...]
    wn = wk / (jnp.sum(wk, axis=0, keepdims=True) + 1e-20) * ROUTED_SCALE
    eye = jnp.where(lax.broadcasted_iota(jnp.int32, (TOP_K, 128), 0)
                    == lax.broadcasted_iota(jnp.int32, (TOP_K, 128), 1), 1.0, 0.0).astype(BF16)
    w1, w2, w3 = _split3(wn)
    wcol_ref[...] = _dot_tn(w1, eye) + _dot_tn(w2, eye) + _dot_tn(w3, eye)


def _router(x1, mod, npre2, w_router, router_bias):
    t, d = x1.shape
    n_e = w_router.shape[1]
    tr = SEQ_TILE
    wrt = w_router.T
    wrh = wrt.astype(BF16)
    wrl = (wrt - wrh.astype(F32)).astype(BF16)
    rb = jnp.broadcast_to(router_bias.reshape(n_e, 1), (n_e, 128))
    args = (x1, mod, npre2.reshape(1, d), wrh, wrl, rb)
    in_specs = [pl.BlockSpec((tr, d), lambda i: (i, 0))] + [_const_spec(a.shape) for a in args[1:]]
    return pl.pallas_call(
        _router_kernel,
        out_shape=(jax.ShapeDtypeStruct((t, d // 2), PACKED),
                   jax.ShapeDtypeStruct((TOP_K, t), jnp.int32),
                   jax.ShapeDtypeStruct((TOP_K, t), jnp.int32),
                   jax.ShapeDtypeStruct((t, 128), F32),
                   jax.ShapeDtypeStruct((n_e, 128), F32)),
        grid=(t // tr,),
        in_specs=in_specs,
        out_specs=(pl.BlockSpec((tr, d // 2), lambda i: (i, 0)),
                   pl.BlockSpec((TOP_K, tr), lambda i: (0, i)),
                   pl.BlockSpec((TOP_K, tr), lambda i: (0, i)),
                   pl.BlockSpec((tr, 128), lambda i: (i, 0)),
                   pl.BlockSpec((n_e, 128), lambda i: (0, 0))),
        scratch_shapes=[pltpu.VMEM((n_e, 128), F32), pltpu.VMEM((TOP_K, tr), F32)],
        compiler_params=pltpu.CompilerParams(dimension_semantics=("arbitrary",),
                                             vmem_limit_bytes=VMEM_LIMIT_BYTES),
        name="moe_router",
    )(*args)


TAB_CHUNK_ROW, TAB_CHUNK_VALID, TAB_START, TAB_COUNT, TAB_CHUNK0, TAB_NCHUNK, TAB_TOTAL = range(7)
TAB_ROWS = 8


def _excl_cumsum_rows(lower, x):
    hi = jnp.floor(x * (1.0 / 128.0))
    lo = x - hi * 128.0
    return 128.0 * _dot(lower, hi.astype(BF16)) + _dot(lower, lo.astype(BF16))


def _dest_kernel(eidx_ref, rank_ref, cnt_ref, dest_ref, tab_ref, pad_ref, *, n_rows):
    n_e = cnt_ref.shape[0]
    tt = eidx_ref.shape[1]
    n_g = tab_ref.shape[1]
    ch = float(EXPERT_CHUNK)
    cnt = jnp.floor((cnt_ref[...] + (ROW_ALIGN - 1)) * (1.0 / ROW_ALIGN)) * ROW_ALIGN
    r = lax.broadcasted_iota(jnp.int32, (n_e, n_e), 0)
    c = lax.broadcasted_iota(jnp.int32, (n_e, n_e), 1)
    lower = jnp.where(c < r, 1.0, 0.0).astype(BF16)
    starts = _excl_cumsum_rows(lower, cnt)
    start_col = starts[:, 0:1]

    @pl.when(pl.program_id(0) == 0)
    def _():
        nch = jnp.floor((cnt + (ch - 1.0)) * (1.0 / ch))
        chunk0 = _excl_cumsum_rows(lower, nch)
        cend_col = (chunk0 + nch)[:, 0:1]
        eye = r == c

        def as_row(col):
            return jnp.sum(jnp.where(eye, col, 0.0), axis=0, keepdims=True)

        g = lax.broadcasted_iota(jnp.int32, (n_e, n_g), 1).astype(F32)
        owner = jnp.sum(jnp.where(cend_col <= g, 1.0, 0.0), axis=0, keepdims=True)
        mine = lax.broadcasted_iota(jnp.int32, (n_e, n_g), 0).astype(F32) == owner
        g_row = lax.broadcasted_iota(jnp.int32, (1, n_g), 1).astype(F32)
        base = jnp.sum(jnp.where(mine, start_col - chunk0[:, 0:1] * ch, 0.0), axis=0, keepdims=True)
        left = jnp.sum(jnp.where(mine, cnt[:, 0:1] + chunk0[:, 0:1] * ch, 0.0), axis=0, keepdims=True)
        pad = jnp.zeros((1, n_g - n_e), F32)

        def wide(row):
            return jnp.concatenate([row, pad], axis=1)

        total = jnp.sum(nch[:, 0:1], axis=0, keepdims=True)
        tab_ref[...] = jnp.zeros(tab_ref.shape, jnp.int32)
        tab_ref[TAB_CHUNK_ROW:TAB_CHUNK_ROW + 1, :] = (base + g_row * ch).astype(jnp.int32)
        tab_ref[TAB_CHUNK_VALID:TAB_CHUNK_VALID + 1, :] = jnp.clip(left - g_row * ch, 0.0, ch).astype(jnp.int32)
        tab_ref[TAB_START:TAB_START + 1, :] = wide(as_row(start_col)).astype(jnp.int32)
        tab_ref[TAB_COUNT:TAB_COUNT + 1, :] = wide(as_row(cnt_ref[:, 0:1])).astype(jnp.int32)
        tab_ref[TAB_CHUNK0:TAB_CHUNK0 + 1, :] = wide(as_row(chunk0[:, 0:1])).astype(jnp.int32)
        tab_ref[TAB_NCHUNK:TAB_NCHUNK + 1, :] = wide(as_row(nch[:, 0:1])).astype(jnp.int32)
        tab_ref[TAB_TOTAL:TAB_TOTAL + 1, :] = jnp.broadcast_to(total, (1, n_g)).astype(jnp.int32)

        cols = pad_ref.shape[1]
        tail_cols = cols - n_e
        j_io = lax.broadcasted_iota(jnp.int32, (ROW_ALIGN, cols), 0).astype(F32)
        c_io = lax.broadcasted_iota(jnp.int32, (ROW_ALIGN, cols), 1).astype(F32)
        spare = float(n_rows) + j_io * cols + c_io
        zeros_tail = jnp.zeros((1, tail_cols), F32)
        raw_col = cnt_ref[:, 0:1]
        end_w = jnp.concatenate([as_row(start_col + raw_col), zeros_tail], axis=1)
        npad_w = jnp.concatenate([as_row(cnt[:, 0:1] - raw_col), zeros_tail], axis=1)
        used = jnp.sum(cnt[:, 0:1], axis=0, keepdims=True)
        after = used + j_io * tail_cols + (c_io - n_e)
        target = jnp.where(c_io < n_e,
                           jnp.where(j_io < npad_w, end_w + j_io, spare),
                           jnp.where(after < float(n_rows), after, spare))
        pad_ref[...] = target.astype(jnp.int32)

    io_e = lax.broadcasted_iota(jnp.int32, (n_e, tt), 0)
    rows = []
    for k in range(TOP_K):
        hit = io_e == eidx_ref[k:k + 1, :]
        rows.append(jnp.sum(jnp.where(hit, start_col, 0.0), axis=0, keepdims=True))
    dest_ref[...] = jnp.concatenate(rows, axis=0).astype(jnp.int32) + rank_ref[...]


def _sorted_rows(t, n_e):
    return t * TOP_K + n_e * ROW_ALIGN + EXPERT_CHUNK + ROW_ALIGN


def _pad_list_cols(t, n_e):
    entries = n_e * ROW_ALIGN + (_sorted_rows(t, n_e) - t * TOP_K)
    per_round = SC_WORKERS * SC_GATHER_ROWS
    return pl.cdiv(entries, per_round) * per_round // ROW_ALIGN


def _dest(eidx, rank, cnt):
    k, t = eidx.shape
    n_e = cnt.shape[0]
    tt = 512
    n_g = (t * k) // EXPERT_CHUNK + n_e
    cols = _pad_list_cols(t, n_e)
    return pl.pallas_call(
        functools.partial(_dest_kernel, n_rows=_sorted_rows(t, n_e)),
        out_shape=(jax.ShapeDtypeStruct((k, t), jnp.int32),
                   jax.ShapeDtypeStruct((TAB_ROWS, n_g), jnp.int32),
                   jax.ShapeDtypeStruct((ROW_ALIGN, cols), jnp.int32)),
        grid=(t // tt,),
        in_specs=[pl.BlockSpec((k, tt), lambda i: (0, i)),
                  pl.BlockSpec((k, tt), lambda i: (0, i)),
                  _const_spec(cnt.shape)],
        out_specs=(pl.BlockSpec((k, tt), lambda i: (0, i)), _const_spec((TAB_ROWS, n_g)),
                   _const_spec((ROW_ALIGN, cols))),
        compiler_params=pltpu.CompilerParams(dimension_semantics=("arbitrary",)),
        name="moe_dest",
    )(eidx, rank, cnt)


SC_CORES = 2
SC_SUBCORES = 16
SC_WORKERS = SC_CORES * SC_SUBCORES
SC_GATHER_ROWS = 64
COMBINE_SPLITS = 4


def _sc_mesh():
    return plsc.VectorSubcoreMesh(core_axis_name="c", subcore_axis_name="s")


def _sc_worker_id():
    return lax.axis_index("s") * SC_CORES + lax.axis_index("c")


def _sc_scatter_rows(rows, dest_flat, zero_list, n_out):
    t, w = rows.shape
    n_k = dest_flat.shape[0] // t
    r = SC_GATHER_ROWS
    per = t // SC_WORKERS
    n_steps = per // r
    zero_rounds = zero_list.shape[0] // (SC_WORKERS * r)
    assert per % (2 * r) == 0 and zero_list.shape[0] % (SC_WORKERS * r) == 0 and zero_rounds <= n_k
    zeros = jnp.zeros((r, w), rows.dtype)

    @functools.partial(
        pl.kernel, mesh=_sc_mesh(), out_type=jax.ShapeDtypeStruct((n_out, w), rows.dtype),
        scratch_types=[pltpu.VMEM((n_k, r), jnp.int32), pltpu.VMEM((2, r, w), rows.dtype),
                       pltpu.SemaphoreType.DMA((2,)), pltpu.SemaphoreType.DMA],
        name="moe_dispatch_sc")
    def scatter(rows_hbm, idx_hbm, zero_idx_hbm, zeros_hbm, out_hbm, idx_v, rows_v, l_sem, s_sem):
        worker = _sc_worker_id()
        base = worker * per

        def load(j, b):
            return pltpu.make_async_copy(rows_hbm.at[pl.ds(pl.multiple_of(base + j * r, ROW_ALIGN), r)],
                                         rows_v.at[b], l_sem.at[b])

        load(0, 0).start()
        pltpu.sync_copy(zeros_hbm, rows_v.at[1])
        zero_copies = []
        for z in range(zero_rounds):
            off = pl.multiple_of((worker * zero_rounds + z) * r, ROW_ALIGN)
            pltpu.sync_copy(zero_idx_hbm.at[pl.ds(off, r)], idx_v.at[z])
            zero_copies.append(pltpu.make_async_copy(rows_v.at[1], out_hbm.at[idx_v.at[z]], s_sem))
            zero_copies[-1].start()
        for cp in zero_copies:
            cp.wait()

        @pl.loop(0, n_steps, step=2)
        def _(j0):
            for b in range(2):
                j = j0 + b
                pl.when(j + 1 < n_steps)(load(j + 1, 1 - b).start)
                for k in range(n_k):
                    off = pl.multiple_of(k * t + base + j * r, ROW_ALIGN)
                    pltpu.sync_copy(idx_hbm.at[pl.ds(off, r)], idx_v.at[k])
                load(j, b).wait()
                copies = [pltpu.make_async_copy(rows_v.at[b], out_hbm.at[idx_v.at[k]], s_sem) for k in range(n_k)]
                for cp in copies:
                    cp.start()
                for cp in copies:
                    cp.wait()

    return scatter(rows, dest_flat, zero_list, zeros)


def _sc_gather_rows(table, idx):
    n = idx.shape[0]
    w = table.shape[1]
    r = SC_GATHER_ROWS
    per = n // SC_WORKERS
    n_steps = per // r
    assert per % (2 * r) == 0

    @functools.partial(
        pl.kernel, mesh=_sc_mesh(), out_type=jax.ShapeDtypeStruct((n, w), table.dtype),
        scratch_types=[pltpu.VMEM((2, r), jnp.int32), pltpu.VMEM((2, r, w), table.dtype),
                       pltpu.SemaphoreType.DMA((2,)), pltpu.SemaphoreType.DMA((2,))],
        name="moe_gather_sc")
    def gather(table_hbm, idx_hbm, out_hbm, idx_v, rows_v, g_sem, w_sem):
        base = _sc_worker_id() * per

        def at(j):
            return pl.ds(pl.multiple_of(base + j * r, ROW_ALIGN), r)

        def fetch(j, b):
            return pltpu.make_async_copy(table_hbm.at[idx_v.at[b]], rows_v.at[b], g_sem.at[b])

        def write(j, b):
            return pltpu.make_async_copy(rows_v.at[b], out_hbm.at[at(j)], w_sem.at[b])

        pltpu.sync_copy(idx_hbm.at[at(0)], idx_v.at[0])
        fetch(0, 0).start()

        @pl.loop(0, n_steps, step=2)
        def _(j0):
            for b in range(2):
                j = j0 + b

                @pl.when(j + 1 < n_steps)
                def _():
                    pl.when(j >= 1)(write(j - 1, 1 - b).wait)
                    pltpu.sync_copy(idx_hbm.at[at(j + 1)], idx_v.at[1 - b])
                    fetch(j + 1, 1 - b).start()

                fetch(j, b).wait()
                write(j, b).start()

        write(n_steps - 2, 0).wait()
        write(n_steps - 1, 1).wait()

    return gather(table, idx)


def _dispatch(dest, pad_list, hp, n_e):
    t = hp.shape[0]
    n_out = _sorted_rows(t, n_e) + pad_list.size
    return _sc_scatter_rows(hp, dest.reshape(-1), pad_list.reshape(-1), n_out)


X_LOOKAHEAD = 2
X_SLOTS = X_LOOKAHEAD + 1
Y_SLOTS = 2


def _expert_kernel(tab_ref, xs_hbm, wgu_ref, wd_ref, y_hbm,
                   wd16_ref, xbuf_ref, ybuf_ref, in_sem, out_sem, *, n_e):
    e = pl.program_id(0)
    ch, half = xbuf_ref.shape[1:]
    ff = wd_ref.shape[1]
    n_total = tab_ref[TAB_TOTAL, 0]
    first = tab_ref[TAB_CHUNK0, e]
    n_chunks = tab_ref[TAB_NCHUNK, e]

    def in_copy(g):
        slot = g % X_SLOTS
        row = pl.multiple_of(tab_ref[TAB_CHUNK_ROW, g], ROW_ALIGN)
        return pltpu.make_async_copy(xs_hbm.at[pl.ds(row, ch), :], xbuf_ref.at[slot], in_sem.at[slot])

    def out_copies(g):
        slot = g % Y_SLOTS
        row = tab_ref[TAB_CHUNK_ROW, g]
        valid = tab_ref[TAB_CHUNK_VALID, g]
        yield valid >= ch, pltpu.make_async_copy(ybuf_ref.at[slot], y_hbm.at[pl.ds(pl.multiple_of(row, ROW_ALIGN), ch), :],
                                                 out_sem.at[slot])
        size = ch // 2
        while size >= ROW_ALIGN:
            off = pl.multiple_of((valid // (2 * size)) * (2 * size), ROW_ALIGN)
            cond = (valid < ch) & ((valid & size) != 0)
            yield cond, pltpu.make_async_copy(ybuf_ref.at[slot, pl.ds(off, size), :],
                                              y_hbm.at[pl.ds(pl.multiple_of(row + off, ROW_ALIGN), size), :],
                                              out_sem.at[slot])
            size //= 2

    def start_out(g):
        for cond, cp in out_copies(g):
            pl.when(cond)(cp.start)

    def wait_out(g):
        for cond, cp in out_copies(g):
            pl.when(cond)(cp.wait)

    @pl.when(e == 0)
    def _():
        for j in range(X_LOOKAHEAD):
            pl.when(j < n_total)(in_copy(j).start)

    @pl.when(n_chunks > 0)
    def _():
        wd16_ref[...] = wd_ref[0].astype(BF16)

    def chunk(c, carry):
        g = first + c
        pl.when(g + X_LOOKAHEAD < n_total)(in_copy(g + X_LOOKAHEAD).start)
        in_copy(g).wait()
        pl.when(g >= Y_SLOTS)(lambda: wait_out(g - Y_SLOTS))
        valid = tab_ref[TAB_CHUNK_VALID, g]

        def ffn(rows):
            xa, xb = _unpack_bf16_pairs(xbuf_ref[g % X_SLOTS, 0:rows, :])
            hgu = (_dot(xa.astype(BF16), wgu_ref[0, 0:half, :])
                   + _dot(xb.astype(BF16), wgu_ref[0, half:2 * half, :]))
            act = _silu(hgu[:, :ff]) * hgu[:, ff:]
            y = _dot(act.astype(BF16), wd16_ref[...])
            ybuf_ref[g % Y_SLOTS, 0:rows, :] = _pack_bf16_pairs(y)

        lo = 0
        for rows in EXPERT_ROW_STEPS:
            pl.when((valid > lo) & (valid <= rows))(functools.partial(ffn, rows))
            lo = rows
        start_out(g)
        return carry

    lax.fori_loop(0, n_chunks, chunk, 0)

    @pl.when(e == n_e - 1)
    def _():
        for j in range(Y_SLOTS, 0, -1):
            pl.when(n_total >= j)(lambda j=j: wait_out(n_total - j))
        n_rows = y_hbm.shape[0]
        used = tab_ref[TAB_START, e] + ((tab_ref[TAB_COUNT, e] + (ROW_ALIGN - 1)) // ROW_ALIGN) * ROW_ALIGN
        ybuf_ref[0] = jnp.zeros((ch, half), PACKED)
        for j in range(pl.cdiv(n_e * ROW_ALIGN + EXPERT_CHUNK + ROW_ALIGN, ch)):
            at = pl.multiple_of(jnp.minimum(used + j * ch, n_rows - ch), ROW_ALIGN)
            tail = pltpu.make_async_copy(ybuf_ref.at[0], y_hbm.at[pl.ds(at, ch), :], out_sem.at[0])
            tail.start()
            tail.wait()


def _experts(tab, xs, wgu16, w_down_e, p):
    half = xs.shape[1]
    n_e, ff, d = w_down_e.shape
    ch = EXPERT_CHUNK
    grid_spec = pltpu.PrefetchScalarGridSpec(
        num_scalar_prefetch=1,
        grid=(n_e,),
        in_specs=[pl.BlockSpec(memory_space=pl.ANY),
                  pl.BlockSpec((1, d, 2 * ff), lambda e, *_: (e, 0, 0)),
                  pl.BlockSpec((1, ff, d), lambda e, *_: (e, 0, 0))],
        out_specs=pl.BlockSpec(memory_space=pl.ANY),
        scratch_shapes=[pltpu.VMEM((ff, d), BF16),
                        pltpu.VMEM((X_SLOTS, ch, half), PACKED), pltpu.VMEM((Y_SLOTS, ch, half), PACKED),
                        pltpu.SemaphoreType.DMA((X_SLOTS,)), pltpu.SemaphoreType.DMA((Y_SLOTS,))],
    )
    return pl.pallas_call(
        functools.partial(_expert_kernel, n_e=n_e),
        out_shape=jax.ShapeDtypeStruct((p, half), PACKED),
        grid_spec=grid_spec,
        compiler_params=pltpu.CompilerParams(dimension_semantics=("arbitrary",),
                                             vmem_limit_bytes=VMEM_LIMIT_BYTES),
        name="moe_experts",
    )(tab, xs, wgu16, w_down_e)


def _combine_kernel(yg_ref, wcol_ref, x1_ref, mod_ref, npre_ref, npost_ref, wgus_ref, wds_ref, o_ref):
    te, d = x1_ref.shape
    half = d // 2
    ff = wds_ref.shape[0]

    x1 = x1_ref[...]
    h2 = _rms(x1, npre_ref[...]) * (1.0 + mod_ref[:, 4 * d:5 * d]) + mod_ref[:, 3 * d:4 * d]
    hgu = _dot(h2.astype(BF16), wgus_ref[...])
    act = _silu(hgu[:, :ff]) * hgu[:, ff:]
    shared = _dot(act.astype(BF16), wds_ref[...])

    acc_a = shared[:, :half]
    acc_b = shared[:, half:]
    for k in range(TOP_K):
        ya, yb = _unpack_bf16_pairs(yg_ref[k])
        wk = wcol_ref[:, k:k + 1]
        acc_a = acc_a + wk * ya
        acc_b = acc_b + wk * yb
    moe = jnp.concatenate([acc_a, acc_b], axis=-1)
    g2 = mod_ref[:, 5 * d:6 * d]
    o_ref[...] = x1 + g2 * _rms(moe, npost_ref[...])


def _combine(dest, y, wcol, x1, mod, npre2, npost2, w_gate_s, w_up_s, w_down_s):
    t, d = x1.shape
    half = d // 2
    te = SEQ_TILE
    wgus = jnp.concatenate([w_gate_s, w_up_s], axis=1).astype(BF16)
    wds = w_down_s.astype(BF16)
    n_split = COMBINE_SPLITS if t % (COMBINE_SPLITS * 2 * SC_GATHER_ROWS * SC_WORKERS // TOP_K) == 0 else 1
    tq = t // n_split
    steps = tq // te
    stream = x1
    for q in range(n_split):
        dest_q = dest[:, q * tq:(q + 1) * tq]
        yg = _sc_gather_rows(y, dest_q.reshape(-1)).reshape(TOP_K, tq, half)
        row = lambda i, q=q: (q * steps + i, 0)
        stream = pl.pallas_call(
            _combine_kernel,
            out_shape=jax.ShapeDtypeStruct((t, d), F32),
            grid=(steps,),
            in_specs=[pl.BlockSpec((TOP_K, te, half), lambda i: (0, i, 0)),
                      pl.BlockSpec((te, 128), row),
                      pl.BlockSpec((te, d), row),
                      _const_spec(mod.shape),
                      _const_spec((1, d)),
                      _const_spec((1, d)),
                      _const_spec(wgus.shape),
                      _const_spec(wds.shape)],
            out_specs=pl.BlockSpec((te, d), row),
            input_output_aliases={2: 0},
            compiler_params=pltpu.CompilerParams(dimension_semantics=("arbitrary",),
                                                 vmem_limit_bytes=VMEM_LIMIT_BYTES),
            name="moe_combine",
        )(yg, wcol, stream, mod, npre2.reshape(1, d), npost2.reshape(1, d), wgus, wds)
    return stream


def kernel(x, c, w_ada, b_ada, norm_pre_mix, norm_post_mix, w_in, conv_w, w_conv_out, hgrn_lower_bounds, hgrn_norm_w, w_hgrn_out, w_o, norm_pre_ffn, norm_post_ffn, w_router, router_bias, w_gate_e, w_up_e, w_down_e, w_gate_s, w_up_s, w_down_s):
    bsz, seq, d = x.shape
    assert bsz == 1 and w_ada.shape[0] == 1
    mod = _ada_mod(c, w_ada[0], b_ada[0])
    x1, wgu16 = _mixer(x.reshape(seq, d), mod, norm_pre_mix[0], norm_post_mix[0], w_in[0], conv_w[0],
                       w_conv_out[0], hgrn_lower_bounds, hgrn_norm_w[0], w_hgrn_out[0], w_o[0],
                       w_gate_e[0], w_up_e[0])
    hp, eidx, rank, wcol, cnt = _router(x1, mod, norm_pre_ffn[0], w_router[0], router_bias[0])
    n_e = w_router.shape[-1]
    dest, tab, pad_list = _dest(eidx, rank, cnt)
    xs = _dispatch(dest, pad_list, hp, n_e)
    y = _experts(tab, xs, wgu16, w_down_e[0], _sorted_rows(seq, n_e))
    out = _combine(dest, y, wcol, x1, mod, norm_pre_ffn[0], norm_post_ffn[0], w_gate_s[0], w_up_s[0], w_down_s[0])
    return out.reshape(bsz, seq, d)
```

```python
import functools

import jax
import jax.numpy as jnp
from jax import lax
from jax.experimental import pallas as pl
from jax.experimental.pallas import tpu as pltpu
from jax.experimental.pallas import tpu_sc as plsc

F32 = jnp.float32
BF16 = jnp.bfloat16

NORM_EPS = 1e-6
CONV_DIM = 512
HGRN_HEADS = 4
HGRN_DK = 128
HGRN_QK = HGRN_HEADS * HGRN_DK
N_GROUPS = 8
TOPK_GROUPS = 4
TOP_K = 8
ROUTED_SCALE = 2.5
CODE_SHIFT = 16

SEQ_TILE = 256
EXPERT_CHUNK = 512
EXPERT_ROW_STEPS = (128, 256, EXPERT_CHUNK)
ROW_ALIGN = 8
VMEM_LIMIT_BYTES = 56 * 1024 * 1024


def _dot(a, b):
    return jnp.dot(a, b, preferred_element_type=F32)


def _dot_nt(a, b):
    return lax.dot_general(a, b, (((1,), (1,)), ((), ())), preferred_element_type=F32)


def _dot_tn(a, b):
    return lax.dot_general(a, b, (((0,), (0,)), ((), ())), preferred_element_type=F32)


def _split3(x):
    hi = x.astype(BF16)
    r1 = x - hi.astype(F32)
    mid = r1.astype(BF16)
    lo = (r1 - mid.astype(F32)).astype(BF16)
    return hi, mid, lo


def _sigmoid(x):
    return 1.0 / (1.0 + jnp.exp(-x))


def _silu(x):
    return x * _sigmoid(x)


def _rms(x, w):
    ms = jnp.mean(x * x, axis=-1, keepdims=True)
    return x * lax.rsqrt(ms + NORM_EPS) * w


def _ada_kernel(c_ref, w_ref, b_ref, o_ref):
    cs = _silu(c_ref[...])
    h1, h2, h3 = _split3(cs)
    w1, w2, w3 = _split3(w_ref[...])
    acc = _dot(h1, w1) + (_dot(h1, w2) + _dot(h2, w1)) + (_dot(h1, w3) + _dot(h2, w2) + _dot(h3, w1))
    o_ref[...] = acc + b_ref[...]


def _ada_mod(c, w_ada, b_ada):
    d = c.shape[-1]
    n = w_ada.shape[-1]
    bn = 1024
    c8 = jnp.broadcast_to(c.reshape(1, d), (8, d))
    out = pl.pallas_call(
        _ada_kernel,
        out_shape=jax.ShapeDtypeStruct((8, n), F32),
        grid=(n // bn,),
        in_specs=[pl.BlockSpec((8, d), lambda j: (0, 0)),
                  pl.BlockSpec((d, bn), lambda j: (0, j)),
                  pl.BlockSpec((1, bn), lambda j: (0, j))],
        out_specs=pl.BlockSpec((8, bn), lambda j: (0, j)),
        compiler_params=pltpu.CompilerParams(dimension_semantics=("arbitrary",),
                                             vmem_limit_bytes=VMEM_LIMIT_BYTES),
        name="ada_mod",
    )(c8, w_ada, b_ada.reshape(1, n))
    return out[0:1]


def _level_reference(b, rolled, s, row):
    ts = b.shape[0]
    c = s // 2 - 1
    if s >= 16:
        pieces = []
        for blk in range(ts // s):
            r = blk * s + c
            pieces.append(jnp.broadcast_to(b[r:r + 1, :], (s, b.shape[1])))
        return pieces[0] if len(pieces) == 1 else jnp.concatenate(pieces, axis=0)
    pos = row & (s - 1)
    out = b
    for p in range(s):
        if p == c:
            continue
        out = jnp.where(pos == p, rolled[p - c], out)
    return out


def _cast_expert_weights(wg_ref, wu_ref, wgu16_ref):
    ff = wg_ref.shape[2]
    for j in range(wg_ref.shape[0]):
        wgu16_ref[j, :, 0:ff] = wg_ref[j].astype(BF16)
        wgu16_ref[j, :, ff:2 * ff] = wu_ref[j].astype(BF16)


def _mixer_kernel(x_ref, mod_ref, npre_ref, npost_ref, win_ref, convw_ref, wco_ref,
                  hlb_ref, hnw_ref, who_ref, wo_ref, wg_ref, wu_ref,
                  o_ref, wgu16_ref, state_ref, ubuf_ref):
    ts, d = x_ref.shape
    step = pl.program_id(0)
    _cast_expert_weights(wg_ref, wu_ref, wgu16_ref)

    @pl.when(step == 0)
    def _():
        state_ref[...] = jnp.zeros_like(state_ref)
        ubuf_ref[0:8, :] = jnp.zeros((8, CONV_DIM), F32)

    x = x_ref[...]
    sh1 = mod_ref[:, 0:d]
    sc1 = mod_ref[:, d:2 * d]
    g1 = mod_ref[:, 2 * d:3 * d]
    h = (_rms(x, npre_ref[...]) * (1.0 + sc1) + sh1).astype(BF16)

    def proj(lo, width):
        return _dot(h, win_ref[:, lo:lo + width])

    c0 = 0
    cb = proj(c0, CONV_DIM)
    u = proj(c0 + CONV_DIM, CONV_DIM) * proj(c0 + 2 * CONV_DIM, CONV_DIM)
    ubuf_ref[8:8 + ts, :] = u
    conv = (ubuf_ref[6:6 + ts, :] * convw_ref[0:1, :]
            + ubuf_ref[7:7 + ts, :] * convw_ref[1:2, :]
            + u * convw_ref[2:3, :])
    ubuf_ref[0:8, :] = ubuf_ref[ts:ts + 8, :]
    y_a = _dot((cb * conv).astype(BF16), wco_ref[...])

    c1 = 3 * CONV_DIM
    q = _silu(proj(c1, HGRN_QK))
    hl = hlb_ref[...]
    hm = jnp.max(hl, axis=0, keepdims=True)
    he = jnp.exp(hl - hm)
    lb = he[0:1, :] / jnp.sum(he, axis=0, keepdims=True)
    fg = lb + (1.0 - lb) * _sigmoid(proj(c1 + HGRN_QK, HGRN_QK))
    k = 1.0 - fg
    g = jnp.log(fg)
    v = proj(c1 + 2 * HGRN_QK, HGRN_QK)
    gg = proj(c1 + 3 * HGRN_QK, HGRN_QK)

    row = lax.broadcasted_iota(jnp.int32, (ts, ts), 0)
    col = lax.broadcasted_iota(jnp.int32, (ts, ts), 1)
    tril = jnp.where(row >= col, 1.0, 0.0).astype(BF16)
    ghi, gmid, glo = _split3(g)
    b = _dot(tril, ghi) + _dot(tril, gmid) + _dot(tril, glo)
    b_last = b[ts - 1:ts, :]

    rowq = lax.broadcasted_iota(jnp.int32, (ts, HGRN_QK), 0)
    rolled = {sft: pltpu.roll(b, sft % ts, axis=0) for sft in (-3, -2, -1, 1, 2, 3, 4)}
    xor = row ^ col

    levels = []
    s = ts
    while s >= 2:
        levels.append(s)
        s //= 2
    qt, kt = [], []
    for s in levels:
        bref = _level_reference(b, rolled, s, rowq)
        e = jnp.exp(-jnp.abs(b - bref))
        upper = (rowq & (s // 2)) != 0
        qt.append(jnp.where(upper, q * e, 0.0).astype(BF16))
        kt.append(jnp.where(upper, 0.0, k * e).astype(BF16))
    q_in = (q * jnp.exp(b)).astype(BF16)
    k_out = (k * jnp.exp(b_last - b)).astype(BF16)
    v16 = v.astype(BF16)
    qk = q * k
    decay_last = jnp.exp(b_last)

    hnw = hnw_ref[...]
    outs = []
    for hd in range(HGRN_HEADS):
        sl = slice(hd * HGRN_DK, (hd + 1) * HGRN_DK)
        a = jnp.zeros((ts, ts), F32)
        for li in range(len(levels) - 1, -1, -1):
            s = levels[li]
            p = _dot_nt(qt[li][:, sl], kt[li][:, sl])
            a = jnp.where(xor >= s // 2, p, a)
        st = state_ref[hd]
        o_h = (_dot(a.astype(BF16), v16[:, sl])
               + jnp.sum(qk[:, sl], axis=-1, keepdims=True) * v[:, sl]
               + _dot_nt(q_in[:, sl], st.astype(BF16)))
        state_ref[hd] = st * decay_last[:, sl] + _dot_tn(v16[:, sl], k_out[:, sl])
        outs.append(_rms(o_h, hnw))
    o = jnp.concatenate(outs, axis=-1) * _silu(gg)
    y_b = _dot(o.astype(BF16), who_ref[...])

    c2 = c1 + 4 * HGRN_QK
    m = _sigmoid(proj(c2, d)) * y_a + _sigmoid(proj(c2 + d, d)) * y_b
    y = _dot(m.astype(BF16), wo_ref[...])
    o_ref[...] = x + g1 * _rms(y, npost_ref[...])


def _const_spec(shape):
    nd = len(shape)
    return pl.BlockSpec(shape, lambda i: (0,) * nd)


def _mixer(x2, mod, npre, npost, w_in, conv_w, w_conv_out, hlb, hnw, w_hgrn_out, w_o, w_gate_e, w_up_e):
    t, d = x2.shape
    ts = SEQ_TILE
    n = t // ts
    n_e, _, ff = w_gate_e.shape
    assert n_e % n == 0 and conv_w.shape == (3, CONV_DIM)
    per = n_e // n
    args = (x2, mod, npre.reshape(1, d), npost.reshape(1, d), w_in.astype(BF16), conv_w,
            w_conv_out.astype(BF16), hlb, hnw.reshape(1, -1), w_hgrn_out.astype(BF16),
            w_o.astype(BF16))
    experts = lambda i: (i, 0, 0)
    in_specs = ([pl.BlockSpec((ts, d), lambda i: (i, 0))] + [_const_spec(a.shape) for a in args[1:]]
                + [pl.BlockSpec((per, d, ff), experts)] * 2)
    return pl.pallas_call(
        _mixer_kernel,
        out_shape=[jax.ShapeDtypeStruct((t, d), F32), jax.ShapeDtypeStruct((n_e, d, 2 * ff), BF16)],
        grid=(n,),
        in_specs=in_specs,
        out_specs=[pl.BlockSpec((ts, d), lambda i: (i, 0)), pl.BlockSpec((per, d, 2 * ff), experts)],
        scratch_shapes=[pltpu.VMEM((HGRN_HEADS, HGRN_DK, HGRN_DK), F32),
                        pltpu.VMEM((ts + 8, CONV_DIM), F32)],
        compiler_params=pltpu.CompilerParams(dimension_semantics=("arbitrary",),
                                             vmem_limit_bytes=VMEM_LIMIT_BYTES),
        name="token_mixer",
    )(*args, w_gate_e, w_up_e)


PACKED = jnp.int32


def _pack_bf16_pairs(x):
    m = x.shape[1] // 2
    hi = lax.bitcast_convert_type(x[:, :m].astype(BF16).astype(F32), jnp.uint32)
    lo = lax.bitcast_convert_type(x[:, m:].astype(BF16).astype(F32), jnp.uint32)
    return lax.bitcast_convert_type(hi | (lo >> 16), PACKED)


def _unpack_bf16_pairs(p):
    p = lax.bitcast_convert_type(p, jnp.uint32)
    hi = lax.bitcast_convert_type(p & jnp.uint32(0xFFFF0000), F32)
    lo = lax.bitcast_convert_type(p << 16, F32)
    return hi, lo


def _router_kernel(x1_ref, mod_ref, npre_ref, wrh_ref, wrl_ref, rb_ref,
                   hp_ref, code_ref, wcol_ref, cnt_ref, carry_ref, rows_ref):
    assert N_GROUPS == TOP_K == rows_ref.shape[0]
    tr, d = x1_ref.shape
    n_e = wrh_ref.shape[0]
    gsz = n_e // N_GROUPS
    neg = -jnp.inf

    @pl.when(pl.program_id(0) == 0)
    def _():
        carry_ref[...] = jnp.zeros_like(carry_ref)

    sh2 = mod_ref[:, 3 * d:4 * d]
    sc2 = mod_ref[:, 4 * d:5 * d]
    h2 = _rms(x1_ref[...], npre_ref[...]) * (1.0 + sc2) + sh2
    hp_ref[...] = _pack_bf16_pairs(h2)
    h_hi = h2.astype(BF16)
    h_lo = (h2 - h_hi.astype(F32)).astype(BF16)
    wrh = wrh_ref[...]
    logits = _dot_nt(wrh, h_hi) + (_dot_nt(wrh, h_lo) + _dot_nt(wrl_ref[...], h_hi))
    scores = _sigmoid(logits)
    sel = scores + rb_ref[:, 0:1]

    io_e = lax.broadcasted_iota(jnp.int32, (n_e, tr), 0)
    for g in range(N_GROUPS):
        blk = sel[g * gsz:(g + 1) * gsz, :]
        io = lax.broadcasted_iota(jnp.int32, (gsz, tr), 0) + g * gsz
        m1 = jnp.max(blk, axis=0, keepdims=True)
        i1 = jnp.min(jnp.where(blk == m1, io, n_e), axis=0, keepdims=True)
        m2 = jnp.max(jnp.where(io == i1, neg, blk), axis=0, keepdims=True)
        rows_ref[g:g + 1, :] = m1 + m2
    gs = rows_ref[...]
    io_g = lax.broadcasted_iota(jnp.int32, (N_GROUPS, tr), 0)
    gsel = jnp.zeros((N_GROUPS, tr), F32)
    for _ in range(TOPK_GROUPS):
        m = jnp.max(gs, axis=0, keepdims=True)
        gi = jnp.min(jnp.where(gs == m, io_g, N_GROUPS), axis=0, keepdims=True)
        hit = io_g == gi
        gsel = jnp.where(hit, 1.0, gsel)
        gs = jnp.where(hit, neg, gs)
    rows_ref[...] = gsel
    cur = jnp.concatenate(
        [jnp.where(rows_ref[g:g + 1, :] > 0.5, sel[g * gsz:(g + 1) * gsz, :], neg) for g in range(N_GROUPS)],
        axis=0)

    idxs = []
    selm = jnp.zeros((n_e, tr), F32)
    for k in range(TOP_K):
        m = jnp.max(cur, axis=0, keepdims=True)
        idx = jnp.min(jnp.where(cur == m, io_e, n_e), axis=0, keepdims=True)
        hit = io_e == idx
        rows_ref[k:k + 1, :] = jnp.sum(jnp.where(hit, scores, 0.0), axis=0, keepdims=True)
        cur = jnp.where(hit, neg, cur)
        selm = jnp.where(hit, 1.0, selm)
        idxs.append(idx)

    r_io = lax.broadcasted_iota(jnp.int32, (tr, tr + 128), 0)
    c_io = lax.broadcasted_iota(jnp.int32, (tr, tr + 128), 1)
    before = jnp.where((r_io < c_io) | (c_io >= tr), 1.0, 0.0).astype(BF16)
    r_ext = _dot(selm.astype(BF16), before)
    rank_full = r_ext[:, :tr] + carry_ref[:, 0:1]
    for k, idx in enumerate(idxs):
        rk = jnp.sum(jnp.where(io_e == idx, rank_full, 0.0), axis=0, keepdims=True)
        code_ref[k:k + 1, :] = (idx << CODE_SHIFT) | rk.astype(jnp.int32)
    carry_ref[...] = carry_ref[...] + r_ext[:, tr:]
    cnt_ref[...] = carry_ref[...]

    wk = rows_ref[...]
    wn = wk / (jnp.sum(wk, axis=0, keepdims=True) + 1e-20) * ROUTED_SCALE
    eye = jnp.where(lax.broadcasted_iota(jnp.int32, (TOP_K, 128), 0)
                    == lax.broadcasted_iota(jnp.int32, (TOP_K, 128), 1), 1.0, 0.0).astype(BF16)
    w1, w2, w3 = _split3(wn)
    wcol_ref[...] = _dot_tn(w1, eye) + _dot_tn(w2, eye) + _dot_tn(w3, eye)


def _router(x1, mod, npre2, w_router, router_bias):
    t, d = x1.shape
    n_e = w_router.shape[1]
    assert t <= (1 << CODE_SHIFT)
    tr = SEQ_TILE
    wrt = w_router.T
    wrh = wrt.astype(BF16)
    wrl = (wrt - wrh.astype(F32)).astype(BF16)
    rb = jnp.broadcast_to(router_bias.reshape(n_e, 1), (n_e, 128))
    args = (x1, mod, npre2.reshape(1, d), wrh, wrl, rb)
    in_specs = [pl.BlockSpec((tr, d), lambda i: (i, 0))] + [_const_spec(a.shape) for a in args[1:]]
    return pl.pallas_call(
        _router_kernel,
        out_shape=(jax.ShapeDtypeStruct((t, d // 2), PACKED),
                   jax.ShapeDtypeStruct((TOP_K, t), jnp.int32),
                   jax.ShapeDtypeStruct((t, 128), F32),
                   jax.ShapeDtypeStruct((n_e, 128), F32)),
        grid=(t // tr,),
        in_specs=in_specs,
        out_specs=(pl.BlockSpec((tr, d // 2), lambda i: (i, 0)),
                   pl.BlockSpec((TOP_K, tr), lambda i: (0, i)),
                   pl.BlockSpec((tr, 128), lambda i: (i, 0)),
                   pl.BlockSpec((n_e, 128), lambda i: (0, 0))),
        scratch_shapes=[pltpu.VMEM((n_e, 128), F32), pltpu.VMEM((TOP_K, tr), F32)],
        compiler_params=pltpu.CompilerParams(dimension_semantics=("arbitrary",),
                                             vmem_limit_bytes=VMEM_LIMIT_BYTES),
        name="moe_router",
    )(*args)


TAB_CHUNK_ROW, TAB_CHUNK_VALID, TAB_START, TAB_COUNT, TAB_CHUNK0, TAB_NCHUNK, TAB_TOTAL = range(7)
TAB_ROWS = 8


def _excl_cumsum_rows(lower, x):
    hi = jnp.floor(x * (1.0 / 128.0))
    lo = x - hi * 128.0
    return 128.0 * _dot(lower, hi.astype(BF16)) + _dot(lower, lo.astype(BF16))


def _table_kernel(cnt_ref, tab_ref, pad_ref, *, n_rows):
    n_e = cnt_ref.shape[0]
    n_g = tab_ref.shape[1]
    ch = float(EXPERT_CHUNK)
    cnt = jnp.floor((cnt_ref[...] + (ROW_ALIGN - 1)) * (1.0 / ROW_ALIGN)) * ROW_ALIGN
    r = lax.broadcasted_iota(jnp.int32, (n_e, n_e), 0)
    c = lax.broadcasted_iota(jnp.int32, (n_e, n_e), 1)
    lower = jnp.where(c < r, 1.0, 0.0).astype(BF16)
    starts = _excl_cumsum_rows(lower, cnt)
    start_col = starts[:, 0:1]

    @pl.when(pl.program_id(0) == 0)
    def _():
        nch = jnp.floor((cnt + (ch - 1.0)) * (1.0 / ch))
        chunk0 = _excl_cumsum_rows(lower, nch)
        cend_col = (chunk0 + nch)[:, 0:1]
        eye = r == c

        def as_row(col):
            return jnp.sum(jnp.where(eye, col, 0.0), axis=0, keepdims=True)

        g = lax.broadcasted_iota(jnp.int32, (n_e, n_g), 1).astype(F32)
        owner = jnp.sum(jnp.where(cend_col <= g, 1.0, 0.0), axis=0, keepdims=True)
        mine = lax.broadcasted_iota(jnp.int32, (n_e, n_g), 0).astype(F32) == owner
        g_row = lax.broadcasted_iota(jnp.int32, (1, n_g), 1).astype(F32)
        base = jnp.sum(jnp.where(mine, start_col - chunk0[:, 0:1] * ch, 0.0), axis=0, keepdims=True)
        left = jnp.sum(jnp.where(mine, cnt[:, 0:1] + chunk0[:, 0:1] * ch, 0.0), axis=0, keepdims=True)
        pad = jnp.zeros((1, n_g - n_e), F32)

        def wide(row):
            return jnp.concatenate([row, pad], axis=1)

        total = jnp.sum(nch[:, 0:1], axis=0, keepdims=True)
        tab_ref[...] = jnp.zeros(tab_ref.shape, jnp.int32)
        tab_ref[TAB_CHUNK_ROW:TAB_CHUNK_ROW + 1, :] = (base + g_row * ch).astype(jnp.int32)
        tab_ref[TAB_CHUNK_VALID:TAB_CHUNK_VALID + 1, :] = jnp.clip(left - g_row * ch, 0.0, ch).astype(jnp.int32)
        tab_ref[TAB_START:TAB_START + 1, :] = wide(as_row(start_col)).astype(jnp.int32)
        tab_ref[TAB_COUNT:TAB_COUNT + 1, :] = wide(as_row(cnt_ref[:, 0:1])).astype(jnp.int32)
        tab_ref[TAB_CHUNK0:TAB_CHUNK0 + 1, :] = wide(as_row(chunk0[:, 0:1])).astype(jnp.int32)
        tab_ref[TAB_NCHUNK:TAB_NCHUNK + 1, :] = wide(as_row(nch[:, 0:1])).astype(jnp.int32)
        tab_ref[TAB_TOTAL:TAB_TOTAL + 1, :] = jnp.broadcast_to(total, (1, n_g)).astype(jnp.int32)

        cols = pad_ref.shape[1]
        tail_cols = cols - n_e
        j_io = lax.broadcasted_iota(jnp.int32, (ROW_ALIGN, cols), 0).astype(F32)
        c_io = lax.broadcasted_iota(jnp.int32, (ROW_ALIGN, cols), 1).astype(F32)
        spare = float(n_rows) + j_io * cols + c_io
        zeros_tail = jnp.zeros((1, tail_cols), F32)
        raw_col = cnt_ref[:, 0:1]
        end_w = jnp.concatenate([as_row(start_col + raw_col), zeros_tail], axis=1)
        npad_w = jnp.concatenate([as_row(cnt[:, 0:1] - raw_col), zeros_tail], axis=1)
        used = jnp.sum(cnt[:, 0:1], axis=0, keepdims=True)
        after = used + j_io * tail_cols + (c_io - n_e)
        target = jnp.where(c_io < n_e,
                           jnp.where(j_io < npad_w, end_w + j_io, spare),
                           jnp.where(after < float(n_rows), after, spare))
        pad_ref[...] = target.astype(jnp.int32)


def _sorted_rows(t, n_e):
    return t * TOP_K + n_e * ROW_ALIGN + EXPERT_CHUNK + ROW_ALIGN


def _pad_list_cols(t, n_e):
    entries = n_e * ROW_ALIGN + (_sorted_rows(t, n_e) - t * TOP_K)
    per_round = SC_WORKERS * SC_GATHER_ROWS
    return pl.cdiv(entries, per_round) * per_round // ROW_ALIGN


def _route_table(cnt, t):
    n_e = cnt.shape[0]
    n_g = (t * TOP_K) // EXPERT_CHUNK + n_e
    cols = _pad_list_cols(t, n_e)
    return pl.pallas_call(
        functools.partial(_table_kernel, n_rows=_sorted_rows(t, n_e)),
        out_shape=(jax.ShapeDtypeStruct((TAB_ROWS, n_g), jnp.int32),
                   jax.ShapeDtypeStruct((ROW_ALIGN, cols), jnp.int32)),
        grid=(1,),
        in_specs=[_const_spec(cnt.shape)],
        out_specs=(_const_spec((TAB_ROWS, n_g)), _const_spec((ROW_ALIGN, cols))),
        compiler_params=pltpu.CompilerParams(dimension_semantics=("arbitrary",)),
        name="moe_table",
    )(cnt)


SC_CORES = 2
SC_SUBCORES = 16
SC_WORKERS = SC_CORES * SC_SUBCORES
SC_GATHER_ROWS = 64
COMBINE_SPLITS = 4


def _sc_mesh():
    return plsc.VectorSubcoreMesh(core_axis_name="c", subcore_axis_name="s")


def _sc_worker_id():
    return lax.axis_index("s") * SC_CORES + lax.axis_index("c")


SC_LANES = 16


def _sc_decode_rows(code_v, starts_v, idx_v):
    for i in range(code_v.shape[0] // SC_LANES):
        lanes = pl.ds(i * SC_LANES, SC_LANES)
        code = code_v[lanes]
        expert = lax.shift_right_logical(code, CODE_SHIFT)
        idx_v[lanes] = plsc.load_gather(starts_v, [expert]) + (code & ((1 << CODE_SHIFT) - 1))


def _sc_params():
    return pltpu.CompilerParams(needs_layout_passes=False)


def _sc_scatter_rows(rows, code_flat, starts, zero_list, n_out):
    t, w = rows.shape
    n_k = code_flat.shape[0] // t
    n_e = starts.shape[0]
    r = SC_GATHER_ROWS
    per = t // SC_WORKERS
    n_steps = per // r
    zero_rounds = zero_list.shape[0] // (SC_WORKERS * r)
    assert per % (2 * r) == 0 and zero_list.shape[0] % (SC_WORKERS * r) == 0 and zero_rounds <= n_k
    zeros = jnp.zeros((r, w), rows.dtype)

    @functools.partial(
        pl.kernel, mesh=_sc_mesh(), out_type=jax.ShapeDtypeStruct((n_out, w), rows.dtype),
        scratch_types=[pltpu.VMEM((n_e,), jnp.int32), pltpu.VMEM((n_k, r), jnp.int32),
                       pltpu.VMEM((n_k, r), jnp.int32), pltpu.VMEM((2, r, w), rows.dtype),
                       pltpu.SemaphoreType.DMA((2,)), pltpu.SemaphoreType.DMA],
        compiler_params=_sc_params(), name="moe_dispatch_sc")
    def scatter(rows_hbm, code_hbm, starts_hbm, zero_idx_hbm, zeros_hbm, out_hbm,
                starts_v, code_v, idx_v, rows_v, l_sem, s_sem):
        worker = _sc_worker_id()
        base = worker * per
        pltpu.sync_copy(starts_hbm, starts_v)

        def load(j, b):
            return pltpu.make_async_copy(rows_hbm.at[pl.ds(pl.multiple_of(base + j * r, ROW_ALIGN), r)],
                                         rows_v.at[b], l_sem.at[b])

        load(0, 0).start()
        pltpu.sync_copy(zeros_hbm, rows_v.at[1])
        zero_copies = []
        for z in range(zero_rounds):
            off = pl.multiple_of((worker * zero_rounds + z) * r, ROW_ALIGN)
            pltpu.sync_copy(zero_idx_hbm.at[pl.ds(off, r)], idx_v.at[z])
            zero_copies.append(pltpu.make_async_copy(rows_v.at[1], out_hbm.at[idx_v.at[z]], s_sem))
            zero_copies[-1].start()
        for cp in zero_copies:
            cp.wait()

        @pl.loop(0, n_steps, step=2)
        def _(j0):
            for b in range(2):
                j = j0 + b
                pl.when(j + 1 < n_steps)(load(j + 1, 1 - b).start)
                for k in range(n_k):
                    off = pl.multiple_of(k * t + base + j * r, ROW_ALIGN)
                    pltpu.sync_copy(code_hbm.at[pl.ds(off, r)], code_v.at[k])
                for k in range(n_k):
                    _sc_decode_rows(code_v.at[k], starts_v, idx_v.at[k])
                load(j, b).wait()
                copies = [pltpu.make_async_copy(rows_v.at[b], out_hbm.at[idx_v.at[k]], s_sem) for k in range(n_k)]
                for cp in copies:
                    cp.start()
                for cp in copies:
                    cp.wait()

    return scatter(rows, code_flat, starts, zero_list, zeros)


def _sc_gather_rows(table, code_flat, starts):
    n = code_flat.shape[0]
    w = table.shape[1]
    n_e = starts.shape[0]
    r = SC_GATHER_ROWS
    per = n // SC_WORKERS
    n_steps = per // r
    assert per % (2 * r) == 0

    @functools.partial(
        pl.kernel, mesh=_sc_mesh(), out_type=jax.ShapeDtypeStruct((n, w), table.dtype),
        scratch_types=[pltpu.VMEM((n_e,), jnp.int32), pltpu.VMEM((r,), jnp.int32), pltpu.VMEM((2, r), jnp.int32),
                       pltpu.VMEM((2, r, w), table.dtype),
                       pltpu.SemaphoreType.DMA((2,)), pltpu.SemaphoreType.DMA((2,))],
        compiler_params=_sc_params(), name="moe_gather_sc")
    def gather(table_hbm, code_hbm, starts_hbm, out_hbm, starts_v, code_v, idx_v, rows_v, g_sem, w_sem):
        base = _sc_worker_id() * per
        pltpu.sync_copy(starts_hbm, starts_v)

        def at(j):
            return pl.ds(pl.multiple_of(base + j * r, ROW_ALIGN), r)

        def fetch(j, b):
            return pltpu.make_async_copy(table_hbm.at[idx_v.at[b]], rows_v.at[b], g_sem.at[b])

        def start_fetch(j, b):
            pltpu.sync_copy(code_hbm.at[at(j)], code_v)
            _sc_decode_rows(code_v, starts_v, idx_v.at[b])
            fetch(j, b).start()

        def write(j, b):
            return pltpu.make_async_copy(rows_v.at[b], out_hbm.at[at(j)], w_sem.at[b])

        start_fetch(0, 0)

        @pl.loop(0, n_steps, step=2)
        def _(j0):
            for b in range(2):
                j = j0 + b

                @pl.when(j + 1 < n_steps)
                def _():
                    pl.when(j >= 1)(write(j - 1, 1 - b).wait)
                    start_fetch(j + 1, 1 - b)

                fetch(j, b).wait()
                write(j, b).start()

        write(n_steps - 2, 0).wait()
        write(n_steps - 1, 1).wait()

    return gather(table, code_flat, starts)


def _dispatch(code, starts, pad_list, hp):
    t = hp.shape[0]
    n_out = _sorted_rows(t, starts.shape[0]) + pad_list.size
    return _sc_scatter_rows(hp, code.reshape(-1), starts, pad_list.reshape(-1), n_out)


X_LOOKAHEAD = 2
X_SLOTS = X_LOOKAHEAD + 1
Y_SLOTS = 2


def _expert_kernel(tab_ref, xs_hbm, wgu_ref, wd_ref, y_hbm,
                   wd16_ref, xbuf_ref, ybuf_ref, in_sem, out_sem, *, n_e):
    e = pl.program_id(0)
    ch, half = xbuf_ref.shape[1:]
    ff = wd_ref.shape[1]
    n_total = tab_ref[TAB_TOTAL, 0]
    first = tab_ref[TAB_CHUNK0, e]
    n_chunks = tab_ref[TAB_NCHUNK, e]

    def in_copy(g):
        slot = g % X_SLOTS
        row = pl.multiple_of(tab_ref[TAB_CHUNK_ROW, g], ROW_ALIGN)
        return pltpu.make_async_copy(xs_hbm.at[pl.ds(row, ch), :], xbuf_ref.at[slot], in_sem.at[slot])

    def out_copies(g):
        slot = g % Y_SLOTS
        row = tab_ref[TAB_CHUNK_ROW, g]
        valid = tab_ref[TAB_CHUNK_VALID, g]
        yield valid >= ch, pltpu.make_async_copy(ybuf_ref.at[slot], y_hbm.at[pl.ds(pl.multiple_of(row, ROW_ALIGN), ch), :],
                                                 out_sem.at[slot])
        size = ch // 2
        while size >= ROW_ALIGN:
            off = pl.multiple_of((valid // (2 * size)) * (2 * size), ROW_ALIGN)
            cond = (valid < ch) & ((valid & size) != 0)
            yield cond, pltpu.make_async_copy(ybuf_ref.at[slot, pl.ds(off, size), :],
                                              y_hbm.at[pl.ds(pl.multiple_of(row + off, ROW_ALIGN), size), :],
                                              out_sem.at[slot])
            size //= 2

    def start_out(g):
        for cond, cp in out_copies(g):
            pl.when(cond)(cp.start)

    def wait_out(g):
        for cond, cp in out_copies(g):
            pl.when(cond)(cp.wait)

    @pl.when(e == 0)
    def _():
        for j in range(X_LOOKAHEAD):
            pl.when(j < n_total)(in_copy(j).start)

    @pl.when(n_chunks > 0)
    def _():
        wd16_ref[...] = wd_ref[0].astype(BF16)

    def chunk(c, carry):
        g = first + c
        pl.when(g + X_LOOKAHEAD < n_total)(in_copy(g + X_LOOKAHEAD).start)
        in_copy(g).wait()
        pl.when(g >= Y_SLOTS)(lambda: wait_out(g - Y_SLOTS))
        valid = tab_ref[TAB_CHUNK_VALID, g]

        def ffn(rows):
            xa, xb = _unpack_bf16_pairs(xbuf_ref[g % X_SLOTS, 0:rows, :])
            hgu = (_dot(xa.astype(BF16), wgu_ref[0, 0:half, :])
                   + _dot(xb.astype(BF16), wgu_ref[0, half:2 * half, :]))
            act = _silu(hgu[:, :ff]) * hgu[:, ff:]
            y = _dot(act.astype(BF16), wd16_ref[...])
            ybuf_ref[g % Y_SLOTS, 0:rows, :] = _pack_bf16_pairs(y)

        lo = 0
        for rows in EXPERT_ROW_STEPS:
            pl.when((valid > lo) & (valid <= rows))(functools.partial(ffn, rows))
            lo = rows
        start_out(g)
        return carry

    lax.fori_loop(0, n_chunks, chunk, 0)

    @pl.when(e == n_e - 1)
    def _():
        for j in range(Y_SLOTS, 0, -1):
            pl.when(n_total >= j)(lambda j=j: wait_out(n_total - j))
        n_rows = y_hbm.shape[0]
        used = tab_ref[TAB_START, e] + ((tab_ref[TAB_COUNT, e] + (ROW_ALIGN - 1)) // ROW_ALIGN) * ROW_ALIGN
        ybuf_ref[0] = jnp.zeros((ch, half), PACKED)
        for j in range(pl.cdiv(n_e * ROW_ALIGN + EXPERT_CHUNK + ROW_ALIGN, ch)):
            at = pl.multiple_of(jnp.minimum(used + j * ch, n_rows - ch), ROW_ALIGN)
            tail = pltpu.make_async_copy(ybuf_ref.at[0], y_hbm.at[pl.ds(at, ch), :], out_sem.at[0])
            tail.start()
            tail.wait()


def _experts(tab, xs, wgu16, w_down_e, p):
    half = xs.shape[1]
    n_e, ff, d = w_down_e.shape
    ch = EXPERT_CHUNK
    grid_spec = pltpu.PrefetchScalarGridSpec(
        num_scalar_prefetch=1,
        grid=(n_e,),
        in_specs=[pl.BlockSpec(memory_space=pl.ANY),
                  pl.BlockSpec((1, d, 2 * ff), lambda e, *_: (e, 0, 0)),
                  pl.BlockSpec((1, ff, d), lambda e, *_: (e, 0, 0))],
        out_specs=pl.BlockSpec(memory_space=pl.ANY),
        scratch_shapes=[pltpu.VMEM((ff, d), BF16),
                        pltpu.VMEM((X_SLOTS, ch, half), PACKED), pltpu.VMEM((Y_SLOTS, ch, half), PACKED),
                        pltpu.SemaphoreType.DMA((X_SLOTS,)), pltpu.SemaphoreType.DMA((Y_SLOTS,))],
    )
    return pl.pallas_call(
        functools.partial(_expert_kernel, n_e=n_e),
        out_shape=jax.ShapeDtypeStruct((p, half), PACKED),
        grid_spec=grid_spec,
        compiler_params=pltpu.CompilerParams(dimension_semantics=("arbitrary",),
                                             vmem_limit_bytes=VMEM_LIMIT_BYTES),
        name="moe_experts",
    )(tab, xs, wgu16, w_down_e)


def _combine_kernel(yg_ref, wcol_ref, x1_ref, mod_ref, npre_ref, npost_ref, wgus_ref, wds_ref, o_ref):
    te, d = x1_ref.shape
    half = d // 2
    ff = wds_ref.shape[0]

    x1 = x1_ref[...]
    h2 = _rms(x1, npre_ref[...]) * (1.0 + mod_ref[:, 4 * d:5 * d]) + mod_ref[:, 3 * d:4 * d]
    hgu = _dot(h2.astype(BF16), wgus_ref[...])
    act = _silu(hgu[:, :ff]) * hgu[:, ff:]
    shared = _dot(act.astype(BF16), wds_ref[...])

    acc_a = shared[:, :half]
    acc_b = shared[:, half:]
    for k in range(TOP_K):
        ya, yb = _unpack_bf16_pairs(yg_ref[k])
        wk = wcol_ref[:, k:k + 1]
        acc_a = acc_a + wk * ya
        acc_b = acc_b + wk * yb
    moe = jnp.concatenate([acc_a, acc_b], axis=-1)
    g2 = mod_ref[:, 5 * d:6 * d]
    o_ref[...] = x1 + g2 * _rms(moe, npost_ref[...])


def _combine(code, starts, y, wcol, x1, mod, npre2, npost2, w_gate_s, w_up_s, w_down_s):
    t, d = x1.shape
    half = d // 2
    te = SEQ_TILE
    wgus = jnp.concatenate([w_gate_s, w_up_s], axis=1).astype(BF16)
    wds = w_down_s.astype(BF16)
    n_split = COMBINE_SPLITS if t % (COMBINE_SPLITS * 2 * SC_GATHER_ROWS * SC_WORKERS // TOP_K) == 0 else 1
    tq = t // n_split
    steps = tq // te
    stream = x1
    for q in range(n_split):
        code_q = code[:, q * tq:(q + 1) * tq]
        yg = _sc_gather_rows(y, code_q.reshape(-1), starts).reshape(TOP_K, tq, half)
        row = lambda i, q=q: (q * steps + i, 0)
        stream = pl.pallas_call(
            _combine_kernel,
            out_shape=jax.ShapeDtypeStruct((t, d), F32),
            grid=(steps,),
            in_specs=[pl.BlockSpec((TOP_K, te, half), lambda i: (0, i, 0)),
                      pl.BlockSpec((te, 128), row),
                      pl.BlockSpec((te, d), row),
                      _const_spec(mod.shape),
                      _const_spec((1, d)),
                      _const_spec((1, d)),
                      _const_spec(wgus.shape),
                      _const_spec(wds.shape)],
            out_specs=pl.BlockSpec((te, d), row),
            input_output_aliases={2: 0},
            compiler_params=pltpu.CompilerParams(dimension_semantics=("arbitrary",),
                                                 vmem_limit_bytes=VMEM_LIMIT_BYTES),
            name="moe_combine",
        )(yg, wcol, stream, mod, npre2.reshape(1, d), npost2.reshape(1, d), wgus, wds)
    return stream


def kernel(x, c, w_ada, b_ada, norm_pre_mix, norm_post_mix, w_in, conv_w, w_conv_out, hgrn_lower_bounds, hgrn_norm_w, w_hgrn_out, w_o, norm_pre_ffn, norm_post_ffn, w_router, router_bias, w_gate_e, w_up_e, w_down_e, w_gate_s, w_up_s, w_down_s):
    bsz, seq, d = x.shape
    assert bsz == 1 and w_ada.shape[0] == 1
    mod = _ada_mod(c, w_ada[0], b_ada[0])
    x1, wgu16 = _mixer(x.reshape(seq, d), mod, norm_pre_mix[0], norm_post_mix[0], w_in[0], conv_w[0],
                       w_conv_out[0], hgrn_lower_bounds, hgrn_norm_w[0], w_hgrn_out[0], w_o[0],
                       w_gate_e[0], w_up_e[0])
    hp, code, wcol, cnt = _router(x1, mod, norm_pre_ffn[0], w_router[0], router_bias[0])
    n_e = w_router.shape[-1]
    tab, pad_list = _route_table(cnt, seq)
    starts = tab[TAB_START, :n_e]
    xs = _dispatch(code, starts, pad_list, hp)
    y = _experts(tab, xs, wgu16, w_down_e[0], _sorted_rows(seq, n_e))
    out = _combine(code, starts, y, wcol, x1, mod, norm_pre_ffn[0], norm_post_ffn[0],
                   w_gate_s[0], w_up_s[0], w_down_s[0])
    return out.reshape(bsz, seq, d)
```

```python
import functools

import jax
import jax.numpy as jnp
from jax import lax
from jax.experimental import pallas as pl
from jax.experimental.pallas import tpu as pltpu
from jax.experimental.pallas import tpu_sc as plsc

F32 = jnp.float32
BF16 = jnp.bfloat16

NORM_EPS = 1e-6
CONV_DIM = 512
HGRN_HEADS = 4
HGRN_DK = 128
HGRN_QK = HGRN_HEADS * HGRN_DK
N_GROUPS = 8
TOPK_GROUPS = 4
TOP_K = 8
ROUTED_SCALE = 2.5
CODE_SHIFT = 16

SEQ_TILE = 256
EXPERT_CHUNK = 512
EXPERT_ROW_STEPS = (128, 256, EXPERT_CHUNK)
ROW_ALIGN = 8
VMEM_LIMIT_BYTES = 56 * 1024 * 1024


def _dot(a, b):
    return jnp.dot(a, b, preferred_element_type=F32)


def _dot_nt(a, b):
    return lax.dot_general(a, b, (((1,), (1,)), ((), ())), preferred_element_type=F32)


def _dot_tn(a, b):
    return lax.dot_general(a, b, (((0,), (0,)), ((), ())), preferred_element_type=F32)


def _split3(x):
    hi = x.astype(BF16)
    r1 = x - hi.astype(F32)
    mid = r1.astype(BF16)
    lo = (r1 - mid.astype(F32)).astype(BF16)
    return hi, mid, lo


def _sigmoid(x):
    return 1.0 / (1.0 + jnp.exp(-x))


def _silu(x):
    return x * _sigmoid(x)


def _rms(x, w):
    ms = jnp.mean(x * x, axis=-1, keepdims=True)
    return x * lax.rsqrt(ms + NORM_EPS) * w


def _ada_kernel(c_ref, w_ref, b_ref, o_ref):
    cs = _silu(c_ref[...])
    h1, h2, h3 = _split3(cs)
    w1, w2, w3 = _split3(w_ref[...])
    r = cs.shape[0]
    stacked = jnp.concatenate([h.astype(F32) for h in (h1, h2, h3)], axis=0).astype(BF16)
    p1 = _dot(stacked, w1)
    p2 = _dot(stacked[0:2 * r], w2)
    p3 = _dot(h1, w3)
    acc = p1[0:r] + (p2[0:r] + p1[r:2 * r]) + (p3 + p2[r:2 * r] + p1[2 * r:3 * r])
    o_ref[...] = acc + b_ref[...]


def _ada_mod(c, w_ada, b_ada):
    d = c.shape[-1]
    n = w_ada.shape[-1]
    bn = 1024
    c8 = jnp.broadcast_to(c.reshape(1, d), (8, d))
    out = pl.pallas_call(
        _ada_kernel,
        out_shape=jax.ShapeDtypeStruct((8, n), F32),
        grid=(n // bn,),
        in_specs=[pl.BlockSpec((8, d), lambda j: (0, 0)),
                  pl.BlockSpec((d, bn), lambda j: (0, j)),
                  pl.BlockSpec((1, bn), lambda j: (0, j))],
        out_specs=pl.BlockSpec((8, bn), lambda j: (0, j)),
        compiler_params=pltpu.CompilerParams(dimension_semantics=("arbitrary",),
                                             vmem_limit_bytes=VMEM_LIMIT_BYTES),
        name="ada_mod",
    )(c8, w_ada, b_ada.reshape(1, n))
    return out[0:1]


def _level_reference(b, rolled, s, row):
    ts = b.shape[0]
    c = s // 2 - 1
    if s >= 16:
        pieces = []
        for blk in range(ts // s):
            r = blk * s + c
            pieces.append(jnp.broadcast_to(b[r:r + 1, :], (s, b.shape[1])))
        return pieces[0] if len(pieces) == 1 else jnp.concatenate(pieces, axis=0)
    pos = row & (s - 1)
    out = b
    for p in range(s):
        if p == c:
            continue
        out = jnp.where(pos == p, rolled[p - c], out)
    return out


def _cast_expert_weights(wg_ref, wu_ref, wgu16_ref):
    ff = wg_ref.shape[2]
    for j in range(wg_ref.shape[0]):
        wgu16_ref[j, :, 0:ff] = wg_ref[j].astype(BF16)
        wgu16_ref[j, :, ff:2 * ff] = wu_ref[j].astype(BF16)


def _mixer_kernel(x_ref, mod_ref, npre_ref, npost_ref, win_ref, convw_ref, wco_ref,
                  hlb_ref, hnw_ref, who_ref, wo_ref, wg_ref, wu_ref,
                  o_ref, wgu16_ref, state_ref, ubuf_ref):
    ts, d = x_ref.shape
    step = pl.program_id(0)
    _cast_expert_weights(wg_ref, wu_ref, wgu16_ref)

    @pl.when(step == 0)
    def _():
        state_ref[...] = jnp.zeros_like(state_ref)
        ubuf_ref[0:8, :] = jnp.zeros((8, CONV_DIM), F32)

    x = x_ref[...]
    sh1 = mod_ref[:, 0:d]
    sc1 = mod_ref[:, d:2 * d]
    g1 = mod_ref[:, 2 * d:3 * d]
    h = (_rms(x, npre_ref[...]) * (1.0 + sc1) + sh1).astype(BF16)

    def proj(lo, width):
        return _dot(h, win_ref[:, lo:lo + width])

    c0 = 0
    cb = proj(c0, CONV_DIM)
    u = proj(c0 + CONV_DIM, CONV_DIM) * proj(c0 + 2 * CONV_DIM, CONV_DIM)
    ubuf_ref[8:8 + ts, :] = u
    conv = (ubuf_ref[6:6 + ts, :] * convw_ref[0:1, :]
            + ubuf_ref[7:7 + ts, :] * convw_ref[1:2, :]
            + u * convw_ref[2:3, :])
    ubuf_ref[0:8, :] = ubuf_ref[ts:ts + 8, :]
    y_a = _dot((cb * conv).astype(BF16), wco_ref[...])

    c1 = 3 * CONV_DIM
    q = _silu(proj(c1, HGRN_QK))
    hl = hlb_ref[...]
    hm = jnp.max(hl, axis=0, keepdims=True)
    he = jnp.exp(hl - hm)
    lb = he[0:1, :] / jnp.sum(he, axis=0, keepdims=True)
    fg = lb + (1.0 - lb) * _sigmoid(proj(c1 + HGRN_QK, HGRN_QK))
    k = 1.0 - fg
    g = jnp.log(fg)
    v = proj(c1 + 2 * HGRN_QK, HGRN_QK)
    gg = proj(c1 + 3 * HGRN_QK, HGRN_QK)

    row = lax.broadcasted_iota(jnp.int32, (ts, ts), 0)
    col = lax.broadcasted_iota(jnp.int32, (ts, ts), 1)
    tril = jnp.where(row >= col, 1.0, 0.0).astype(BF16)
    ghi, gmid, glo = _split3(g)
    b = _dot(tril, ghi) + _dot(tril, gmid) + _dot(tril, glo)
    b_last = b[ts - 1:ts, :]

    rowq = lax.broadcasted_iota(jnp.int32, (ts, HGRN_QK), 0)
    rolled = {sft: pltpu.roll(b, sft % ts, axis=0) for sft in (-3, -2, -1, 1, 2, 3, 4)}
    xor = row ^ col

    levels = []
    s = ts
    while s >= 2:
        levels.append(s)
        s //= 2
    qt, kt = [], []
    for s in levels:
        bref = _level_reference(b, rolled, s, rowq)
        e = jnp.exp(-jnp.abs(b - bref))
        upper = (rowq & (s // 2)) != 0
        qt.append(jnp.where(upper, q * e, 0.0).astype(BF16))
        kt.append(jnp.where(upper, 0.0, k * e).astype(BF16))
    q_in = (q * jnp.exp(b)).astype(BF16)
    k_out = (k * jnp.exp(b_last - b)).astype(BF16)
    v16 = v.astype(BF16)
    qk = q * k
    decay_last = jnp.exp(b_last)

    hnw = hnw_ref[...]
    outs = []
    for hd in range(HGRN_HEADS):
        sl = slice(hd * HGRN_DK, (hd + 1) * HGRN_DK)
        a = jnp.zeros((ts, ts), F32)
        for li in range(len(levels) - 1, -1, -1):
            s = levels[li]
            p = _dot_nt(qt[li][:, sl], kt[li][:, sl])
            a = jnp.where(xor >= s // 2, p, a)
        st = state_ref[hd]
        o_h = (_dot(a.astype(BF16), v16[:, sl])
               + jnp.sum(qk[:, sl], axis=-1, keepdims=True) * v[:, sl]
               + _dot_nt(q_in[:, sl], st.astype(BF16)))
        state_ref[hd] = st * decay_last[:, sl] + _dot_tn(v16[:, sl], k_out[:, sl])
        outs.append(_rms(o_h, hnw))
    o = jnp.concatenate(outs, axis=-1) * _silu(gg)
    y_b = _dot(o.astype(BF16), who_ref[...])

    c2 = c1 + 4 * HGRN_QK
    m = _sigmoid(proj(c2, d)) * y_a + _sigmoid(proj(c2 + d, d)) * y_b
    y = _dot(m.astype(BF16), wo_ref[...])
    o_ref[...] = x + g1 * _rms(y, npost_ref[...])


def _const_spec(shape):
    nd = len(shape)
    return pl.BlockSpec(shape, lambda i: (0,) * nd)


def _mixer(x2, mod, npre, npost, w_in, conv_w, w_conv_out, hlb, hnw, w_hgrn_out, w_o, w_gate_e, w_up_e):
    t, d = x2.shape
    ts = SEQ_TILE
    n = t // ts
    n_e, _, ff = w_gate_e.shape
    assert n_e % n == 0 and conv_w.shape == (3, CONV_DIM)
    per = n_e // n
    args = (x2, mod, npre.reshape(1, d), npost.reshape(1, d), w_in.astype(BF16), conv_w,
            w_conv_out.astype(BF16), hlb, hnw.reshape(1, -1), w_hgrn_out.astype(BF16),
            w_o.astype(BF16))
    experts = lambda i: (i, 0, 0)
    in_specs = ([pl.BlockSpec((ts, d), lambda i: (i, 0))] + [_const_spec(a.shape) for a in args[1:]]
                + [pl.BlockSpec((per, d, ff), experts)] * 2)
    return pl.pallas_call(
        _mixer_kernel,
        out_shape=[jax.ShapeDtypeStruct((t, d), F32), jax.ShapeDtypeStruct((n_e, d, 2 * ff), BF16)],
        grid=(n,),
        in_specs=in_specs,
        out_specs=[pl.BlockSpec((ts, d), lambda i: (i, 0)), pl.BlockSpec((per, d, 2 * ff), experts)],
        scratch_shapes=[pltpu.VMEM((HGRN_HEADS, HGRN_DK, HGRN_DK), F32),
                        pltpu.VMEM((ts + 8, CONV_DIM), F32)],
        compiler_params=pltpu.CompilerParams(dimension_semantics=("arbitrary",),
                                             vmem_limit_bytes=VMEM_LIMIT_BYTES),
        name="token_mixer",
    )(*args, w_gate_e, w_up_e)


PACKED = jnp.int32


def _pack_bf16_pairs(x):
    m = x.shape[1] // 2
    hi = lax.bitcast_convert_type(x[:, :m].astype(BF16).astype(F32), jnp.uint32)
    lo = lax.bitcast_convert_type(x[:, m:].astype(BF16).astype(F32), jnp.uint32)
    return lax.bitcast_convert_type(hi | (lo >> 16), PACKED)


def _unpack_bf16_pairs(p):
    p = lax.bitcast_convert_type(p, jnp.uint32)
    hi = lax.bitcast_convert_type(p & jnp.uint32(0xFFFF0000), F32)
    lo = lax.bitcast_convert_type(p << 16, F32)
    return hi, lo


def _router_kernel(x1_ref, mod_ref, npre_ref, wrh_ref, wrl_ref, rb_ref,
                   hp_ref, code_ref, wcol_ref, cnt_ref, carry_ref, rows_ref):
    assert N_GROUPS == TOP_K == rows_ref.shape[0]
    tr, d = x1_ref.shape
    n_e = wrh_ref.shape[0]
    gsz = n_e // N_GROUPS
    neg = -jnp.inf

    @pl.when(pl.program_id(0) == 0)
    def _():
        carry_ref[...] = jnp.zeros_like(carry_ref)

    sh2 = mod_ref[:, 3 * d:4 * d]
    sc2 = mod_ref[:, 4 * d:5 * d]
    h2 = _rms(x1_ref[...], npre_ref[...]) * (1.0 + sc2) + sh2
    hp_ref[...] = _pack_bf16_pairs(h2)
    h_hi = h2.astype(BF16)
    h_lo = (h2 - h_hi.astype(F32)).astype(BF16)
    wrh = wrh_ref[...]
    logits = _dot_nt(wrh, h_hi) + (_dot_nt(wrh, h_lo) + _dot_nt(wrl_ref[...], h_hi))
    scores = _sigmoid(logits)
    sel = scores + rb_ref[:, 0:1]

    io_e = lax.broadcasted_iota(jnp.int32, (n_e, tr), 0)
    for g in range(N_GROUPS):
        blk = sel[g * gsz:(g + 1) * gsz, :]
        io = lax.broadcasted_iota(jnp.int32, (gsz, tr), 0) + g * gsz
        m1 = jnp.max(blk, axis=0, keepdims=True)
        i1 = jnp.min(jnp.where(blk == m1, io, n_e), axis=0, keepdims=True)
        m2 = jnp.max(jnp.where(io == i1, neg, blk), axis=0, keepdims=True)
        rows_ref[g:g + 1, :] = m1 + m2
    gs = rows_ref[...]
    io_g = lax.broadcasted_iota(jnp.int32, (N_GROUPS, tr), 0)
    gsel = jnp.zeros((N_GROUPS, tr), F32)
    for _ in range(TOPK_GROUPS):
        m = jnp.max(gs, axis=0, keepdims=True)
        gi = jnp.min(jnp.where(gs == m, io_g, N_GROUPS), axis=0, keepdims=True)
        hit = io_g == gi
        gsel = jnp.where(hit, 1.0, gsel)
        gs = jnp.where(hit, neg, gs)
    rows_ref[...] = gsel
    cur = jnp.concatenate(
        [jnp.where(rows_ref[g:g + 1, :] > 0.5, sel[g * gsz:(g + 1) * gsz, :], neg) for g in range(N_GROUPS)],
        axis=0)

    idxs = []
    selm = jnp.zeros((n_e, tr), F32)
    for k in range(TOP_K):
        m = jnp.max(cur, axis=0, keepdims=True)
        idx = jnp.min(jnp.where(cur == m, io_e, n_e), axis=0, keepdims=True)
        hit = io_e == idx
        rows_ref[k:k + 1, :] = jnp.sum(jnp.where(hit, scores, 0.0), axis=0, keepdims=True)
        cur = jnp.where(hit, neg, cur)
        selm = jnp.where(hit, 1.0, selm)
        idxs.append(idx)

    r_io = lax.broadcasted_iota(jnp.int32, (tr, tr + 128), 0)
    c_io = lax.broadcasted_iota(jnp.int32, (tr, tr + 128), 1)
    before = jnp.where((r_io < c_io) | (c_io >= tr), 1.0, 0.0).astype(BF16)
    r_ext = _dot(selm.astype(BF16), before)
    rank_full = r_ext[:, :tr] + carry_ref[:, 0:1]
    for k, idx in enumerate(idxs):
        rk = jnp.sum(jnp.where(io_e == idx, rank_full, 0.0), axis=0, keepdims=True)
        code_ref[k:k + 1, :] = (idx << CODE_SHIFT) | rk.astype(jnp.int32)
    carry_ref[...] = carry_ref[...] + r_ext[:, tr:]
    cnt_ref[...] = carry_ref[...]

    wk = rows_ref[...]
    wn = wk / (jnp.sum(wk, axis=0, keepdims=True) + 1e-20) * ROUTED_SCALE
    eye = jnp.where(lax.broadcasted_iota(jnp.int32, (TOP_K, 128), 0)
                    == lax.broadcasted_iota(jnp.int32, (TOP_K, 128), 1), 1.0, 0.0).astype(BF16)
    w1, w2, w3 = _split3(wn)
    wcol_ref[...] = _dot_tn(w1, eye) + _dot_tn(w2, eye) + _dot_tn(w3, eye)


def _router(x1, mod, npre2, w_router, router_bias):
    t, d = x1.shape
    n_e = w_router.shape[1]
    assert t <= (1 << CODE_SHIFT)
    tr = SEQ_TILE
    wrt = w_router.T
    wrh = wrt.astype(BF16)
    wrl = (wrt - wrh.astype(F32)).astype(BF16)
    rb = jnp.broadcast_to(router_bias.reshape(n_e, 1), (n_e, 128))
    args = (x1, mod, npre2.reshape(1, d), wrh, wrl, rb)
    in_specs = [pl.BlockSpec((tr, d), lambda i: (i, 0))] + [_const_spec(a.shape) for a in args[1:]]
    return pl.pallas_call(
        _router_kernel,
        out_shape=(jax.ShapeDtypeStruct((t, d // 2), PACKED),
                   jax.ShapeDtypeStruct((TOP_K, t), jnp.int32),
                   jax.ShapeDtypeStruct((t, 128), F32),
                   jax.ShapeDtypeStruct((n_e, 128), F32)),
        grid=(t // tr,),
        in_specs=in_specs,
        out_specs=(pl.BlockSpec((tr, d // 2), lambda i: (i, 0)),
                   pl.BlockSpec((TOP_K, tr), lambda i: (0, i)),
                   pl.BlockSpec((tr, 128), lambda i: (i, 0)),
                   pl.BlockSpec((n_e, 128), lambda i: (0, 0))),
        scratch_shapes=[pltpu.VMEM((n_e, 128), F32), pltpu.VMEM((TOP_K, tr), F32)],
        compiler_params=pltpu.CompilerParams(dimension_semantics=("arbitrary",),
                                             vmem_limit_bytes=VMEM_LIMIT_BYTES),
        name="moe_router",
    )(*args)


TAB_CHUNK_ROW, TAB_CHUNK_VALID, TAB_START, TAB_COUNT, TAB_CHUNK0, TAB_NCHUNK, TAB_TOTAL = range(7)
TAB_ROWS = 8


def _excl_cumsum_rows(lower, x):
    hi = jnp.floor(x * (1.0 / 128.0))
    lo = x - hi * 128.0
    return 128.0 * _dot(lower, hi.astype(BF16)) + _dot(lower, lo.astype(BF16))


def _table_kernel(cnt_ref, tab_ref, pad_ref, *, n_rows):
    n_e = cnt_ref.shape[0]
    n_g = tab_ref.shape[1]
    ch = float(EXPERT_CHUNK)
    cnt = jnp.floor((cnt_ref[...] + (ROW_ALIGN - 1)) * (1.0 / ROW_ALIGN)) * ROW_ALIGN
    r = lax.broadcasted_iota(jnp.int32, (n_e, n_e), 0)
    c = lax.broadcasted_iota(jnp.int32, (n_e, n_e), 1)
    lower = jnp.where(c < r, 1.0, 0.0).astype(BF16)
    starts = _excl_cumsum_rows(lower, cnt)
    start_col = starts[:, 0:1]

    nch = jnp.floor((cnt + (ch - 1.0)) * (1.0 / ch))
    chunk0 = _excl_cumsum_rows(lower, nch)
    cend_col = (chunk0 + nch)[:, 0:1]
    eye = r == c

    def as_row(col):
        return jnp.sum(jnp.where(eye, col, 0.0), axis=0, keepdims=True)

    g = lax.broadcasted_iota(jnp.int32, (n_e, n_g), 1).astype(F32)
    owner = jnp.sum(jnp.where(cend_col <= g, 1.0, 0.0), axis=0, keepdims=True)
    mine = lax.broadcasted_iota(jnp.int32, (n_e, n_g), 0).astype(F32) == owner
    g_row = lax.broadcasted_iota(jnp.int32, (1, n_g), 1).astype(F32)
    base = jnp.sum(jnp.where(mine, start_col - chunk0[:, 0:1] * ch, 0.0), axis=0, keepdims=True)
    left = jnp.sum(jnp.where(mine, cnt[:, 0:1] + chunk0[:, 0:1] * ch, 0.0), axis=0, keepdims=True)
    pad = jnp.zeros((1, n_g - n_e), F32)

    def wide(row):
        return jnp.concatenate([row, pad], axis=1)

    total = jnp.sum(nch[:, 0:1], axis=0, keepdims=True)
    tab_ref[...] = jnp.zeros(tab_ref.shape, jnp.int32)
    tab_ref[TAB_CHUNK_ROW:TAB_CHUNK_ROW + 1, :] = (base + g_row * ch).astype(jnp.int32)
    tab_ref[TAB_CHUNK_VALID:TAB_CHUNK_VALID + 1, :] = jnp.clip(left - g_row * ch, 0.0, ch).astype(jnp.int32)
    tab_ref[TAB_START:TAB_START + 1, :] = wide(as_row(start_col)).astype(jnp.int32)
    tab_ref[TAB_COUNT:TAB_COUNT + 1, :] = wide(as_row(cnt_ref[:, 0:1])).astype(jnp.int32)
    tab_ref[TAB_CHUNK0:TAB_CHUNK0 + 1, :] = wide(as_row(chunk0[:, 0:1])).astype(jnp.int32)
    tab_ref[TAB_NCHUNK:TAB_NCHUNK + 1, :] = wide(as_row(nch[:, 0:1])).astype(jnp.int32)
    tab_ref[TAB_TOTAL:TAB_TOTAL + 1, :] = jnp.broadcast_to(total, (1, n_g)).astype(jnp.int32)

    cols = pad_ref.shape[1]
    tail_cols = cols - n_e
    j_io = lax.broadcasted_iota(jnp.int32, (ROW_ALIGN, cols), 0).astype(F32)
    c_io = lax.broadcasted_iota(jnp.int32, (ROW_ALIGN, cols), 1).astype(F32)
    spare = float(n_rows) + j_io * cols + c_io
    zeros_tail = jnp.zeros((1, tail_cols), F32)
    raw_col = cnt_ref[:, 0:1]
    end_w = jnp.concatenate([as_row(start_col + raw_col), zeros_tail], axis=1)
    npad_w = jnp.concatenate([as_row(cnt[:, 0:1] - raw_col), zeros_tail], axis=1)
    used = jnp.sum(cnt[:, 0:1], axis=0, keepdims=True)
    after = used + j_io * tail_cols + (c_io - n_e)
    target = jnp.where(c_io < n_e,
                       jnp.where(j_io < npad_w, end_w + j_io, spare),
                       jnp.where(after < float(n_rows), after, spare))
    pad_ref[...] = target.astype(jnp.int32)


def _sorted_rows(t, n_e):
    return t * TOP_K + n_e * ROW_ALIGN + EXPERT_CHUNK + ROW_ALIGN


def _pad_list_cols(t, n_e):
    entries = n_e * ROW_ALIGN + (_sorted_rows(t, n_e) - t * TOP_K)
    per_round = SC_WORKERS * SC_GATHER_ROWS
    return pl.cdiv(entries, per_round) * per_round // ROW_ALIGN


def _route_table(cnt, t):
    n_e = cnt.shape[0]
    n_g = (t * TOP_K) // EXPERT_CHUNK + n_e
    cols = _pad_list_cols(t, n_e)
    return pl.pallas_call(
        functools.partial(_table_kernel, n_rows=_sorted_rows(t, n_e)),
        out_shape=(jax.ShapeDtypeStruct((TAB_ROWS, n_g), jnp.int32),
                   jax.ShapeDtypeStruct((ROW_ALIGN, cols), jnp.int32)),
        grid=(1,),
        in_specs=[_const_spec(cnt.shape)],
        out_specs=(_const_spec((TAB_ROWS, n_g)), _const_spec((ROW_ALIGN, cols))),
        compiler_params=pltpu.CompilerParams(dimension_semantics=("arbitrary",)),
        name="moe_table",
    )(cnt)


SC_CORES = 2
SC_SUBCORES = 16
SC_WORKERS = SC_CORES * SC_SUBCORES
SC_GATHER_ROWS = 64
COMBINE_SPLITS = 4


def _sc_mesh():
    return plsc.VectorSubcoreMesh(core_axis_name="c", subcore_axis_name="s")


def _sc_worker_id():
    return lax.axis_index("s") * SC_CORES + lax.axis_index("c")


SC_LANES = 16


def _sc_decode_rows(code_v, starts_v, idx_v):
    for i in range(code_v.shape[0] // SC_LANES):
        lanes = pl.ds(i * SC_LANES, SC_LANES)
        code = code_v[lanes]
        expert = lax.shift_right_logical(code, CODE_SHIFT)
        idx_v[lanes] = plsc.load_gather(starts_v, [expert]) + (code & ((1 << CODE_SHIFT) - 1))


def _sc_params():
    return pltpu.CompilerParams(needs_layout_passes=False)


def _sc_scatter_rows(rows, code_flat, starts, zero_list, n_out):
    t, w = rows.shape
    n_k = code_flat.shape[0] // t
    n_e = starts.shape[0]
    r = SC_GATHER_ROWS
    per = t // SC_WORKERS
    n_steps = per // r
    zero_rounds = zero_list.shape[0] // (SC_WORKERS * r)
    assert per % (2 * r) == 0 and zero_list.shape[0] % (SC_WORKERS * r) == 0 and zero_rounds <= n_k
    zeros = jnp.zeros((r, w), rows.dtype)

    @functools.partial(
        pl.kernel, mesh=_sc_mesh(), out_type=jax.ShapeDtypeStruct((n_out, w), rows.dtype),
        scratch_types=[pltpu.VMEM((n_e,), jnp.int32), pltpu.VMEM((n_k, r), jnp.int32),
                       pltpu.VMEM((n_k, r), jnp.int32), pltpu.VMEM((2, r, w), rows.dtype),
                       pltpu.SemaphoreType.DMA((2,)), pltpu.SemaphoreType.DMA],
        compiler_params=_sc_params(), name="moe_dispatch_sc")
    def scatter(rows_hbm, code_hbm, starts_hbm, zero_idx_hbm, zeros_hbm, out_hbm,
                starts_v, code_v, idx_v, rows_v, l_sem, s_sem):
        worker = _sc_worker_id()
        base = worker * per
        pltpu.sync_copy(starts_hbm, starts_v)

        def load(j, b):
            return pltpu.make_async_copy(rows_hbm.at[pl.ds(pl.multiple_of(base + j * r, ROW_ALIGN), r)],
                                         rows_v.at[b], l_sem.at[b])

        load(0, 0).start()
        pltpu.sync_copy(zeros_hbm, rows_v.at[1])
        zero_copies = []
        for z in range(zero_rounds):
            off = pl.multiple_of((worker * zero_rounds + z) * r, ROW_ALIGN)
            pltpu.sync_copy(zero_idx_hbm.at[pl.ds(off, r)], idx_v.at[z])
            zero_copies.append(pltpu.make_async_copy(rows_v.at[1], out_hbm.at[idx_v.at[z]], s_sem))
            zero_copies[-1].start()
        for cp in zero_copies:
            cp.wait()

        @pl.loop(0, n_steps, step=2)
        def _(j0):
            for b in range(2):
                j = j0 + b
                pl.when(j + 1 < n_steps)(load(j + 1, 1 - b).start)
                for k in range(n_k):
                    off = pl.multiple_of(k * t + base + j * r, ROW_ALIGN)
                    pltpu.sync_copy(code_hbm.at[pl.ds(off, r)], code_v.at[k])
                for k in range(n_k):
                    _sc_decode_rows(code_v.at[k], starts_v, idx_v.at[k])
                load(j, b).wait()
                copies = [pltpu.make_async_copy(rows_v.at[b], out_hbm.at[idx_v.at[k]], s_sem) for k in range(n_k)]
                for cp in copies:
                    cp.start()
                for cp in copies:
                    cp.wait()

    return scatter(rows, code_flat, starts, zero_list, zeros)


def _sc_gather_rows(table, code_flat, starts):
    n = code_flat.shape[0]
    w = table.shape[1]
    n_e = starts.shape[0]
    r = SC_GATHER_ROWS
    per = n // SC_WORKERS
    n_steps = per // r
    assert per % (2 * r) == 0

    @functools.partial(
        pl.kernel, mesh=_sc_mesh(), out_type=jax.ShapeDtypeStruct((n, w), table.dtype),
        scratch_types=[pltpu.VMEM((n_e,), jnp.int32), pltpu.VMEM((r,), jnp.int32), pltpu.VMEM((2, r), jnp.int32),
                       pltpu.VMEM((2, r, w), table.dtype),
                       pltpu.SemaphoreType.DMA((2,)), pltpu.SemaphoreType.DMA((2,))],
        compiler_params=_sc_params(), name="moe_gather_sc")
    def gather(table_hbm, code_hbm, starts_hbm, out_hbm, starts_v, code_v, idx_v, rows_v, g_sem, w_sem):
        base = _sc_worker_id() * per
        pltpu.sync_copy(starts_hbm, starts_v)

        def at(j):
            return pl.ds(pl.multiple_of(base + j * r, ROW_ALIGN), r)

        def fetch(j, b):
            return pltpu.make_async_copy(table_hbm.at[idx_v.at[b]], rows_v.at[b], g_sem.at[b])

        def start_fetch(j, b):
            pltpu.sync_copy(code_hbm.at[at(j)], code_v)
            _sc_decode_rows(code_v, starts_v, idx_v.at[b])
            fetch(j, b).start()

        def write(j, b):
            return pltpu.make_async_copy(rows_v.at[b], out_hbm.at[at(j)], w_sem.at[b])

        start_fetch(0, 0)

        @pl.loop(0, n_steps, step=2)
        def _(j0):
            for b in range(2):
                j = j0 + b

                @pl.when(j + 1 < n_steps)
                def _():
                    pl.when(j >= 1)(write(j - 1, 1 - b).wait)
                    start_fetch(j + 1, 1 - b)

                fetch(j, b).wait()
                write(j, b).start()

        write(n_steps - 2, 0).wait()
        write(n_steps - 1, 1).wait()

    return gather(table, code_flat, starts)


def _dispatch(code, starts, pad_list, hp):
    t = hp.shape[0]
    n_out = _sorted_rows(t, starts.shape[0]) + pad_list.size
    return _sc_scatter_rows(hp, code.reshape(-1), starts, pad_list.reshape(-1), n_out)


X_LOOKAHEAD = 2
X_SLOTS = X_LOOKAHEAD + 1
Y_SLOTS = 2


def _expert_kernel(tab_ref, xs_hbm, wgu_ref, wd_ref, y_hbm,
                   wd16_ref, xbuf_ref, ybuf_ref, in_sem, out_sem, *, n_e):
    e = pl.program_id(0)
    ch, half = xbuf_ref.shape[1:]
    ff = wd_ref.shape[1]
    n_total = tab_ref[TAB_TOTAL, 0]
    first = tab_ref[TAB_CHUNK0, e]
    n_chunks = tab_ref[TAB_NCHUNK, e]

    def in_copy(g):
        slot = g % X_SLOTS
        row = pl.multiple_of(tab_ref[TAB_CHUNK_ROW, g], ROW_ALIGN)
        return pltpu.make_async_copy(xs_hbm.at[pl.ds(row, ch), :], xbuf_ref.at[slot], in_sem.at[slot])

    def out_copies(g):
        slot = g % Y_SLOTS
        row = tab_ref[TAB_CHUNK_ROW, g]
        valid = tab_ref[TAB_CHUNK_VALID, g]
        yield valid >= ch, pltpu.make_async_copy(ybuf_ref.at[slot], y_hbm.at[pl.ds(pl.multiple_of(row, ROW_ALIGN), ch), :],
                                                 out_sem.at[slot])
        size = ch // 2
        while size >= ROW_ALIGN:
            off = pl.multiple_of((valid // (2 * size)) * (2 * size), ROW_ALIGN)
            cond = (valid < ch) & ((valid & size) != 0)
            yield cond, pltpu.make_async_copy(ybuf_ref.at[slot, pl.ds(off, size), :],
                                              y_hbm.at[pl.ds(pl.multiple_of(row + off, ROW_ALIGN), size), :],
                                              out_sem.at[slot])
            size //= 2

    def start_out(g):
        for cond, cp in out_copies(g):
            pl.when(cond)(cp.start)

    def wait_out(g):
        for cond, cp in out_copies(g):
            pl.when(cond)(cp.wait)

    @pl.when(e == 0)
    def _():
        for j in range(X_LOOKAHEAD):
            pl.when(j < n_total)(in_copy(j).start)

    @pl.when(n_chunks > 0)
    def _():
        wd16_ref[...] = wd_ref[0].astype(BF16)

    def chunk(c, carry):
        g = first + c
        pl.when(g + X_LOOKAHEAD < n_total)(in_copy(g + X_LOOKAHEAD).start)
        in_copy(g).wait()
        pl.when(g >= Y_SLOTS)(lambda: wait_out(g - Y_SLOTS))
        valid = tab_ref[TAB_CHUNK_VALID, g]

        def ffn(rows):
            xa, xb = _unpack_bf16_pairs(xbuf_ref[g % X_SLOTS, 0:rows, :])
            hgu = (_dot(xa.astype(BF16), wgu_ref[0, 0:half, :])
                   + _dot(xb.astype(BF16), wgu_ref[0, half:2 * half, :]))
            act = _silu(hgu[:, :ff]) * hgu[:, ff:]
            y = _dot(act.astype(BF16), wd16_ref[...])
            ybuf_ref[g % Y_SLOTS, 0:rows, :] = _pack_bf16_pairs(y)

        lo = 0
        for rows in EXPERT_ROW_STEPS:
            pl.when((valid > lo) & (valid <= rows))(functools.partial(ffn, rows))
            lo = rows
        start_out(g)
        return carry

    lax.fori_loop(0, n_chunks, chunk, 0)

    @pl.when(e == n_e - 1)
    def _():
        for j in range(Y_SLOTS, 0, -1):
            pl.when(n_total >= j)(lambda j=j: wait_out(n_total - j))
        n_rows = y_hbm.shape[0]
        used = tab_ref[TAB_START, e] + ((tab_ref[TAB_COUNT, e] + (ROW_ALIGN - 1)) // ROW_ALIGN) * ROW_ALIGN
        ybuf_ref[0] = jnp.zeros((ch, half), PACKED)
        for j in range(pl.cdiv(n_e * ROW_ALIGN + EXPERT_CHUNK + ROW_ALIGN, ch)):
            at = pl.multiple_of(jnp.minimum(used + j * ch, n_rows - ch), ROW_ALIGN)
            tail = pltpu.make_async_copy(ybuf_ref.at[0], y_hbm.at[pl.ds(at, ch), :], out_sem.at[0])
            tail.start()
            tail.wait()


def _experts(tab, xs, wgu16, w_down_e, p):
    half = xs.shape[1]
    n_e, ff, d = w_down_e.shape
    ch = EXPERT_CHUNK
    grid_spec = pltpu.PrefetchScalarGridSpec(
        num_scalar_prefetch=1,
        grid=(n_e,),
        in_specs=[pl.BlockSpec(memory_space=pl.ANY),
                  pl.BlockSpec((1, d, 2 * ff), lambda e, *_: (e, 0, 0)),
                  pl.BlockSpec((1, ff, d), lambda e, *_: (e, 0, 0))],
        out_specs=pl.BlockSpec(memory_space=pl.ANY),
        scratch_shapes=[pltpu.VMEM((ff, d), BF16),
                        pltpu.VMEM((X_SLOTS, ch, half), PACKED), pltpu.VMEM((Y_SLOTS, ch, half), PACKED),
                        pltpu.SemaphoreType.DMA((X_SLOTS,)), pltpu.SemaphoreType.DMA((Y_SLOTS,))],
    )
    return pl.pallas_call(
        functools.partial(_expert_kernel, n_e=n_e),
        out_shape=jax.ShapeDtypeStruct((p, half), PACKED),
        grid_spec=grid_spec,
        compiler_params=pltpu.CompilerParams(dimension_semantics=("arbitrary",),
                                             vmem_limit_bytes=VMEM_LIMIT_BYTES),
        name="moe_experts",
    )(tab, xs, wgu16, w_down_e)


def _combine_kernel(yg_ref, wcol_ref, x1_ref, mod_ref, npre_ref, npost_ref, wgus_ref, wds_ref, o_ref):
    te, d = x1_ref.shape
    half = d // 2
    ff = wds_ref.shape[0]

    x1 = x1_ref[...]
    h2 = _rms(x1, npre_ref[...]) * (1.0 + mod_ref[:, 4 * d:5 * d]) + mod_ref[:, 3 * d:4 * d]
    hgu = _dot(h2.astype(BF16), wgus_ref[...])
    act = _silu(hgu[:, :ff]) * hgu[:, ff:]
    shared = _dot(act.astype(BF16), wds_ref[...])

    acc_a = shared[:, :half]
    acc_b = shared[:, half:]
    for k in range(TOP_K):
        ya, yb = _unpack_bf16_pairs(yg_ref[k])
        wk = wcol_ref[:, k:k + 1]
        acc_a = acc_a + wk * ya
        acc_b = acc_b + wk * yb
    moe = jnp.concatenate([acc_a, acc_b], axis=-1)
    g2 = mod_ref[:, 5 * d:6 * d]
    o_ref[...] = x1 + g2 * _rms(moe, npost_ref[...])


def _combine(code, starts, y, wcol, x1, mod, npre2, npost2, w_gate_s, w_up_s, w_down_s):
    t, d = x1.shape
    half = d // 2
    te = SEQ_TILE
    wgus = jnp.concatenate([w_gate_s, w_up_s], axis=1).astype(BF16)
    wds = w_down_s.astype(BF16)
    n_split = COMBINE_SPLITS if t % (COMBINE_SPLITS * 2 * SC_GATHER_ROWS * SC_WORKERS // TOP_K) == 0 else 1
    tq = t // n_split
    steps = tq // te
    stream = x1
    for q in range(n_split):
        code_q = code[:, q * tq:(q + 1) * tq]
        yg = _sc_gather_rows(y, code_q.reshape(-1), starts).reshape(TOP_K, tq, half)
        row = lambda i, q=q: (q * steps + i, 0)
        stream = pl.pallas_call(
            _combine_kernel,
            out_shape=jax.ShapeDtypeStruct((t, d), F32),
            grid=(steps,),
            in_specs=[pl.BlockSpec((TOP_K, te, half), lambda i: (0, i, 0)),
                      pl.BlockSpec((te, 128), row),
                      pl.BlockSpec((te, d), row),
                      _const_spec(mod.shape),
                      _const_spec((1, d)),
                      _const_spec((1, d)),
                      _const_spec(wgus.shape),
                      _const_spec(wds.shape)],
            out_specs=pl.BlockSpec((te, d), row),
            input_output_aliases={2: 0},
            compiler_params=pltpu.CompilerParams(dimension_semantics=("arbitrary",),
                                                 vmem_limit_bytes=VMEM_LIMIT_BYTES),
            name="moe_combine",
        )(yg, wcol, stream, mod, npre2.reshape(1, d), npost2.reshape(1, d), wgus, wds)
    return stream


def kernel(x, c, w_ada, b_ada, norm_pre_mix, norm_post_mix, w_in, conv_w, w_conv_out, hgrn_lower_bounds, hgrn_norm_w, w_hgrn_out, w_o, norm_pre_ffn, norm_post_ffn, w_router, router_bias, w_gate_e, w_up_e, w_down_e, w_gate_s, w_up_s, w_down_s):
    bsz, seq, d = x.shape
    assert bsz == 1 and w_ada.shape[0] == 1
    mod = _ada_mod(c, w_ada[0], b_ada[0])
    x1, wgu16 = _mixer(x.reshape(seq, d), mod, norm_pre_mix[0], norm_post_mix[0], w_in[0], conv_w[0],
                       w_conv_out[0], hgrn_lower_bounds, hgrn_norm_w[0], w_hgrn_out[0], w_o[0],
                       w_gate_e[0], w_up_e[0])
    hp, code, wcol, cnt = _router(x1, mod, norm_pre_ffn[0], w_router[0], router_bias[0])
    n_e = w_router.shape[-1]
    tab, pad_list = _route_table(cnt, seq)
    starts = tab[TAB_START, :n_e]
    xs = _dispatch(code, starts, pad_list, hp)
    y = _experts(tab, xs, wgu16, w_down_e[0], _sorted_rows(seq, n_e))
    out = _combine(code, starts, y, wcol, x1, mod, norm_pre_ffn[0], norm_post_ffn[0],
                   w_gate_s[0], w_up_s[0], w_down_s[0])
    return out.reshape(bsz, seq, d)
```

```python
import functools

import jax
import jax.numpy as jnp
from jax import lax
from jax.experimental import pallas as pl
from jax.experimental.pallas import tpu as pltpu
from jax.experimental.pallas import tpu_sc as plsc

F32 = jnp.float32
BF16 = jnp.bfloat16

NORM_EPS = 1e-6
CONV_DIM = 512
HGRN_HEADS = 4
HGRN_DK = 128
HGRN_QK = HGRN_HEADS * HGRN_DK
N_GROUPS = 8
TOPK_GROUPS = 4
TOP_K = 8
ROUTED_SCALE = 2.5
CODE_SHIFT = 16

SEQ_TILE = 256
EXPERT_CHUNK = 512
EXPERT_ROW_STEPS = (128, 256, EXPERT_CHUNK)
ROW_ALIGN = 8
VMEM_LIMIT_BYTES = 56 * 1024 * 1024


def _dot(a, b):
    return jnp.dot(a, b, preferred_element_type=F32)


def _dot_nt(a, b):
    return lax.dot_general(a, b, (((1,), (1,)), ((), ())), preferred_element_type=F32)


def _dot_tn(a, b):
    return lax.dot_general(a, b, (((0,), (0,)), ((), ())), preferred_element_type=F32)


def _split3(x):
    hi = x.astype(BF16)
    r1 = x - hi.astype(F32)
    mid = r1.astype(BF16)
    lo = (r1 - mid.astype(F32)).astype(BF16)
    return hi, mid, lo


def _sigmoid(x):
    return 1.0 / (1.0 + jnp.exp(-x))


def _silu(x):
    return x * _sigmoid(x)


def _rms(x, w):
    ms = jnp.mean(x * x, axis=-1, keepdims=True)
    return x * lax.rsqrt(ms + NORM_EPS) * w


def _ada_kernel(c_ref, w_ref, b_ref, win_ref, o_ref, win16_ref):
    win16_ref[...] = win_ref[...].astype(BF16)
    cs = _silu(c_ref[...])
    h1, h2, h3 = _split3(cs)
    w1, w2, w3 = _split3(w_ref[...])
    r = cs.shape[0]
    stacked = jnp.concatenate([h.astype(F32) for h in (h1, h2, h3)], axis=0).astype(BF16)
    p1 = _dot(stacked, w1)
    p2 = _dot(stacked[0:2 * r], w2)
    p3 = _dot(h1, w3)
    acc = p1[0:r] + (p2[0:r] + p1[r:2 * r]) + (p3 + p2[r:2 * r] + p1[2 * r:3 * r])
    o_ref[...] = acc + b_ref[...]


def _ada_mod(c, w_ada, b_ada, w_in):
    d = c.shape[-1]
    n = w_ada.shape[-1]
    bn = 512
    n_in = w_in.shape[1] // bn
    assert w_in.shape[1] % bn == 0 and n_in <= n // bn
    c8 = jnp.broadcast_to(c.reshape(1, d), (8, d))
    win_block = lambda j: (0, jnp.minimum(j, n_in - 1))
    out, w_in16 = pl.pallas_call(
        _ada_kernel,
        out_shape=(jax.ShapeDtypeStruct((8, n), F32), jax.ShapeDtypeStruct(w_in.shape, BF16)),
        grid=(n // bn,),
        in_specs=[pl.BlockSpec((8, d), lambda j: (0, 0)),
                  pl.BlockSpec((d, bn), lambda j: (0, j)),
                  pl.BlockSpec((1, bn), lambda j: (0, j)),
                  pl.BlockSpec((d, bn), win_block)],
        out_specs=(pl.BlockSpec((8, bn), lambda j: (0, j)), pl.BlockSpec((d, bn), win_block)),
        compiler_params=pltpu.CompilerParams(dimension_semantics=("arbitrary",),
                                             vmem_limit_bytes=VMEM_LIMIT_BYTES),
        name="ada_mod",
    )(c8, w_ada, b_ada.reshape(1, n), w_in)
    return out[0:1], w_in16


def _level_reference(b, rolled, s, row):
    ts = b.shape[0]
    c = s // 2 - 1
    if s >= 16:
        pieces = []
        for blk in range(ts // s):
            r = blk * s + c
            pieces.append(jnp.broadcast_to(b[r:r + 1, :], (s, b.shape[1])))
        return pieces[0] if len(pieces) == 1 else jnp.concatenate(pieces, axis=0)
    pos = row & (s - 1)
    out = b
    for p in range(s):
        if p == c:
            continue
        out = jnp.where(pos == p, rolled[p - c], out)
    return out


def _cast_expert_weights(wg_ref, wu_ref, wgu16_ref):
    ff = wg_ref.shape[2]
    for j in range(wg_ref.shape[0]):
        wgu16_ref[j, :, 0:ff] = wg_ref[j].astype(BF16)
        wgu16_ref[j, :, ff:2 * ff] = wu_ref[j].astype(BF16)


def _mixer_kernel(x_ref, mod_ref, npre_ref, npost_ref, win_ref, convw_ref, wco_ref,
                  hlb_ref, hnw_ref, who_ref, wo_ref, wg_ref, wu_ref,
                  o_ref, wgu16_ref, state_ref, ubuf_ref):
    ts, d = x_ref.shape
    step = pl.program_id(0)
    _cast_expert_weights(wg_ref, wu_ref, wgu16_ref)

    @pl.when(step == 0)
    def _():
        state_ref[...] = jnp.zeros_like(state_ref)
        ubuf_ref[0:8, :] = jnp.zeros((8, CONV_DIM), F32)

    x = x_ref[...]
    sh1 = mod_ref[:, 0:d]
    sc1 = mod_ref[:, d:2 * d]
    g1 = mod_ref[:, 2 * d:3 * d]
    h = (_rms(x, npre_ref[...]) * (1.0 + sc1) + sh1).astype(BF16)

    def proj(lo, width):
        return _dot(h, win_ref[:, lo:lo + width])

    c0 = 0
    cb = proj(c0, CONV_DIM)
    u = proj(c0 + CONV_DIM, CONV_DIM) * proj(c0 + 2 * CONV_DIM, CONV_DIM)
    ubuf_ref[8:8 + ts, :] = u
    conv = (ubuf_ref[6:6 + ts, :] * convw_ref[0:1, :]
            + ubuf_ref[7:7 + ts, :] * convw_ref[1:2, :]
            + u * convw_ref[2:3, :])
    ubuf_ref[0:8, :] = ubuf_ref[ts:ts + 8, :]
    y_a = _dot((cb * conv).astype(BF16), wco_ref[...])

    c1 = 3 * CONV_DIM
    q = _silu(proj(c1, HGRN_QK))
    hl = hlb_ref[...]
    hm = jnp.max(hl, axis=0, keepdims=True)
    he = jnp.exp(hl - hm)
    lb = he[0:1, :] / jnp.sum(he, axis=0, keepdims=True)
    fg = lb + (1.0 - lb) * _sigmoid(proj(c1 + HGRN_QK, HGRN_QK))
    k = 1.0 - fg
    g = jnp.log(fg)
    v = proj(c1 + 2 * HGRN_QK, HGRN_QK)
    gg = proj(c1 + 3 * HGRN_QK, HGRN_QK)

    row = lax.broadcasted_iota(jnp.int32, (ts, ts), 0)
    col = lax.broadcasted_iota(jnp.int32, (ts, ts), 1)
    tril = jnp.where(row >= col, 1.0, 0.0).astype(BF16)
    ghi, gmid, glo = _split3(g)
    b = _dot(tril, ghi) + _dot(tril, gmid) + _dot(tril, glo)
    b_last = b[ts - 1:ts, :]

    rowq = lax.broadcasted_iota(jnp.int32, (ts, HGRN_QK), 0)
    rolled = {sft: pltpu.roll(b, sft % ts, axis=0) for sft in (-3, -2, -1, 1, 2, 3, 4)}
    xor = row ^ col

    levels = []
    s = ts
    while s >= 2:
        levels.append(s)
        s //= 2
    qt, kt = [], []
    for s in levels:
        bref = _level_reference(b, rolled, s, rowq)
        e = jnp.exp(-jnp.abs(b - bref))
        upper = (rowq & (s // 2)) != 0
        qt.append(jnp.where(upper, q * e, 0.0).astype(BF16))
        kt.append(jnp.where(upper, 0.0, k * e).astype(BF16))
    q_in = (q * jnp.exp(b)).astype(BF16)
    k_out = (k * jnp.exp(b_last - b)).astype(BF16)
    v16 = v.astype(BF16)
    qk = q * k
    decay_last = jnp.exp(b_last)

    hnw = hnw_ref[...]
    outs = []
    for hd in range(HGRN_HEADS):
        sl = slice(hd * HGRN_DK, (hd + 1) * HGRN_DK)
        a = jnp.zeros((ts, ts), F32)
        for li in range(len(levels) - 1, -1, -1):
            s = levels[li]
            p = _dot_nt(qt[li][:, sl], kt[li][:, sl])
            a = jnp.where(xor >= s // 2, p, a)
        st = state_ref[hd]
        o_h = (_dot(a.astype(BF16), v16[:, sl])
               + jnp.sum(qk[:, sl], axis=-1, keepdims=True) * v[:, sl]
               + _dot_nt(q_in[:, sl], st.astype(BF16)))
        state_ref[hd] = st * decay_last[:, sl] + _dot_tn(v16[:, sl], k_out[:, sl])
        outs.append(_rms(o_h, hnw))
    o = jnp.concatenate(outs, axis=-1) * _silu(gg)
    y_b = _dot(o.astype(BF16), who_ref[...])

    c2 = c1 + 4 * HGRN_QK
    m = _sigmoid(proj(c2, d)) * y_a + _sigmoid(proj(c2 + d, d)) * y_b
    y = _dot(m.astype(BF16), wo_ref[...])
    o_ref[...] = x + g1 * _rms(y, npost_ref[...])


def _const_spec(shape):
    nd = len(shape)
    return pl.BlockSpec(shape, lambda i: (0,) * nd)


def _mixer(x2, mod, npre, npost, w_in, conv_w, w_conv_out, hlb, hnw, w_hgrn_out, w_o, w_gate_e, w_up_e):
    t, d = x2.shape
    ts = SEQ_TILE
    n = t // ts
    n_e, _, ff = w_gate_e.shape
    assert n_e % n == 0 and conv_w.shape == (3, CONV_DIM)
    per = n_e // n
    args = (x2, mod, npre.reshape(1, d), npost.reshape(1, d), w_in.astype(BF16), conv_w,
            w_conv_out.astype(BF16), hlb, hnw.reshape(1, -1), w_hgrn_out.astype(BF16),
            w_o.astype(BF16))
    experts = lambda i: (i, 0, 0)
    in_specs = ([pl.BlockSpec((ts, d), lambda i: (i, 0))] + [_const_spec(a.shape) for a in args[1:]]
                + [pl.BlockSpec((per, d, ff), experts)] * 2)
    return pl.pallas_call(
        _mixer_kernel,
        out_shape=[jax.ShapeDtypeStruct((t, d), F32), jax.ShapeDtypeStruct((n_e, d, 2 * ff), BF16)],
        grid=(n,),
        in_specs=in_specs,
        out_specs=[pl.BlockSpec((ts, d), lambda i: (i, 0)), pl.BlockSpec((per, d, 2 * ff), experts)],
        scratch_shapes=[pltpu.VMEM((HGRN_HEADS, HGRN_DK, HGRN_DK), F32),
                        pltpu.VMEM((ts + 8, CONV_DIM), F32)],
        compiler_params=pltpu.CompilerParams(dimension_semantics=("arbitrary",),
                                             vmem_limit_bytes=VMEM_LIMIT_BYTES),
        name="token_mixer",
    )(*args, w_gate_e, w_up_e)


PACKED = jnp.int32


def _pack_bf16_pairs(x):
    m = x.shape[1] // 2
    hi = lax.bitcast_convert_type(x[:, :m].astype(BF16).astype(F32), jnp.uint32)
    lo = lax.bitcast_convert_type(x[:, m:].astype(BF16).astype(F32), jnp.uint32)
    return lax.bitcast_convert_type(hi | (lo >> 16), PACKED)


def _unpack_bf16_pairs(p):
    p = lax.bitcast_convert_type(p, jnp.uint32)
    hi = lax.bitcast_convert_type(p & jnp.uint32(0xFFFF0000), F32)
    lo = lax.bitcast_convert_type(p << 16, F32)
    return hi, lo


def _router_kernel(x1_ref, mod_ref, npre_ref, wrh_ref, wrl_ref, rb_ref,
                   hp_ref, code_ref, wcol_ref, cnt_ref, carry_ref, rows_ref):
    assert N_GROUPS == TOP_K == rows_ref.shape[0]
    tr, d = x1_ref.shape
    n_e = wrh_ref.shape[0]
    gsz = n_e // N_GROUPS
    neg = -jnp.inf

    @pl.when(pl.program_id(0) == 0)
    def _():
        carry_ref[...] = jnp.zeros_like(carry_ref)

    sh2 = mod_ref[:, 3 * d:4 * d]
    sc2 = mod_ref[:, 4 * d:5 * d]
    h2 = _rms(x1_ref[...], npre_ref[...]) * (1.0 + sc2) + sh2
    hp_ref[...] = _pack_bf16_pairs(h2)
    h_hi = h2.astype(BF16)
    h_lo = (h2 - h_hi.astype(F32)).astype(BF16)
    wrh = wrh_ref[...]
    logits = _dot_nt(wrh, h_hi) + (_dot_nt(wrh, h_lo) + _dot_nt(wrl_ref[...], h_hi))
    scores = _sigmoid(logits)
    sel = scores + rb_ref[:, 0:1]

    io_e = lax.broadcasted_iota(jnp.int32, (n_e, tr), 0)
    for g in range(N_GROUPS):
        blk = sel[g * gsz:(g + 1) * gsz, :]
        io = lax.broadcasted_iota(jnp.int32, (gsz, tr), 0) + g * gsz
        m1 = jnp.max(blk, axis=0, keepdims=True)
        i1 = jnp.min(jnp.where(blk == m1, io, n_e), axis=0, keepdims=True)
        m2 = jnp.max(jnp.where(io == i1, neg, blk), axis=0, keepdims=True)
        rows_ref[g:g + 1, :] = m1 + m2
    gs = rows_ref[...]
    io_g = lax.broadcasted_iota(jnp.int32, (N_GROUPS, tr), 0)
    gsel = jnp.zeros((N_GROUPS, tr), F32)
    for _ in range(TOPK_GROUPS):
        m = jnp.max(gs, axis=0, keepdims=True)
        gi = jnp.min(jnp.where(gs == m, io_g, N_GROUPS), axis=0, keepdims=True)
        hit = io_g == gi
        gsel = jnp.where(hit, 1.0, gsel)
        gs = jnp.where(hit, neg, gs)
    rows_ref[...] = gsel
    cur = jnp.concatenate(
        [jnp.where(rows_ref[g:g + 1, :] > 0.5, sel[g * gsz:(g + 1) * gsz, :], neg) for g in range(N_GROUPS)],
        axis=0)

    idxs = []
    selm = jnp.zeros((n_e, tr), F32)
    for k in range(TOP_K):
        m = jnp.max(cur, axis=0, keepdims=True)
        idx = jnp.min(jnp.where(cur == m, io_e, n_e), axis=0, keepdims=True)
        hit = io_e == idx
        rows_ref[k:k + 1, :] = jnp.sum(jnp.where(hit, scores, 0.0), axis=0, keepdims=True)
        cur = jnp.where(hit, neg, cur)
        selm = jnp.where(hit, 1.0, selm)
        idxs.append(idx)

    r_io = lax.broadcasted_iota(jnp.int32, (tr, tr + 128), 0)
    c_io = lax.broadcasted_iota(jnp.int32, (tr, tr + 128), 1)
    before = jnp.where((r_io < c_io) | (c_io >= tr), 1.0, 0.0).astype(BF16)
    r_ext = _dot(selm.astype(BF16), before)
    rank_full = r_ext[:, :tr] + carry_ref[:, 0:1]
    for k, idx in enumerate(idxs):
        rk = jnp.sum(jnp.where(io_e == idx, rank_full, 0.0), axis=0, keepdims=True)
        code_ref[k:k + 1, :] = (idx << CODE_SHIFT) | rk.astype(jnp.int32)
    carry_ref[...] = carry_ref[...] + r_ext[:, tr:]
    cnt_ref[...] = carry_ref[...]

    wk = rows_ref[...]
    wn = wk / (jnp.sum(wk, axis=0, keepdims=True) + 1e-20) * ROUTED_SCALE
    eye = jnp.where(lax.broadcasted_iota(jnp.int32, (TOP_K, 128), 0)
                    == lax.broadcasted_iota(jnp.int32, (TOP_K, 128), 1), 1.0, 0.0).astype(BF16)
    w1, w2, w3 = _split3(wn)
    wcol_ref[...] = _dot_tn(w1, eye) + _dot_tn(w2, eye) + _dot_tn(w3, eye)


def _router(x1, mod, npre2, w_router, router_bias):
    t, d = x1.shape
    n_e = w_router.shape[1]
    assert t <= (1 << CODE_SHIFT)
    tr = SEQ_TILE
    wrt = w_router.T
    wrh = wrt.astype(BF16)
    wrl = (wrt - wrh.astype(F32)).astype(BF16)
    rb = jnp.broadcast_to(router_bias.reshape(n_e, 1), (n_e, 128))
    args = (x1, mod, npre2.reshape(1, d), wrh, wrl, rb)
    in_specs = [pl.BlockSpec((tr, d), lambda i: (i, 0))] + [_const_spec(a.shape) for a in args[1:]]
    return pl.pallas_call(
        _router_kernel,
        out_shape=(jax.ShapeDtypeStruct((t, d // 2), PACKED),
                   jax.ShapeDtypeStruct((TOP_K, t), jnp.int32),
                   jax.ShapeDtypeStruct((t, 128), F32),
                   jax.ShapeDtypeStruct((n_e, 128), F32)),
        grid=(t // tr,),
        in_specs=in_specs,
        out_specs=(pl.BlockSpec((tr, d // 2), lambda i: (i, 0)),
                   pl.BlockSpec((TOP_K, tr), lambda i: (0, i)),
                   pl.BlockSpec((tr, 128), lambda i: (i, 0)),
                   pl.BlockSpec((n_e, 128), lambda i: (0, 0))),
        scratch_shapes=[pltpu.VMEM((n_e, 128), F32), pltpu.VMEM((TOP_K, tr), F32)],
        compiler_params=pltpu.CompilerParams(dimension_semantics=("arbitrary",),
                                             vmem_limit_bytes=VMEM_LIMIT_BYTES),
        name="moe_router",
    )(*args)


TAB_CHUNK_ROW, TAB_CHUNK_VALID, TAB_START, TAB_COUNT, TAB_CHUNK0, TAB_NCHUNK, TAB_TOTAL = range(7)
TAB_ROWS = 8


def _excl_cumsum_rows(lower, x):
    hi = jnp.floor(x * (1.0 / 128.0))
    lo = x - hi * 128.0
    return 128.0 * _dot(lower, hi.astype(BF16)) + _dot(lower, lo.astype(BF16))


def _table_kernel(cnt_ref, tab_ref, pad_ref, *, n_rows):
    n_e = cnt_ref.shape[0]
    n_g = tab_ref.shape[1]
    ch = float(EXPERT_CHUNK)
    cnt = jnp.floor((cnt_ref[...] + (ROW_ALIGN - 1)) * (1.0 / ROW_ALIGN)) * ROW_ALIGN
    r = lax.broadcasted_iota(jnp.int32, (n_e, n_e), 0)
    c = lax.broadcasted_iota(jnp.int32, (n_e, n_e), 1)
    lower = jnp.where(c < r, 1.0, 0.0).astype(BF16)
    starts = _excl_cumsum_rows(lower, cnt)
    start_col = starts[:, 0:1]

    nch = jnp.floor((cnt + (ch - 1.0)) * (1.0 / ch))
    chunk0 = _excl_cumsum_rows(lower, nch)
    cend_col = (chunk0 + nch)[:, 0:1]
    eye = r == c

    def as_row(col):
        return jnp.sum(jnp.where(eye, col, 0.0), axis=0, keepdims=True)

    g = lax.broadcasted_iota(jnp.int32, (n_e, n_g), 1).astype(F32)
    owner = jnp.sum(jnp.where(cend_col <= g, 1.0, 0.0), axis=0, keepdims=True)
    mine = lax.broadcasted_iota(jnp.int32, (n_e, n_g), 0).astype(F32) == owner
    g_row = lax.broadcasted_iota(jnp.int32, (1, n_g), 1).astype(F32)
    base = jnp.sum(jnp.where(mine, start_col - chunk0[:, 0:1] * ch, 0.0), axis=0, keepdims=True)
    left = jnp.sum(jnp.where(mine, cnt[:, 0:1] + chunk0[:, 0:1] * ch, 0.0), axis=0, keepdims=True)
    pad = jnp.zeros((1, n_g - n_e), F32)

    def wide(row):
        return jnp.concatenate([row, pad], axis=1)

    total = jnp.sum(nch[:, 0:1], axis=0, keepdims=True)
    tab_ref[...] = jnp.zeros(tab_ref.shape, jnp.int32)
    tab_ref[TAB_CHUNK_ROW:TAB_CHUNK_ROW + 1, :] = (base + g_row * ch).astype(jnp.int32)
    tab_ref[TAB_CHUNK_VALID:TAB_CHUNK_VALID + 1, :] = jnp.clip(left - g_row * ch, 0.0, ch).astype(jnp.int32)
    tab_ref[TAB_START:TAB_START + 1, :] = wide(as_row(start_col)).astype(jnp.int32)
    tab_ref[TAB_COUNT:TAB_COUNT + 1, :] = wide(as_row(cnt_ref[:, 0:1])).astype(jnp.int32)
    tab_ref[TAB_CHUNK0:TAB_CHUNK0 + 1, :] = wide(as_row(chunk0[:, 0:1])).astype(jnp.int32)
    tab_ref[TAB_NCHUNK:TAB_NCHUNK + 1, :] = wide(as_row(nch[:, 0:1])).astype(jnp.int32)
    tab_ref[TAB_TOTAL:TAB_TOTAL + 1, :] = jnp.broadcast_to(total, (1, n_g)).astype(jnp.int32)

    cols = pad_ref.shape[1]
    tail_cols = cols - n_e
    j_io = lax.broadcasted_iota(jnp.int32, (ROW_ALIGN, cols), 0).astype(F32)
    c_io = lax.broadcasted_iota(jnp.int32, (ROW_ALIGN, cols), 1).astype(F32)
    spare = float(n_rows) + j_io * cols + c_io
    zeros_tail = jnp.zeros((1, tail_cols), F32)
    raw_col = cnt_ref[:, 0:1]
    end_w = jnp.concatenate([as_row(start_col + raw_col), zeros_tail], axis=1)
    npad_w = jnp.concatenate([as_row(cnt[:, 0:1] - raw_col), zeros_tail], axis=1)
    used = jnp.sum(cnt[:, 0:1], axis=0, keepdims=True)
    after = used + j_io * tail_cols + (c_io - n_e)
    target = jnp.where(c_io < n_e,
                       jnp.where(j_io < npad_w, end_w + j_io, spare),
                       jnp.where(after < float(n_rows), after, spare))
    pad_ref[...] = target.astype(jnp.int32)


def _sorted_rows(t, n_e):
    return t * TOP_K + n_e * ROW_ALIGN + EXPERT_CHUNK + ROW_ALIGN


def _pad_list_cols(t, n_e):
    entries = n_e * ROW_ALIGN + (_sorted_rows(t, n_e) - t * TOP_K)
    per_round = SC_WORKERS * SC_GATHER_ROWS
    return pl.cdiv(entries, per_round) * per_round // ROW_ALIGN


def _route_table(cnt, t):
    n_e = cnt.shape[0]
    n_g = (t * TOP_K) // EXPERT_CHUNK + n_e
    cols = _pad_list_cols(t, n_e)
    return pl.pallas_call(
        functools.partial(_table_kernel, n_rows=_sorted_rows(t, n_e)),
        out_shape=(jax.ShapeDtypeStruct((TAB_ROWS, n_g), jnp.int32),
                   jax.ShapeDtypeStruct((ROW_ALIGN, cols), jnp.int32)),
        grid=(1,),
        in_specs=[_const_spec(cnt.shape)],
        out_specs=(_const_spec((TAB_ROWS, n_g)), _const_spec((ROW_ALIGN, cols))),
        compiler_params=pltpu.CompilerParams(dimension_semantics=("arbitrary",)),
        name="moe_table",
    )(cnt)


SC_CORES = 2
SC_SUBCORES = 16
SC_WORKERS = SC_CORES * SC_SUBCORES
SC_GATHER_ROWS = 64
COMBINE_SPLITS = 4


def _sc_mesh():
    return plsc.VectorSubcoreMesh(core_axis_name="c", subcore_axis_name="s")


def _sc_worker_id():
    return lax.axis_index("s") * SC_CORES + lax.axis_index("c")


SC_LANES = 16


def _sc_decode_rows(code_v, starts_v, idx_v):
    for i in range(code_v.shape[0] // SC_LANES):
        lanes = pl.ds(i * SC_LANES, SC_LANES)
        code = code_v[lanes]
        expert = lax.shift_right_logical(code, CODE_SHIFT)
        idx_v[lanes] = plsc.load_gather(starts_v, [expert]) + (code & ((1 << CODE_SHIFT) - 1))


def _sc_params():
    return pltpu.CompilerParams(needs_layout_passes=False)


def _sc_scatter_rows(rows, code_flat, starts, zero_list, n_out):
    t, w = rows.shape
    n_k = code_flat.shape[0] // t
    n_e = starts.shape[0]
    r = SC_GATHER_ROWS
    per = t // SC_WORKERS
    n_steps = per // r
    zero_rounds = zero_list.shape[0] // (SC_WORKERS * r)
    assert per % (2 * r) == 0 and zero_list.shape[0] % (SC_WORKERS * r) == 0 and zero_rounds <= n_k
    zeros = jnp.zeros((r, w), rows.dtype)

    @functools.partial(
        pl.kernel, mesh=_sc_mesh(), out_type=jax.ShapeDtypeStruct((n_out, w), rows.dtype),
        scratch_types=[pltpu.VMEM((n_e,), jnp.int32), pltpu.VMEM((n_k, r), jnp.int32),
                       pltpu.VMEM((n_k, r), jnp.int32), pltpu.VMEM((2, r, w), rows.dtype),
                       pltpu.SemaphoreType.DMA((2,)), pltpu.SemaphoreType.DMA],
        compiler_params=_sc_params(), name="moe_dispatch_sc")
    def scatter(rows_hbm, code_hbm, starts_hbm, zero_idx_hbm, zeros_hbm, out_hbm,
                starts_v, code_v, idx_v, rows_v, l_sem, s_sem):
        worker = _sc_worker_id()
        base = worker * per
        pltpu.sync_copy(starts_hbm, starts_v)

        def load(j, b):
            return pltpu.make_async_copy(rows_hbm.at[pl.ds(pl.multiple_of(base + j * r, ROW_ALIGN), r)],
                                         rows_v.at[b], l_sem.at[b])

        load(0, 0).start()
        pltpu.sync_copy(zeros_hbm, rows_v.at[1])
        zero_copies = []
        for z in range(zero_rounds):
            off = pl.multiple_of((worker * zero_rounds + z) * r, ROW_ALIGN)
            pltpu.sync_copy(zero_idx_hbm.at[pl.ds(off, r)], idx_v.at[z])
            zero_copies.append(pltpu.make_async_copy(rows_v.at[1], out_hbm.at[idx_v.at[z]], s_sem))
            zero_copies[-1].start()
        for cp in zero_copies:
            cp.wait()

        @pl.loop(0, n_steps, step=2)
        def _(j0):
            for b in range(2):
                j = j0 + b
                pl.when(j + 1 < n_steps)(load(j + 1, 1 - b).start)
                for k in range(n_k):
                    off = pl.multiple_of(k * t + base + j * r, ROW_ALIGN)
                    pltpu.sync_copy(code_hbm.at[pl.ds(off, r)], code_v.at[k])
                for k in range(n_k):
                    _sc_decode_rows(code_v.at[k], starts_v, idx_v.at[k])
                load(j, b).wait()
                copies = [pltpu.make_async_copy(rows_v.at[b], out_hbm.at[idx_v.at[k]], s_sem) for k in range(n_k)]
                for cp in copies:
                    cp.start()
                for cp in copies:
                    cp.wait()

    return scatter(rows, code_flat, starts, zero_list, zeros)


def _sc_gather_rows(table, code_flat, starts):
    n = code_flat.shape[0]
    w = table.shape[1]
    n_e = starts.shape[0]
    r = SC_GATHER_ROWS
    per = n // SC_WORKERS
    n_steps = per // r
    assert per % (2 * r) == 0

    @functools.partial(
        pl.kernel, mesh=_sc_mesh(), out_type=jax.ShapeDtypeStruct((n, w), table.dtype),
        scratch_types=[pltpu.VMEM((n_e,), jnp.int32), pltpu.VMEM((r,), jnp.int32), pltpu.VMEM((2, r), jnp.int32),
                       pltpu.VMEM((2, r, w), table.dtype),
                       pltpu.SemaphoreType.DMA((2,)), pltpu.SemaphoreType.DMA((2,))],
        compiler_params=_sc_params(), name="moe_gather_sc")
    def gather(table_hbm, code_hbm, starts_hbm, out_hbm, starts_v, code_v, idx_v, rows_v, g_sem, w_sem):
        base = _sc_worker_id() * per
        pltpu.sync_copy(starts_hbm, starts_v)

        def at(j):
            return pl.ds(pl.multiple_of(base + j * r, ROW_ALIGN), r)

        def fetch(j, b):
            return pltpu.make_async_copy(table_hbm.at[idx_v.at[b]], rows_v.at[b], g_sem.at[b])

        def start_fetch(j, b):
            pltpu.sync_copy(code_hbm.at[at(j)], code_v)
            _sc_decode_rows(code_v, starts_v, idx_v.at[b])
            fetch(j, b).start()

        def write(j, b):
            return pltpu.make_async_copy(rows_v.at[b], out_hbm.at[at(j)], w_sem.at[b])

        start_fetch(0, 0)

        @pl.loop(0, n_steps, step=2)
        def _(j0):
            for b in range(2):
                j = j0 + b

                @pl.when(j + 1 < n_steps)
                def _():
                    pl.when(j >= 1)(write(j - 1, 1 - b).wait)
                    start_fetch(j + 1, 1 - b)

                fetch(j, b).wait()
                write(j, b).start()

        write(n_steps - 2, 0).wait()
        write(n_steps - 1, 1).wait()

    return gather(table, code_flat, starts)


def _dispatch(code, starts, pad_list, hp):
    t = hp.shape[0]
    n_out = _sorted_rows(t, starts.shape[0]) + pad_list.size
    return _sc_scatter_rows(hp, code.reshape(-1), starts, pad_list.reshape(-1), n_out)


X_LOOKAHEAD = 2
X_SLOTS = X_LOOKAHEAD + 1
Y_SLOTS = 2


def _expert_kernel(tab_ref, xs_hbm, wgu_ref, wd_ref, y_hbm,
                   wd16_ref, xbuf_ref, ybuf_ref, in_sem, out_sem, *, n_e):
    e = pl.program_id(0)
    ch, half = xbuf_ref.shape[1:]
    ff = wd_ref.shape[1]
    n_total = tab_ref[TAB_TOTAL, 0]
    first = tab_ref[TAB_CHUNK0, e]
    n_chunks = tab_ref[TAB_NCHUNK, e]

    def in_copy(g):
        slot = g % X_SLOTS
        row = pl.multiple_of(tab_ref[TAB_CHUNK_ROW, g], ROW_ALIGN)
        return pltpu.make_async_copy(xs_hbm.at[pl.ds(row, ch), :], xbuf_ref.at[slot], in_sem.at[slot])

    def out_copies(g):
        slot = g % Y_SLOTS
        row = tab_ref[TAB_CHUNK_ROW, g]
        valid = tab_ref[TAB_CHUNK_VALID, g]
        yield valid >= ch, pltpu.make_async_copy(ybuf_ref.at[slot], y_hbm.at[pl.ds(pl.multiple_of(row, ROW_ALIGN), ch), :],
                                                 out_sem.at[slot])
        size = ch // 2
        while size >= ROW_ALIGN:
            off = pl.multiple_of((valid // (2 * size)) * (2 * size), ROW_ALIGN)
            cond = (valid < ch) & ((valid & size) != 0)
            yield cond, pltpu.make_async_copy(ybuf_ref.at[slot, pl.ds(off, size), :],
                                              y_hbm.at[pl.ds(pl.multiple_of(row + off, ROW_ALIGN), size), :],
                                              out_sem.at[slot])
            size //= 2

    def start_out(g):
        for cond, cp in out_copies(g):
            pl.when(cond)(cp.start)

    def wait_out(g):
        for cond, cp in out_copies(g):
            pl.when(cond)(cp.wait)

    @pl.when(e == 0)
    def _():
        for j in range(X_LOOKAHEAD):
            pl.when(j < n_total)(in_copy(j).start)

    @pl.when(n_chunks > 0)
    def _():
        wd16_ref[...] = wd_ref[0].astype(BF16)

    def chunk(c, carry):
        g = first + c
        pl.when(g + X_LOOKAHEAD < n_total)(in_copy(g + X_LOOKAHEAD).start)
        in_copy(g).wait()
        pl.when(g >= Y_SLOTS)(lambda: wait_out(g - Y_SLOTS))
        valid = tab_ref[TAB_CHUNK_VALID, g]

        def ffn(rows):
            xa, xb = _unpack_bf16_pairs(xbuf_ref[g % X_SLOTS, 0:rows, :])
            hgu = (_dot(xa.astype(BF16), wgu_ref[0, 0:half, :])
                   + _dot(xb.astype(BF16), wgu_ref[0, half:2 * half, :]))
            act = _silu(hgu[:, :ff]) * hgu[:, ff:]
            y = _dot(act.astype(BF16), wd16_ref[...])
            ybuf_ref[g % Y_SLOTS, 0:rows, :] = _pack_bf16_pairs(y)

        lo = 0
        for rows in EXPERT_ROW_STEPS:
            pl.when((valid > lo) & (valid <= rows))(functools.partial(ffn, rows))
            lo = rows
        start_out(g)
        return carry

    lax.fori_loop(0, n_chunks, chunk, 0)

    @pl.when(e == n_e - 1)
    def _():
        for j in range(Y_SLOTS, 0, -1):
            pl.when(n_total >= j)(lambda j=j: wait_out(n_total - j))
        n_rows = y_hbm.shape[0]
        used = tab_ref[TAB_START, e] + ((tab_ref[TAB_COUNT, e] + (ROW_ALIGN - 1)) // ROW_ALIGN) * ROW_ALIGN
        ybuf_ref[0] = jnp.zeros((ch, half), PACKED)
        for j in range(pl.cdiv(n_e * ROW_ALIGN + EXPERT_CHUNK + ROW_ALIGN, ch)):
            at = pl.multiple_of(jnp.minimum(used + j * ch, n_rows - ch), ROW_ALIGN)
            tail = pltpu.make_async_copy(ybuf_ref.at[0], y_hbm.at[pl.ds(at, ch), :], out_sem.at[0])
            tail.start()
            tail.wait()


def _experts(tab, xs, wgu16, w_down_e, p):
    half = xs.shape[1]
    n_e, ff, d = w_down_e.shape
    ch = EXPERT_CHUNK
    grid_spec = pltpu.PrefetchScalarGridSpec(
        num_scalar_prefetch=1,
        grid=(n_e,),
        in_specs=[pl.BlockSpec(memory_space=pl.ANY),
                  pl.BlockSpec((1, d, 2 * ff), lambda e, *_: (e, 0, 0)),
                  pl.BlockSpec((1, ff, d), lambda e, *_: (e, 0, 0))],
        out_specs=pl.BlockSpec(memory_space=pl.ANY),
        scratch_shapes=[pltpu.VMEM((ff, d), BF16),
                        pltpu.VMEM((X_SLOTS, ch, half), PACKED), pltpu.VMEM((Y_SLOTS, ch, half), PACKED),
                        pltpu.SemaphoreType.DMA((X_SLOTS,)), pltpu.SemaphoreType.DMA((Y_SLOTS,))],
    )
    return pl.pallas_call(
        functools.partial(_expert_kernel, n_e=n_e),
        out_shape=jax.ShapeDtypeStruct((p, half), PACKED),
        grid_spec=grid_spec,
        compiler_params=pltpu.CompilerParams(dimension_semantics=("arbitrary",),
                                             vmem_limit_bytes=VMEM_LIMIT_BYTES),
        name="moe_experts",
    )(tab, xs, wgu16, w_down_e)


def _combine_kernel(yg_ref, wcol_ref, x1_ref, mod_ref, npre_ref, npost_ref, wgus_ref, wds_ref, o_ref):
    te, d = x1_ref.shape
    half = d // 2
    ff = wds_ref.shape[0]

    x1 = x1_ref[...]
    h2 = _rms(x1, npre_ref[...]) * (1.0 + mod_ref[:, 4 * d:5 * d]) + mod_ref[:, 3 * d:4 * d]
    hgu = _dot(h2.astype(BF16), wgus_ref[...])
    act = _silu(hgu[:, :ff]) * hgu[:, ff:]
    shared = _dot(act.astype(BF16), wds_ref[...])

    acc_a = shared[:, :half]
    acc_b = shared[:, half:]
    for k in range(TOP_K):
        ya, yb = _unpack_bf16_pairs(yg_ref[k])
        wk = wcol_ref[:, k:k + 1]
        acc_a = acc_a + wk * ya
        acc_b = acc_b + wk * yb
    moe = jnp.concatenate([acc_a, acc_b], axis=-1)
    g2 = mod_ref[:, 5 * d:6 * d]
    o_ref[...] = x1 + g2 * _rms(moe, npost_ref[...])


def _combine(code, starts, y, wcol, x1, mod, npre2, npost2, w_gate_s, w_up_s, w_down_s):
    t, d = x1.shape
    half = d // 2
    te = SEQ_TILE
    wgus = jnp.concatenate([w_gate_s, w_up_s], axis=1).astype(BF16)
    wds = w_down_s.astype(BF16)
    n_split = COMBINE_SPLITS if t % (COMBINE_SPLITS * 2 * SC_GATHER_ROWS * SC_WORKERS // TOP_K) == 0 else 1
    tq = t // n_split
    steps = tq // te
    stream = x1
    for q in range(n_split):
        code_q = code[:, q * tq:(q + 1) * tq]
        yg = _sc_gather_rows(y, code_q.reshape(-1), starts).reshape(TOP_K, tq, half)
        row = lambda i, q=q: (q * steps + i, 0)
        stream = pl.pallas_call(
            _combine_kernel,
            out_shape=jax.ShapeDtypeStruct((t, d), F32),
            grid=(steps,),
            in_specs=[pl.BlockSpec((TOP_K, te, half), lambda i: (0, i, 0)),
                      pl.BlockSpec((te, 128), row),
                      pl.BlockSpec((te, d), row),
                      _const_spec(mod.shape),
                      _const_spec((1, d)),
                      _const_spec((1, d)),
                      _const_spec(wgus.shape),
                      _const_spec(wds.shape)],
            out_specs=pl.BlockSpec((te, d), row),
            input_output_aliases={2: 0},
            compiler_params=pltpu.CompilerParams(dimension_semantics=("arbitrary",),
                                                 vmem_limit_bytes=VMEM_LIMIT_BYTES),
            name="moe_combine",
        )(yg, wcol, stream, mod, npre2.reshape(1, d), npost2.reshape(1, d), wgus, wds)
    return stream


def kernel(x, c, w_ada, b_ada, norm_pre_mix, norm_post_mix, w_in, conv_w, w_conv_out, hgrn_lower_bounds, hgrn_norm_w, w_hgrn_out, w_o, norm_pre_ffn, norm_post_ffn, w_router, router_bias, w_gate_e, w_up_e, w_down_e, w_gate_s, w_up_s, w_down_s):
    bsz, seq, d = x.shape
    assert bsz == 1 and w_ada.shape[0] == 1
    mod, w_in16 = _ada_mod(c, w_ada[0], b_ada[0], w_in[0])
    x1, wgu16 = _mixer(x.reshape(seq, d), mod, norm_pre_mix[0], norm_post_mix[0], w_in16, conv_w[0],
                       w_conv_out[0], hgrn_lower_bounds, hgrn_norm_w[0], w_hgrn_out[0], w_o[0],
                       w_gate_e[0], w_up_e[0])
    hp, code, wcol, cnt = _router(x1, mod, norm_pre_ffn[0], w_router[0], router_bias[0])
    n_e = w_router.shape[-1]
    tab, pad_list = _route_table(cnt, seq)
    starts = tab[TAB_START, :n_e]
    xs = _dispatch(code, starts, pad_list, hp)
    y = _experts(tab, xs, wgu16, w_down_e[0], _sorted_rows(seq, n_e))
    out = _combine(code, starts, y, wcol, x1, mod, norm_pre_ffn[0], norm_post_ffn[0],
                   w_gate_s[0], w_up_s[0], w_down_s[0])
    return out.reshape(bsz, seq, d)
```

```python
import functools

import jax
import jax.numpy as jnp
from jax import lax
from jax.experimental import pallas as pl
from jax.experimental.pallas import tpu as pltpu
from jax.experimental.pallas import tpu_sc as plsc

F32 = jnp.float32
BF16 = jnp.bfloat16

NORM_EPS = 1e-6
CONV_DIM = 512
HGRN_HEADS = 4
HGRN_DK = 128
HGRN_QK = HGRN_HEADS * HGRN_DK
N_GROUPS = 8
TOPK_GROUPS = 4
TOP_K = 8
ROUTED_SCALE = 2.5
CODE_SHIFT = 16

SEQ_TILE = 256
ROUTER_TILE = 512
EXPERT_CHUNK = 512
EXPERT_ROW_STEPS = (128, 256, EXPERT_CHUNK)
ROW_ALIGN = 8
VMEM_LIMIT_BYTES = 56 * 1024 * 1024


def _dot(a, b):
    return jnp.dot(a, b, preferred_element_type=F32)


def _dot_nt(a, b):
    return lax.dot_general(a, b, (((1,), (1,)), ((), ())), preferred_element_type=F32)


def _dot_tn(a, b):
    return lax.dot_general(a, b, (((0,), (0,)), ((), ())), preferred_element_type=F32)


def _split3(x):
    hi = x.astype(BF16)
    r1 = x - hi.astype(F32)
    mid = r1.astype(BF16)
    lo = (r1 - mid.astype(F32)).astype(BF16)
    return hi, mid, lo


def _sigmoid(x):
    return 1.0 / (1.0 + jnp.exp(-x))


def _silu(x):
    return x * _sigmoid(x)


def _rms(x, w):
    ms = jnp.mean(x * x, axis=-1, keepdims=True)
    return x * lax.rsqrt(ms + NORM_EPS) * w


def _ada_kernel(c_ref, w_ref, b_ref, o_ref):
    cs = _silu(c_ref[...])
    h1, h2, h3 = _split3(cs)
    w1, w2, w3 = _split3(w_ref[...])
    r = cs.shape[0]
    stacked = jnp.concatenate([h.astype(F32) for h in (h1, h2, h3)], axis=0).astype(BF16)
    p1 = _dot(stacked, w1)
    p2 = _dot(stacked[0:2 * r], w2)
    p3 = _dot(h1, w3)
    acc = p1[0:r] + (p2[0:r] + p1[r:2 * r]) + (p3 + p2[r:2 * r] + p1[2 * r:3 * r])
    o_ref[...] = acc + b_ref[...]


def _ada_mod(c, w_ada, b_ada):
    d = c.shape[-1]
    n = w_ada.shape[-1]
    bn = 1024
    c8 = jnp.broadcast_to(c.reshape(1, d), (8, d))
    out = pl.pallas_call(
        _ada_kernel,
        out_shape=jax.ShapeDtypeStruct((8, n), F32),
        grid=(n // bn,),
        in_specs=[pl.BlockSpec((8, d), lambda j: (0, 0)),
                  pl.BlockSpec((d, bn), lambda j: (0, j)),
                  pl.BlockSpec((1, bn), lambda j: (0, j))],
        out_specs=pl.BlockSpec((8, bn), lambda j: (0, j)),
        compiler_params=pltpu.CompilerParams(dimension_semantics=("arbitrary",),
                                             vmem_limit_bytes=VMEM_LIMIT_BYTES),
        name="ada_mod",
    )(c8, w_ada, b_ada.reshape(1, n))
    return out[0:1]


def _level_reference(b, rolled, s, row):
    ts = b.shape[0]
    c = s // 2 - 1
    if s >= 16:
        pieces = []
        for blk in range(ts // s):
            r = blk * s + c
            pieces.append(jnp.broadcast_to(b[r:r + 1, :], (s, b.shape[1])))
        return pieces[0] if len(pieces) == 1 else jnp.concatenate(pieces, axis=0)
    pos = row & (s - 1)
    out = b
    for p in range(s):
        if p == c:
            continue
        out = jnp.where(pos == p, rolled[p - c], out)
    return out


def _cast_expert_weights(wg_ref, wu_ref, wgu16_ref):
    ff = wg_ref.shape[2]
    for j in range(wg_ref.shape[0]):
        wgu16_ref[j, :, 0:ff] = wg_ref[j].astype(BF16)
        wgu16_ref[j, :, ff:2 * ff] = wu_ref[j].astype(BF16)


def _mixer_kernel(x_ref, mod_ref, npre_ref, npost_ref, win_ref, convw_ref, wco_ref,
                  hlb_ref, hnw_ref, who_ref, wo_ref, wg_ref, wu_ref,
                  o_ref, wgu16_ref, state_ref, ubuf_ref):
    ts, d = x_ref.shape
    step = pl.program_id(0)
    _cast_expert_weights(wg_ref, wu_ref, wgu16_ref)

    @pl.when(step == 0)
    def _():
        state_ref[...] = jnp.zeros_like(state_ref)
        ubuf_ref[0:8, :] = jnp.zeros((8, CONV_DIM), F32)

    x = x_ref[...]
    sh1 = mod_ref[:, 0:d]
    sc1 = mod_ref[:, d:2 * d]
    g1 = mod_ref[:, 2 * d:3 * d]
    h = (_rms(x, npre_ref[...]) * (1.0 + sc1) + sh1).astype(BF16)

    def proj(lo, width):
        return _dot(h, win_ref[:, lo:lo + width])

    c0 = 0
    cb = proj(c0, CONV_DIM)
    u = proj(c0 + CONV_DIM, CONV_DIM) * proj(c0 + 2 * CONV_DIM, CONV_DIM)
    ubuf_ref[8:8 + ts, :] = u
    conv = (ubuf_ref[6:6 + ts, :] * convw_ref[0:1, :]
            + ubuf_ref[7:7 + ts, :] * convw_ref[1:2, :]
            + u * convw_ref[2:3, :])
    ubuf_ref[0:8, :] = ubuf_ref[ts:ts + 8, :]
    y_a = _dot((cb * conv).astype(BF16), wco_ref[...])

    c1 = 3 * CONV_DIM
    q = _silu(proj(c1, HGRN_QK))
    hl = hlb_ref[...]
    hm = jnp.max(hl, axis=0, keepdims=True)
    he = jnp.exp(hl - hm)
    lb = he[0:1, :] / jnp.sum(he, axis=0, keepdims=True)
    fg = lb + (1.0 - lb) * _sigmoid(proj(c1 + HGRN_QK, HGRN_QK))
    k = 1.0 - fg
    g = jnp.log(fg)
    v = proj(c1 + 2 * HGRN_QK, HGRN_QK)
    gg = proj(c1 + 3 * HGRN_QK, HGRN_QK)

    row = lax.broadcasted_iota(jnp.int32, (ts, ts), 0)
    col = lax.broadcasted_iota(jnp.int32, (ts, ts), 1)
    tril = jnp.where(row >= col, 1.0, 0.0).astype(BF16)
    ghi, gmid, glo = _split3(g)
    b = _dot(tril, ghi) + _dot(tril, gmid) + _dot(tril, glo)
    b_last = b[ts - 1:ts, :]

    rowq = lax.broadcasted_iota(jnp.int32, (ts, HGRN_QK), 0)
    rolled = {sft: pltpu.roll(b, sft % ts, axis=0) for sft in (-3, -2, -1, 1, 2, 3, 4)}
    xor = row ^ col

    levels = []
    s = ts
    while s >= 2:
        levels.append(s)
        s //= 2
    qt, kt = [], []
    for s in levels:
        bref = _level_reference(b, rolled, s, rowq)
        e = jnp.exp(-jnp.abs(b - bref))
        upper = (rowq & (s // 2)) != 0
        qt.append(jnp.where(upper, q * e, 0.0).astype(BF16))
        kt.append(jnp.where(upper, 0.0, k * e).astype(BF16))
    q_in = (q * jnp.exp(b)).astype(BF16)
    k_out = (k * jnp.exp(b_last - b)).astype(BF16)
    v16 = v.astype(BF16)
    qk = q * k
    decay_last = jnp.exp(b_last)

    hnw = hnw_ref[...]
    outs = []
    for hd in range(HGRN_HEADS):
        sl = slice(hd * HGRN_DK, (hd + 1) * HGRN_DK)
        a = jnp.zeros((ts, ts), F32)
        for li in range(len(levels) - 1, -1, -1):
            s = levels[li]
            p = _dot_nt(qt[li][:, sl], kt[li][:, sl])
            a = jnp.where(xor >= s // 2, p, a)
        st = state_ref[hd]
        o_h = (_dot(a.astype(BF16), v16[:, sl])
               + jnp.sum(qk[:, sl], axis=-1, keepdims=True) * v[:, sl]
               + _dot_nt(q_in[:, sl], st.astype(BF16)))
        state_ref[hd] = st * decay_last[:, sl] + _dot_tn(v16[:, sl], k_out[:, sl])
        outs.append(_rms(o_h, hnw))
    o = jnp.concatenate(outs, axis=-1) * _silu(gg)
    y_b = _dot(o.astype(BF16), who_ref[...])

    c2 = c1 + 4 * HGRN_QK
    m = _sigmoid(proj(c2, d)) * y_a + _sigmoid(proj(c2 + d, d)) * y_b
    y = _dot(m.astype(BF16), wo_ref[...])
    o_ref[...] = x + g1 * _rms(y, npost_ref[...])


def _const_spec(shape):
    nd = len(shape)
    return pl.BlockSpec(shape, lambda i: (0,) * nd)


def _mixer(x2, mod, npre, npost, w_in, conv_w, w_conv_out, hlb, hnw, w_hgrn_out, w_o, w_gate_e, w_up_e):
    t, d = x2.shape
    ts = SEQ_TILE
    n = t // ts
    n_e, _, ff = w_gate_e.shape
    assert n_e % n == 0 and conv_w.shape == (3, CONV_DIM)
    per = n_e // n
    args = (x2, mod, npre.reshape(1, d), npost.reshape(1, d), w_in.astype(BF16), conv_w,
            w_conv_out.astype(BF16), hlb, hnw.reshape(1, -1), w_hgrn_out.astype(BF16),
            w_o.astype(BF16))
    experts = lambda i: (i, 0, 0)
    in_specs = ([pl.BlockSpec((ts, d), lambda i: (i, 0))] + [_const_spec(a.shape) for a in args[1:]]
                + [pl.BlockSpec((per, d, ff), experts)] * 2)
    return pl.pallas_call(
        _mixer_kernel,
        out_shape=[jax.ShapeDtypeStruct((t, d), F32), jax.ShapeDtypeStruct((n_e, d, 2 * ff), BF16)],
        grid=(n,),
        in_specs=in_specs,
        out_specs=[pl.BlockSpec((ts, d), lambda i: (i, 0)), pl.BlockSpec((per, d, 2 * ff), experts)],
        scratch_shapes=[pltpu.VMEM((HGRN_HEADS, HGRN_DK, HGRN_DK), F32),
                        pltpu.VMEM((ts + 8, CONV_DIM), F32)],
        compiler_params=pltpu.CompilerParams(dimension_semantics=("arbitrary",),
                                             vmem_limit_bytes=VMEM_LIMIT_BYTES),
        name="token_mixer",
    )(*args, w_gate_e, w_up_e)


PACKED = jnp.int32


def _pack_bf16_pairs(x):
    m = x.shape[1] // 2
    hi = lax.bitcast_convert_type(x[:, :m].astype(BF16).astype(F32), jnp.uint32)
    lo = lax.bitcast_convert_type(x[:, m:].astype(BF16).astype(F32), jnp.uint32)
    return lax.bitcast_convert_type(hi | (lo >> 16), PACKED)


def _unpack_bf16_pairs(p):
    p = lax.bitcast_convert_type(p, jnp.uint32)
    hi = lax.bitcast_convert_type(p & jnp.uint32(0xFFFF0000), F32)
    lo = lax.bitcast_convert_type(p << 16, F32)
    return hi, lo


def _router_kernel(x1_ref, mod_ref, npre_ref, wrh_ref, wrl_ref, rb_ref,
                   hp_ref, code_ref, wcol_ref, cnt_ref, carry_ref, rows_ref):
    assert N_GROUPS == TOP_K == rows_ref.shape[0]
    tr, d = x1_ref.shape
    n_e = wrh_ref.shape[0]
    gsz = n_e // N_GROUPS
    neg = -jnp.inf

    @pl.when(pl.program_id(0) == 0)
    def _():
        carry_ref[...] = jnp.zeros_like(carry_ref)

    sh2 = mod_ref[:, 3 * d:4 * d]
    sc2 = mod_ref[:, 4 * d:5 * d]
    h2 = _rms(x1_ref[...], npre_ref[...]) * (1.0 + sc2) + sh2
    hp_ref[...] = _pack_bf16_pairs(h2)
    h_hi = h2.astype(BF16)
    h_lo = (h2 - h_hi.astype(F32)).astype(BF16)
    wrh = wrh_ref[...]
    logits = _dot_nt(wrh, h_hi) + (_dot_nt(wrh, h_lo) + _dot_nt(wrl_ref[...], h_hi))
    scores = _sigmoid(logits)
    sel = scores + rb_ref[:, 0:1]

    io_e = lax.broadcasted_iota(jnp.int32, (n_e, tr), 0)
    for g in range(N_GROUPS):
        blk = sel[g * gsz:(g + 1) * gsz, :]
        io = lax.broadcasted_iota(jnp.int32, (gsz, tr), 0) + g * gsz
        m1 = jnp.max(blk, axis=0, keepdims=True)
        i1 = jnp.min(jnp.where(blk == m1, io, n_e), axis=0, keepdims=True)
        m2 = jnp.max(jnp.where(io == i1, neg, blk), axis=0, keepdims=True)
        rows_ref[g:g + 1, :] = m1 + m2
    gs = rows_ref[...]
    io_g = lax.broadcasted_iota(jnp.int32, (N_GROUPS, tr), 0)
    gsel = jnp.zeros((N_GROUPS, tr), F32)
    for _ in range(TOPK_GROUPS):
        m = jnp.max(gs, axis=0, keepdims=True)
        gi = jnp.min(jnp.where(gs == m, io_g, N_GROUPS), axis=0, keepdims=True)
        hit = io_g == gi
        gsel = jnp.where(hit, 1.0, gsel)
        gs = jnp.where(hit, neg, gs)
    rows_ref[...] = gsel
    cur = jnp.concatenate(
        [jnp.where(rows_ref[g:g + 1, :] > 0.5, sel[g * gsz:(g + 1) * gsz, :], neg) for g in range(N_GROUPS)],
        axis=0)

    idxs = []
    selm = jnp.zeros((n_e, tr), F32)
    for k in range(TOP_K):
        m = jnp.max(cur, axis=0, keepdims=True)
        idx = jnp.min(jnp.where(cur == m, io_e, n_e), axis=0, keepdims=True)
        hit = io_e == idx
        rows_ref[k:k + 1, :] = jnp.sum(jnp.where(hit, scores, 0.0), axis=0, keepdims=True)
        cur = jnp.where(hit, neg, cur)
        selm = jnp.where(hit, 1.0, selm)
        idxs.append(idx)

    r_io = lax.broadcasted_iota(jnp.int32, (tr, tr + 128), 0)
    c_io = lax.broadcasted_iota(jnp.int32, (tr, tr + 128), 1)
    before = jnp.where((r_io < c_io) | (c_io >= tr), 1.0, 0.0).astype(BF16)
    r_ext = _dot(selm.astype(BF16), before)
    rank_full = r_ext[:, :tr] + carry_ref[:, 0:1]
    for k, idx in enumerate(idxs):
        rk = jnp.sum(jnp.where(io_e == idx, rank_full, 0.0), axis=0, keepdims=True)
        code_ref[k:k + 1, :] = (idx << CODE_SHIFT) | rk.astype(jnp.int32)
    carry_ref[...] = carry_ref[...] + r_ext[:, tr:]
    cnt_ref[...] = carry_ref[...]

    wk = rows_ref[...]
    wn = wk / (jnp.sum(wk, axis=0, keepdims=True) + 1e-20) * ROUTED_SCALE
    eye = jnp.where(lax.broadcasted_iota(jnp.int32, (TOP_K, 128), 0)
                    == lax.broadcasted_iota(jnp.int32, (TOP_K, 128), 1), 1.0, 0.0).astype(BF16)
    w1, w2, w3 = _split3(wn)
    wcol_ref[...] = _dot_tn(w1, eye) + _dot_tn(w2, eye) + _dot_tn(w3, eye)


def _router(x1, mod, npre2, w_router, router_bias):
    t, d = x1.shape
    n_e = w_router.shape[1]
    assert t <= (1 << CODE_SHIFT)
    tr = ROUTER_TILE if t % ROUTER_TILE == 0 else SEQ_TILE
    wrt = w_router.T
    wrh = wrt.astype(BF16)
    wrl = (wrt - wrh.astype(F32)).astype(BF16)
    rb = jnp.broadcast_to(router_bias.reshape(n_e, 1), (n_e, 128))
    args = (x1, mod, npre2.reshape(1, d), wrh, wrl, rb)
    in_specs = [pl.BlockSpec((tr, d), lambda i: (i, 0))] + [_const_spec(a.shape) for a in args[1:]]
    return pl.pallas_call(
        _router_kernel,
        out_shape=(jax.ShapeDtypeStruct((t, d // 2), PACKED),
                   jax.ShapeDtypeStruct((TOP_K, t), jnp.int32),
                   jax.ShapeDtypeStruct((t, 128), F32),
                   jax.ShapeDtypeStruct((n_e, 128), F32)),
        grid=(t // tr,),
        in_specs=in_specs,
        out_specs=(pl.BlockSpec((tr, d // 2), lambda i: (i, 0)),
                   pl.BlockSpec((TOP_K, tr), lambda i: (0, i)),
                   pl.BlockSpec((tr, 128), lambda i: (i, 0)),
                   pl.BlockSpec((n_e, 128), lambda i: (0, 0))),
        scratch_shapes=[pltpu.VMEM((n_e, 128), F32), pltpu.VMEM((TOP_K, tr), F32)],
        compiler_params=pltpu.CompilerParams(dimension_semantics=("arbitrary",),
                                             vmem_limit_bytes=VMEM_LIMIT_BYTES),
        name="moe_router",
    )(*args)


TAB_CHUNK_ROW, TAB_CHUNK_VALID, TAB_START, TAB_COUNT, TAB_CHUNK0, TAB_NCHUNK, TAB_TOTAL = range(7)
TAB_ROWS = 8


def _excl_cumsum_rows(lower, x):
    hi = jnp.floor(x * (1.0 / 128.0))
    lo = x - hi * 128.0
    return 128.0 * _dot(lower, hi.astype(BF16)) + _dot(lower, lo.astype(BF16))


def _table_kernel(cnt_ref, tab_ref, pad_ref, *, n_rows):
    n_e = cnt_ref.shape[0]
    n_g = tab_ref.shape[1]
    ch = float(EXPERT_CHUNK)
    cnt = jnp.floor((cnt_ref[...] + (ROW_ALIGN - 1)) * (1.0 / ROW_ALIGN)) * ROW_ALIGN
    r = lax.broadcasted_iota(jnp.int32, (n_e, n_e), 0)
    c = lax.broadcasted_iota(jnp.int32, (n_e, n_e), 1)
    lower = jnp.where(c < r, 1.0, 0.0).astype(BF16)
    starts = _excl_cumsum_rows(lower, cnt)
    start_col = starts[:, 0:1]

    nch = jnp.floor((cnt + (ch - 1.0)) * (1.0 / ch))
    chunk0 = _excl_cumsum_rows(lower, nch)
    cend_col = (chunk0 + nch)[:, 0:1]
    eye = r == c

    def as_row(col):
        return jnp.sum(jnp.where(eye, col, 0.0), axis=0, keepdims=True)

    g = lax.broadcasted_iota(jnp.int32, (n_e, n_g), 1).astype(F32)
    owner = jnp.sum(jnp.where(cend_col <= g, 1.0, 0.0), axis=0, keepdims=True)
    mine = lax.broadcasted_iota(jnp.int32, (n_e, n_g), 0).astype(F32) == owner
    g_row = lax.broadcasted_iota(jnp.int32, (1, n_g), 1).astype(F32)
    base = jnp.sum(jnp.where(mine, start_col - chunk0[:, 0:1] * ch, 0.0), axis=0, keepdims=True)
    left = jnp.sum(jnp.where(mine, cnt[:, 0:1] + chunk0[:, 0:1] * ch, 0.0), axis=0, keepdims=True)
    pad = jnp.zeros((1, n_g - n_e), F32)

    def wide(row):
        return jnp.concatenate([row, pad], axis=1)

    total = jnp.sum(nch[:, 0:1], axis=0, keepdims=True)
    tab_ref[...] = jnp.zeros(tab_ref.shape, jnp.int32)
    tab_ref[TAB_CHUNK_ROW:TAB_CHUNK_ROW + 1, :] = (base + g_row * ch).astype(jnp.int32)
    tab_ref[TAB_CHUNK_VALID:TAB_CHUNK_VALID + 1, :] = jnp.clip(left - g_row * ch, 0.0, ch).astype(jnp.int32)
    tab_ref[TAB_START:TAB_START + 1, :] = wide(as_row(start_col)).astype(jnp.int32)
    tab_ref[TAB_COUNT:TAB_COUNT + 1, :] = wide(as_row(cnt_ref[:, 0:1])).astype(jnp.int32)
    tab_ref[TAB_CHUNK0:TAB_CHUNK0 + 1, :] = wide(as_row(chunk0[:, 0:1])).astype(jnp.int32)
    tab_ref[TAB_NCHUNK:TAB_NCHUNK + 1, :] = wide(as_row(nch[:, 0:1])).astype(jnp.int32)
    tab_ref[TAB_TOTAL:TAB_TOTAL + 1, :] = jnp.broadcast_to(total, (1, n_g)).astype(jnp.int32)

    cols = pad_ref.shape[1]
    tail_cols = cols - n_e
    j_io = lax.broadcasted_iota(jnp.int32, (ROW_ALIGN, cols), 0).astype(F32)
    c_io = lax.broadcasted_iota(jnp.int32, (ROW_ALIGN, cols), 1).astype(F32)
    spare = float(n_rows) + j_io * cols + c_io
    zeros_tail = jnp.zeros((1, tail_cols), F32)
    raw_col = cnt_ref[:, 0:1]
    end_w = jnp.concatenate([as_row(start_col + raw_col), zeros_tail], axis=1)
    npad_w = jnp.concatenate([as_row(cnt[:, 0:1] - raw_col), zeros_tail], axis=1)
    used = jnp.sum(cnt[:, 0:1], axis=0, keepdims=True)
    after = used + j_io * tail_cols + (c_io - n_e)
    target = jnp.where(c_io < n_e,
                       jnp.where(j_io < npad_w, end_w + j_io, spare),
                       jnp.where(after < float(n_rows), after, spare))
    pad_ref[...] = target.astype(jnp.int32)


def _sorted_rows(t, n_e):
    return t * TOP_K + n_e * ROW_ALIGN + EXPERT_CHUNK + ROW_ALIGN


def _pad_list_cols(t, n_e):
    entries = n_e * ROW_ALIGN + (_sorted_rows(t, n_e) - t * TOP_K)
    per_round = SC_WORKERS * SC_GATHER_ROWS
    return pl.cdiv(entries, per_round) * per_round // ROW_ALIGN


def _route_table(cnt, t):
    n_e = cnt.shape[0]
    n_g = (t * TOP_K) // EXPERT_CHUNK + n_e
    cols = _pad_list_cols(t, n_e)
    return pl.pallas_call(
        functools.partial(_table_kernel, n_rows=_sorted_rows(t, n_e)),
        out_shape=(jax.ShapeDtypeStruct((TAB_ROWS, n_g), jnp.int32),
                   jax.ShapeDtypeStruct((ROW_ALIGN, cols), jnp.int32)),
        grid=(1,),
        in_specs=[_const_spec(cnt.shape)],
        out_specs=(_const_spec((TAB_ROWS, n_g)), _const_spec((ROW_ALIGN, cols))),
        compiler_params=pltpu.CompilerParams(dimension_semantics=("arbitrary",)),
        name="moe_table",
    )(cnt)


SC_CORES = 2
SC_SUBCORES = 16
SC_WORKERS = SC_CORES * SC_SUBCORES
SC_GATHER_ROWS = 64
COMBINE_SPLITS = 4


def _sc_mesh():
    return plsc.VectorSubcoreMesh(core_axis_name="c", subcore_axis_name="s")


def _sc_worker_id():
    return lax.axis_index("s") * SC_CORES + lax.axis_index("c")


SC_LANES = 16


def _sc_decode_rows(code_v, starts_v, idx_v):
    for i in range(code_v.shape[0] // SC_LANES):
        lanes = pl.ds(i * SC_LANES, SC_LANES)
        code = code_v[lanes]
        expert = lax.shift_right_logical(code, CODE_SHIFT)
        idx_v[lanes] = plsc.load_gather(starts_v, [expert]) + (code & ((1 << CODE_SHIFT) - 1))


def _sc_params():
    return pltpu.CompilerParams(needs_layout_passes=False)


def _sc_scatter_rows(rows, code_flat, starts, zero_list, n_out):
    t, w = rows.shape
    n_k = code_flat.shape[0] // t
    n_e = starts.shape[0]
    r = SC_GATHER_ROWS
    per = t // SC_WORKERS
    n_steps = per // r
    zero_rounds = zero_list.shape[0] // (SC_WORKERS * r)
    assert per % (2 * r) == 0 and zero_list.shape[0] % (SC_WORKERS * r) == 0 and zero_rounds <= n_k
    zeros = jnp.zeros((r, w), rows.dtype)

    @functools.partial(
        pl.kernel, mesh=_sc_mesh(), out_type=jax.ShapeDtypeStruct((n_out, w), rows.dtype),
        scratch_types=[pltpu.VMEM((n_e,), jnp.int32), pltpu.VMEM((n_k, r), jnp.int32),
                       pltpu.VMEM((n_k, r), jnp.int32), pltpu.VMEM((2, r, w), rows.dtype),
                       pltpu.SemaphoreType.DMA((2,)), pltpu.SemaphoreType.DMA],
        compiler_params=_sc_params(), name="moe_dispatch_sc")
    def scatter(rows_hbm, code_hbm, starts_hbm, zero_idx_hbm, zeros_hbm, out_hbm,
                starts_v, code_v, idx_v, rows_v, l_sem, s_sem):
        worker = _sc_worker_id()
        base = worker * per
        pltpu.sync_copy(starts_hbm, starts_v)

        def load(j, b):
            return pltpu.make_async_copy(rows_hbm.at[pl.ds(pl.multiple_of(base + j * r, ROW_ALIGN), r)],
                                         rows_v.at[b], l_sem.at[b])

        load(0, 0).start()
        pltpu.sync_copy(zeros_hbm, rows_v.at[1])
        zero_copies = []
        for z in range(zero_rounds):
            off = pl.multiple_of((worker * zero_rounds + z) * r, ROW_ALIGN)
            pltpu.sync_copy(zero_idx_hbm.at[pl.ds(off, r)], idx_v.at[z])
            zero_copies.append(pltpu.make_async_copy(rows_v.at[1], out_hbm.at[idx_v.at[z]], s_sem))
            zero_copies[-1].start()
        for cp in zero_copies:
            cp.wait()

        @pl.loop(0, n_steps, step=2)
        def _(j0):
            for b in range(2):
                j = j0 + b
                pl.when(j + 1 < n_steps)(load(j + 1, 1 - b).start)
                for k in range(n_k):
                    off = pl.multiple_of(k * t + base + j * r, ROW_ALIGN)
                    pltpu.sync_copy(code_hbm.at[pl.ds(off, r)], code_v.at[k])
                for k in range(n_k):
                    _sc_decode_rows(code_v.at[k], starts_v, idx_v.at[k])
                load(j, b).wait()
                copies = [pltpu.make_async_copy(rows_v.at[b], out_hbm.at[idx_v.at[k]], s_sem) for k in range(n_k)]
                for cp in copies:
                    cp.start()
                for cp in copies:
                    cp.wait()

    return scatter(rows, code_flat, starts, zero_list, zeros)


def _sc_gather_rows(table, code_flat, starts):
    n = code_flat.shape[0]
    w = table.shape[1]
    n_e = starts.shape[0]
    r = SC_GATHER_ROWS
    per = n // SC_WORKERS
    n_steps = per // r
    assert per % (2 * r) == 0

    @functools.partial(
        pl.kernel, mesh=_sc_mesh(), out_type=jax.ShapeDtypeStruct((n, w), table.dtype),
        scratch_types=[pltpu.VMEM((n_e,), jnp.int32), pltpu.VMEM((r,), jnp.int32), pltpu.VMEM((2, r), jnp.int32),
                       pltpu.VMEM((2, r, w), table.dtype),
                       pltpu.SemaphoreType.DMA((2,)), pltpu.SemaphoreType.DMA((2,))],
        compiler_params=_sc_params(), name="moe_gather_sc")
    def gather(table_hbm, code_hbm, starts_hbm, out_hbm, starts_v, code_v, idx_v, rows_v, g_sem, w_sem):
        base = _sc_worker_id() * per
        pltpu.sync_copy(starts_hbm, starts_v)

        def at(j):
            return pl.ds(pl.multiple_of(base + j * r, ROW_ALIGN), r)

        def fetch(j, b):
            return pltpu.make_async_copy(table_hbm.at[idx_v.at[b]], rows_v.at[b], g_sem.at[b])

        def start_fetch(j, b):
            pltpu.sync_copy(code_hbm.at[at(j)], code_v)
            _sc_decode_rows(code_v, starts_v, idx_v.at[b])
            fetch(j, b).start()

        def write(j, b):
            return pltpu.make_async_copy(rows_v.at[b], out_hbm.at[at(j)], w_sem.at[b])

        start_fetch(0, 0)

        @pl.loop(0, n_steps, step=2)
        def _(j0):
            for b in range(2):
                j = j0 + b

                @pl.when(j + 1 < n_steps)
                def _():
                    pl.when(j >= 1)(write(j - 1, 1 - b).wait)
                    start_fetch(j + 1, 1 - b)

                fetch(j, b).wait()
                write(j, b).start()

        write(n_steps - 2, 0).wait()
        write(n_steps - 1, 1).wait()

    return gather(table, code_flat, starts)


def _dispatch(code, starts, pad_list, hp):
    t = hp.shape[0]
    n_out = _sorted_rows(t, starts.shape[0]) + pad_list.size
    return _sc_scatter_rows(hp, code.reshape(-1), starts, pad_list.reshape(-1), n_out)


X_LOOKAHEAD = 2
X_SLOTS = X_LOOKAHEAD + 1
Y_SLOTS = 2


def _expert_kernel(tab_ref, xs_hbm, wgu_ref, wd_ref, y_hbm,
                   wd16_ref, xbuf_ref, ybuf_ref, in_sem, out_sem, *, n_e):
    e = pl.program_id(0)
    ch, half = xbuf_ref.shape[1:]
    ff = wd_ref.shape[1]
    n_total = tab_ref[TAB_TOTAL, 0]
    first = tab_ref[TAB_CHUNK0, e]
    n_chunks = tab_ref[TAB_NCHUNK, e]

    def in_copy(g):
        slot = g % X_SLOTS
        row = pl.multiple_of(tab_ref[TAB_CHUNK_ROW, g], ROW_ALIGN)
        return pltpu.make_async_copy(xs_hbm.at[pl.ds(row, ch), :], xbuf_ref.at[slot], in_sem.at[slot])

    def out_copies(g):
        slot = g % Y_SLOTS
        row = tab_ref[TAB_CHUNK_ROW, g]
        valid = tab_ref[TAB_CHUNK_VALID, g]
        yield valid >= ch, pltpu.make_async_copy(ybuf_ref.at[slot], y_hbm.at[pl.ds(pl.multiple_of(row, ROW_ALIGN), ch), :],
                                                 out_sem.at[slot])
        size = ch // 2
        while size >= ROW_ALIGN:
            off = pl.multiple_of((valid // (2 * size)) * (2 * size), ROW_ALIGN)
            cond = (valid < ch) & ((valid & size) != 0)
            yield cond, pltpu.make_async_copy(ybuf_ref.at[slot, pl.ds(off, size), :],
                                              y_hbm.at[pl.ds(pl.multiple_of(row + off, ROW_ALIGN), size), :],
                                              out_sem.at[slot])
            size //= 2

    def start_out(g):
        for cond, cp in out_copies(g):
            pl.when(cond)(cp.start)

    def wait_out(g):
        for cond, cp in out_copies(g):
            pl.when(cond)(cp.wait)

    @pl.when(e == 0)
    def _():
        for j in range(X_LOOKAHEAD):
            pl.when(j < n_total)(in_copy(j).start)

    @pl.when(n_chunks > 0)
    def _():
        wd16_ref[...] = wd_ref[0].astype(BF16)

    def chunk(c, carry):
        g = first + c
        pl.when(g + X_LOOKAHEAD < n_total)(in_copy(g + X_LOOKAHEAD).start)
        in_copy(g).wait()
        pl.when(g >= Y_SLOTS)(lambda: wait_out(g - Y_SLOTS))
        valid = tab_ref[TAB_CHUNK_VALID, g]

        def ffn(rows):
            xa, xb = _unpack_bf16_pairs(xbuf_ref[g % X_SLOTS, 0:rows, :])
            hgu = (_dot(xa.astype(BF16), wgu_ref[0, 0:half, :])
                   + _dot(xb.astype(BF16), wgu_ref[0, half:2 * half, :]))
            act = _silu(hgu[:, :ff]) * hgu[:, ff:]
            y = _dot(act.astype(BF16), wd16_ref[...])
            ybuf_ref[g % Y_SLOTS, 0:rows, :] = _pack_bf16_pairs(y)

        lo = 0
        for rows in EXPERT_ROW_STEPS:
            pl.when((valid > lo) & (valid <= rows))(functools.partial(ffn, rows))
            lo = rows
        start_out(g)
        return carry

    lax.fori_loop(0, n_chunks, chunk, 0)

    @pl.when(e == n_e - 1)
    def _():
        for j in range(Y_SLOTS, 0, -1):
            pl.when(n_total >= j)(lambda j=j: wait_out(n_total - j))
        n_rows = y_hbm.shape[0]
        used = tab_ref[TAB_START, e] + ((tab_ref[TAB_COUNT, e] + (ROW_ALIGN - 1)) // ROW_ALIGN) * ROW_ALIGN
        ybuf_ref[0] = jnp.zeros((ch, half), PACKED)
        for j in range(pl.cdiv(n_e * ROW_ALIGN + EXPERT_CHUNK + ROW_ALIGN, ch)):
            at = pl.multiple_of(jnp.minimum(used + j * ch, n_rows - ch), ROW_ALIGN)
            tail = pltpu.make_async_copy(ybuf_ref.at[0], y_hbm.at[pl.ds(at, ch), :], out_sem.at[0])
            tail.start()
            tail.wait()


def _experts(tab, xs, wgu16, w_down_e, p):
    half = xs.shape[1]
    n_e, ff, d = w_down_e.shape
    ch = EXPERT_CHUNK
    grid_spec = pltpu.PrefetchScalarGridSpec(
        num_scalar_prefetch=1,
        grid=(n_e,),
        in_specs=[pl.BlockSpec(memory_space=pl.ANY),
                  pl.BlockSpec((1, d, 2 * ff), lambda e, *_: (e, 0, 0)),
                  pl.BlockSpec((1, ff, d), lambda e, *_: (e, 0, 0))],
        out_specs=pl.BlockSpec(memory_space=pl.ANY),
        scratch_shapes=[pltpu.VMEM((ff, d), BF16),
                        pltpu.VMEM((X_SLOTS, ch, half), PACKED), pltpu.VMEM((Y_SLOTS, ch, half), PACKED),
                        pltpu.SemaphoreType.DMA((X_SLOTS,)), pltpu.SemaphoreType.DMA((Y_SLOTS,))],
    )
    return pl.pallas_call(
        functools.partial(_expert_kernel, n_e=n_e),
        out_shape=jax.ShapeDtypeStruct((p, half), PACKED),
        grid_spec=grid_spec,
        compiler_params=pltpu.CompilerParams(dimension_semantics=("arbitrary",),
                                             vmem_limit_bytes=VMEM_LIMIT_BYTES),
        name="moe_experts",
    )(tab, xs, wgu16, w_down_e)


def _combine_kernel(yg_ref, wcol_ref, x1_ref, mod_ref, npre_ref, npost_ref, wgus_ref, wds_ref, o_ref):
    te, d = x1_ref.shape
    half = d // 2
    ff = wds_ref.shape[0]

    x1 = x1_ref[...]
    h2 = _rms(x1, npre_ref[...]) * (1.0 + mod_ref[:, 4 * d:5 * d]) + mod_ref[:, 3 * d:4 * d]
    hgu = _dot(h2.astype(BF16), wgus_ref[...])
    act = _silu(hgu[:, :ff]) * hgu[:, ff:]
    shared = _dot(act.astype(BF16), wds_ref[...])

    acc_a = shared[:, :half]
    acc_b = shared[:, half:]
    for k in range(TOP_K):
        ya, yb = _unpack_bf16_pairs(yg_ref[k])
        wk = wcol_ref[:, k:k + 1]
        acc_a = acc_a + wk * ya
        acc_b = acc_b + wk * yb
    moe = jnp.concatenate([acc_a, acc_b], axis=-1)
    g2 = mod_ref[:, 5 * d:6 * d]
    o_ref[...] = x1 + g2 * _rms(moe, npost_ref[...])


def _combine(code, starts, y, wcol, x1, mod, npre2, npost2, w_gate_s, w_up_s, w_down_s):
    t, d = x1.shape
    half = d // 2
    te = SEQ_TILE
    wgus = jnp.concatenate([w_gate_s, w_up_s], axis=1).astype(BF16)
    wds = w_down_s.astype(BF16)
    n_split = COMBINE_SPLITS if t % (COMBINE_SPLITS * 2 * SC_GATHER_ROWS * SC_WORKERS // TOP_K) == 0 else 1
    tq = t // n_split
    steps = tq // te
    stream = x1
    for q in range(n_split):
        code_q = code[:, q * tq:(q + 1) * tq]
        yg = _sc_gather_rows(y, code_q.reshape(-1), starts).reshape(TOP_K, tq, half)
        row = lambda i, q=q: (q * steps + i, 0)
        stream = pl.pallas_call(
            _combine_kernel,
            out_shape=jax.ShapeDtypeStruct((t, d), F32),
            grid=(steps,),
            in_specs=[pl.BlockSpec((TOP_K, te, half), lambda i: (0, i, 0)),
                      pl.BlockSpec((te, 128), row),
                      pl.BlockSpec((te, d), row),
                      _const_spec(mod.shape),
                      _const_spec((1, d)),
                      _const_spec((1, d)),
                      _const_spec(wgus.shape),
                      _const_spec(wds.shape)],
            out_specs=pl.BlockSpec((te, d), row),
            input_output_aliases={2: 0},
            compiler_params=pltpu.CompilerParams(dimension_semantics=("arbitrary",),
                                                 vmem_limit_bytes=VMEM_LIMIT_BYTES),
            name="moe_combine",
        )(yg, wcol, stream, mod, npre2.reshape(1, d), npost2.reshape(1, d), wgus, wds)
    return stream


def kernel(x, c, w_ada, b_ada, norm_pre_mix, norm_post_mix, w_in, conv_w, w_conv_out, hgrn_lower_bounds, hgrn_norm_w, w_hgrn_out, w_o, norm_pre_ffn, norm_post_ffn, w_router, router_bias, w_gate_e, w_up_e, w_down_e, w_gate_s, w_up_s, w_down_s):
    bsz, seq, d = x.shape
    assert bsz == 1 and w_ada.shape[0] == 1
    mod = _ada_mod(c, w_ada[0], b_ada[0])
    x1, wgu16 = _mixer(x.reshape(seq, d), mod, norm_pre_mix[0], norm_post_mix[0], w_in[0], conv_w[0],
                       w_conv_out[0], hgrn_lower_bounds, hgrn_norm_w[0], w_hgrn_out[0], w_o[0],
                       w_gate_e[0], w_up_e[0])
    hp, code, wcol, cnt = _router(x1, mod, norm_pre_ffn[0], w_router[0], router_bias[0])
    n_e = w_router.shape[-1]
    tab, pad_list = _route_table(cnt, seq)
    starts = tab[TAB_START, :n_e]
    xs = _dispatch(code, starts, pad_list, hp)
    y = _experts(tab, xs, wgu16, w_down_e[0], _sorted_rows(seq, n_e))
    out = _combine(code, starts, y, wcol, x1, mod, norm_pre_ffn[0], norm_post_ffn[0],
                   w_gate_s[0], w_up_s[0], w_down_s[0])
    return out.reshape(bsz, seq, d)
```

```python
import functools

import jax
import jax.numpy as jnp
from jax import lax
from jax.experimental import pallas as pl
from jax.experimental.pallas import tpu as pltpu
from jax.experimental.pallas import tpu_sc as plsc

F32 = jnp.float32
BF16 = jnp.bfloat16

NORM_EPS = 1e-6
CONV_DIM = 512
HGRN_HEADS = 4
HGRN_DK = 128
HGRN_QK = HGRN_HEADS * HGRN_DK
N_GROUPS = 8
TOPK_GROUPS = 4
TOP_K = 8
ROUTED_SCALE = 2.5
CODE_SHIFT = 16

SEQ_TILE = 256
ROUTER_TILE = 1024
EXPERT_CHUNK = 512
EXPERT_ROW_STEPS = (128, 256, EXPERT_CHUNK)
ROW_ALIGN = 8
VMEM_LIMIT_BYTES = 56 * 1024 * 1024


def _dot(a, b):
    return jnp.dot(a, b, preferred_element_type=F32)


def _dot_nt(a, b):
    return lax.dot_general(a, b, (((1,), (1,)), ((), ())), preferred_element_type=F32)


def _dot_tn(a, b):
    return lax.dot_general(a, b, (((0,), (0,)), ((), ())), preferred_element_type=F32)


def _split3(x):
    hi = x.astype(BF16)
    r1 = x - hi.astype(F32)
    mid = r1.astype(BF16)
    lo = (r1 - mid.astype(F32)).astype(BF16)
    return hi, mid, lo


def _sigmoid(x):
    return 1.0 / (1.0 + jnp.exp(-x))


def _silu(x):
    return x * _sigmoid(x)


def _rms(x, w):
    ms = jnp.mean(x * x, axis=-1, keepdims=True)
    return x * lax.rsqrt(ms + NORM_EPS) * w


def _ada_kernel(c_ref, w_ref, b_ref, o_ref):
    cs = _silu(c_ref[...])
    h1, h2, h3 = _split3(cs)
    w1, w2, w3 = _split3(w_ref[...])
    r = cs.shape[0]
    stacked = jnp.concatenate([h.astype(F32) for h in (h1, h2, h3)], axis=0).astype(BF16)
    p1 = _dot(stacked, w1)
    p2 = _dot(stacked[0:2 * r], w2)
    p3 = _dot(h1, w3)
    acc = p1[0:r] + (p2[0:r] + p1[r:2 * r]) + (p3 + p2[r:2 * r] + p1[2 * r:3 * r])
    o_ref[...] = acc + b_ref[...]


def _ada_mod(c, w_ada, b_ada):
    d = c.shape[-1]
    n = w_ada.shape[-1]
    bn = 1024
    c8 = jnp.broadcast_to(c.reshape(1, d), (8, d))
    out = pl.pallas_call(
        _ada_kernel,
        out_shape=jax.ShapeDtypeStruct((8, n), F32),
        grid=(n // bn,),
        in_specs=[pl.BlockSpec((8, d), lambda j: (0, 0)),
                  pl.BlockSpec((d, bn), lambda j: (0, j)),
                  pl.BlockSpec((1, bn), lambda j: (0, j))],
        out_specs=pl.BlockSpec((8, bn), lambda j: (0, j)),
        compiler_params=pltpu.CompilerParams(dimension_semantics=("arbitrary",),
                                             vmem_limit_bytes=VMEM_LIMIT_BYTES),
        name="ada_mod",
    )(c8, w_ada, b_ada.reshape(1, n))
    return out[0:1]


def _level_reference(b, rolled, s, row):
    ts = b.shape[0]
    c = s // 2 - 1
    if s >= 16:
        pieces = []
        for blk in range(ts // s):
            r = blk * s + c
            pieces.append(jnp.broadcast_to(b[r:r + 1, :], (s, b.shape[1])))
        return pieces[0] if len(pieces) == 1 else jnp.concatenate(pieces, axis=0)
    pos = row & (s - 1)
    out = b
    for p in range(s):
        if p == c:
            continue
        out = jnp.where(pos == p, rolled[p - c], out)
    return out


def _cast_expert_weights(wg_ref, wu_ref, wgu16_ref):
    ff = wg_ref.shape[2]
    for j in range(wg_ref.shape[0]):
        wgu16_ref[j, :, 0:ff] = wg_ref[j].astype(BF16)
        wgu16_ref[j, :, ff:2 * ff] = wu_ref[j].astype(BF16)


def _mixer_kernel(x_ref, mod_ref, npre_ref, npost_ref, win_ref, convw_ref, wco_ref,
                  hlb_ref, hnw_ref, who_ref, wo_ref, wg_ref, wu_ref,
                  o_ref, wgu16_ref, state_ref, ubuf_ref):
    ts, d = x_ref.shape
    step = pl.program_id(0)
    _cast_expert_weights(wg_ref, wu_ref, wgu16_ref)

    @pl.when(step == 0)
    def _():
        state_ref[...] = jnp.zeros_like(state_ref)
        ubuf_ref[0:8, :] = jnp.zeros((8, CONV_DIM), F32)

    x = x_ref[...]
    sh1 = mod_ref[:, 0:d]
    sc1 = mod_ref[:, d:2 * d]
    g1 = mod_ref[:, 2 * d:3 * d]
    h = (_rms(x, npre_ref[...]) * (1.0 + sc1) + sh1).astype(BF16)

    def proj(lo, width):
        return _dot(h, win_ref[:, lo:lo + width])

    c0 = 0
    cb = proj(c0, CONV_DIM)
    u = proj(c0 + CONV_DIM, CONV_DIM) * proj(c0 + 2 * CONV_DIM, CONV_DIM)
    ubuf_ref[8:8 + ts, :] = u
    conv = (ubuf_ref[6:6 + ts, :] * convw_ref[0:1, :]
            + ubuf_ref[7:7 + ts, :] * convw_ref[1:2, :]
            + u * convw_ref[2:3, :])
    ubuf_ref[0:8, :] = ubuf_ref[ts:ts + 8, :]
    y_a = _dot((cb * conv).astype(BF16), wco_ref[...])

    c1 = 3 * CONV_DIM
    q = _silu(proj(c1, HGRN_QK))
    hl = hlb_ref[...]
    hm = jnp.max(hl, axis=0, keepdims=True)
    he = jnp.exp(hl - hm)
    lb = he[0:1, :] / jnp.sum(he, axis=0, keepdims=True)
    fg = lb + (1.0 - lb) * _sigmoid(proj(c1 + HGRN_QK, HGRN_QK))
    k = 1.0 - fg
    g = jnp.log(fg)
    v = proj(c1 + 2 * HGRN_QK, HGRN_QK)
    gg = proj(c1 + 3 * HGRN_QK, HGRN_QK)

    row = lax.broadcasted_iota(jnp.int32, (ts, ts), 0)
    col = lax.broadcasted_iota(jnp.int32, (ts, ts), 1)
    tril = jnp.where(row >= col, 1.0, 0.0).astype(BF16)
    ghi, gmid, glo = _split3(g)
    b = _dot(tril, ghi) + _dot(tril, gmid) + _dot(tril, glo)
    b_last = b[ts - 1:ts, :]

    rowq = lax.broadcasted_iota(jnp.int32, (ts, HGRN_QK), 0)
    rolled = {sft: pltpu.roll(b, sft % ts, axis=0) for sft in (-3, -2, -1, 1, 2, 3, 4)}
    xor = row ^ col

    levels = []
    s = ts
    while s >= 2:
        levels.append(s)
        s //= 2
    qt, kt = [], []
    for s in levels:
        bref = _level_reference(b, rolled, s, rowq)
        e = jnp.exp(-jnp.abs(b - bref))
        upper = (rowq & (s // 2)) != 0
        qt.append(jnp.where(upper, q * e, 0.0).astype(BF16))
        kt.append(jnp.where(upper, 0.0, k * e).astype(BF16))
    q_in = (q * jnp.exp(b)).astype(BF16)
    k_out = (k * jnp.exp(b_last - b)).astype(BF16)
    v16 = v.astype(BF16)
    qk = q * k
    decay_last = jnp.exp(b_last)

    hnw = hnw_ref[...]
    outs = []
    for hd in range(HGRN_HEADS):
        sl = slice(hd * HGRN_DK, (hd + 1) * HGRN_DK)
        a = jnp.zeros((ts, ts), F32)
        for li in range(len(levels) - 1, -1, -1):
            s = levels[li]
            p = _dot_nt(qt[li][:, sl], kt[li][:, sl])
            a = jnp.where(xor >= s // 2, p, a)
        st = state_ref[hd]
        o_h = (_dot(a.astype(BF16), v16[:, sl])
               + jnp.sum(qk[:, sl], axis=-1, keepdims=True) * v[:, sl]
               + _dot_nt(q_in[:, sl], st.astype(BF16)))
        state_ref[hd] = st * decay_last[:, sl] + _dot_tn(v16[:, sl], k_out[:, sl])
        outs.append(_rms(o_h, hnw))
    o = jnp.concatenate(outs, axis=-1) * _silu(gg)
    y_b = _dot(o.astype(BF16), who_ref[...])

    c2 = c1 + 4 * HGRN_QK
    m = _sigmoid(proj(c2, d)) * y_a + _sigmoid(proj(c2 + d, d)) * y_b
    y = _dot(m.astype(BF16), wo_ref[...])
    o_ref[...] = x + g1 * _rms(y, npost_ref[...])


def _const_spec(shape):
    nd = len(shape)
    return pl.BlockSpec(shape, lambda i: (0,) * nd)


def _mixer(x2, mod, npre, npost, w_in, conv_w, w_conv_out, hlb, hnw, w_hgrn_out, w_o, w_gate_e, w_up_e):
    t, d = x2.shape
    ts = SEQ_TILE
    n = t // ts
    n_e, _, ff = w_gate_e.shape
    assert n_e % n == 0 and conv_w.shape == (3, CONV_DIM)
    per = n_e // n
    args = (x2, mod, npre.reshape(1, d), npost.reshape(1, d), w_in.astype(BF16), conv_w,
            w_conv_out.astype(BF16), hlb, hnw.reshape(1, -1), w_hgrn_out.astype(BF16),
            w_o.astype(BF16))
    experts = lambda i: (i, 0, 0)
    in_specs = ([pl.BlockSpec((ts, d), lambda i: (i, 0))] + [_const_spec(a.shape) for a in args[1:]]
                + [pl.BlockSpec((per, d, ff), experts)] * 2)
    return pl.pallas_call(
        _mixer_kernel,
        out_shape=[jax.ShapeDtypeStruct((t, d), F32), jax.ShapeDtypeStruct((n_e, d, 2 * ff), BF16)],
        grid=(n,),
        in_specs=in_specs,
        out_specs=[pl.BlockSpec((ts, d), lambda i: (i, 0)), pl.BlockSpec((per, d, 2 * ff), experts)],
        scratch_shapes=[pltpu.VMEM((HGRN_HEADS, HGRN_DK, HGRN_DK), F32),
                        pltpu.VMEM((ts + 8, CONV_DIM), F32)],
        compiler_params=pltpu.CompilerParams(dimension_semantics=("arbitrary",),
                                             vmem_limit_bytes=VMEM_LIMIT_BYTES),
        name="token_mixer",
    )(*args, w_gate_e, w_up_e)


PACKED = jnp.int32


def _pack_bf16_pairs(x):
    m = x.shape[1] // 2
    hi = lax.bitcast_convert_type(x[:, :m].astype(BF16).astype(F32), jnp.uint32)
    lo = lax.bitcast_convert_type(x[:, m:].astype(BF16).astype(F32), jnp.uint32)
    return lax.bitcast_convert_type(hi | (lo >> 16), PACKED)


def _unpack_bf16_pairs(p):
    p = lax.bitcast_convert_type(p, jnp.uint32)
    hi = lax.bitcast_convert_type(p & jnp.uint32(0xFFFF0000), F32)
    lo = lax.bitcast_convert_type(p << 16, F32)
    return hi, lo


def _router_kernel(x1_ref, mod_ref, npre_ref, wrh_ref, wrl_ref, rb_ref,
                   hp_ref, code_ref, wcol_ref, cnt_ref, carry_ref, rows_ref):
    assert N_GROUPS == TOP_K == rows_ref.shape[0]
    tr, d = x1_ref.shape
    n_e = wrh_ref.shape[0]
    gsz = n_e // N_GROUPS
    neg = -jnp.inf

    @pl.when(pl.program_id(0) == 0)
    def _():
        carry_ref[...] = jnp.zeros_like(carry_ref)

    sh2 = mod_ref[:, 3 * d:4 * d]
    sc2 = mod_ref[:, 4 * d:5 * d]
    h2 = _rms(x1_ref[...], npre_ref[...]) * (1.0 + sc2) + sh2
    hp_ref[...] = _pack_bf16_pairs(h2)
    h_hi = h2.astype(BF16)
    h_lo = (h2 - h_hi.astype(F32)).astype(BF16)
    wrh = wrh_ref[...]
    logits = _dot_nt(wrh, h_hi) + (_dot_nt(wrh, h_lo) + _dot_nt(wrl_ref[...], h_hi))
    scores = _sigmoid(logits)
    sel = scores + rb_ref[:, 0:1]

    io_e = lax.broadcasted_iota(jnp.int32, (n_e, tr), 0)
    for g in range(N_GROUPS):
        blk = sel[g * gsz:(g + 1) * gsz, :]
        io = lax.broadcasted_iota(jnp.int32, (gsz, tr), 0) + g * gsz
        m1 = jnp.max(blk, axis=0, keepdims=True)
        i1 = jnp.min(jnp.where(blk == m1, io, n_e), axis=0, keepdims=True)
        m2 = jnp.max(jnp.where(io == i1, neg, blk), axis=0, keepdims=True)
        rows_ref[g:g + 1, :] = m1 + m2
    gs = rows_ref[...]
    io_g = lax.broadcasted_iota(jnp.int32, (N_GROUPS, tr), 0)
    gsel = jnp.zeros((N_GROUPS, tr), F32)
    for _ in range(TOPK_GROUPS):
        m = jnp.max(gs, axis=0, keepdims=True)
        gi = jnp.min(jnp.where(gs == m, io_g, N_GROUPS), axis=0, keepdims=True)
        hit = io_g == gi
        gsel = jnp.where(hit, 1.0, gsel)
        gs = jnp.where(hit, neg, gs)
    rows_ref[...] = gsel
    cur = jnp.concatenate(
        [jnp.where(rows_ref[g:g + 1, :] > 0.5, sel[g * gsz:(g + 1) * gsz, :], neg) for g in range(N_GROUPS)],
        axis=0)

    idxs = []
    selm = jnp.zeros((n_e, tr), F32)
    for k in range(TOP_K):
        m = jnp.max(cur, axis=0, keepdims=True)
        idx = jnp.min(jnp.where(cur == m, io_e, n_e), axis=0, keepdims=True)
        hit = io_e == idx
        rows_ref[k:k + 1, :] = jnp.sum(jnp.where(hit, scores, 0.0), axis=0, keepdims=True)
        cur = jnp.where(hit, neg, cur)
        selm = jnp.where(hit, 1.0, selm)
        idxs.append(idx)

    r_io = lax.broadcasted_iota(jnp.int32, (tr, tr + 128), 0)
    c_io = lax.broadcasted_iota(jnp.int32, (tr, tr + 128), 1)
    before = jnp.where((r_io < c_io) | (c_io >= tr), 1.0, 0.0).astype(BF16)
    r_ext = _dot(selm.astype(BF16), before)
    rank_full = r_ext[:, :tr] + carry_ref[:, 0:1]
    for k, idx in enumerate(idxs):
        rk = jnp.sum(jnp.where(io_e == idx, rank_full, 0.0), axis=0, keepdims=True)
        code_ref[k:k + 1, :] = (idx << CODE_SHIFT) | rk.astype(jnp.int32)
    carry_ref[...] = carry_ref[...] + r_ext[:, tr:]
    cnt_ref[...] = carry_ref[...]

    wk = rows_ref[...]
    wn = wk / (jnp.sum(wk, axis=0, keepdims=True) + 1e-20) * ROUTED_SCALE
    eye = jnp.where(lax.broadcasted_iota(jnp.int32, (TOP_K, 128), 0)
                    == lax.broadcasted_iota(jnp.int32, (TOP_K, 128), 1), 1.0, 0.0).astype(BF16)
    w1, w2, w3 = _split3(wn)
    wcol_ref[...] = _dot_tn(w1, eye) + _dot_tn(w2, eye) + _dot_tn(w3, eye)


def _router(x1, mod, npre2, w_router, router_bias):
    t, d = x1.shape
    n_e = w_router.shape[1]
    assert t <= (1 << CODE_SHIFT)
    tr = ROUTER_TILE if t % ROUTER_TILE == 0 else SEQ_TILE
    wrt = w_router.T
    wrh = wrt.astype(BF16)
    wrl = (wrt - wrh.astype(F32)).astype(BF16)
    rb = jnp.broadcast_to(router_bias.reshape(n_e, 1), (n_e, 128))
    args = (x1, mod, npre2.reshape(1, d), wrh, wrl, rb)
    in_specs = [pl.BlockSpec((tr, d), lambda i: (i, 0))] + [_const_spec(a.shape) for a in args[1:]]
    return pl.pallas_call(
        _router_kernel,
        out_shape=(jax.ShapeDtypeStruct((t, d // 2), PACKED),
                   jax.ShapeDtypeStruct((TOP_K, t), jnp.int32),
                   jax.ShapeDtypeStruct((t, 128), F32),
                   jax.ShapeDtypeStruct((n_e, 128), F32)),
        grid=(t // tr,),
        in_specs=in_specs,
        out_specs=(pl.BlockSpec((tr, d // 2), lambda i: (i, 0)),
                   pl.BlockSpec((TOP_K, tr), lambda i: (0, i)),
                   pl.BlockSpec((tr, 128), lambda i: (i, 0)),
                   pl.BlockSpec((n_e, 128), lambda i: (0, 0))),
        scratch_shapes=[pltpu.VMEM((n_e, 128), F32), pltpu.VMEM((TOP_K, tr), F32)],
        compiler_params=pltpu.CompilerParams(dimension_semantics=("arbitrary",),
                                             vmem_limit_bytes=VMEM_LIMIT_BYTES),
        name="moe_router",
    )(*args)


TAB_CHUNK_ROW, TAB_CHUNK_VALID, TAB_START, TAB_COUNT, TAB_CHUNK0, TAB_NCHUNK, TAB_TOTAL = range(7)
TAB_ROWS = 8


def _excl_cumsum_rows(lower, x):
    hi = jnp.floor(x * (1.0 / 128.0))
    lo = x - hi * 128.0
    return 128.0 * _dot(lower, hi.astype(BF16)) + _dot(lower, lo.astype(BF16))


def _table_kernel(cnt_ref, tab_ref, pad_ref, *, n_rows):
    n_e = cnt_ref.shape[0]
    n_g = tab_ref.shape[1]
    ch = float(EXPERT_CHUNK)
    cnt = jnp.floor((cnt_ref[...] + (ROW_ALIGN - 1)) * (1.0 / ROW_ALIGN)) * ROW_ALIGN
    r = lax.broadcasted_iota(jnp.int32, (n_e, n_e), 0)
    c = lax.broadcasted_iota(jnp.int32, (n_e, n_e), 1)
    lower = jnp.where(c < r, 1.0, 0.0).astype(BF16)
    starts = _excl_cumsum_rows(lower, cnt)
    start_col = starts[:, 0:1]

    nch = jnp.floor((cnt + (ch - 1.0)) * (1.0 / ch))
    chunk0 = _excl_cumsum_rows(lower, nch)
    cend_col = (chunk0 + nch)[:, 0:1]
    eye = r == c

    def as_row(col):
        return jnp.sum(jnp.where(eye, col, 0.0), axis=0, keepdims=True)

    g = lax.broadcasted_iota(jnp.int32, (n_e, n_g), 1).astype(F32)
    owner = jnp.sum(jnp.where(cend_col <= g, 1.0, 0.0), axis=0, keepdims=True)
    mine = lax.broadcasted_iota(jnp.int32, (n_e, n_g), 0).astype(F32) == owner
    g_row = lax.broadcasted_iota(jnp.int32, (1, n_g), 1).astype(F32)
    base = jnp.sum(jnp.where(mine, start_col - chunk0[:, 0:1] * ch, 0.0), axis=0, keepdims=True)
    left = jnp.sum(jnp.where(mine, cnt[:, 0:1] + chunk0[:, 0:1] * ch, 0.0), axis=0, keepdims=True)
    pad = jnp.zeros((1, n_g - n_e), F32)

    def wide(row):
        return jnp.concatenate([row, pad], axis=1)

    total = jnp.sum(nch[:, 0:1], axis=0, keepdims=True)
    tab_ref[...] = jnp.zeros(tab_ref.shape, jnp.int32)
    tab_ref[TAB_CHUNK_ROW:TAB_CHUNK_ROW + 1, :] = (base + g_row * ch).astype(jnp.int32)
    tab_ref[TAB_CHUNK_VALID:TAB_CHUNK_VALID + 1, :] = jnp.clip(left - g_row * ch, 0.0, ch).astype(jnp.int32)
    tab_ref[TAB_START:TAB_START + 1, :] = wide(as_row(start_col)).astype(jnp.int32)
    tab_ref[TAB_COUNT:TAB_COUNT + 1, :] = wide(as_row(cnt_ref[:, 0:1])).astype(jnp.int32)
    tab_ref[TAB_CHUNK0:TAB_CHUNK0 + 1, :] = wide(as_row(chunk0[:, 0:1])).astype(jnp.int32)
    tab_ref[TAB_NCHUNK:TAB_NCHUNK + 1, :] = wide(as_row(nch[:, 0:1])).astype(jnp.int32)
    tab_ref[TAB_TOTAL:TAB_TOTAL + 1, :] = jnp.broadcast_to(total, (1, n_g)).astype(jnp.int32)

    cols = pad_ref.shape[1]
    tail_cols = cols - n_e
    j_io = lax.broadcasted_iota(jnp.int32, (ROW_ALIGN, cols), 0).astype(F32)
    c_io = lax.broadcasted_iota(jnp.int32, (ROW_ALIGN, cols), 1).astype(F32)
    spare = float(n_rows) + j_io * cols + c_io
    zeros_tail = jnp.zeros((1, tail_cols), F32)
    raw_col = cnt_ref[:, 0:1]
    end_w = jnp.concatenate([as_row(start_col + raw_col), zeros_tail], axis=1)
    npad_w = jnp.concatenate([as_row(cnt[:, 0:1] - raw_col), zeros_tail], axis=1)
    used = jnp.sum(cnt[:, 0:1], axis=0, keepdims=True)
    after = used + j_io * tail_cols + (c_io - n_e)
    target = jnp.where(c_io < n_e,
                       jnp.where(j_io < npad_w, end_w + j_io, spare),
                       jnp.where(after < float(n_rows), after, spare))
    pad_ref[...] = target.astype(jnp.int32)


def _sorted_rows(t, n_e):
    return t * TOP_K + n_e * ROW_ALIGN + EXPERT_CHUNK + ROW_ALIGN


def _pad_list_cols(t, n_e):
    entries = n_e * ROW_ALIGN + (_sorted_rows(t, n_e) - t * TOP_K)
    per_round = SC_WORKERS * SC_GATHER_ROWS
    return pl.cdiv(entries, per_round) * per_round // ROW_ALIGN


def _route_table(cnt, t):
    n_e = cnt.shape[0]
    n_g = (t * TOP_K) // EXPERT_CHUNK + n_e
    cols = _pad_list_cols(t, n_e)
    return pl.pallas_call(
        functools.partial(_table_kernel, n_rows=_sorted_rows(t, n_e)),
        out_shape=(jax.ShapeDtypeStruct((TAB_ROWS, n_g), jnp.int32),
                   jax.ShapeDtypeStruct((ROW_ALIGN, cols), jnp.int32)),
        grid=(1,),
        in_specs=[_const_spec(cnt.shape)],
        out_specs=(_const_spec((TAB_ROWS, n_g)), _const_spec((ROW_ALIGN, cols))),
        compiler_params=pltpu.CompilerParams(dimension_semantics=("arbitrary",)),
        name="moe_table",
    )(cnt)


SC_CORES = 2
SC_SUBCORES = 16
SC_WORKERS = SC_CORES * SC_SUBCORES
SC_GATHER_ROWS = 64
COMBINE_SPLITS = 4


def _sc_mesh():
    return plsc.VectorSubcoreMesh(core_axis_name="c", subcore_axis_name="s")


def _sc_worker_id():
    return lax.axis_index("s") * SC_CORES + lax.axis_index("c")


SC_LANES = 16


def _sc_decode_rows(code_v, starts_v, idx_v):
    for i in range(code_v.shape[0] // SC_LANES):
        lanes = pl.ds(i * SC_LANES, SC_LANES)
        code = code_v[lanes]
        expert = lax.shift_right_logical(code, CODE_SHIFT)
        idx_v[lanes] = plsc.load_gather(starts_v, [expert]) + (code & ((1 << CODE_SHIFT) - 1))


def _sc_params():
    return pltpu.CompilerParams(needs_layout_passes=False)


def _sc_scatter_rows(rows, code_flat, starts, zero_list, n_out):
    t, w = rows.shape
    n_k = code_flat.shape[0] // t
    n_e = starts.shape[0]
    r = SC_GATHER_ROWS
    per = t // SC_WORKERS
    n_steps = per // r
    zero_rounds = zero_list.shape[0] // (SC_WORKERS * r)
    assert per % (2 * r) == 0 and zero_list.shape[0] % (SC_WORKERS * r) == 0 and zero_rounds <= n_k
    zeros = jnp.zeros((r, w), rows.dtype)

    @functools.partial(
        pl.kernel, mesh=_sc_mesh(), out_type=jax.ShapeDtypeStruct((n_out, w), rows.dtype),
        scratch_types=[pltpu.VMEM((n_e,), jnp.int32), pltpu.VMEM((n_k, r), jnp.int32),
                       pltpu.VMEM((n_k, r), jnp.int32), pltpu.VMEM((2, r, w), rows.dtype),
                       pltpu.SemaphoreType.DMA((2,)), pltpu.SemaphoreType.DMA],
        compiler_params=_sc_params(), name="moe_dispatch_sc")
    def scatter(rows_hbm, code_hbm, starts_hbm, zero_idx_hbm, zeros_hbm, out_hbm,
                starts_v, code_v, idx_v, rows_v, l_sem, s_sem):
        worker = _sc_worker_id()
        base = worker * per
        pltpu.sync_copy(starts_hbm, starts_v)

        def load(j, b):
            return pltpu.make_async_copy(rows_hbm.at[pl.ds(pl.multiple_of(base + j * r, ROW_ALIGN), r)],
                                         rows_v.at[b], l_sem.at[b])

        load(0, 0).start()
        pltpu.sync_copy(zeros_hbm, rows_v.at[1])
        zero_copies = []
        for z in range(zero_rounds):
            off = pl.multiple_of((worker * zero_rounds + z) * r, ROW_ALIGN)
            pltpu.sync_copy(zero_idx_hbm.at[pl.ds(off, r)], idx_v.at[z])
            zero_copies.append(pltpu.make_async_copy(rows_v.at[1], out_hbm.at[idx_v.at[z]], s_sem))
            zero_copies[-1].start()
        for cp in zero_copies:
            cp.wait()

        @pl.loop(0, n_steps, step=2)
        def _(j0):
            for b in range(2):
                j = j0 + b
                pl.when(j + 1 < n_steps)(load(j + 1, 1 - b).start)
                for k in range(n_k):
                    off = pl.multiple_of(k * t + base + j * r, ROW_ALIGN)
                    pltpu.sync_copy(code_hbm.at[pl.ds(off, r)], code_v.at[k])
                for k in range(n_k):
                    _sc_decode_rows(code_v.at[k], starts_v, idx_v.at[k])
                load(j, b).wait()
                copies = [pltpu.make_async_copy(rows_v.at[b], out_hbm.at[idx_v.at[k]], s_sem) for k in range(n_k)]
                for cp in copies:
                    cp.start()
                for cp in copies:
                    cp.wait()

    return scatter(rows, code_flat, starts, zero_list, zeros)


def _sc_gather_rows(table, code_flat, starts):
    n = code_flat.shape[0]
    w = table.shape[1]
    n_e = starts.shape[0]
    r = SC_GATHER_ROWS
    per = n // SC_WORKERS
    n_steps = per // r
    assert per % (2 * r) == 0

    @functools.partial(
        pl.kernel, mesh=_sc_mesh(), out_type=jax.ShapeDtypeStruct((n, w), table.dtype),
        scratch_types=[pltpu.VMEM((n_e,), jnp.int32), pltpu.VMEM((r,), jnp.int32), pltpu.VMEM((2, r), jnp.int32),
                       pltpu.VMEM((2, r, w), table.dtype),
                       pltpu.SemaphoreType.DMA((2,)), pltpu.SemaphoreType.DMA((2,))],
        compiler_params=_sc_params(), name="moe_gather_sc")
    def gather(table_hbm, code_hbm, starts_hbm, out_hbm, starts_v, code_v, idx_v, rows_v, g_sem, w_sem):
        base = _sc_worker_id() * per
        pltpu.sync_copy(starts_hbm, starts_v)

        def at(j):
            return pl.ds(pl.multiple_of(base + j * r, ROW_ALIGN), r)

        def fetch(j, b):
            return pltpu.make_async_copy(table_hbm.at[idx_v.at[b]], rows_v.at[b], g_sem.at[b])

        def start_fetch(j, b):
            pltpu.sync_copy(code_hbm.at[at(j)], code_v)
            _sc_decode_rows(code_v, starts_v, idx_v.at[b])
            fetch(j, b).start()

        def write(j, b):
            return pltpu.make_async_copy(rows_v.at[b], out_hbm.at[at(j)], w_sem.at[b])

        start_fetch(0, 0)

        @pl.loop(0, n_steps, step=2)
        def _(j0):
            for b in range(2):
                j = j0 + b

                @pl.when(j + 1 < n_steps)
                def _():
                    pl.when(j >= 1)(write(j - 1, 1 - b).wait)
                    start_fetch(j + 1, 1 - b)

                fetch(j, b).wait()
                write(j, b).start()

        write(n_steps - 2, 0).wait()
        write(n_steps - 1, 1).wait()

    return gather(table, code_flat, starts)


def _dispatch(code, starts, pad_list, hp):
    t = hp.shape[0]
    n_out = _sorted_rows(t, starts.shape[0]) + pad_list.size
    return _sc_scatter_rows(hp, code.reshape(-1), starts, pad_list.reshape(-1), n_out)


X_LOOKAHEAD = 2
X_SLOTS = X_LOOKAHEAD + 1
Y_SLOTS = 2


def _expert_kernel(tab_ref, xs_hbm, wgu_ref, wd_ref, y_hbm,
                   wd16_ref, xbuf_ref, ybuf_ref, in_sem, out_sem, *, n_e):
    e = pl.program_id(0)
    ch, half = xbuf_ref.shape[1:]
    ff = wd_ref.shape[1]
    n_total = tab_ref[TAB_TOTAL, 0]
    first = tab_ref[TAB_CHUNK0, e]
    n_chunks = tab_ref[TAB_NCHUNK, e]

    def in_copy(g):
        slot = g % X_SLOTS
        row = pl.multiple_of(tab_ref[TAB_CHUNK_ROW, g], ROW_ALIGN)
        return pltpu.make_async_copy(xs_hbm.at[pl.ds(row, ch), :], xbuf_ref.at[slot], in_sem.at[slot])

    def out_copies(g):
        slot = g % Y_SLOTS
        row = tab_ref[TAB_CHUNK_ROW, g]
        valid = tab_ref[TAB_CHUNK_VALID, g]
        yield valid >= ch, pltpu.make_async_copy(ybuf_ref.at[slot], y_hbm.at[pl.ds(pl.multiple_of(row, ROW_ALIGN), ch), :],
                                                 out_sem.at[slot])
        size = ch // 2
        while size >= ROW_ALIGN:
            off = pl.multiple_of((valid // (2 * size)) * (2 * size), ROW_ALIGN)
            cond = (valid < ch) & ((valid & size) != 0)
            yield cond, pltpu.make_async_copy(ybuf_ref.at[slot, pl.ds(off, size), :],
                                              y_hbm.at[pl.ds(pl.multiple_of(row + off, ROW_ALIGN), size), :],
                                              out_sem.at[slot])
            size //= 2

    def start_out(g):
        for cond, cp in out_copies(g):
            pl.when(cond)(cp.start)

    def wait_out(g):
        for cond, cp in out_copies(g):
            pl.when(cond)(cp.wait)

    @pl.when(e == 0)
    def _():
        for j in range(X_LOOKAHEAD):
            pl.when(j < n_total)(in_copy(j).start)

    @pl.when(n_chunks > 0)
    def _():
        wd16_ref[...] = wd_ref[0].astype(BF16)

    def chunk(c, carry):
        g = first + c
        pl.when(g + X_LOOKAHEAD < n_total)(in_copy(g + X_LOOKAHEAD).start)
        in_copy(g).wait()
        pl.when(g >= Y_SLOTS)(lambda: wait_out(g - Y_SLOTS))
        valid = tab_ref[TAB_CHUNK_VALID, g]

        def ffn(rows):
            xa, xb = _unpack_bf16_pairs(xbuf_ref[g % X_SLOTS, 0:rows, :])
            hgu = (_dot(xa.astype(BF16), wgu_ref[0, 0:half, :])
                   + _dot(xb.astype(BF16), wgu_ref[0, half:2 * half, :]))
            act = _silu(hgu[:, :ff]) * hgu[:, ff:]
            y = _dot(act.astype(BF16), wd16_ref[...])
            ybuf_ref[g % Y_SLOTS, 0:rows, :] = _pack_bf16_pairs(y)

        lo = 0
        for rows in EXPERT_ROW_STEPS:
            pl.when((valid > lo) & (valid <= rows))(functools.partial(ffn, rows))
            lo = rows
        start_out(g)
        return carry

    lax.fori_loop(0, n_chunks, chunk, 0)

    @pl.when(e == n_e - 1)
    def _():
        for j in range(Y_SLOTS, 0, -1):
            pl.when(n_total >= j)(lambda j=j: wait_out(n_total - j))
        n_rows = y_hbm.shape[0]
        used = tab_ref[TAB_START, e] + ((tab_ref[TAB_COUNT, e] + (ROW_ALIGN - 1)) // ROW_ALIGN) * ROW_ALIGN
        ybuf_ref[0] = jnp.zeros((ch, half), PACKED)
        for j in range(pl.cdiv(n_e * ROW_ALIGN + EXPERT_CHUNK + ROW_ALIGN, ch)):
            at = pl.multiple_of(jnp.minimum(used + j * ch, n_rows - ch), ROW_ALIGN)
            tail = pltpu.make_async_copy(ybuf_ref.at[0], y_hbm.at[pl.ds(at, ch), :], out_sem.at[0])
            tail.start()
            tail.wait()


def _experts(tab, xs, wgu16, w_down_e, p):
    half = xs.shape[1]
    n_e, ff, d = w_down_e.shape
    ch = EXPERT_CHUNK
    grid_spec = pltpu.PrefetchScalarGridSpec(
        num_scalar_prefetch=1,
        grid=(n_e,),
        in_specs=[pl.BlockSpec(memory_space=pl.ANY),
                  pl.BlockSpec((1, d, 2 * ff), lambda e, *_: (e, 0, 0)),
                  pl.BlockSpec((1, ff, d), lambda e, *_: (e, 0, 0))],
        out_specs=pl.BlockSpec(memory_space=pl.ANY),
        scratch_shapes=[pltpu.VMEM((ff, d), BF16),
                        pltpu.VMEM((X_SLOTS, ch, half), PACKED), pltpu.VMEM((Y_SLOTS, ch, half), PACKED),
                        pltpu.SemaphoreType.DMA((X_SLOTS,)), pltpu.SemaphoreType.DMA((Y_SLOTS,))],
    )
    return pl.pallas_call(
        functools.partial(_expert_kernel, n_e=n_e),
        out_shape=jax.ShapeDtypeStruct((p, half), PACKED),
        grid_spec=grid_spec,
        compiler_params=pltpu.CompilerParams(dimension_semantics=("arbitrary",),
                                             vmem_limit_bytes=VMEM_LIMIT_BYTES),
        name="moe_experts",
    )(tab, xs, wgu16, w_down_e)


def _combine_kernel(yg_ref, wcol_ref, x1_ref, mod_ref, npre_ref, npost_ref, wgus_ref, wds_ref, o_ref):
    te, d = x1_ref.shape
    half = d // 2
    ff = wds_ref.shape[0]

    x1 = x1_ref[...]
    h2 = _rms(x1, npre_ref[...]) * (1.0 + mod_ref[:, 4 * d:5 * d]) + mod_ref[:, 3 * d:4 * d]
    hgu = _dot(h2.astype(BF16), wgus_ref[...])
    act = _silu(hgu[:, :ff]) * hgu[:, ff:]
    shared = _dot(act.astype(BF16), wds_ref[...])

    acc_a = shared[:, :half]
    acc_b = shared[:, half:]
    for k in range(TOP_K):
        ya, yb = _unpack_bf16_pairs(yg_ref[k])
        wk = wcol_ref[:, k:k + 1]
        acc_a = acc_a + wk * ya
        acc_b = acc_b + wk * yb
    moe = jnp.concatenate([acc_a, acc_b], axis=-1)
    g2 = mod_ref[:, 5 * d:6 * d]
    o_ref[...] = x1 + g2 * _rms(moe, npost_ref[...])


def _combine(code, starts, y, wcol, x1, mod, npre2, npost2, w_gate_s, w_up_s, w_down_s):
    t, d = x1.shape
    half = d // 2
    te = SEQ_TILE
    wgus = jnp.concatenate([w_gate_s, w_up_s], axis=1).astype(BF16)
    wds = w_down_s.astype(BF16)
    n_split = COMBINE_SPLITS if t % (COMBINE_SPLITS * 2 * SC_GATHER_ROWS * SC_WORKERS // TOP_K) == 0 else 1
    tq = t // n_split
    steps = tq // te
    stream = x1
    for q in range(n_split):
        code_q = code[:, q * tq:(q + 1) * tq]
        yg = _sc_gather_rows(y, code_q.reshape(-1), starts).reshape(TOP_K, tq, half)
        row = lambda i, q=q: (q * steps + i, 0)
        stream = pl.pallas_call(
            _combine_kernel,
            out_shape=jax.ShapeDtypeStruct((t, d), F32),
            grid=(steps,),
            in_specs=[pl.BlockSpec((TOP_K, te, half), lambda i: (0, i, 0)),
                      pl.BlockSpec((te, 128), row),
                      pl.BlockSpec((te, d), row),
                      _const_spec(mod.shape),
                      _const_spec((1, d)),
                      _const_spec((1, d)),
                      _const_spec(wgus.shape),
                      _const_spec(wds.shape)],
            out_specs=pl.BlockSpec((te, d), row),
            input_output_aliases={2: 0},
            compiler_params=pltpu.CompilerParams(dimension_semantics=("arbitrary",),
                                                 vmem_limit_bytes=VMEM_LIMIT_BYTES),
            name="moe_combine",
        )(yg, wcol, stream, mod, npre2.reshape(1, d), npost2.reshape(1, d), wgus, wds)
    return stream


def kernel(x, c, w_ada, b_ada, norm_pre_mix, norm_post_mix, w_in, conv_w, w_conv_out, hgrn_lower_bounds, hgrn_norm_w, w_hgrn_out, w_o, norm_pre_ffn, norm_post_ffn, w_router, router_bias, w_gate_e, w_up_e, w_down_e, w_gate_s, w_up_s, w_down_s):
    bsz, seq, d = x.shape
    assert bsz == 1 and w_ada.shape[0] == 1
    mod = _ada_mod(c, w_ada[0], b_ada[0])
    x1, wgu16 = _mixer(x.reshape(seq, d), mod, norm_pre_mix[0], norm_post_mix[0], w_in[0], conv_w[0],
                       w_conv_out[0], hgrn_lower_bounds, hgrn_norm_w[0], w_hgrn_out[0], w_o[0],
                       w_gate_e[0], w_up_e[0])
    hp, code, wcol, cnt = _router(x1, mod, norm_pre_ffn[0], w_router[0], router_bias[0])
    n_e = w_router.shape[-1]
    tab, pad_list = _route_table(cnt, seq)
    starts = tab[TAB_START, :n_e]
    xs = _dispatch(code, starts, pad_list, hp)
    y = _experts(tab, xs, wgu16, w_down_e[0], _sorted_rows(seq, n_e))
    out = _combine(code, starts, y, wcol, x1, mod, norm_pre_ffn[0], norm_post_ffn[0],
                   w_gate_s[0], w_up_s[0], w_down_s[0])
    return out.reshape(bsz, seq, d)
```

```python
import functools

import jax
import jax.numpy as jnp
from jax import lax
from jax.experimental import pallas as pl
from jax.experimental.pallas import tpu as pltpu
from jax.experimental.pallas import tpu_sc as plsc

F32 = jnp.float32
BF16 = jnp.bfloat16

NORM_EPS = 1e-6
CONV_DIM = 512
HGRN_HEADS = 4
HGRN_DK = 128
HGRN_QK = HGRN_HEADS * HGRN_DK
N_GROUPS = 8
TOPK_GROUPS = 4
TOP_K = 8
ROUTED_SCALE = 2.5
CODE_SHIFT = 16

SEQ_TILE = 256
ROUTER_TILE = 1024
EXPERT_CHUNK = 512
EXPERT_ROW_STEPS = (128, 256, EXPERT_CHUNK)
ROW_ALIGN = 8
VMEM_LIMIT_BYTES = 56 * 1024 * 1024


def _dot(a, b):
    return jnp.dot(a, b, preferred_element_type=F32)


def _dot_nt(a, b):
    return lax.dot_general(a, b, (((1,), (1,)), ((), ())), preferred_element_type=F32)


def _dot_tn(a, b):
    return lax.dot_general(a, b, (((0,), (0,)), ((), ())), preferred_element_type=F32)


def _split3(x):
    hi = x.astype(BF16)
    r1 = x - hi.astype(F32)
    mid = r1.astype(BF16)
    lo = (r1 - mid.astype(F32)).astype(BF16)
    return hi, mid, lo


def _sigmoid(x):
    return 1.0 / (1.0 + jnp.exp(-x))


def _silu(x):
    return x * _sigmoid(x)


def _rms(x, w):
    ms = jnp.mean(x * x, axis=-1, keepdims=True)
    return x * lax.rsqrt(ms + NORM_EPS) * w


def _ada_kernel(c_ref, w_ref, b_ref, o_ref):
    cs = _silu(c_ref[...])
    h1, h2, h3 = _split3(cs)
    w1, w2, w3 = _split3(w_ref[...])
    r = cs.shape[0]
    stacked = jnp.concatenate([h.astype(F32) for h in (h1, h2, h3)], axis=0).astype(BF16)
    p1 = _dot(stacked, w1)
    p2 = _dot(stacked[0:2 * r], w2)
    p3 = _dot(h1, w3)
    acc = p1[0:r] + (p2[0:r] + p1[r:2 * r]) + (p3 + p2[r:2 * r] + p1[2 * r:3 * r])
    o_ref[...] = acc + b_ref[...]


def _ada_mod(c, w_ada, b_ada):
    d = c.shape[-1]
    n = w_ada.shape[-1]
    bn = 1024
    c8 = jnp.broadcast_to(c.reshape(1, d), (8, d))
    out = pl.pallas_call(
        _ada_kernel,
        out_shape=jax.ShapeDtypeStruct((8, n), F32),
        grid=(n // bn,),
        in_specs=[pl.BlockSpec((8, d), lambda j: (0, 0)),
                  pl.BlockSpec((d, bn), lambda j: (0, j)),
                  pl.BlockSpec((1, bn), lambda j: (0, j))],
        out_specs=pl.BlockSpec((8, bn), lambda j: (0, j)),
        compiler_params=pltpu.CompilerParams(dimension_semantics=("arbitrary",),
                                             vmem_limit_bytes=VMEM_LIMIT_BYTES),
        name="ada_mod",
    )(c8, w_ada, b_ada.reshape(1, n))
    return out[0:1]


def _level_reference(b, rolled, s, row):
    ts = b.shape[0]
    c = s // 2 - 1
    if s >= 16:
        pieces = []
        for blk in range(ts // s):
            r = blk * s + c
            pieces.append(jnp.broadcast_to(b[r:r + 1, :], (s, b.shape[1])))
        return pieces[0] if len(pieces) == 1 else jnp.concatenate(pieces, axis=0)
    pos = row & (s - 1)
    out = b
    for p in range(s):
        if p == c:
            continue
        out = jnp.where(pos == p, rolled[p - c], out)
    return out


def _cast_expert_weights(wg_ref, wu_ref, wgu16_ref):
    ff = wg_ref.shape[2]
    for j in range(wg_ref.shape[0]):
        wgu16_ref[j, :, 0:ff] = wg_ref[j].astype(BF16)
        wgu16_ref[j, :, ff:2 * ff] = wu_ref[j].astype(BF16)


def _mixer_kernel(x_ref, mod_ref, npre_ref, npost_ref, win_ref, convw_ref, wco_ref,
                  hlb_ref, hnw_ref, who_ref, wo_ref, wg_ref, wu_ref,
                  o_ref, wgu16_ref, state_ref, ubuf_ref):
    ts, d = x_ref.shape
    step = pl.program_id(0)
    _cast_expert_weights(wg_ref, wu_ref, wgu16_ref)

    @pl.when(step == 0)
    def _():
        state_ref[...] = jnp.zeros_like(state_ref)
        ubuf_ref[0:8, :] = jnp.zeros((8, CONV_DIM), F32)

    x = x_ref[...]
    sh1 = mod_ref[:, 0:d]
    sc1 = mod_ref[:, d:2 * d]
    g1 = mod_ref[:, 2 * d:3 * d]
    h = (_rms(x, npre_ref[...]) * (1.0 + sc1) + sh1).astype(BF16)

    def proj(lo, width):
        return _dot(h, win_ref[:, lo:lo + width])

    c0 = 0
    cb = proj(c0, CONV_DIM)
    u = proj(c0 + CONV_DIM, CONV_DIM) * proj(c0 + 2 * CONV_DIM, CONV_DIM)
    ubuf_ref[8:8 + ts, :] = u
    conv = (ubuf_ref[6:6 + ts, :] * convw_ref[0:1, :]
            + ubuf_ref[7:7 + ts, :] * convw_ref[1:2, :]
            + u * convw_ref[2:3, :])
    ubuf_ref[0:8, :] = ubuf_ref[ts:ts + 8, :]
    y_a = _dot((cb * conv).astype(BF16), wco_ref[...])

    c1 = 3 * CONV_DIM
    q = _silu(proj(c1, HGRN_QK))
    hl = hlb_ref[...]
    hm = jnp.max(hl, axis=0, keepdims=True)
    he = jnp.exp(hl - hm)
    lb = he[0:1, :] / jnp.sum(he, axis=0, keepdims=True)
    fg = lb + (1.0 - lb) * _sigmoid(proj(c1 + HGRN_QK, HGRN_QK))
    k = 1.0 - fg
    g = jnp.log(fg)
    v = proj(c1 + 2 * HGRN_QK, HGRN_QK)
    gg = proj(c1 + 3 * HGRN_QK, HGRN_QK)

    row = lax.broadcasted_iota(jnp.int32, (ts, ts), 0)
    col = lax.broadcasted_iota(jnp.int32, (ts, ts), 1)
    tril = jnp.where(row >= col, 1.0, 0.0).astype(BF16)
    ghi, gmid, glo = _split3(g)
    b = _dot(tril, ghi) + _dot(tril, gmid) + _dot(tril, glo)
    b_last = b[ts - 1:ts, :]

    rowq = lax.broadcasted_iota(jnp.int32, (ts, HGRN_QK), 0)
    rolled = {sft: pltpu.roll(b, sft % ts, axis=0) for sft in (-3, -2, -1, 1, 2, 3, 4)}
    xor = row ^ col

    levels = []
    s = ts
    while s >= 2:
        levels.append(s)
        s //= 2
    qt, kt = [], []
    for s in levels:
        bref = _level_reference(b, rolled, s, rowq)
        e = jnp.exp(-jnp.abs(b - bref))
        upper = (rowq & (s // 2)) != 0
        qt.append(jnp.where(upper, q * e, 0.0).astype(BF16))
        kt.append(jnp.where(upper, 0.0, k * e).astype(BF16))
    q_in = (q * jnp.exp(b)).astype(BF16)
    k_out = (k * jnp.exp(b_last - b)).astype(BF16)
    v16 = v.astype(BF16)
    qk = q * k
    decay_last = jnp.exp(b_last)

    hnw = hnw_ref[...]
    outs = []
    for hd in range(HGRN_HEADS):
        sl = slice(hd * HGRN_DK, (hd + 1) * HGRN_DK)
        a = jnp.zeros((ts, ts), F32)
        for li in range(len(levels) - 1, -1, -1):
            s = levels[li]
            p = _dot_nt(qt[li][:, sl], kt[li][:, sl])
            a = jnp.where(xor >= s // 2, p, a)
        st = state_ref[hd]
        o_h = (_dot(a.astype(BF16), v16[:, sl])
               + jnp.sum(qk[:, sl], axis=-1, keepdims=True) * v[:, sl]
               + _dot_nt(q_in[:, sl], st.astype(BF16)))
        state_ref[hd] = st * decay_last[:, sl] + _dot_tn(v16[:, sl], k_out[:, sl])
        outs.append(_rms(o_h, hnw))
    o = jnp.concatenate(outs, axis=-1) * _silu(gg)
    y_b = _dot(o.astype(BF16), who_ref[...])

    c2 = c1 + 4 * HGRN_QK
    m = _sigmoid(proj(c2, d)) * y_a + _sigmoid(proj(c2 + d, d)) * y_b
    y = _dot(m.astype(BF16), wo_ref[...])
    o_ref[...] = x + g1 * _rms(y, npost_ref[...])


def _const_spec(shape):
    nd = len(shape)
    return pl.BlockSpec(shape, lambda i: (0,) * nd)


def _mixer(x2, mod, npre, npost, w_in, conv_w, w_conv_out, hlb, hnw, w_hgrn_out, w_o, w_gate_e, w_up_e):
    t, d = x2.shape
    ts = SEQ_TILE
    n = t // ts
    n_e, _, ff = w_gate_e.shape
    assert n_e % n == 0 and conv_w.shape == (3, CONV_DIM)
    per = n_e // n
    args = (x2, mod, npre.reshape(1, d), npost.reshape(1, d), w_in.astype(BF16), conv_w,
            w_conv_out.astype(BF16), hlb, hnw.reshape(1, -1), w_hgrn_out.astype(BF16),
            w_o.astype(BF16))
    experts = lambda i: (i, 0, 0)
    in_specs = ([pl.BlockSpec((ts, d), lambda i: (i, 0))] + [_const_spec(a.shape) for a in args[1:]]
                + [pl.BlockSpec((per, d, ff), experts)] * 2)
    return pl.pallas_call(
        _mixer_kernel,
        out_shape=[jax.ShapeDtypeStruct((t, d), F32), jax.ShapeDtypeStruct((n_e, d, 2 * ff), BF16)],
        grid=(n,),
        in_specs=in_specs,
        out_specs=[pl.BlockSpec((ts, d), lambda i: (i, 0)), pl.BlockSpec((per, d, 2 * ff), experts)],
        scratch_shapes=[pltpu.VMEM((HGRN_HEADS, HGRN_DK, HGRN_DK), F32),
                        pltpu.VMEM((ts + 8, CONV_DIM), F32)],
        compiler_params=pltpu.CompilerParams(dimension_semantics=("arbitrary",),
                                             vmem_limit_bytes=VMEM_LIMIT_BYTES),
        name="token_mixer",
    )(*args, w_gate_e, w_up_e)


PACKED = jnp.int32


def _pack_bf16_pairs(x):
    m = x.shape[1] // 2
    hi = lax.bitcast_convert_type(x[:, :m].astype(BF16).astype(F32), jnp.uint32)
    lo = lax.bitcast_convert_type(x[:, m:].astype(BF16).astype(F32), jnp.uint32)
    return lax.bitcast_convert_type(hi | (lo >> 16), PACKED)


def _unpack_bf16_pairs(p):
    p = lax.bitcast_convert_type(p, jnp.uint32)
    hi = lax.bitcast_convert_type(p & jnp.uint32(0xFFFF0000), F32)
    lo = lax.bitcast_convert_type(p << 16, F32)
    return hi, lo


def _router_kernel(x1_ref, mod_ref, npre_ref, wrh_ref, wrl_ref, rb_ref,
                   hp_ref, code_ref, wcol_ref, cnt_ref, carry_ref, rows_ref):
    assert N_GROUPS == TOP_K == rows_ref.shape[0]
    tr, d = x1_ref.shape
    n_e = wrh_ref.shape[0]
    gsz = n_e // N_GROUPS
    neg = -jnp.inf

    @pl.when(pl.program_id(0) == 0)
    def _():
        carry_ref[...] = jnp.zeros_like(carry_ref)

    sh2 = mod_ref[:, 3 * d:4 * d]
    sc2 = mod_ref[:, 4 * d:5 * d]
    h2 = _rms(x1_ref[...], npre_ref[...]) * (1.0 + sc2) + sh2
    hp_ref[...] = _pack_bf16_pairs(h2)
    h_hi = h2.astype(BF16)
    h_lo = (h2 - h_hi.astype(F32)).astype(BF16)
    wrh = wrh_ref[...]
    logits = _dot_nt(wrh, h_hi) + (_dot_nt(wrh, h_lo) + _dot_nt(wrl_ref[...], h_hi))
    scores = _sigmoid(logits)
    sel = scores + rb_ref[:, 0:1]

    io_e = lax.broadcasted_iota(jnp.int32, (n_e, tr), 0)
    for g in range(N_GROUPS):
        blk = sel[g * gsz:(g + 1) * gsz, :]
        io = lax.broadcasted_iota(jnp.int32, (gsz, tr), 0) + g * gsz
        m1 = jnp.max(blk, axis=0, keepdims=True)
        i1 = jnp.min(jnp.where(blk == m1, io, n_e), axis=0, keepdims=True)
        m2 = jnp.max(jnp.where(io == i1, neg, blk), axis=0, keepdims=True)
        rows_ref[g:g + 1, :] = m1 + m2
    gs = rows_ref[...]
    io_g = lax.broadcasted_iota(jnp.int32, (N_GROUPS, tr), 0)
    gsel = jnp.zeros((N_GROUPS, tr), F32)
    for _ in range(TOPK_GROUPS):
        m = jnp.max(gs, axis=0, keepdims=True)
        gi = jnp.min(jnp.where(gs == m, io_g, N_GROUPS), axis=0, keepdims=True)
        hit = io_g == gi
        gsel = jnp.where(hit, 1.0, gsel)
        gs = jnp.where(hit, neg, gs)
    rows_ref[...] = gsel
    cur = jnp.concatenate(
        [jnp.where(rows_ref[g:g + 1, :] > 0.5, sel[g * gsz:(g + 1) * gsz, :], neg) for g in range(N_GROUPS)],
        axis=0)

    idxs = []
    selm = jnp.zeros((n_e, tr), F32)
    for k in range(TOP_K):
        m = jnp.max(cur, axis=0, keepdims=True)
        idx = jnp.min(jnp.where(cur == m, io_e, n_e), axis=0, keepdims=True)
        hit = io_e == idx
        rows_ref[k:k + 1, :] = jnp.sum(jnp.where(hit, scores, 0.0), axis=0, keepdims=True)
        cur = jnp.where(hit, neg, cur)
        selm = jnp.where(hit, 1.0, selm)
        idxs.append(idx)

    r_io = lax.broadcasted_iota(jnp.int32, (tr, tr + 128), 0)
    c_io = lax.broadcasted_iota(jnp.int32, (tr, tr + 128), 1)
    before = jnp.where((r_io < c_io) | (c_io >= tr), 1.0, 0.0).astype(BF16)
    r_ext = _dot(selm.astype(BF16), before)
    rank_full = r_ext[:, :tr] + carry_ref[:, 0:1]
    for k, idx in enumerate(idxs):
        rk = jnp.sum(jnp.where(io_e == idx, rank_full, 0.0), axis=0, keepdims=True)
        code_ref[k:k + 1, :] = (idx << CODE_SHIFT) | rk.astype(jnp.int32)
    carry_ref[...] = carry_ref[...] + r_ext[:, tr:]
    cnt_ref[...] = carry_ref[...]

    wk = rows_ref[...]
    wn = wk / (jnp.sum(wk, axis=0, keepdims=True) + 1e-20) * ROUTED_SCALE
    eye = jnp.where(lax.broadcasted_iota(jnp.int32, (TOP_K, 128), 0)
                    == lax.broadcasted_iota(jnp.int32, (TOP_K, 128), 1), 1.0, 0.0).astype(BF16)
    w1, w2, w3 = _split3(wn)
    wcol_ref[...] = _dot_tn(w1, eye) + _dot_tn(w2, eye) + _dot_tn(w3, eye)


def _router(x1, mod, npre2, w_router, router_bias):
    t, d = x1.shape
    n_e = w_router.shape[1]
    assert t <= (1 << CODE_SHIFT)
    tr = ROUTER_TILE if t % ROUTER_TILE == 0 else SEQ_TILE
    wrt = w_router.T
    wrh = wrt.astype(BF16)
    wrl = (wrt - wrh.astype(F32)).astype(BF16)
    rb = jnp.broadcast_to(router_bias.reshape(n_e, 1), (n_e, 128))
    args = (x1, mod, npre2.reshape(1, d), wrh, wrl, rb)
    in_specs = [pl.BlockSpec((tr, d), lambda i: (i, 0))] + [_const_spec(a.shape) for a in args[1:]]
    return pl.pallas_call(
        _router_kernel,
        out_shape=(jax.ShapeDtypeStruct((t, d // 2), PACKED),
                   jax.ShapeDtypeStruct((TOP_K, t), jnp.int32),
                   jax.ShapeDtypeStruct((t, 128), F32),
                   jax.ShapeDtypeStruct((n_e, 128), F32)),
        grid=(t // tr,),
        in_specs=in_specs,
        out_specs=(pl.BlockSpec((tr, d // 2), lambda i: (i, 0)),
                   pl.BlockSpec((TOP_K, tr), lambda i: (0, i)),
                   pl.BlockSpec((tr, 128), lambda i: (i, 0)),
                   pl.BlockSpec((n_e, 128), lambda i: (0, 0))),
        scratch_shapes=[pltpu.VMEM((n_e, 128), F32), pltpu.VMEM((TOP_K, tr), F32)],
        compiler_params=pltpu.CompilerParams(dimension_semantics=("arbitrary",),
                                             vmem_limit_bytes=VMEM_LIMIT_BYTES),
        name="moe_router",
    )(*args)


TAB_CHUNK_ROW, TAB_CHUNK_VALID, TAB_START, TAB_COUNT, TAB_CHUNK0, TAB_NCHUNK, TAB_TOTAL = range(7)
TAB_ROWS = 8


def _excl_cumsum_rows(lower, x):
    hi = jnp.floor(x * (1.0 / 128.0))
    lo = x - hi * 128.0
    return 128.0 * _dot(lower, hi.astype(BF16)) + _dot(lower, lo.astype(BF16))


def _table_kernel(cnt_ref, tab_ref, pad_ref, *, n_rows):
    n_e = cnt_ref.shape[0]
    n_g = tab_ref.shape[1]
    ch = float(EXPERT_CHUNK)
    cnt = jnp.floor((cnt_ref[...] + (ROW_ALIGN - 1)) * (1.0 / ROW_ALIGN)) * ROW_ALIGN
    r = lax.broadcasted_iota(jnp.int32, (n_e, n_e), 0)
    c = lax.broadcasted_iota(jnp.int32, (n_e, n_e), 1)
    lower = jnp.where(c < r, 1.0, 0.0).astype(BF16)
    starts = _excl_cumsum_rows(lower, cnt)
    start_col = starts[:, 0:1]

    nch = jnp.floor((cnt + (ch - 1.0)) * (1.0 / ch))
    chunk0 = _excl_cumsum_rows(lower, nch)
    cend_col = (chunk0 + nch)[:, 0:1]
    eye = r == c

    def as_row(col):
        return jnp.sum(jnp.where(eye, col, 0.0), axis=0, keepdims=True)

    g = lax.broadcasted_iota(jnp.int32, (n_e, n_g), 1).astype(F32)
    owner = jnp.sum(jnp.where(cend_col <= g, 1.0, 0.0), axis=0, keepdims=True)
    mine = lax.broadcasted_iota(jnp.int32, (n_e, n_g), 0).astype(F32) == owner
    g_row = lax.broadcasted_iota(jnp.int32, (1, n_g), 1).astype(F32)
    base = jnp.sum(jnp.where(mine, start_col - chunk0[:, 0:1] * ch, 0.0), axis=0, keepdims=True)
    left = jnp.sum(jnp.where(mine, cnt[:, 0:1] + chunk0[:, 0:1] * ch, 0.0), axis=0, keepdims=True)
    pad = jnp.zeros((1, n_g - n_e), F32)

    def wide(row):
        return jnp.concatenate([row, pad], axis=1)

    total = jnp.sum(nch[:, 0:1], axis=0, keepdims=True)
    tab_ref[...] = jnp.zeros(tab_ref.shape, jnp.int32)
    tab_ref[TAB_CHUNK_ROW:TAB_CHUNK_ROW + 1, :] = (base + g_row * ch).astype(jnp.int32)
    tab_ref[TAB_CHUNK_VALID:TAB_CHUNK_VALID + 1, :] = jnp.clip(left - g_row * ch, 0.0, ch).astype(jnp.int32)
    tab_ref[TAB_START:TAB_START + 1, :] = wide(as_row(start_col)).astype(jnp.int32)
    tab_ref[TAB_COUNT:TAB_COUNT + 1, :] = wide(as_row(cnt_ref[:, 0:1])).astype(jnp.int32)
    tab_ref[TAB_CHUNK0:TAB_CHUNK0 + 1, :] = wide(as_row(chunk0[:, 0:1])).astype(jnp.int32)
    tab_ref[TAB_NCHUNK:TAB_NCHUNK + 1, :] = wide(as_row(nch[:, 0:1])).astype(jnp.int32)
    tab_ref[TAB_TOTAL:TAB_TOTAL + 1, :] = jnp.broadcast_to(total, (1, n_g)).astype(jnp.int32)

    cols = pad_ref.shape[1]
    tail_cols = cols - n_e
    j_io = lax.broadcasted_iota(jnp.int32, (ROW_ALIGN, cols), 0).astype(F32)
    c_io = lax.broadcasted_iota(jnp.int32, (ROW_ALIGN, cols), 1).astype(F32)
    spare = float(n_rows) + j_io * cols + c_io
    zeros_tail = jnp.zeros((1, tail_cols), F32)
    raw_col = cnt_ref[:, 0:1]
    end_w = jnp.concatenate([as_row(start_col + raw_col), zeros_tail], axis=1)
    npad_w = jnp.concatenate([as_row(cnt[:, 0:1] - raw_col), zeros_tail], axis=1)
    used = jnp.sum(cnt[:, 0:1], axis=0, keepdims=True)
    after = used + j_io * tail_cols + (c_io - n_e)
    target = jnp.where(c_io < n_e,
                       jnp.where(j_io < npad_w, end_w + j_io, spare),
                       jnp.where(after < float(n_rows), after, spare))
    pad_ref[...] = target.astype(jnp.int32)


def _sorted_rows(t, n_e):
    return t * TOP_K + n_e * ROW_ALIGN + EXPERT_CHUNK + ROW_ALIGN


def _pad_list_cols(t, n_e):
    entries = n_e * ROW_ALIGN + (_sorted_rows(t, n_e) - t * TOP_K)
    per_round = SC_WORKERS * SC_GATHER_ROWS
    return pl.cdiv(entries, per_round) * per_round // ROW_ALIGN


def _route_table(cnt, t):
    n_e = cnt.shape[0]
    n_g = (t * TOP_K) // EXPERT_CHUNK + n_e
    cols = _pad_list_cols(t, n_e)
    return pl.pallas_call(
        functools.partial(_table_kernel, n_rows=_sorted_rows(t, n_e)),
        out_shape=(jax.ShapeDtypeStruct((TAB_ROWS, n_g), jnp.int32),
                   jax.ShapeDtypeStruct((ROW_ALIGN, cols), jnp.int32)),
        grid=(1,),
        in_specs=[_const_spec(cnt.shape)],
        out_specs=(_const_spec((TAB_ROWS, n_g)), _const_spec((ROW_ALIGN, cols))),
        compiler_params=pltpu.CompilerParams(dimension_semantics=("arbitrary",)),
        name="moe_table",
    )(cnt)


SC_CORES = 2
SC_SUBCORES = 16
SC_WORKERS = SC_CORES * SC_SUBCORES
SC_GATHER_ROWS = 64
COMBINE_SPLITS = 4


def _sc_mesh():
    return plsc.VectorSubcoreMesh(core_axis_name="c", subcore_axis_name="s")


def _sc_worker_id():
    return lax.axis_index("s") * SC_CORES + lax.axis_index("c")


SC_LANES = 16


def _sc_decode_rows(code_v, starts_v, idx_v):
    for i in range(code_v.shape[0] // SC_LANES):
        lanes = pl.ds(i * SC_LANES, SC_LANES)
        code = code_v[lanes]
        expert = lax.shift_right_logical(code, CODE_SHIFT)
        idx_v[lanes] = plsc.load_gather(starts_v, [expert]) + (code & ((1 << CODE_SHIFT) - 1))


def _sc_params():
    return pltpu.CompilerParams(needs_layout_passes=False)


def _sc_scatter_rows(rows, code_flat, starts, zero_list, n_out):
    t, w = rows.shape
    n_k = code_flat.shape[0] // t
    n_e = starts.shape[0]
    r = SC_GATHER_ROWS
    per = t // SC_WORKERS
    n_steps = per // r
    zero_rounds = zero_list.shape[0] // (SC_WORKERS * r)
    assert per % (2 * r) == 0 and zero_list.shape[0] % (SC_WORKERS * r) == 0 and zero_rounds <= n_k
    zeros = jnp.zeros((r, w), rows.dtype)

    @functools.partial(
        pl.kernel, mesh=_sc_mesh(), out_type=jax.ShapeDtypeStruct((n_out, w), rows.dtype),
        scratch_types=[pltpu.VMEM((n_e,), jnp.int32), pltpu.VMEM((n_k, r), jnp.int32),
                       pltpu.VMEM((n_k, r), jnp.int32), pltpu.VMEM((2, r, w), rows.dtype),
                       pltpu.SemaphoreType.DMA((2,)), pltpu.SemaphoreType.DMA],
        compiler_params=_sc_params(), name="moe_dispatch_sc")
    def scatter(rows_hbm, code_hbm, starts_hbm, zero_idx_hbm, zeros_hbm, out_hbm,
                starts_v, code_v, idx_v, rows_v, l_sem, s_sem):
        worker = _sc_worker_id()
        base = worker * per
        pltpu.sync_copy(starts_hbm, starts_v)

        def load(j, b):
            return pltpu.make_async_copy(rows_hbm.at[pl.ds(pl.multiple_of(base + j * r, ROW_ALIGN), r)],
                                         rows_v.at[b], l_sem.at[b])

        load(0, 0).start()
        pltpu.sync_copy(zeros_hbm, rows_v.at[1])
        zero_copies = []
        for z in range(zero_rounds):
            off = pl.multiple_of((worker * zero_rounds + z) * r, ROW_ALIGN)
            pltpu.sync_copy(zero_idx_hbm.at[pl.ds(off, r)], idx_v.at[z])
            zero_copies.append(pltpu.make_async_copy(rows_v.at[1], out_hbm.at[idx_v.at[z]], s_sem))
            zero_copies[-1].start()
        for cp in zero_copies:
            cp.wait()

        @pl.loop(0, n_steps, step=2)
        def _(j0):
            for b in range(2):
                j = j0 + b
                pl.when(j + 1 < n_steps)(load(j + 1, 1 - b).start)
                pltpu.sync_copy(code_hbm.at[worker * n_steps + j], code_v)
                for k in range(n_k):
                    _sc_decode_rows(code_v.at[k], starts_v, idx_v.at[k])
                load(j, b).wait()
                copies = [pltpu.make_async_copy(rows_v.at[b], out_hbm.at[idx_v.at[k]], s_sem) for k in range(n_k)]
                for cp in copies:
                    cp.start()
                for cp in copies:
                    cp.wait()

    code_tiles = code_flat.reshape(n_k, t // r, r).transpose(1, 0, 2)
    return scatter(rows, code_tiles, starts, zero_list, zeros)


def _sc_gather_rows(table, code_flat, starts):
    n = code_flat.shape[0]
    w = table.shape[1]
    n_e = starts.shape[0]
    r = SC_GATHER_ROWS
    per = n // SC_WORKERS
    n_steps = per // r
    assert per % (2 * r) == 0

    @functools.partial(
        pl.kernel, mesh=_sc_mesh(), out_type=jax.ShapeDtypeStruct((n, w), table.dtype),
        scratch_types=[pltpu.VMEM((n_e,), jnp.int32), pltpu.VMEM((r,), jnp.int32), pltpu.VMEM((2, r), jnp.int32),
                       pltpu.VMEM((2, r, w), table.dtype),
                       pltpu.SemaphoreType.DMA((2,)), pltpu.SemaphoreType.DMA((2,))],
        compiler_params=_sc_params(), name="moe_gather_sc")
    def gather(table_hbm, code_hbm, starts_hbm, out_hbm, starts_v, code_v, idx_v, rows_v, g_sem, w_sem):
        base = _sc_worker_id() * per
        pltpu.sync_copy(starts_hbm, starts_v)

        def at(j):
            return pl.ds(pl.multiple_of(base + j * r, ROW_ALIGN), r)

        def fetch(j, b):
            return pltpu.make_async_copy(table_hbm.at[idx_v.at[b]], rows_v.at[b], g_sem.at[b])

        def start_fetch(j, b):
            pltpu.sync_copy(code_hbm.at[at(j)], code_v)
            _sc_decode_rows(code_v, starts_v, idx_v.at[b])
            fetch(j, b).start()

        def write(j, b):
            return pltpu.make_async_copy(rows_v.at[b], out_hbm.at[at(j)], w_sem.at[b])

        start_fetch(0, 0)

        @pl.loop(0, n_steps, step=2)
        def _(j0):
            for b in range(2):
                j = j0 + b

                @pl.when(j + 1 < n_steps)
                def _():
                    pl.when(j >= 1)(write(j - 1, 1 - b).wait)
                    start_fetch(j + 1, 1 - b)

                fetch(j, b).wait()
                write(j, b).start()

        write(n_steps - 2, 0).wait()
        write(n_steps - 1, 1).wait()

    return gather(table, code_flat, starts)


def _dispatch(code, starts, pad_list, hp):
    t = hp.shape[0]
    n_out = _sorted_rows(t, starts.shape[0]) + pad_list.size
    return _sc_scatter_rows(hp, code.reshape(-1), starts, pad_list.reshape(-1), n_out)


X_LOOKAHEAD = 2
X_SLOTS = X_LOOKAHEAD + 1
Y_SLOTS = 2


def _expert_kernel(tab_ref, xs_hbm, wgu_ref, wd_ref, y_hbm,
                   wd16_ref, xbuf_ref, ybuf_ref, in_sem, out_sem, *, n_e):
    e = pl.program_id(0)
    ch, half = xbuf_ref.shape[1:]
    ff = wd_ref.shape[1]
    n_total = tab_ref[TAB_TOTAL, 0]
    first = tab_ref[TAB_CHUNK0, e]
    n_chunks = tab_ref[TAB_NCHUNK, e]

    def in_copy(g):
        slot = g % X_SLOTS
        row = pl.multiple_of(tab_ref[TAB_CHUNK_ROW, g], ROW_ALIGN)
        return pltpu.make_async_copy(xs_hbm.at[pl.ds(row, ch), :], xbuf_ref.at[slot], in_sem.at[slot])

    def out_copies(g):
        slot = g % Y_SLOTS
        row = tab_ref[TAB_CHUNK_ROW, g]
        valid = tab_ref[TAB_CHUNK_VALID, g]
        yield valid >= ch, pltpu.make_async_copy(ybuf_ref.at[slot], y_hbm.at[pl.ds(pl.multiple_of(row, ROW_ALIGN), ch), :],
                                                 out_sem.at[slot])
        size = ch // 2
        while size >= ROW_ALIGN:
            off = pl.multiple_of((valid // (2 * size)) * (2 * size), ROW_ALIGN)
            cond = (valid < ch) & ((valid & size) != 0)
            yield cond, pltpu.make_async_copy(ybuf_ref.at[slot, pl.ds(off, size), :],
                                              y_hbm.at[pl.ds(pl.multiple_of(row + off, ROW_ALIGN), size), :],
                                              out_sem.at[slot])
            size //= 2

    def start_out(g):
        for cond, cp in out_copies(g):
            pl.when(cond)(cp.start)

    def wait_out(g):
        for cond, cp in out_copies(g):
            pl.when(cond)(cp.wait)

    @pl.when(e == 0)
    def _():
        for j in range(X_LOOKAHEAD):
            pl.when(j < n_total)(in_copy(j).start)

    @pl.when(n_chunks > 0)
    def _():
        wd16_ref[...] = wd_ref[0].astype(BF16)

    def chunk(c, carry):
        g = first + c
        pl.when(g + X_LOOKAHEAD < n_total)(in_copy(g + X_LOOKAHEAD).start)
        in_copy(g).wait()
        pl.when(g >= Y_SLOTS)(lambda: wait_out(g - Y_SLOTS))
        valid = tab_ref[TAB_CHUNK_VALID, g]

        def ffn(rows):
            xa, xb = _unpack_bf16_pairs(xbuf_ref[g % X_SLOTS, 0:rows, :])
            hgu = (_dot(xa.astype(BF16), wgu_ref[0, 0:half, :])
                   + _dot(xb.astype(BF16), wgu_ref[0, half:2 * half, :]))
            act = _silu(hgu[:, :ff]) * hgu[:, ff:]
            y = _dot(act.astype(BF16), wd16_ref[...])
            ybuf_ref[g % Y_SLOTS, 0:rows, :] = _pack_bf16_pairs(y)

        lo = 0
        for rows in EXPERT_ROW_STEPS:
            pl.when((valid > lo) & (valid <= rows))(functools.partial(ffn, rows))
            lo = rows
        start_out(g)
        return carry

    lax.fori_loop(0, n_chunks, chunk, 0)

    @pl.when(e == n_e - 1)
    def _():
        for j in range(Y_SLOTS, 0, -1):
            pl.when(n_total >= j)(lambda j=j: wait_out(n_total - j))
        n_rows = y_hbm.shape[0]
        used = tab_ref[TAB_START, e] + ((tab_ref[TAB_COUNT, e] + (ROW_ALIGN - 1)) // ROW_ALIGN) * ROW_ALIGN
        ybuf_ref[0] = jnp.zeros((ch, half), PACKED)
        for j in range(pl.cdiv(n_e * ROW_ALIGN + EXPERT_CHUNK + ROW_ALIGN, ch)):
            at = pl.multiple_of(jnp.minimum(used + j * ch, n_rows - ch), ROW_ALIGN)
            tail = pltpu.make_async_copy(ybuf_ref.at[0], y_hbm.at[pl.ds(at, ch), :], out_sem.at[0])
            tail.start()
            tail.wait()


def _experts(tab, xs, wgu16, w_down_e, p):
    half = xs.shape[1]
    n_e, ff, d = w_down_e.shape
    ch = EXPERT_CHUNK
    grid_spec = pltpu.PrefetchScalarGridSpec(
        num_scalar_prefetch=1,
        grid=(n_e,),
        in_specs=[pl.BlockSpec(memory_space=pl.ANY),
                  pl.BlockSpec((1, d, 2 * ff), lambda e, *_: (e, 0, 0)),
                  pl.BlockSpec((1, ff, d), lambda e, *_: (e, 0, 0))],
        out_specs=pl.BlockSpec(memory_space=pl.ANY),
        scratch_shapes=[pltpu.VMEM((ff, d), BF16),
                        pltpu.VMEM((X_SLOTS, ch, half), PACKED), pltpu.VMEM((Y_SLOTS, ch, half), PACKED),
                        pltpu.SemaphoreType.DMA((X_SLOTS,)), pltpu.SemaphoreType.DMA((Y_SLOTS,))],
    )
    return pl.pallas_call(
        functools.partial(_expert_kernel, n_e=n_e),
        out_shape=jax.ShapeDtypeStruct((p, half), PACKED),
        grid_spec=grid_spec,
        compiler_params=pltpu.CompilerParams(dimension_semantics=("arbitrary",),
                                             vmem_limit_bytes=VMEM_LIMIT_BYTES),
        name="moe_experts",
    )(tab, xs, wgu16, w_down_e)


def _combine_kernel(yg_ref, wcol_ref, x1_ref, mod_ref, npre_ref, npost_ref, wgus_ref, wds_ref, o_ref):
    te, d = x1_ref.shape
    half = d // 2
    ff = wds_ref.shape[0]

    x1 = x1_ref[...]
    h2 = _rms(x1, npre_ref[...]) * (1.0 + mod_ref[:, 4 * d:5 * d]) + mod_ref[:, 3 * d:4 * d]
    hgu = _dot(h2.astype(BF16), wgus_ref[...])
    act = _silu(hgu[:, :ff]) * hgu[:, ff:]
    shared = _dot(act.astype(BF16), wds_ref[...])

    acc_a = shared[:, :half]
    acc_b = shared[:, half:]
    for k in range(TOP_K):
        ya, yb = _unpack_bf16_pairs(yg_ref[k])
        wk = wcol_ref[:, k:k + 1]
        acc_a = acc_a + wk * ya
        acc_b = acc_b + wk * yb
    moe = jnp.concatenate([acc_a, acc_b], axis=-1)
    g2 = mod_ref[:, 5 * d:6 * d]
    o_ref[...] = x1 + g2 * _rms(moe, npost_ref[...])


def _combine(code, starts, y, wcol, x1, mod, npre2, npost2, w_gate_s, w_up_s, w_down_s):
    t, d = x1.shape
    half = d // 2
    te = SEQ_TILE
    wgus = jnp.concatenate([w_gate_s, w_up_s], axis=1).astype(BF16)
    wds = w_down_s.astype(BF16)
    n_split = COMBINE_SPLITS if t % (COMBINE_SPLITS * 2 * SC_GATHER_ROWS * SC_WORKERS // TOP_K) == 0 else 1
    tq = t // n_split
    steps = tq // te
    stream = x1
    for q in range(n_split):
        code_q = code[:, q * tq:(q + 1) * tq]
        yg = _sc_gather_rows(y, code_q.reshape(-1), starts).reshape(TOP_K, tq, half)
        row = lambda i, q=q: (q * steps + i, 0)
        stream = pl.pallas_call(
            _combine_kernel,
            out_shape=jax.ShapeDtypeStruct((t, d), F32),
            grid=(steps,),
            in_specs=[pl.BlockSpec((TOP_K, te, half), lambda i: (0, i, 0)),
                      pl.BlockSpec((te, 128), row),
                      pl.BlockSpec((te, d), row),
                      _const_spec(mod.shape),
                      _const_spec((1, d)),
                      _const_spec((1, d)),
                      _const_spec(wgus.shape),
                      _const_spec(wds.shape)],
            out_specs=pl.BlockSpec((te, d), row),
            input_output_aliases={2: 0},
            compiler_params=pltpu.CompilerParams(dimension_semantics=("arbitrary",),
                                                 vmem_limit_bytes=VMEM_LIMIT_BYTES),
            name="moe_combine",
        )(yg, wcol, stream, mod, npre2.reshape(1, d), npost2.reshape(1, d), wgus, wds)
    return stream


def kernel(x, c, w_ada, b_ada, norm_pre_mix, norm_post_mix, w_in, conv_w, w_conv_out, hgrn_lower_bounds, hgrn_norm_w, w_hgrn_out, w_o, norm_pre_ffn, norm_post_ffn, w_router, router_bias, w_gate_e, w_up_e, w_down_e, w_gate_s, w_up_s, w_down_s):
    bsz, seq, d = x.shape
    assert bsz == 1 and w_ada.shape[0] == 1
    mod = _ada_mod(c, w_ada[0], b_ada[0])
    x1, wgu16 = _mixer(x.reshape(seq, d), mod, norm_pre_mix[0], norm_post_mix[0], w_in[0], conv_w[0],
                       w_conv_out[0], hgrn_lower_bounds, hgrn_norm_w[0], w_hgrn_out[0], w_o[0],
                       w_gate_e[0], w_up_e[0])
    hp, code, wcol, cnt = _router(x1, mod, norm_pre_ffn[0], w_router[0], router_bias[0])
    n_e = w_router.shape[-1]
    tab, pad_list = _route_table(cnt, seq)
    starts = tab[TAB_START, :n_e]
    xs = _dispatch(code, starts, pad_list, hp)
    y = _experts(tab, xs, wgu16, w_down_e[0], _sorted_rows(seq, n_e))
    out = _combine(code, starts, y, wcol, x1, mod, norm_pre_ffn[0], norm_post_ffn[0],
                   w_gate_s[0], w_up_s[0], w_down_s[0])
    return out.reshape(bsz, seq, d)
```
